```python
import math
import jax, jax.numpy as jnp
from jax import lax
import numpy as np

D_MODEL = 1024
BATCH = 8
SEQ = 2048
DEPTH = 4
DEC_BATCH = 128
DEC_SEQ = 4
PAST_LEN = 16384
PAGE_SIZE = 128

N_MIXERS = 3
N_A = (DEPTH + 2) // 3
N_B = (DEPTH + 1) // 3
N_C = DEPTH // 3
EPS = 1e-6
CHUNK = 128
D_SGU = D_MODEL
HEADS_A = 8
HD_A = D_SGU // HEADS_A
D_RNN = D_MODEL
HEADS_B = 16
HD_B = D_RNN // HEADS_B
CONV_W = 4
LRU_C = 8.0
D_S5 = D_MODEL
GROUP_C = 16
G_C = D_S5 // GROUP_C
P_C = 64
D_FF = 4 * D_MODEL

kernel_name = "hybrid_sgu_rglru_s5_decoder_step"


def rmsnorm(x, g):
    xf = x.astype(jnp.float32)
    y = xf * lax.rsqrt(jnp.mean(xf * xf, axis=-1, keepdims=True) + EPS)
    return (y * g.astype(jnp.float32)).astype(x.dtype)


def chunk_sgu(h, w_in, sgu_g, w_s, b_s, w_out):
    bsz, T, _ = h.shape
    u, v = jnp.split(jax.nn.gelu(h @ w_in), 2, axis=-1)
    v = rmsnorm(v, sgu_g)
    L = min(T, CHUNK)
    nc = T // L
    mask = jnp.tril(jnp.ones((L, L), dtype=bool))
    w = jnp.where(mask, w_s[:, :L, :L], 0.0)
    vc = v.reshape(bsz, nc, L, HEADS_A, HD_A)
    mixed = jnp.einsum('gts,bcsgd->bctgd', w, vc) + b_s[:, :L].T[None, None, :, :, None]
    y = u * mixed.reshape(bsz, T, D_SGU)
    return y @ w_out, v


def rglru_block(h, conv_buf, h0, w_in, conv_w, conv_b, w_a, b_a, w_x, b_x, lam, w_out):
    bsz, T, _ = h.shape
    gate, xb = jnp.split(h @ w_in, 2, axis=-1)
    gate = jax.nn.gelu(gate)
    x_ext = jnp.concatenate([conv_buf.astype(xb.dtype), xb], axis=1)
    xc = conv_b + sum(x_ext[:, k:k + T] * conv_w[k] for k in range(CONV_W))
    new_buf = x_ext[:, T:]
    xh = xc.reshape(bsz, T, HEADS_B, HD_B)
    r = jax.nn.sigmoid(jnp.einsum('bthi,hij->bthj', xh, w_a).reshape(bsz, T, D_RNN) + b_a)
    i = jax.nn.sigmoid(jnp.einsum('bthi,hij->bthj', xh, w_x).reshape(bsz, T, D_RNN) + b_x)
    log_a = -LRU_C * r.astype(jnp.float32) * jax.nn.softplus(-lam.astype(jnp.float32))
    a = jnp.exp(log_a)
    mult = jnp.sqrt(jnp.maximum(-jnp.expm1(2.0 * log_a), 0.0))
    bx = mult * (i * xc).astype(jnp.float32)
    bx = bx.at[:, 0].add(a[:, 0] * h0.astype(jnp.float32))

    def comb(left, right):
        a1, b1 = left
        a2, b2 = right
        return a1 * a2, a2 * b1 + b2

    _, hs = lax.associative_scan(comb, (a, bx), axis=1)
    y = (hs.astype(h.dtype) * gate) @ w_out
    return y, new_buf, hs[:, -1].astype(h0.dtype)


def s5_block(h, s_re, s_im, w_in, lam_re, lam_im, log_dt, b_re, b_im, c_re, c_im, d_skip, w_glu):
    f32 = jnp.float32
    bsz, T, _ = h.shape
    u = h @ w_in
    dt = jnp.exp(log_dt.astype(f32))[:, None]
    lr, li = lam_re.astype(f32), lam_im.astype(f32)
    mag = jnp.exp(lr * dt)
    ab_re, ab_im = mag * jnp.cos(li * dt), mag * jnp.sin(li * dt)
    zr, zi = ab_re - 1.0, ab_im
    den = lr * lr + li * li
    q_re = (zr * lr + zi * li) / den
    q_im = (zi * lr - zr * li) / den
    br, bi = b_re.astype(f32), b_im.astype(f32)
    bb_re = q_re[..., None] * br - q_im[..., None] * bi
    bb_im = q_re[..., None] * bi + q_im[..., None] * br
    cr, ci = c_re.astype(f32), c_im.astype(f32)
    L = min(T, CHUNK)
    nc = T // L
    uc = u.astype(f32).reshape(bsz, nc, L, G_C, GROUP_C).transpose(1, 0, 2, 3, 4)

    def comb(left, right):
        ar1, ai1, br1, bi1 = left
        ar2, ai2, br2, bi2 = right
        return (ar1 * ar2 - ai1 * ai2, ar1 * ai2 + ai1 * ar2,
                ar2 * br1 - ai2 * bi1 + br2, ar2 * bi1 + ai2 * br1 + bi2)

    def step(carry, u_blk):
        x_re, x_im = carry
        bu_re = jnp.einsum('blgh,gph->blgp', u_blk, bb_re)
        bu_im = jnp.einsum('blgh,gph->blgp', u_blk, bb_im)
        bu_re = bu_re.at[:, 0].add(ab_re * x_re - ab_im * x_im)
        bu_im = bu_im.at[:, 0].add(ab_re * x_im + ab_im * x_re)
        a_re = jnp.broadcast_to(ab_re, bu_re.shape)
        a_im = jnp.broadcast_to(ab_im, bu_im.shape)
        _, _, xs_re, xs_im = lax.associative_scan(comb, (a_re, a_im, bu_re, bu_im), axis=1)
        y = jnp.einsum('blgp,ghp->blgh', xs_re, cr) - jnp.einsum('blgp,ghp->blgh', xs_im, ci)
        return (xs_re[:, -1], xs_im[:, -1]), y

    (fr, fi), ys = lax.scan(step, (s_re.astype(f32), s_im.astype(f32)), uc)
    y = ys.transpose(1, 0, 2, 3, 4).reshape(bsz, T, D_S5) + d_skip.astype(f32) * u.astype(f32)
    g = jax.nn.gelu(y).astype(h.dtype)
    o_a, o_b = jnp.split(g @ w_glu, 2, axis=-1)
    return o_a * jax.nn.sigmoid(o_b), fr.astype(s_re.dtype), fi.astype(s_im.dtype)


def sqrelu_ffn(h, w1, w2):
    return jnp.square(jax.nn.relu(h @ w1)) @ w2


def trunk(x, conv0, h0, sre0, sim0, p, keep_chunk_v):
    vs, convs, hs, sres, sims = [], [], [], [], []
    for layer in range(DEPTH):
        j = layer // N_MIXERS
        kind = layer % N_MIXERS
        h = rmsnorm(x, p['norm_mix'][layer])
        if kind == 0:
            out, v = chunk_sgu(h, p['w_in_a'][j], p['sgu_g'][j], p['w_s'][j], p['b_s'][j], p['w_out_a'][j])
            if keep_chunk_v:
                vs.append(v)
        elif kind == 1:
            out, cb, hl = rglru_block(h, conv0[j], h0[j], p['w_in_b'][j], p['conv_w'][j], p['conv_b'][j],
                                      p['w_a'][j], p['b_a'][j], p['w_x'][j], p['b_x'][j], p['lam'][j],
                                      p['w_out_b'][j])
            convs.append(cb)
            hs.append(hl)
        else:
            out, fr, fi = s5_block(h, sre0[j], sim0[j], p['w_in_c'][j], p['lam_re'][j], p['lam_im'][j],
                                   p['log_dt'][j], p['b_re'][j], p['b_im'][j], p['c_re'][j], p['c_im'][j],
                                   p['d_skip'][j], p['w_glu'][j])
            sres.append(fr)
            sims.append(fi)
        x = x + out.astype(x.dtype)
        x = x + sqrelu_ffn(rmsnorm(x, p['norm_ffn'][layer]), p['w_ff1'][layer], p['w_ff2'][layer]).astype(x.dtype)
    v_out = jnp.stack(vs) if keep_chunk_v else None
    return rmsnorm(x, p['norm_f']), v_out, jnp.stack(convs), jnp.stack(hs), jnp.stack(sres), jnp.stack(sims)


def setup_inputs(seed: int = 0) -> dict:
    key = jax.random.key(seed)
    ks = list(jax.random.split(key, 40))
    f32 = jnp.float32

    def nrm(idx, shape, scale):
        return jax.random.normal(ks[idx], shape, f32) * scale

    u_lam = jax.random.uniform(ks[20], (N_B, D_RNN), f32, minval=0.9, maxval=0.999)
    s_lam = u_lam ** (1.0 / LRU_C)
    lam = jnp.log(s_lam) - jnp.log1p(-s_lam)
    lam_im = jnp.broadcast_to(jnp.pi * jnp.arange(P_C, dtype=f32), (N_C, G_C, P_C)) + nrm(25, (N_C, G_C, P_C), 0.01)
    log_dt = jax.random.uniform(ks[26], (N_C, G_C), f32, minval=math.log(1e-3), maxval=math.log(1e-1))
    return {
        'x_prompt': nrm(0, (BATCH, SEQ, D_MODEL), 1.0),
        'x_sample': nrm(1, (DEC_BATCH, DEC_SEQ, D_MODEL), 1.0),
        'state_rglru_conv': nrm(2, (N_B, DEC_BATCH, CONV_W - 1, D_RNN), 1.0),
        'state_rglru_h': nrm(3, (N_B, DEC_BATCH, D_RNN), 0.5),
        'state_s5_re': nrm(4, (N_C, DEC_BATCH, G_C, P_C), 0.5),
        'state_s5_im': nrm(5, (N_C, DEC_BATCH, G_C, P_C), 0.5),
        'norm_mix': 1.0 + nrm(6, (DEPTH, D_MODEL), 0.01),
        'norm_ffn': 1.0 + nrm(7, (DEPTH, D_MODEL), 0.01),
        'norm_f': 1.0 + nrm(8, (D_MODEL,), 0.01),
        'w_ff1': nrm(9, (DEPTH, D_MODEL, D_FF), D_MODEL ** -0.5),
        'w_ff2': nrm(10, (DEPTH, D_FF, D_MODEL), D_FF ** -0.5),
        'w_in_a': nrm(11, (N_A, D_MODEL, 2 * D_SGU), D_MODEL ** -0.5),
        'sgu_g': 1.0 + nrm(12, (N_A, D_SGU), 0.01),
        'w_s': nrm(13, (N_A, HEADS_A, CHUNK, CHUNK), CHUNK ** -0.5),
        'b_s': 1.0 + nrm(14, (N_A, HEADS_A, CHUNK), 0.01),
        'w_out_a': nrm(15, (N_A, D_SGU, D_MODEL), D_SGU ** -0.5),
        'w_in_b': nrm(16, (N_B, D_MODEL, 2 * D_RNN), D_MODEL ** -0.5),
        'conv_w': nrm(17, (N_B, CONV_W, D_RNN), CONV_W ** -0.5),
        'conv_b': nrm(18, (N_B, D_RNN), 0.01),
        'w_a': nrm(19, (N_B, HEADS_B, HD_B, HD_B), HD_B ** -0.5),
        'b_a': nrm(21, (N_B, D_RNN), 0.01),
        'w_x': nrm(22, (N_B, HEADS_B, HD_B, HD_B), HD_B ** -0.5),
        'b_x': nrm(23, (N_B, D_RNN), 0.01),
        'lam': lam,
        'w_out_b': nrm(24, (N_B, D_RNN, D_MODEL), D_RNN ** -0.5),
        'w_in_c': nrm(27, (N_C, D_MODEL, D_S5), D_MODEL ** -0.5),
        'lam_re': -0.5 + nrm(28, (N_C, G_C, P_C), 0.01),
        'lam_im': lam_im,
        'log_dt': log_dt,
        'b_re': nrm(29, (N_C, G_C, P_C, GROUP_C), (2.0 * GROUP_C) ** -0.5),
        'b_im': nrm(30, (N_C, G_C, P_C, GROUP_C), (2.0 * GROUP_C) ** -0.5),
        'c_re': nrm(31, (N_C, G_C, GROUP_C, P_C), (2.0 * P_C) ** -0.5),
        'c_im': nrm(32, (N_C, G_C, GROUP_C, P_C), (2.0 * P_C) ** -0.5),
        'd_skip': nrm(33, (N_C, D_S5), 0.5),
        'w_glu': nrm(34, (N_C, D_S5, 2 * D_MODEL), D_S5 ** -0.5),
    }


def reference(x_prompt, x_sample, state_rglru_conv, state_rglru_h, state_s5_re, state_s5_im,
              norm_mix, norm_ffn, norm_f, w_ff1, w_ff2,
              w_in_a, sgu_g, w_s, b_s, w_out_a,
              w_in_b, conv_w, conv_b, w_a, b_a, w_x, b_x, lam, w_out_b,
              w_in_c, lam_re, lam_im, log_dt, b_re, b_im, c_re, c_im, d_skip, w_glu):
    p = dict(norm_mix=norm_mix, norm_ffn=norm_ffn, norm_f=norm_f, w_ff1=w_ff1, w_ff2=w_ff2,
             w_in_a=w_in_a, sgu_g=sgu_g, w_s=w_s, b_s=b_s, w_out_a=w_out_a,
             w_in_b=w_in_b, conv_w=conv_w, conv_b=conv_b, w_a=w_a, b_a=b_a, w_x=w_x, b_x=b_x,
             lam=lam, w_out_b=w_out_b,
             w_in_c=w_in_c, lam_re=lam_re, lam_im=lam_im, log_dt=log_dt, b_re=b_re, b_im=b_im,
             c_re=c_re, c_im=c_im, d_skip=d_skip, w_glu=w_glu)
    bp = x_prompt.shape[0]
    dt_s = state_rglru_h.dtype
    conv0_p = jnp.zeros((N_B, bp, CONV_W - 1, D_RNN), dt_s)
    h0_p = jnp.zeros((N_B, bp, D_RNN), dt_s)
    sre0_p = jnp.zeros((N_C, bp, G_C, P_C), state_s5_re.dtype)
    sim0_p = jnp.zeros((N_C, bp, G_C, P_C), state_s5_im.dtype)
    y_prompt, _, conv_p, h_p, sre_p, sim_p = trunk(x_prompt, conv0_p, h0_p, sre0_p, sim0_p, p, False)
    y_sample, v_s, conv_s, h_s, sre_s, sim_s = trunk(x_sample, state_rglru_conv, state_rglru_h,
                                                    state_s5_re, state_s5_im, p, True)
    return (y_prompt, y_sample, v_s, conv_p, h_p, conv_s, h_s, sre_p, sim_p, sre_s, sim_s)
```

```python
import functools
import math

import jax
import jax.numpy as jnp
from jax import lax
from jax.experimental import pallas as pl
from jax.experimental.pallas import tpu as pltpu

F32 = jnp.float32
BF16 = jnp.bfloat16

D_MODEL = 1024
DEPTH = 4
N_MIXERS = 3
EPS = 1e-6
CHUNK = 128
HEADS_A = 8
HD_A = D_MODEL // HEADS_A
HEADS_B = 16
HD_B = D_MODEL // HEADS_B
CONV_W = 4
LRU_C = 8.0
GROUP_C = 16
G_C = D_MODEL // GROUP_C
P_C = 64
D_STATE = G_C * P_C
D_FF = 4 * D_MODEL

V7X_LANES = 128
V7X_SUBLANES = 8
V7X_MXU_DIM = 256
V7X_VMEM_BYTES = 64 * 1024 * 1024
VMEM_LIMIT = V7X_VMEM_BYTES - 8 * 1024 * 1024

N_BLK = D_MODEL // V7X_MXU_DIM
HEADS_PER_BLK = V7X_MXU_DIM // HD_B
GROUPS_PER_BLK = V7X_MXU_DIM // GROUP_C
STATE_PER_BLK = GROUPS_PER_BLK * P_C

FFN_ROWS = 512
FFN_COL_CHUNK = 1024


def _rms(x, g):
    return (x * lax.rsqrt(jnp.mean(x * x, axis=-1, keepdims=True) + EPS)) * g


def _dot(a, b):
    return jnp.dot(a, b, preferred_element_type=F32)


def _const_spec(shape):
    zeros = (0,) * len(shape)
    return pl.BlockSpec(shape, lambda i: zeros, pipeline_mode=pl.Buffered(1))


def _params():
    return pltpu.CompilerParams(dimension_semantics=("arbitrary",), vmem_limit_bytes=VMEM_LIMIT)


def _ffn_kernel(x_ref, g_ref, w1_ref, w2_ref, gf_ref, o_ref, *, final_norm):
    x = x_ref[...]
    h = _rms(x, g_ref[...]).astype(BF16)
    acc = None
    for j in range(D_FF // FFN_COL_CHUNK):
        cols = slice(j * FFN_COL_CHUNK, (j + 1) * FFN_COL_CHUNK)
        a = jnp.square(jnp.maximum(_dot(h, w1_ref[:, cols]), 0.0)).astype(BF16)
        part = _dot(a, w2_ref[cols, :])
        acc = part if acc is None else acc + part
    y = x + acc
    if final_norm:
        y = _rms(y, gf_ref[...])
    o_ref[...] = y


def _ffn(x, g, w1, w2, gf, final_norm):
    rows = x.shape[0]
    row_spec = pl.BlockSpec((FFN_ROWS, D_MODEL), lambda i: (i, 0))
    return pl.pallas_call(
        functools.partial(_ffn_kernel, final_norm=final_norm),
        grid=(rows // FFN_ROWS,),
        in_specs=[row_spec, _const_spec((1, D_MODEL)), _const_spec((D_MODEL, D_FF)),
                  _const_spec((D_FF, D_MODEL)), _const_spec((1, D_MODEL))],
        out_specs=row_spec,
        out_shape=jax.ShapeDtypeStruct(x.shape, F32),
        input_output_aliases={0: 0},
        compiler_params=_params(),
        name="ffn",
    )(x, g, w1, w2, gf)


def _sgu_front(x_ref, g_ref, win_ref, sg_ref):
    x = x_ref[...]
    h = _rms(x, g_ref[...]).astype(BF16)
    uv = jax.nn.gelu(_dot(h, win_ref[...]))
    u = uv[:, :D_MODEL]
    v = _rms(uv[:, D_MODEL:], sg_ref[...])
    return x, u, v


def _sgu_prompt_kernel(x_ref, g_ref, win_ref, sg_ref, k_ref, bias_ref, wout_ref, o_ref, y_s, *, nb):
    x, u, v = _sgu_front(x_ref, g_ref, win_ref, sg_ref)
    vb = v.astype(BF16)
    for g in range(HEADS_A):
        cols = slice(g * HD_A, (g + 1) * HD_A)
        mixed = _dot(k_ref[g], vb[:, cols]) + bias_ref[:, g:g + 1]
        y_s[:, cols] = (u[:, cols] * mixed).astype(BF16)
    o_ref[...] = x + _dot(y_s[...], wout_ref[...])


def _sgu_sample_kernel(x_ref, g_ref, win_ref, sg_ref, wl_ref, bl_ref, wout_ref, o_ref, v_ref, *, nb, steps):
    x, u, v = _sgu_front(x_ref, g_ref, win_ref, sg_ref)
    v_ref[...] = v
    mixed = []
    for t in range(steps):
        m = None
        for s in range(t + 1):
            term = wl_ref[t * steps + s:t * steps + s + 1, :] * v[s * nb:(s + 1) * nb, :]
            m = term if m is None else m + term
        mixed.append(m + bl_ref[t:t + 1, :])
    y = (u * jnp.concatenate(mixed, axis=0)).astype(BF16)
    o_ref[...] = x + _dot(y, wout_ref[...])


def _sgu_prompt(x, g, win, sg, kmat, bias, wout, nb, n_rows):
    tile = CHUNK * nb
    row_spec = pl.BlockSpec((tile, D_MODEL), lambda i: (i, 0))
    return pl.pallas_call(
        functools.partial(_sgu_prompt_kernel, nb=nb),
        grid=(n_rows // tile,),
        in_specs=[row_spec, _const_spec((1, D_MODEL)), _const_spec((D_MODEL, 2 * D_MODEL)),
                  _const_spec((1, D_MODEL)), _const_spec((HEADS_A, tile, tile)),
                  _const_spec((tile, HEADS_A)), _const_spec((D_MODEL, D_MODEL))],
        out_specs=row_spec,
        out_shape=jax.ShapeDtypeStruct(x.shape, F32),
        scratch_shapes=[pltpu.VMEM((tile, D_MODEL), BF16)],
        input_output_aliases={0: 0},
        compiler_params=_params(),
        name="sgu_prompt",
    )(x, g, win, sg, kmat, bias, wout)


def _sgu_sample(x, g, win, sg, wl, bl, wout, nb, steps, row0):
    tile = steps * nb
    blk0 = row0 // tile
    row_spec = pl.BlockSpec((tile, D_MODEL), lambda i: (blk0, 0))
    return pl.pallas_call(
        functools.partial(_sgu_sample_kernel, nb=nb, steps=steps),
        grid=(1,),
        in_specs=[row_spec, _const_spec((1, D_MODEL)), _const_spec((D_MODEL, 2 * D_MODEL)),
                  _const_spec((1, D_MODEL)), _const_spec((steps * steps, D_MODEL)),
                  _const_spec((steps, D_MODEL)), _const_spec((D_MODEL, D_MODEL))],
        out_specs=[row_spec, pl.BlockSpec((tile, D_MODEL), lambda i: (0, 0))],
        out_shape=[jax.ShapeDtypeStruct(x.shape, F32), jax.ShapeDtypeStruct((tile, D_MODEL), F32)],
        input_output_aliases={0: 0},
        compiler_params=_params(),
        name="sgu_sample",
    )(x, g, win, sg, wl, bl, wout)


def _rglru_kernel(x_ref, g_ref, win_ref, cw_ref, cb_ref, wa_ref, ba_ref, wx_ref, bx_ref, lam_ref, wout_ref,
                  conv0_ref, h0_ref, o_ref, conv_ref, h_ref, xext_s, a_s, b_s, *, nb, steps):
    rows = nb * steps
    halo = (CONV_W - 1) * nb

    @pl.when(pl.program_id(0) == 0)
    def _():
        xext_s[0:halo, :] = conv0_ref[...]
        h_ref[...] = h0_ref[...]

    x = x_ref[...]
    h = _rms(x, g_ref[...]).astype(BF16)
    gx = _dot(h, win_ref[...])
    gate = jax.nn.gelu(gx[:, :D_MODEL])
    xext_s[halo:halo + rows, :] = gx[:, D_MODEL:]

    conv = None
    for k in range(CONV_W):
        term = xext_s[k * nb:k * nb + rows, :] * cw_ref[k:k + 1, :]
        conv = term if conv is None else conv + term
    xc = cb_ref[...] + conv
    tail = xext_s[rows:rows + halo, :]
    conv_ref[...] = tail
    xext_s[0:halo, :] = tail

    xcb = xc.astype(BF16)
    ra, ia = [], []
    for j in range(N_BLK):
        cols = slice(j * V7X_MXU_DIM, (j + 1) * V7X_MXU_DIM)
        ra.append(_dot(xcb[:, cols], wa_ref[j]))
        ia.append(_dot(xcb[:, cols], wx_ref[j]))
    r = jax.nn.sigmoid(jnp.concatenate(ra, axis=1) + ba_ref[...])
    ig = jax.nn.sigmoid(jnp.concatenate(ia, axis=1) + bx_ref[...])
    log_a = (-LRU_C * r) * jax.nn.softplus(-lam_ref[...])
    a = jnp.exp(log_a)
    a_s[...] = a
    mult = jnp.sqrt(jnp.maximum(-(jnp.tanh(log_a) * (a * a + 1.0)), 0.0))
    b_s[...] = mult * (ig * xc)

    def step(t, hprev):
        sl = pl.ds(pl.multiple_of(t * nb, nb), nb)
        hn = a_s[sl, :] * hprev + b_s[sl, :]
        b_s[sl, :] = hn
        return hn

    h_ref[...] = lax.fori_loop(0, steps, step, h_ref[...], unroll=min(steps, 8))
    y = (b_s[...] * gate).astype(BF16)
    o_ref[...] = x + _dot(y, wout_ref[...])


def _rglru(x, p, conv0, h0, nb, steps, row0, n_rows):
    rows = nb * steps
    blk0 = row0 // rows
    halo = (CONV_W - 1) * nb
    row_spec = pl.BlockSpec((rows, D_MODEL), lambda i: (blk0 + i, 0))
    vec = _const_spec((1, D_MODEL))
    blk = _const_spec((N_BLK, V7X_MXU_DIM, V7X_MXU_DIM))
    return pl.pallas_call(
        functools.partial(_rglru_kernel, nb=nb, steps=steps),
        grid=(n_rows // rows,),
        in_specs=[row_spec, vec, _const_spec((D_MODEL, 2 * D_MODEL)), _const_spec((CONV_W, D_MODEL)), vec,
                  blk, vec, blk, vec, vec, _const_spec((D_MODEL, D_MODEL)),
                  _const_spec((halo, D_MODEL)), _const_spec((nb, D_MODEL))],
        out_specs=[row_spec, pl.BlockSpec((halo, D_MODEL), lambda i: (0, 0)),
                   pl.BlockSpec((nb, D_MODEL), lambda i: (0, 0))],
        out_shape=[jax.ShapeDtypeStruct(x.shape, F32), jax.ShapeDtypeStruct((halo, D_MODEL), F32),
                   jax.ShapeDtypeStruct((nb, D_MODEL), F32)],
        scratch_shapes=[pltpu.VMEM((rows + halo, D_MODEL), F32), pltpu.VMEM((rows, D_MODEL), F32),
                        pltpu.VMEM((rows, D_MODEL), F32)],
        input_output_aliases={0: 0},
        compiler_params=_params(),
        name="rglru",
    )(x, p["g"], p["win"], p["cw"], p["cb"], p["wa"], p["ba"], p["wx"], p["bx"], p["lam"], p["wout"], conv0, h0)


def _s5_disc_kernel(lre_ref, lim_ref, ldt_ref, bre_ref, bim_ref, are_ref, aim_ref, bbre_ref, bbim_ref):
    lr, li = lre_ref[...], lim_ref[...]
    dt = jnp.exp(ldt_ref[...])
    mag = jnp.exp(lr * dt)
    ab_re, ab_im = mag * jnp.cos(li * dt), mag * jnp.sin(li * dt)
    zr, zi = ab_re - 1.0, ab_im
    den = lr * lr + li * li
    q_re = (zr * lr + zi * li) / den
    q_im = (zi * lr - zr * li) / den
    are_ref[...] = ab_re
    aim_ref[...] = ab_im
    br, bi = bre_ref[...], bim_ref[...]
    bbre_ref[...] = q_re[:, None, :] * br - q_im[:, None, :] * bi
    bbim_ref[...] = q_re[:, None, :] * bi + q_im[:, None, :] * br


def _s5_disc(lam_re, lam_im, log_dt, b_re_t, b_im_t):
    gp = jax.ShapeDtypeStruct((G_C, P_C), F32)
    ghp = jax.ShapeDtypeStruct((G_C, GROUP_C, P_C), F32)
    return pl.pallas_call(_s5_disc_kernel, out_shape=[gp, gp, ghp, ghp], name="s5_disc")(
        lam_re, lam_im, log_dt.reshape(G_C, 1), b_re_t, b_im_t)


def _s5_kernel(x_ref, g_ref, win_ref, bre_ref, bim_ref, are_ref, aim_ref, cre_ref, cim_ref, dskip_ref, wglu_ref,
               sre0_ref, sim0_ref, o_ref, sre_ref, sim_ref, xre_s, xim_s, *, nb, steps, lane_chunk):
    @pl.when(pl.program_id(0) == 0)
    def _():
        sre_ref[...] = sre0_ref[...]
        sim_ref[...] = sim0_ref[...]

    x = x_ref[...]
    h = _rms(x, g_ref[...]).astype(BF16)
    u = _dot(h, win_ref[...])
    ub = u.astype(BF16)
    for j in range(N_BLK):
        ucols = slice(j * V7X_MXU_DIM, (j + 1) * V7X_MXU_DIM)
        scols = slice(j * STATE_PER_BLK, (j + 1) * STATE_PER_BLK)
        xre_s[:, scols] = _dot(ub[:, ucols], bre_ref[j])
        xim_s[:, scols] = _dot(ub[:, ucols], bim_ref[j])

    for c in range(D_STATE // lane_chunk):
        cols = slice(c * lane_chunk, (c + 1) * lane_chunk)
        a_re = jnp.broadcast_to(are_ref[:, cols], (nb, lane_chunk))
        a_im = jnp.broadcast_to(aim_ref[:, cols], (nb, lane_chunk))

        def step(t, carry, cols=cols, a_re=a_re, a_im=a_im):
            s_re, s_im = carry
            sl = pl.ds(pl.multiple_of(t * nb, nb), nb)
            n_re = (a_re * s_re - a_im * s_im) + xre_s[sl, cols]
            n_im = (a_re * s_im + a_im * s_re) + xim_s[sl, cols]
            xre_s[sl, cols] = n_re
            xim_s[sl, cols] = n_im
            return n_re, n_im

        f_re, f_im = lax.fori_loop(0, steps, step, (sre_ref[:, cols], sim_ref[:, cols]), unroll=min(steps, 4))
        sre_ref[:, cols] = f_re
        sim_ref[:, cols] = f_im

    ys = []
    for j in range(N_BLK):
        scols = slice(j * STATE_PER_BLK, (j + 1) * STATE_PER_BLK)
        ys.append(_dot(xre_s[:, scols].astype(BF16), cre_ref[j]) - _dot(xim_s[:, scols].astype(BF16), cim_ref[j]))
    y = jnp.concatenate(ys, axis=1) + dskip_ref[...] * u
    o = _dot(jax.nn.gelu(y).astype(BF16), wglu_ref[...])
    o_ref[...] = x + o[:, :D_MODEL] * jax.nn.sigmoid(o[:, D_MODEL:])


def _s5(x, p, sre0, sim0, nb, steps, lane_chunk, row0, n_rows):
    rows = nb * steps
    blk0 = row0 // rows
    row_spec = pl.BlockSpec((rows, D_MODEL), lambda i: (blk0 + i, 0))
    vec = _const_spec((1, D_MODEL))
    svec = _const_spec((1, D_STATE))
    state_spec = _const_spec((nb, D_STATE))
    bspec = _const_spec((N_BLK, V7X_MXU_DIM, STATE_PER_BLK))
    cspec = _const_spec((N_BLK, STATE_PER_BLK, V7X_MXU_DIM))
    state_out = pl.BlockSpec((nb, D_STATE), lambda i: (0, 0))
    return pl.pallas_call(
        functools.partial(_s5_kernel, nb=nb, steps=steps, lane_chunk=lane_chunk),
        grid=(n_rows // rows,),
        in_specs=[row_spec, vec, _const_spec((D_MODEL, D_MODEL)), bspec, bspec, svec, svec, cspec, cspec, vec,
                  _const_spec((D_MODEL, 2 * D_MODEL)), state_spec, state_spec],
        out_specs=[row_spec, state_out, state_out],
        out_shape=[jax.ShapeDtypeStruct(x.shape, F32), jax.ShapeDtypeStruct((nb, D_STATE), F32),
                   jax.ShapeDtypeStruct((nb, D_STATE), F32)],
        scratch_shapes=[pltpu.VMEM((rows, D_STATE), F32), pltpu.VMEM((rows, D_STATE), F32)],
        input_output_aliases={0: 0},
        compiler_params=_params(),
        name="s5",
    )(x, p["g"], p["win"], p["bre"], p["bim"], p["are"], p["aim"], p["cre"], p["cim"], p["dskip"], p["wglu"],
      sre0, sim0)


def _block_diag(w, n_per_blk):
    n, k_in, k_out = w.shape
    wb = w.reshape(n // n_per_blk, n_per_blk, k_in, k_out)
    eye = jnp.eye(n_per_blk, dtype=w.dtype)
    out = jnp.einsum("jgio,gk->jgiko", wb, eye)
    return out.reshape(n // n_per_blk, n_per_blk * k_in, n_per_blk * k_out)


def _to_rows(a):
    return jnp.swapaxes(a, 0, 1).reshape(a.shape[0] * a.shape[1], a.shape[2])


def _from_rows(a, nb):
    return jnp.swapaxes(a.reshape(a.shape[0] // nb, nb, a.shape[1]), 0, 1)


def kernel(x_prompt, x_sample, state_rglru_conv, state_rglru_h, state_s5_re, state_s5_im, norm_mix, norm_ffn, norm_f, w_ff1, w_ff2, w_in_a, sgu_g, w_s, b_s, w_out_a, w_in_b, conv_w, conv_b, w_a, b_a, w_x, b_x, lam, w_out_b, w_in_c, lam_re, lam_im, log_dt, b_re, b_im, c_re, c_im, d_skip, w_glu):
    bp, tp, _ = x_prompt.shape
    bs, ts, _ = x_sample.shape
    rows_p, rows_s = bp * tp, bs * ts
    assert tp % CHUNK == 0 and ts < CHUNK and rows_p % (CHUNK * bp) == 0 and rows_p % rows_s == 0
    assert rows_s % FFN_ROWS == 0 and rows_p % FFN_ROWS == 0

    x = jnp.concatenate([_to_rows(x_prompt), _to_rows(x_sample)], axis=0)
    row = lambda v: v.reshape(1, -1)

    s5_steps_p = 32
    outs_v, conv_p, h_p, conv_s, h_s, sre_p, sim_p, sre_s, sim_s = [], [], [], [], [], [], [], [], []
    for layer in range(DEPTH):
        j, kind = layer // N_MIXERS, layer % N_MIXERS
        g = row(norm_mix[layer])
        if kind == 0:
            win, wout, sg = w_in_a[j].astype(BF16), w_out_a[j].astype(BF16), row(sgu_g[j])
            w_tril = jnp.where(jnp.tril(jnp.ones((CHUNK, CHUNK), dtype=bool)), w_s[j], 0.0)
            kmat = jnp.einsum("gts,bc->gtbsc", w_tril, jnp.eye(bp, dtype=F32)).reshape(
                HEADS_A, CHUNK * bp, CHUNK * bp).astype(BF16)
            bias = jnp.repeat(b_s[j].T, bp, axis=0)
            x = _sgu_prompt(x, g, win, sg, kmat, bias, wout, bp, rows_p)
            wl = jnp.repeat(w_s[j][:, :ts, :ts].reshape(HEADS_A, ts * ts).T, HD_A, axis=1)
            bl = jnp.repeat(b_s[j][:, :ts].T, HD_A, axis=1)
            x, v = _sgu_sample(x, g, win, sg, wl, bl, wout, bs, ts, rows_p)
            outs_v.append(_from_rows(v, bs))
        elif kind == 1:
            p = dict(g=g, win=w_in_b[j].astype(BF16), cw=conv_w[j], cb=row(conv_b[j]),
                     wa=_block_diag(w_a[j], HEADS_PER_BLK).astype(BF16), ba=row(b_a[j]),
                     wx=_block_diag(w_x[j], HEADS_PER_BLK).astype(BF16), bx=row(b_x[j]),
                     lam=row(lam[j]), wout=w_out_b[j].astype(BF16))
            dt_s = state_rglru_h.dtype
            x, cp, hp = _rglru(x, p, jnp.zeros(((CONV_W - 1) * bp, D_MODEL), dt_s), jnp.zeros((bp, D_MODEL), dt_s),
                               bp, FFN_ROWS // bp, 0, rows_p)
            x, cs, hs = _rglru(x, p, _to_rows(state_rglru_conv[j]), state_rglru_h[j], bs, ts, rows_p, rows_s)
            conv_p.append(_from_rows(cp, bp)); h_p.append(hp)
            conv_s.append(_from_rows(cs, bs)); h_s.append(hs)
        else:
            are, aim, bbre, bbim = _s5_disc(lam_re[j], lam_im[j], log_dt[j],
                                            jnp.swapaxes(b_re[j], 1, 2), jnp.swapaxes(b_im[j], 1, 2))
            p = dict(g=g, win=w_in_c[j].astype(BF16),
                     bre=_block_diag(bbre, GROUPS_PER_BLK).astype(BF16),
                     bim=_block_diag(bbim, GROUPS_PER_BLK).astype(BF16),
                     are=are.reshape(1, D_STATE), aim=aim.reshape(1, D_STATE),
                     cre=_block_diag(jnp.swapaxes(c_re[j], 1, 2), GROUPS_PER_BLK).astype(BF16),
                     cim=_block_diag(jnp.swapaxes(c_im[j], 1, 2), GROUPS_PER_BLK).astype(BF16),
                     dskip=row(d_skip[j]), wglu=w_glu[j].astype(BF16))
            zs = jnp.zeros((bp, D_STATE), state_s5_re.dtype)
            x, rp, ip = _s5(x, p, zs, zs, bp, s5_steps_p, 4 * V7X_LANES, 0, rows_p)
            x, rs, is_ = _s5(x, p, state_s5_re[j].reshape(bs, D_STATE), state_s5_im[j].reshape(bs, D_STATE),
                             bs, ts, V7X_LANES, rows_p, rows_s)
            sre_p.append(rp.reshape(bp, G_C, P_C)); sim_p.append(ip.reshape(bp, G_C, P_C))
            sre_s.append(rs.reshape(bs, G_C, P_C)); sim_s.append(is_.reshape(bs, G_C, P_C))
        x = _ffn(x, row(norm_ffn[layer]), w_ff1[layer].astype(BF16), w_ff2[layer].astype(BF16), row(norm_f),
                 layer == DEPTH - 1)

    y_prompt = _from_rows(x[:rows_p], bp)
    y_sample = _from_rows(x[rows_p:], bs)
    return (y_prompt, y_sample, jnp.stack(outs_v), jnp.stack(conv_p), jnp.stack(h_p), jnp.stack(conv_s),
            jnp.stack(h_s), jnp.stack(sre_p), jnp.stack(sim_p), jnp.stack(sre_s), jnp.stack(sim_s))
```

```python
import functools

import jax
import jax.numpy as jnp
from jax import lax
from jax.experimental import pallas as pl
from jax.experimental.pallas import tpu as pltpu

F32 = jnp.float32
BF16 = jnp.bfloat16

D_MODEL = 1024
DEPTH = 4
N_MIXERS = 3
EPS = 1e-6
CHUNK = 128
HEADS_A = 8
HD_A = D_MODEL // HEADS_A
HEADS_B = 16
HD_B = D_MODEL // HEADS_B
CONV_W = 4
LRU_C = 8.0
GROUP_C = 16
G_C = D_MODEL // GROUP_C
P_C = 64
D_STATE = G_C * P_C
D_FF = 4 * D_MODEL

V7X_LANES = 128
V7X_MXU_DIM = 256
V7X_VMEM_BYTES = 64 * 1024 * 1024
VMEM_LIMIT = V7X_VMEM_BYTES - 8 * 1024 * 1024

N_BLK = D_MODEL // V7X_MXU_DIM
HEADS_PER_BLK = V7X_MXU_DIM // HD_B
GROUPS_PER_BLK = V7X_MXU_DIM // GROUP_C
STATE_PER_BLK = GROUPS_PER_BLK * P_C

FFN_ROWS = 512
FFN_COL_CHUNK = 1024
S5_PROMPT_STEPS = 32


def _rms(x, g):
    return (x * lax.rsqrt(jnp.mean(x * x, axis=-1, keepdims=True) + EPS)) * g


def _dot(a, b):
    return jnp.dot(a, b, preferred_element_type=F32)


def _const_spec(shape):
    zeros = (0,) * len(shape)
    return pl.BlockSpec(shape, lambda i: zeros, pipeline_mode=pl.Buffered(1))


def _layer_spec(shape, layer):
    idx = (layer,) + (0,) * len(shape)
    return pl.BlockSpec((None,) + tuple(shape), lambda i: idx, pipeline_mode=pl.Buffered(1))


def _params():
    return pltpu.CompilerParams(dimension_semantics=("arbitrary",), vmem_limit_bytes=VMEM_LIMIT)


def _ffn_kernel(x_ref, g_ref, w1_ref, w2_ref, gf_ref, o_ref, *, final_norm):
    x = x_ref[...]
    h = _rms(x, g_ref[...]).astype(BF16)
    acc = None
    for j in range(D_FF // FFN_COL_CHUNK):
        cols = slice(j * FFN_COL_CHUNK, (j + 1) * FFN_COL_CHUNK)
        a = jnp.square(jnp.maximum(_dot(h, w1_ref[:, cols]), 0.0)).astype(BF16)
        part = _dot(a, w2_ref[cols, :])
        acc = part if acc is None else acc + part
    y = x + acc
    if final_norm:
        y = _rms(y, gf_ref[...])
    o_ref[...] = y


def _ffn(x, g, w1, w2, gf, layer, final_norm, row0, n_rows, in_place):
    blk0 = row0 // FFN_ROWS
    in_spec = pl.BlockSpec((FFN_ROWS, D_MODEL), lambda i: (blk0 + i, 0))
    out_spec = in_spec if in_place else pl.BlockSpec((FFN_ROWS, D_MODEL), lambda i: (i, 0))
    out_rows = x.shape[0] if in_place else n_rows
    return pl.pallas_call(
        functools.partial(_ffn_kernel, final_norm=final_norm),
        grid=(n_rows // FFN_ROWS,),
        in_specs=[in_spec, _layer_spec((1, D_MODEL), layer), _layer_spec((D_MODEL, D_FF), layer),
                  _layer_spec((D_FF, D_MODEL), layer), _const_spec((1, D_MODEL))],
        out_specs=out_spec,
        out_shape=jax.ShapeDtypeStruct((out_rows, D_MODEL), F32),
        input_output_aliases={0: 0} if in_place else {},
        compiler_params=_params(),
        name="ffn",
    )(x, g, w1, w2, gf)


def _sgu_front(x, g_ref, win_ref, sg_ref):
    h = _rms(x, g_ref[...]).astype(BF16)
    uv = jax.nn.gelu(_dot(h, win_ref[...]))
    return uv[:, :D_MODEL], _rms(uv[:, D_MODEL:], sg_ref[...])


def _rows_to_batch_major(x, nb):
    steps = x.shape[0] // nb
    return jnp.swapaxes(x.reshape(steps, nb, x.shape[1]), 0, 1).reshape(x.shape)


def _rows_to_time_major(x, nb):
    steps = x.shape[0] // nb
    return jnp.swapaxes(x.reshape(nb, steps, x.shape[1]), 0, 1).reshape(x.shape)


def _sgu_prompt_kernel(x_ref, g_ref, win_ref, sg_ref, w_ref, bias_ref, wout_ref, o_ref, y_s, *, nb, in_tm, out_tm):
    rows = nb * CHUNK
    if in_tm:
        x = _rows_to_batch_major(x_ref[...], nb)
    else:
        x = x_ref[...].reshape(rows, D_MODEL)
    u, v = _sgu_front(x, g_ref, win_ref, sg_ref)
    vb = v.astype(BF16)
    for b in range(nb):
        rs = slice(b * CHUNK, (b + 1) * CHUNK)
        for g in range(HEADS_A):
            cs = slice(g * HD_A, (g + 1) * HD_A)
            mixed = _dot(w_ref[g], vb[rs, cs]) + bias_ref[:, cs]
            y_s[rs, cs] = (u[rs, cs] * mixed).astype(BF16)
    o = x + _dot(y_s[...], wout_ref[...])
    if out_tm:
        o_ref[...] = _rows_to_time_major(o, nb)
    else:
        o_ref[...] = o.reshape(nb, CHUNK, D_MODEL)


def _sgu_sample_kernel(x_ref, g_ref, win_ref, sg_ref, wl_ref, bl_ref, wout_ref, *rest, nb, steps, aliased):
    o_ref, v_ref = rest[1:] if aliased else rest
    x = x_ref[...]
    u, v = _sgu_front(x, g_ref, win_ref, sg_ref)
    v_ref[...] = v
    mixed = []
    for t in range(steps):
        m = None
        for s in range(t + 1):
            term = wl_ref[t * steps + s:t * steps + s + 1, :] * v[s * nb:(s + 1) * nb, :]
            m = term if m is None else m + term
        mixed.append(m + bl_ref[t:t + 1, :])
    y = (u * jnp.concatenate(mixed, axis=0)).astype(BF16)
    o_ref[...] = x + _dot(y, wout_ref[...])


def _sgu_prompt(x, p, nb, n_steps, in_tm, out_tm, total_rows):
    rows = nb * CHUNK
    tm_spec = pl.BlockSpec((rows, D_MODEL), lambda i: (i, 0))
    bm_spec = pl.BlockSpec((nb, CHUNK, D_MODEL), lambda i: (0, i, 0))
    out_shape = (jax.ShapeDtypeStruct((total_rows, D_MODEL), F32) if out_tm
                 else jax.ShapeDtypeStruct((nb, n_steps, D_MODEL), F32))
    return pl.pallas_call(
        functools.partial(_sgu_prompt_kernel, nb=nb, in_tm=in_tm, out_tm=out_tm),
        grid=(n_steps // CHUNK,),
        in_specs=[tm_spec if in_tm else bm_spec, _const_spec((1, D_MODEL)), _const_spec((D_MODEL, 2 * D_MODEL)),
                  _const_spec((1, D_MODEL)), _const_spec((HEADS_A, CHUNK, CHUNK)),
                  _const_spec((CHUNK, D_MODEL)), _const_spec((D_MODEL, D_MODEL))],
        out_specs=tm_spec if out_tm else bm_spec,
        out_shape=out_shape,
        scratch_shapes=[pltpu.VMEM((rows, D_MODEL), BF16)],
        compiler_params=_params(),
        name="sgu_prompt",
    )(x, p["g"], p["win"], p["sg"], p["w_tril"], p["bias"], p["wout"])


def _sgu_sample(x_in, in_row0, dest, dest_row0, p, nb, steps):
    tile = steps * nb
    in_spec = pl.BlockSpec((tile, D_MODEL), lambda i: (in_row0 // tile, 0))
    specs = [in_spec, _const_spec((1, D_MODEL)), _const_spec((D_MODEL, 2 * D_MODEL)),
             _const_spec((1, D_MODEL)), _const_spec((steps * steps, D_MODEL)),
             _const_spec((steps, D_MODEL)), _const_spec((D_MODEL, D_MODEL))]
    args = [x_in, p["g"], p["win"], p["sg"], p["wl"], p["bl"], p["wout"]]
    aliased = dest is not None
    if aliased:
        specs.append(pl.BlockSpec(memory_space=pl.ANY))
        args.append(dest)
        out_spec = pl.BlockSpec((tile, D_MODEL), lambda i: (dest_row0 // tile, 0))
        out_shape = jax.ShapeDtypeStruct(dest.shape, F32)
    else:
        out_spec = pl.BlockSpec((tile, D_MODEL), lambda i: (0, 0))
        out_shape = jax.ShapeDtypeStruct((tile, D_MODEL), F32)
    return pl.pallas_call(
        functools.partial(_sgu_sample_kernel, nb=nb, steps=steps, aliased=aliased),
        grid=(1,),
        in_specs=specs,
        out_specs=[out_spec, pl.BlockSpec((tile, D_MODEL), lambda i: (0, 0))],
        out_shape=[out_shape, jax.ShapeDtypeStruct((tile, D_MODEL), F32)],
        input_output_aliases={len(args) - 1: 0} if aliased else {},
        compiler_params=_params(),
        name="sgu_sample",
    )(*args)


def _rglru_kernel(x_ref, g_ref, win_ref, cw_ref, cb_ref, wa_ref, ba_ref, wx_ref, bx_ref, lam_ref, wout_ref,
                  conv0_ref, h0_ref, o_ref, conv_ref, h_ref, xext_s, a_s, b_s, *, nb, steps):
    rows = nb * steps
    halo = (CONV_W - 1) * nb

    @pl.when(pl.program_id(0) == 0)
    def _():
        xext_s[0:halo, :] = conv0_ref[...]
        h_ref[...] = h0_ref[...]

    x = x_ref[...]
    h = _rms(x, g_ref[...]).astype(BF16)
    gx = _dot(h, win_ref[...])
    gate = jax.nn.gelu(gx[:, :D_MODEL])
    xext_s[halo:halo + rows, :] = gx[:, D_MODEL:]

    conv = None
    for k in range(CONV_W):
        term = xext_s[k * nb:k * nb + rows, :] * cw_ref[k:k + 1, :]
        conv = term if conv is None else conv + term
    xc = cb_ref[...] + conv
    tail = xext_s[rows:rows + halo, :]
    conv_ref[...] = tail
    xext_s[0:halo, :] = tail

    xcb = xc.astype(BF16)
    ra, ia = [], []
    for j in range(N_BLK):
        cols = slice(j * V7X_MXU_DIM, (j + 1) * V7X_MXU_DIM)
        ra.append(_dot(xcb[:, cols], wa_ref[j]))
        ia.append(_dot(xcb[:, cols], wx_ref[j]))
    r = jax.nn.sigmoid(jnp.concatenate(ra, axis=1) + ba_ref[...])
    ig = jax.nn.sigmoid(jnp.concatenate(ia, axis=1) + bx_ref[...])
    log_a = (-LRU_C * r) * jax.nn.softplus(-lam_ref[...])
    a = jnp.exp(log_a)
    a_s[...] = a
    mult = jnp.sqrt(jnp.maximum(-(jnp.tanh(log_a) * (a * a + 1.0)), 0.0))
    b_s[...] = mult * (ig * xc)

    def step(t, hprev):
        sl = pl.ds(pl.multiple_of(t * nb, nb), nb)
        hn = a_s[sl, :] * hprev + b_s[sl, :]
        b_s[sl, :] = hn
        return hn

    h_ref[...] = lax.fori_loop(0, steps, step, h_ref[...], unroll=min(steps, 8))
    y = (b_s[...] * gate).astype(BF16)
    o_ref[...] = x + _dot(y, wout_ref[...])


def _rglru(x, p, conv0, h0, nb, steps, row0, n_rows):
    rows = nb * steps
    blk0 = row0 // rows
    halo = (CONV_W - 1) * nb
    row_spec = pl.BlockSpec((rows, D_MODEL), lambda i: (blk0 + i, 0))
    vec = _const_spec((1, D_MODEL))
    blk = _const_spec((N_BLK, V7X_MXU_DIM, V7X_MXU_DIM))
    return pl.pallas_call(
        functools.partial(_rglru_kernel, nb=nb, steps=steps),
        grid=(n_rows // rows,),
        in_specs=[row_spec, vec, _const_spec((D_MODEL, 2 * D_MODEL)), _const_spec((CONV_W, D_MODEL)), vec,
                  blk, vec, blk, vec, vec, _const_spec((D_MODEL, D_MODEL)),
                  _const_spec((halo, D_MODEL)), _const_spec((nb, D_MODEL))],
        out_specs=[row_spec, pl.BlockSpec((halo, D_MODEL), lambda i: (0, 0)),
                   pl.BlockSpec((nb, D_MODEL), lambda i: (0, 0))],
        out_shape=[jax.ShapeDtypeStruct(x.shape, F32), jax.ShapeDtypeStruct((halo, D_MODEL), F32),
                   jax.ShapeDtypeStruct((nb, D_MODEL), F32)],
        scratch_shapes=[pltpu.VMEM((rows + halo, D_MODEL), F32), pltpu.VMEM((rows, D_MODEL), F32),
                        pltpu.VMEM((rows, D_MODEL), F32)],
        input_output_aliases={0: 0},
        compiler_params=_params(),
        name="rglru",
    )(x, p["g"], p["win"], p["cw"], p["cb"], p["wa"], p["ba"], p["wx"], p["bx"], p["lam"], p["wout"], conv0, h0)


def _s5_disc_kernel(lre_ref, lim_ref, ldt_ref, bre_ref, bim_ref, are_ref, aim_ref, bbre_ref, bbim_ref):
    lr, li = lre_ref[...], lim_ref[...]
    dt = jnp.exp(ldt_ref[...])
    mag = jnp.exp(lr * dt)
    ab_re, ab_im = mag * jnp.cos(li * dt), mag * jnp.sin(li * dt)
    zr, zi = ab_re - 1.0, ab_im
    den = lr * lr + li * li
    q_re = (zr * lr + zi * li) / den
    q_im = (zi * lr - zr * li) / den
    are_ref[...] = ab_re
    aim_ref[...] = ab_im
    br, bi = bre_ref[...], bim_ref[...]
    bbre_ref[...] = q_re[:, None, :] * br - q_im[:, None, :] * bi
    bbim_ref[...] = q_re[:, None, :] * bi + q_im[:, None, :] * br


def _s5_disc(lam_re, lam_im, log_dt, b_re_t, b_im_t):
    gp = jax.ShapeDtypeStruct((G_C, P_C), F32)
    ghp = jax.ShapeDtypeStruct((G_C, GROUP_C, P_C), F32)
    return pl.pallas_call(_s5_disc_kernel, out_shape=[gp, gp, ghp, ghp], name="s5_disc")(
        lam_re, lam_im, log_dt.reshape(G_C, 1), b_re_t, b_im_t)


def _s5_kernel(x_ref, g_ref, win_ref, bre_ref, bim_ref, are_ref, aim_ref, cre_ref, cim_ref, dskip_ref, wglu_ref,
               sre0_ref, sim0_ref, o_ref, sre_ref, sim_ref, xre_s, xim_s, *, nb, steps, lane_chunk):
    @pl.when(pl.program_id(0) == 0)
    def _():
        sre_ref[...] = sre0_ref[...]
        sim_ref[...] = sim0_ref[...]

    x = x_ref[...]
    h = _rms(x, g_ref[...]).astype(BF16)
    u = _dot(h, win_ref[...])
    ub = u.astype(BF16)
    for j in range(N_BLK):
        ucols = slice(j * V7X_MXU_DIM, (j + 1) * V7X_MXU_DIM)
        scols = slice(j * STATE_PER_BLK, (j + 1) * STATE_PER_BLK)
        xre_s[:, scols] = _dot(ub[:, ucols], bre_ref[j])
        xim_s[:, scols] = _dot(ub[:, ucols], bim_ref[j])

    for c in range(D_STATE // lane_chunk):
        cols = slice(c * lane_chunk, (c + 1) * lane_chunk)
        a_re = jnp.broadcast_to(are_ref[:, cols], (nb, lane_chunk))
        a_im = jnp.broadcast_to(aim_ref[:, cols], (nb, lane_chunk))

        def step(t, carry, cols=cols, a_re=a_re, a_im=a_im):
            s_re, s_im = carry
            sl = pl.ds(pl.multiple_of(t * nb, nb), nb)
            n_re = (a_re * s_re - a_im * s_im) + xre_s[sl, cols]
            n_im = (a_re * s_im + a_im * s_re) + xim_s[sl, cols]
            xre_s[sl, cols] = n_re
            xim_s[sl, cols] = n_im
            return n_re, n_im

        f_re, f_im = lax.fori_loop(0, steps, step, (sre_ref[:, cols], sim_ref[:, cols]), unroll=min(steps, 4))
        sre_ref[:, cols] = f_re
        sim_ref[:, cols] = f_im

    ys = []
    for j in range(N_BLK):
        scols = slice(j * STATE_PER_BLK, (j + 1) * STATE_PER_BLK)
        ys.append(_dot(xre_s[:, scols].astype(BF16), cre_ref[j]) - _dot(xim_s[:, scols].astype(BF16), cim_ref[j]))
    y = jnp.concatenate(ys, axis=1) + dskip_ref[...] * u
    o = _dot(jax.nn.gelu(y).astype(BF16), wglu_ref[...])
    o_ref[...] = x + o[:, :D_MODEL] * jax.nn.sigmoid(o[:, D_MODEL:])


def _s5(x, p, sre0, sim0, nb, steps, lane_chunk, row0, n_rows):
    rows = nb * steps
    blk0 = row0 // rows
    row_spec = pl.BlockSpec((rows, D_MODEL), lambda i: (blk0 + i, 0))
    vec = _const_spec((1, D_MODEL))
    svec = _const_spec((1, D_STATE))
    state_spec = _const_spec((nb, D_STATE))
    bspec = _const_spec((N_BLK, V7X_MXU_DIM, STATE_PER_BLK))
    cspec = _const_spec((N_BLK, STATE_PER_BLK, V7X_MXU_DIM))
    state_out = pl.BlockSpec((nb, D_STATE), lambda i: (0, 0))
    return pl.pallas_call(
        functools.partial(_s5_kernel, nb=nb, steps=steps, lane_chunk=lane_chunk),
        grid=(n_rows // rows,),
        in_specs=[row_spec, vec, _const_spec((D_MODEL, D_MODEL)), bspec, bspec, svec, svec, cspec, cspec, vec,
                  _const_spec((D_MODEL, 2 * D_MODEL)), state_spec, state_spec],
        out_specs=[row_spec, state_out, state_out],
        out_shape=[jax.ShapeDtypeStruct(x.shape, F32), jax.ShapeDtypeStruct((nb, D_STATE), F32),
                   jax.ShapeDtypeStruct((nb, D_STATE), F32)],
        scratch_shapes=[pltpu.VMEM((rows, D_STATE), F32), pltpu.VMEM((rows, D_STATE), F32)],
        input_output_aliases={0: 0},
        compiler_params=_params(),
        name="s5",
    )(x, p["g"], p["win"], p["bre"], p["bim"], p["are"], p["aim"], p["cre"], p["cim"], p["dskip"], p["wglu"],
      sre0, sim0)


def _block_diag(w, n_per_blk):
    n, k_in, k_out = w.shape
    wb = w.reshape(n // n_per_blk, n_per_blk, k_in, k_out)
    eye = jnp.eye(n_per_blk, dtype=w.dtype)
    out = jnp.einsum("jgio,gk->jgiko", wb, eye)
    return out.reshape(n // n_per_blk, n_per_blk * k_in, n_per_blk * k_out)


def _to_rows(a):
    return jnp.swapaxes(a, 0, 1).reshape(a.shape[0] * a.shape[1], a.shape[2])


def _from_rows(a, nb):
    return jnp.swapaxes(a.reshape(a.shape[0] // nb, nb, a.shape[1]), 0, 1)


def kernel(x_prompt, x_sample, state_rglru_conv, state_rglru_h, state_s5_re, state_s5_im, norm_mix, norm_ffn, norm_f, w_ff1, w_ff2, w_in_a, sgu_g, w_s, b_s, w_out_a, w_in_b, conv_w, conv_b, w_a, b_a, w_x, b_x, lam, w_out_b, w_in_c, lam_re, lam_im, log_dt, b_re, b_im, c_re, c_im, d_skip, w_glu):
    bp, tp, _ = x_prompt.shape
    bs, ts, _ = x_sample.shape
    rows_p, rows_s = bp * tp, bs * ts
    rows_all = rows_p + rows_s
    assert DEPTH % N_MIXERS == 1 and DEPTH > 1, "first and last layers must be SGU layers"
    assert tp % CHUNK == 0 and ts < CHUNK and rows_p % rows_s == 0
    assert rows_s % FFN_ROWS == 0 and rows_p % FFN_ROWS == 0

    row = lambda v: v.reshape(1, -1)
    norm_ffn3 = norm_ffn.reshape(DEPTH, 1, D_MODEL)
    w1, w2, gf = w_ff1.astype(BF16), w_ff2.astype(BF16), row(norm_f)
    xs_rows = _to_rows(x_sample)

    x = None
    outs_v, conv_p, h_p, conv_s, h_s, sre_p, sim_p, sre_s, sim_s = [], [], [], [], [], [], [], [], []
    for layer in range(DEPTH):
        j, kind = layer // N_MIXERS, layer % N_MIXERS
        first, last = layer == 0, layer == DEPTH - 1
        g = row(norm_mix[layer])
        if kind == 0:
            p = dict(g=g, win=w_in_a[j].astype(BF16), wout=w_out_a[j].astype(BF16), sg=row(sgu_g[j]),
                     w_tril=jnp.where(jnp.tril(jnp.ones((CHUNK, CHUNK), dtype=bool)), w_s[j], 0.0).astype(BF16),
                     bias=jnp.repeat(b_s[j].T, HD_A, axis=1),
                     wl=jnp.repeat(w_s[j][:, :ts, :ts].reshape(HEADS_A, ts * ts).T, HD_A, axis=1),
                     bl=jnp.repeat(b_s[j][:, :ts].T, HD_A, axis=1))
            if first:
                x = _sgu_prompt(x_prompt, p, bp, tp, False, True, rows_all)
                x, v = _sgu_sample(xs_rows, 0, x, rows_p, p, bs, ts)
            elif last:
                x_s, v = _sgu_sample(x, rows_p, None, 0, p, bs, ts)
                x_p = _sgu_prompt(x, p, bp, tp, True, False, rows_all).reshape(rows_p, D_MODEL)
            else:
                raise NotImplementedError("interior SGU layers")
            outs_v.append(_from_rows(v, bs))
        elif kind == 1:
            p = dict(g=g, win=w_in_b[j].astype(BF16), cw=conv_w[j], cb=row(conv_b[j]),
                     wa=_block_diag(w_a[j], HEADS_PER_BLK).astype(BF16), ba=row(b_a[j]),
                     wx=_block_diag(w_x[j], HEADS_PER_BLK).astype(BF16), bx=row(b_x[j]),
                     lam=row(lam[j]), wout=w_out_b[j].astype(BF16))
            dt_s = state_rglru_h.dtype
            x, cp, hp = _rglru(x, p, jnp.zeros(((CONV_W - 1) * bp, D_MODEL), dt_s), jnp.zeros((bp, D_MODEL), dt_s),
                               bp, FFN_ROWS // bp, 0, rows_p)
            x, cs, hs = _rglru(x, p, _to_rows(state_rglru_conv[j]), state_rglru_h[j], bs, ts, rows_p, rows_s)
            conv_p.append(_from_rows(cp, bp)); h_p.append(hp)
            conv_s.append(_from_rows(cs, bs)); h_s.append(hs)
        else:
            are, aim, bbre, bbim = _s5_disc(lam_re[j], lam_im[j], log_dt[j],
                                            jnp.swapaxes(b_re[j], 1, 2), jnp.swapaxes(b_im[j], 1, 2))
            p = dict(g=g, win=w_in_c[j].astype(BF16),
                     bre=_block_diag(bbre, GROUPS_PER_BLK).astype(BF16),
                     bim=_block_diag(bbim, GROUPS_PER_BLK).astype(BF16),
                     are=are.reshape(1, D_STATE), aim=aim.reshape(1, D_STATE),
                     cre=_block_diag(jnp.swapaxes(c_re[j], 1, 2), GROUPS_PER_BLK).astype(BF16),
                     cim=_block_diag(jnp.swapaxes(c_im[j], 1, 2), GROUPS_PER_BLK).astype(BF16),
                     dskip=row(d_skip[j]), wglu=w_glu[j].astype(BF16))
            zs = jnp.zeros((bp, D_STATE), state_s5_re.dtype)
            x, rp, ip = _s5(x, p, zs, zs, bp, S5_PROMPT_STEPS, 4 * V7X_LANES, 0, rows_p)
            x, rs, is_ = _s5(x, p, state_s5_re[j].reshape(bs, D_STATE), state_s5_im[j].reshape(bs, D_STATE),
                             bs, ts, V7X_LANES, rows_p, rows_s)
            sre_p.append(rp.reshape(bp, G_C, P_C)); sim_p.append(ip.reshape(bp, G_C, P_C))
            sre_s.append(rs.reshape(bs, G_C, P_C)); sim_s.append(is_.reshape(bs, G_C, P_C))
        if last:
            y_p = _ffn(x_p, norm_ffn3, w1, w2, gf, layer, True, 0, rows_p, False)
            y_s = _ffn(x_s, norm_ffn3, w1, w2, gf, layer, True, 0, rows_s, False)
        else:
            x = _ffn(x, norm_ffn3, w1, w2, gf, layer, False, 0, rows_all, True)

    y_prompt = y_p.reshape(bp, tp, D_MODEL)
    y_sample = _from_rows(y_s, bs)
    return (y_prompt, y_sample, jnp.stack(outs_v), jnp.stack(conv_p), jnp.stack(h_p), jnp.stack(conv_s),
            jnp.stack(h_s), jnp.stack(sre_p), jnp.stack(sim_p), jnp.stack(sre_s), jnp.stack(sim_s))
```

```python
import functools

import jax
import jax.numpy as jnp
from jax import lax
from jax.experimental import pallas as pl
from jax.experimental.pallas import tpu as pltpu

F32 = jnp.float32
BF16 = jnp.bfloat16

D_MODEL = 1024
DEPTH = 4
N_MIXERS = 3
EPS = 1e-6
CHUNK = 128
HEADS_A = 8
HD_A = D_MODEL // HEADS_A
HEADS_B = 16
HD_B = D_MODEL // HEADS_B
CONV_W = 4
LRU_C = 8.0
GROUP_C = 16
G_C = D_MODEL // GROUP_C
P_C = 64
D_STATE = G_C * P_C
D_FF = 4 * D_MODEL

V7X_LANES = 128
V7X_MXU_DIM = 256
V7X_VMEM_BYTES = 64 * 1024 * 1024
VMEM_LIMIT = V7X_VMEM_BYTES - 8 * 1024 * 1024

N_BLK = D_MODEL // V7X_MXU_DIM
HEADS_PER_BLK = V7X_MXU_DIM // HD_B
GROUPS_PER_BLK = V7X_MXU_DIM // GROUP_C
STATE_PER_BLK = GROUPS_PER_BLK * P_C

FFN_ROWS = 512
FFN_COL_CHUNK = 1024
S5_PROMPT_STEPS = 32
RGLRU_PROMPT_STEPS = 64
RGLRU_PROMPT_SUB_STEPS = 32


def _rms(x, g):
    return (x * lax.rsqrt(jnp.mean(x * x, axis=-1, keepdims=True) + EPS)) * g


def _dot(a, b):
    return jnp.dot(a, b, preferred_element_type=F32)


def _const_spec(shape):
    zeros = (0,) * len(shape)
    return pl.BlockSpec(shape, lambda i: zeros, pipeline_mode=pl.Buffered(1))


def _layer_spec(shape, layer):
    idx = (layer,) + (0,) * len(shape)
    return pl.BlockSpec((None,) + tuple(shape), lambda i: idx, pipeline_mode=pl.Buffered(1))


def _params():
    return pltpu.CompilerParams(dimension_semantics=("arbitrary",), vmem_limit_bytes=VMEM_LIMIT)


def _ffn_kernel(x_ref, g_ref, w1_ref, w2_ref, gf_ref, o_ref, *, final_norm):
    x = x_ref[...]
    h = _rms(x, g_ref[...]).astype(BF16)
    acc = None
    for j in range(D_FF // FFN_COL_CHUNK):
        cols = slice(j * FFN_COL_CHUNK, (j + 1) * FFN_COL_CHUNK)
        a = jnp.square(jnp.maximum(_dot(h, w1_ref[:, cols]), 0.0)).astype(BF16)
        part = _dot(a, w2_ref[cols, :])
        acc = part if acc is None else acc + part
    y = x + acc
    if final_norm:
        y = _rms(y, gf_ref[...])
    o_ref[...] = y


def _ffn(x, g, w1, w2, gf, layer, final_norm, row0, n_rows, in_place):
    blk0 = row0 // FFN_ROWS
    in_spec = pl.BlockSpec((FFN_ROWS, D_MODEL), lambda i: (blk0 + i, 0))
    out_spec = in_spec if in_place else pl.BlockSpec((FFN_ROWS, D_MODEL), lambda i: (i, 0))
    out_rows = x.shape[0] if in_place else n_rows
    return pl.pallas_call(
        functools.partial(_ffn_kernel, final_norm=final_norm),
        grid=(n_rows // FFN_ROWS,),
        in_specs=[in_spec, _layer_spec((1, D_MODEL), layer), _layer_spec((D_MODEL, D_FF), layer),
                  _layer_spec((D_FF, D_MODEL), layer), _const_spec((1, D_MODEL))],
        out_specs=out_spec,
        out_shape=jax.ShapeDtypeStruct((out_rows, D_MODEL), F32),
        input_output_aliases={0: 0} if in_place else {},
        compiler_params=_params(),
        name="ffn",
    )(x, g, w1, w2, gf)


def _sgu_front(x, g_ref, win_ref, sg_ref):
    h = _rms(x, g_ref[...]).astype(BF16)
    uv = jax.nn.gelu(_dot(h, win_ref[...]))
    return uv[:, :D_MODEL], _rms(uv[:, D_MODEL:], sg_ref[...])


def _rows_to_batch_major(x, nb):
    steps = x.shape[0] // nb
    return jnp.swapaxes(x.reshape(steps, nb, x.shape[1]), 0, 1).reshape(x.shape)


def _rows_to_time_major(x, nb):
    steps = x.shape[0] // nb
    return jnp.swapaxes(x.reshape(nb, steps, x.shape[1]), 0, 1).reshape(x.shape)


def _sgu_prompt_kernel(x_ref, g_ref, win_ref, sg_ref, w_ref, bias_ref, wout_ref, o_ref, y_s, *, nb, in_tm, out_tm):
    rows = nb * CHUNK
    if in_tm:
        x = _rows_to_batch_major(x_ref[...], nb)
    else:
        x = x_ref[...].reshape(rows, D_MODEL)
    u, v = _sgu_front(x, g_ref, win_ref, sg_ref)
    vb = v.astype(BF16)
    for b in range(nb):
        rs = slice(b * CHUNK, (b + 1) * CHUNK)
        for g in range(HEADS_A):
            cs = slice(g * HD_A, (g + 1) * HD_A)
            mixed = _dot(w_ref[g], vb[rs, cs]) + bias_ref[:, cs]
            y_s[rs, cs] = (u[rs, cs] * mixed).astype(BF16)
    o = x + _dot(y_s[...], wout_ref[...])
    if out_tm:
        o_ref[...] = _rows_to_time_major(o, nb)
    else:
        o_ref[...] = o.reshape(nb, CHUNK, D_MODEL)


def _sgu_sample_kernel(x_ref, g_ref, win_ref, sg_ref, wl_ref, bl_ref, wout_ref, *rest, nb, steps, aliased):
    o_ref, v_ref = rest[1:] if aliased else rest
    x = x_ref[...]
    u, v = _sgu_front(x, g_ref, win_ref, sg_ref)
    v_ref[...] = v
    mixed = []
    for t in range(steps):
        m = None
        for s in range(t + 1):
            term = wl_ref[t * steps + s:t * steps + s + 1, :] * v[s * nb:(s + 1) * nb, :]
            m = term if m is None else m + term
        mixed.append(m + bl_ref[t:t + 1, :])
    y = (u * jnp.concatenate(mixed, axis=0)).astype(BF16)
    o_ref[...] = x + _dot(y, wout_ref[...])


def _sgu_prompt(x, p, nb, n_steps, in_tm, out_tm, total_rows):
    rows = nb * CHUNK
    tm_spec = pl.BlockSpec((rows, D_MODEL), lambda i: (i, 0))
    bm_spec = pl.BlockSpec((nb, CHUNK, D_MODEL), lambda i: (0, i, 0))
    out_shape = (jax.ShapeDtypeStruct((total_rows, D_MODEL), F32) if out_tm
                 else jax.ShapeDtypeStruct((nb, n_steps, D_MODEL), F32))
    return pl.pallas_call(
        functools.partial(_sgu_prompt_kernel, nb=nb, in_tm=in_tm, out_tm=out_tm),
        grid=(n_steps // CHUNK,),
        in_specs=[tm_spec if in_tm else bm_spec, _const_spec((1, D_MODEL)), _const_spec((D_MODEL, 2 * D_MODEL)),
                  _const_spec((1, D_MODEL)), _const_spec((HEADS_A, CHUNK, CHUNK)),
                  _const_spec((CHUNK, D_MODEL)), _const_spec((D_MODEL, D_MODEL))],
        out_specs=tm_spec if out_tm else bm_spec,
        out_shape=out_shape,
        scratch_shapes=[pltpu.VMEM((rows, D_MODEL), BF16)],
        compiler_params=_params(),
        name="sgu_prompt",
    )(x, p["g"], p["win"], p["sg"], p["w_tril"], p["bias"], p["wout"])


def _sgu_sample(x_in, in_row0, dest, dest_row0, p, nb, steps):
    tile = steps * nb
    in_spec = pl.BlockSpec((tile, D_MODEL), lambda i: (in_row0 // tile, 0))
    specs = [in_spec, _const_spec((1, D_MODEL)), _const_spec((D_MODEL, 2 * D_MODEL)),
             _const_spec((1, D_MODEL)), _const_spec((steps * steps, D_MODEL)),
             _const_spec((steps, D_MODEL)), _const_spec((D_MODEL, D_MODEL))]
    args = [x_in, p["g"], p["win"], p["sg"], p["wl"], p["bl"], p["wout"]]
    aliased = dest is not None
    if aliased:
        specs.append(pl.BlockSpec(memory_space=pl.ANY))
        args.append(dest)
        out_spec = pl.BlockSpec((tile, D_MODEL), lambda i: (dest_row0 // tile, 0))
        out_shape = jax.ShapeDtypeStruct(dest.shape, F32)
    else:
        out_spec = pl.BlockSpec((tile, D_MODEL), lambda i: (0, 0))
        out_shape = jax.ShapeDtypeStruct((tile, D_MODEL), F32)
    return pl.pallas_call(
        functools.partial(_sgu_sample_kernel, nb=nb, steps=steps, aliased=aliased),
        grid=(1,),
        in_specs=specs,
        out_specs=[out_spec, pl.BlockSpec((tile, D_MODEL), lambda i: (0, 0))],
        out_shape=[out_shape, jax.ShapeDtypeStruct((tile, D_MODEL), F32)],
        input_output_aliases={len(args) - 1: 0} if aliased else {},
        compiler_params=_params(),
        name="sgu_sample",
    )(*args)


def _rglru_kernel(x_ref, g_ref, win_ref, cw_ref, cb_ref, wa_ref, ba_ref, wx_ref, bx_ref, lam_ref, wout_ref,
                  conv0_ref, h0_ref, o_ref, conv_ref, h_ref, xext_s, gate_s, a_s, b_s, *, nb, steps, sub_steps):
    rows = nb * steps
    sub = nb * sub_steps
    halo = (CONV_W - 1) * nb

    @pl.when(pl.program_id(0) == 0)
    def _():
        xext_s[0:halo, :] = conv0_ref[...]
        h_ref[...] = h0_ref[...]

    def project(k):
        rs = slice(k * sub, (k + 1) * sub)
        gx = _dot(_rms(x_ref[rs, :], g_ref[...]).astype(BF16), win_ref[...])
        gate_s[rs, :] = jax.nn.gelu(gx[:, :D_MODEL])
        xext_s[halo + k * sub:halo + (k + 1) * sub, :] = gx[:, D_MODEL:]

    def gates(k):
        rs = slice(k * sub, (k + 1) * sub)
        conv = None
        for w in range(CONV_W):
            term = xext_s[k * sub + w * nb:(k + 1) * sub + w * nb, :] * cw_ref[w:w + 1, :]
            conv = term if conv is None else conv + term
        xc = cb_ref[...] + conv
        xcb = xc.astype(BF16)
        ra, ia = [], []
        for j in range(N_BLK):
            cols = slice(j * V7X_MXU_DIM, (j + 1) * V7X_MXU_DIM)
            ra.append(_dot(xcb[:, cols], wa_ref[j]))
            ia.append(_dot(xcb[:, cols], wx_ref[j]))
        r = jax.nn.sigmoid(jnp.concatenate(ra, axis=1) + ba_ref[...])
        ig = jax.nn.sigmoid(jnp.concatenate(ia, axis=1) + bx_ref[...])
        log_a = (-LRU_C * r) * jax.nn.softplus(-lam_ref[...])
        a = jnp.exp(log_a)
        a_s[rs, :] = a
        mult = jnp.sqrt(jnp.maximum(-(jnp.tanh(log_a) * (a * a + 1.0)), 0.0))
        b_s[rs, :] = mult * (ig * xc)

    def recur(k, hcur):
        rs = slice(k * sub, (k + 1) * sub)
        for t in range(k * sub_steps, (k + 1) * sub_steps):
            sl = slice(t * nb, (t + 1) * nb)
            hcur = a_s[sl, :] * hcur + b_s[sl, :]
            b_s[sl, :] = hcur
        y = (b_s[rs, :] * gate_s[rs, :]).astype(BF16)
        o_ref[rs, :] = x_ref[rs, :] + _dot(y, wout_ref[...])
        return hcur

    n_sub = steps // sub_steps
    hcur = h_ref[...]
    for k in range(n_sub + 2):
        if k < n_sub:
            project(k)
        if 0 <= k - 1 < n_sub:
            gates(k - 1)
        if 0 <= k - 2 < n_sub:
            hcur = recur(k - 2, hcur)
    h_ref[...] = hcur
    tail = xext_s[rows:rows + halo, :]
    conv_ref[...] = tail
    xext_s[0:halo, :] = tail


def _rglru(x, p, conv0, h0, nb, steps, sub_steps, row0, n_rows):
    rows = nb * steps
    blk0 = row0 // rows
    halo = (CONV_W - 1) * nb
    row_spec = pl.BlockSpec((rows, D_MODEL), lambda i: (blk0 + i, 0))
    vec = _const_spec((1, D_MODEL))
    blk = _const_spec((N_BLK, V7X_MXU_DIM, V7X_MXU_DIM))
    return pl.pallas_call(
        functools.partial(_rglru_kernel, nb=nb, steps=steps, sub_steps=sub_steps),
        grid=(n_rows // rows,),
        in_specs=[row_spec, vec, _const_spec((D_MODEL, 2 * D_MODEL)), _const_spec((CONV_W, D_MODEL)), vec,
                  blk, vec, blk, vec, vec, _const_spec((D_MODEL, D_MODEL)),
                  _const_spec((halo, D_MODEL)), _const_spec((nb, D_MODEL))],
        out_specs=[row_spec, pl.BlockSpec((halo, D_MODEL), lambda i: (0, 0)),
                   pl.BlockSpec((nb, D_MODEL), lambda i: (0, 0))],
        out_shape=[jax.ShapeDtypeStruct(x.shape, F32), jax.ShapeDtypeStruct((halo, D_MODEL), F32),
                   jax.ShapeDtypeStruct((nb, D_MODEL), F32)],
        scratch_shapes=[pltpu.VMEM((rows + halo, D_MODEL), F32), pltpu.VMEM((rows, D_MODEL), F32),
                        pltpu.VMEM((rows, D_MODEL), F32), pltpu.VMEM((rows, D_MODEL), F32)],
        input_output_aliases={0: 0},
        compiler_params=_params(),
        name="rglru",
    )(x, p["g"], p["win"], p["cw"], p["cb"], p["wa"], p["ba"], p["wx"], p["bx"], p["lam"], p["wout"], conv0, h0)


def _s5_disc_kernel(lre_ref, lim_ref, ldt_ref, bre_ref, bim_ref, are_ref, aim_ref, bbre_ref, bbim_ref):
    lr, li = lre_ref[...], lim_ref[...]
    dt = jnp.exp(ldt_ref[...])
    mag = jnp.exp(lr * dt)
    ab_re, ab_im = mag * jnp.cos(li * dt), mag * jnp.sin(li * dt)
    zr, zi = ab_re - 1.0, ab_im
    den = lr * lr + li * li
    q_re = (zr * lr + zi * li) / den
    q_im = (zi * lr - zr * li) / den
    are_ref[...] = ab_re
    aim_ref[...] = ab_im
    br, bi = bre_ref[...], bim_ref[...]
    bbre_ref[...] = q_re[:, None, :] * br - q_im[:, None, :] * bi
    bbim_ref[...] = q_re[:, None, :] * bi + q_im[:, None, :] * br


def _s5_disc(lam_re, lam_im, log_dt, b_re_t, b_im_t):
    gp = jax.ShapeDtypeStruct((G_C, P_C), F32)
    ghp = jax.ShapeDtypeStruct((G_C, GROUP_C, P_C), F32)
    return pl.pallas_call(_s5_disc_kernel, out_shape=[gp, gp, ghp, ghp], name="s5_disc")(
        lam_re, lam_im, log_dt.reshape(G_C, 1), b_re_t, b_im_t)


def _s5_kernel(x_ref, g_ref, win_ref, bre_ref, bim_ref, are_ref, aim_ref, cre_ref, cim_ref, dskip_ref, wglu_ref,
               sre0_ref, sim0_ref, o_ref, sre_ref, sim_ref, xre_s, xim_s, *, nb, steps, lane_chunk):
    @pl.when(pl.program_id(0) == 0)
    def _():
        sre_ref[...] = sre0_ref[...]
        sim_ref[...] = sim0_ref[...]

    x = x_ref[...]
    h = _rms(x, g_ref[...]).astype(BF16)
    u = _dot(h, win_ref[...])
    ub = u.astype(BF16)

    def expand(j):
        ucols = slice(j * V7X_MXU_DIM, (j + 1) * V7X_MXU_DIM)
        scols = slice(j * STATE_PER_BLK, (j + 1) * STATE_PER_BLK)
        xre_s[:, scols] = _dot(ub[:, ucols], bre_ref[j])
        xim_s[:, scols] = _dot(ub[:, ucols], bim_ref[j])

    def scan(j):
        for c in range(STATE_PER_BLK // lane_chunk):
            cols = slice(j * STATE_PER_BLK + c * lane_chunk, j * STATE_PER_BLK + (c + 1) * lane_chunk)
            a_re = jnp.broadcast_to(are_ref[:, cols], (nb, lane_chunk))
            a_im = jnp.broadcast_to(aim_ref[:, cols], (nb, lane_chunk))
            s_re, s_im = sre_ref[:, cols], sim_ref[:, cols]
            for t in range(steps):
                sl = slice(t * nb, (t + 1) * nb)
                s_re, s_im = ((a_re * s_re - a_im * s_im) + xre_s[sl, cols],
                              (a_re * s_im + a_im * s_re) + xim_s[sl, cols])
                xre_s[sl, cols] = s_re
                xim_s[sl, cols] = s_im
            sre_ref[:, cols] = s_re
            sim_ref[:, cols] = s_im

    ys = []
    expand(0)
    for j in range(N_BLK):
        if j + 1 < N_BLK:
            expand(j + 1)
        scan(j)
        scols = slice(j * STATE_PER_BLK, (j + 1) * STATE_PER_BLK)
        ys.append(_dot(xre_s[:, scols].astype(BF16), cre_ref[j]) - _dot(xim_s[:, scols].astype(BF16), cim_ref[j]))
    y = jnp.concatenate(ys, axis=1) + dskip_ref[...] * u
    o = _dot(jax.nn.gelu(y).astype(BF16), wglu_ref[...])
    o_ref[...] = x + o[:, :D_MODEL] * jax.nn.sigmoid(o[:, D_MODEL:])


def _s5(x, p, sre0, sim0, nb, steps, lane_chunk, row0, n_rows):
    rows = nb * steps
    blk0 = row0 // rows
    row_spec = pl.BlockSpec((rows, D_MODEL), lambda i: (blk0 + i, 0))
    vec = _const_spec((1, D_MODEL))
    svec = _const_spec((1, D_STATE))
    state_spec = _const_spec((nb, D_STATE))
    bspec = _const_spec((N_BLK, V7X_MXU_DIM, STATE_PER_BLK))
    cspec = _const_spec((N_BLK, STATE_PER_BLK, V7X_MXU_DIM))
    state_out = pl.BlockSpec((nb, D_STATE), lambda i: (0, 0))
    return pl.pallas_call(
        functools.partial(_s5_kernel, nb=nb, steps=steps, lane_chunk=lane_chunk),
        grid=(n_rows // rows,),
        in_specs=[row_spec, vec, _const_spec((D_MODEL, D_MODEL)), bspec, bspec, svec, svec, cspec, cspec, vec,
                  _const_spec((D_MODEL, 2 * D_MODEL)), state_spec, state_spec],
        out_specs=[row_spec, state_out, state_out],
        out_shape=[jax.ShapeDtypeStruct(x.shape, F32), jax.ShapeDtypeStruct((nb, D_STATE), F32),
                   jax.ShapeDtypeStruct((nb, D_STATE), F32)],
        scratch_shapes=[pltpu.VMEM((rows, D_STATE), F32), pltpu.VMEM((rows, D_STATE), F32)],
        input_output_aliases={0: 0},
        compiler_params=_params(),
        name="s5",
    )(x, p["g"], p["win"], p["bre"], p["bim"], p["are"], p["aim"], p["cre"], p["cim"], p["dskip"], p["wglu"],
      sre0, sim0)


def _block_diag(w, n_per_blk):
    n, k_in, k_out = w.shape
    wb = w.reshape(n // n_per_blk, n_per_blk, k_in, k_out)
    eye = jnp.eye(n_per_blk, dtype=w.dtype)
    out = jnp.einsum("jgio,gk->jgiko", wb, eye)
    return out.reshape(n // n_per_blk, n_per_blk * k_in, n_per_blk * k_out)


def _to_rows(a):
    return jnp.swapaxes(a, 0, 1).reshape(a.shape[0] * a.shape[1], a.shape[2])


def _from_rows(a, nb):
    return jnp.swapaxes(a.reshape(a.shape[0] // nb, nb, a.shape[1]), 0, 1)


def kernel(x_prompt, x_sample, state_rglru_conv, state_rglru_h, state_s5_re, state_s5_im, norm_mix, norm_ffn, norm_f, w_ff1, w_ff2, w_in_a, sgu_g, w_s, b_s, w_out_a, w_in_b, conv_w, conv_b, w_a, b_a, w_x, b_x, lam, w_out_b, w_in_c, lam_re, lam_im, log_dt, b_re, b_im, c_re, c_im, d_skip, w_glu):
    bp, tp, _ = x_prompt.shape
    bs, ts, _ = x_sample.shape
    rows_p, rows_s = bp * tp, bs * ts
    rows_all = rows_p + rows_s
    assert DEPTH % N_MIXERS == 1 and DEPTH > 1, "first and last layers must be SGU layers"
    assert tp % CHUNK == 0 and ts < CHUNK and rows_p % rows_s == 0
    assert rows_s % FFN_ROWS == 0 and rows_p % FFN_ROWS == 0

    row = lambda v: v.reshape(1, -1)
    norm_ffn3 = norm_ffn.reshape(DEPTH, 1, D_MODEL)
    w1, w2, gf = w_ff1.astype(BF16), w_ff2.astype(BF16), row(norm_f)
    xs_rows = _to_rows(x_sample)

    x = None
    outs_v, conv_p, h_p, conv_s, h_s, sre_p, sim_p, sre_s, sim_s = [], [], [], [], [], [], [], [], []
    for layer in range(DEPTH):
        j, kind = layer // N_MIXERS, layer % N_MIXERS
        first, last = layer == 0, layer == DEPTH - 1
        g = row(norm_mix[layer])
        if kind == 0:
            p = dict(g=g, win=w_in_a[j].astype(BF16), wout=w_out_a[j].astype(BF16), sg=row(sgu_g[j]),
                     w_tril=jnp.where(jnp.tril(jnp.ones((CHUNK, CHUNK), dtype=bool)), w_s[j], 0.0).astype(BF16),
                     bias=jnp.repeat(b_s[j].T, HD_A, axis=1),
                     wl=jnp.repeat(w_s[j][:, :ts, :ts].reshape(HEADS_A, ts * ts).T, HD_A, axis=1),
                     bl=jnp.repeat(b_s[j][:, :ts].T, HD_A, axis=1))
            if first:
                x = _sgu_prompt(x_prompt, p, bp, tp, False, True, rows_all)
                x, v = _sgu_sample(xs_rows, 0, x, rows_p, p, bs, ts)
            elif last:
                x_s, v = _sgu_sample(x, rows_p, None, 0, p, bs, ts)
                x_p = _sgu_prompt(x, p, bp, tp, True, False, rows_all).reshape(rows_p, D_MODEL)
            else:
                raise NotImplementedError("interior SGU layers")
            outs_v.append(_from_rows(v, bs))
        elif kind == 1:
            p = dict(g=g, win=w_in_b[j].astype(BF16), cw=conv_w[j], cb=row(conv_b[j]),
                     wa=_block_diag(w_a[j], HEADS_PER_BLK).astype(BF16), ba=row(b_a[j]),
                     wx=_block_diag(w_x[j], HEADS_PER_BLK).astype(BF16), bx=row(b_x[j]),
                     lam=row(lam[j]), wout=w_out_b[j].astype(BF16))
            dt_s = state_rglru_h.dtype
            x, cp, hp = _rglru(x, p, jnp.zeros(((CONV_W - 1) * bp, D_MODEL), dt_s), jnp.zeros((bp, D_MODEL), dt_s),
                               bp, RGLRU_PROMPT_STEPS, RGLRU_PROMPT_SUB_STEPS, 0, rows_p)
            x, cs, hs = _rglru(x, p, _to_rows(state_rglru_conv[j]), state_rglru_h[j], bs, ts, ts, rows_p, rows_s)
            conv_p.append(_from_rows(cp, bp)); h_p.append(hp)
            conv_s.append(_from_rows(cs, bs)); h_s.append(hs)
        else:
            are, aim, bbre, bbim = _s5_disc(lam_re[j], lam_im[j], log_dt[j],
                                            jnp.swapaxes(b_re[j], 1, 2), jnp.swapaxes(b_im[j], 1, 2))
            p = dict(g=g, win=w_in_c[j].astype(BF16),
                     bre=_block_diag(bbre, GROUPS_PER_BLK).astype(BF16),
                     bim=_block_diag(bbim, GROUPS_PER_BLK).astype(BF16),
                     are=are.reshape(1, D_STATE), aim=aim.reshape(1, D_STATE),
                     cre=_block_diag(jnp.swapaxes(c_re[j], 1, 2), GROUPS_PER_BLK).astype(BF16),
                     cim=_block_diag(jnp.swapaxes(c_im[j], 1, 2), GROUPS_PER_BLK).astype(BF16),
                     dskip=row(d_skip[j]), wglu=w_glu[j].astype(BF16))
            zs = jnp.zeros((bp, D_STATE), state_s5_re.dtype)
            x, rp, ip = _s5(x, p, zs, zs, bp, S5_PROMPT_STEPS, 4 * V7X_LANES, 0, rows_p)
            x, rs, is_ = _s5(x, p, state_s5_re[j].reshape(bs, D_STATE), state_s5_im[j].reshape(bs, D_STATE),
                             bs, ts, V7X_LANES, rows_p, rows_s)
            sre_p.append(rp.reshape(bp, G_C, P_C)); sim_p.append(ip.reshape(bp, G_C, P_C))
            sre_s.append(rs.reshape(bs, G_C, P_C)); sim_s.append(is_.reshape(bs, G_C, P_C))
        if last:
            y_p = _ffn(x_p, norm_ffn3, w1, w2, gf, layer, True, 0, rows_p, False)
            y_s = _ffn(x_s, norm_ffn3, w1, w2, gf, layer, True, 0, rows_s, False)
        else:
            x = _ffn(x, norm_ffn3, w1, w2, gf, layer, False, 0, rows_all, True)

    y_prompt = y_p.reshape(bp, tp, D_MODEL)
    y_sample = _from_rows(y_s, bs)
    return (y_prompt, y_sample, jnp.stack(outs_v), jnp.stack(conv_p), jnp.stack(h_p), jnp.stack(conv_s),
            jnp.stack(h_s), jnp.stack(sre_p), jnp.stack(sim_p), jnp.stack(sre_s), jnp.stack(sim_s))
```

```python
import functools

import jax
import jax.numpy as jnp
from jax import lax
from jax.experimental import pallas as pl
from jax.experimental.pallas import tpu as pltpu

F32 = jnp.float32
BF16 = jnp.bfloat16

D_MODEL = 1024
DEPTH = 4
N_MIXERS = 3
EPS = 1e-6
CHUNK = 128
HEADS_A = 8
HD_A = D_MODEL // HEADS_A
HEADS_B = 16
HD_B = D_MODEL // HEADS_B
CONV_W = 4
LRU_C = 8.0
GROUP_C = 16
G_C = D_MODEL // GROUP_C
P_C = 64
D_STATE = G_C * P_C
D_FF = 4 * D_MODEL

V7X_LANES = 128
V7X_MXU_DIM = 256
V7X_VMEM_BYTES = 64 * 1024 * 1024
VMEM_LIMIT = V7X_VMEM_BYTES - 8 * 1024 * 1024

N_BLK = D_MODEL // V7X_MXU_DIM
HEADS_PER_BLK = V7X_MXU_DIM // HD_B
GROUPS_PER_BLK = V7X_MXU_DIM // GROUP_C
STATE_PER_BLK = GROUPS_PER_BLK * P_C

FFN_ROWS = 512
FFN_COL_CHUNK = 1024
S5_PROMPT_STEPS = 32
RGLRU_PROMPT_STEPS = 64
RGLRU_PROMPT_SUB_STEPS = 32


def _rms(x, g):
    return (x * lax.rsqrt(jnp.mean(x * x, axis=-1, keepdims=True) + EPS)) * g


def _dot(a, b):
    return jnp.dot(a, b, preferred_element_type=F32)


def _const_spec(shape):
    zeros = (0,) * len(shape)
    return pl.BlockSpec(shape, lambda i: zeros, pipeline_mode=pl.Buffered(1))


def _layer_spec(shape, layer):
    idx = (layer,) + (0,) * len(shape)
    return pl.BlockSpec((None,) + tuple(shape), lambda i: idx, pipeline_mode=pl.Buffered(1))


def _params():
    return pltpu.CompilerParams(dimension_semantics=("arbitrary",), vmem_limit_bytes=VMEM_LIMIT)


def _ffn_tile(x_ref, g_ref, w1_ref, w2_ref, gf_ref, o_ref, final_norm):
    x = x_ref[...]
    h = _rms(x, g_ref[...]).astype(BF16)
    acc = None
    for j in range(D_FF // FFN_COL_CHUNK):
        cols = slice(j * FFN_COL_CHUNK, (j + 1) * FFN_COL_CHUNK)
        a = jnp.square(jnp.maximum(_dot(h, w1_ref[:, cols]), 0.0)).astype(BF16)
        part = _dot(a, w2_ref[cols, :])
        acc = part if acc is None else acc + part
    y = x + acc
    if final_norm:
        y = _rms(y, gf_ref[...])
    o_ref[...] = y


def _ffn_kernel(xp_ref, xs_ref, g_ref, w1_ref, w2_ref, gf_ref, op_ref, os_ref, *, final_norm, n_p):
    i = pl.program_id(0)

    @pl.when(i < n_p)
    def _():
        _ffn_tile(xp_ref, g_ref, w1_ref, w2_ref, gf_ref, op_ref, final_norm)

    @pl.when(i >= n_p)
    def _():
        _ffn_tile(xs_ref, g_ref, w1_ref, w2_ref, gf_ref, os_ref, final_norm)


def _ffn(x_p, x_s, g, w1, w2, gf, layer, final_norm):
    n_p, n_s = x_p.shape[0] // FFN_ROWS, x_s.shape[0] // FFN_ROWS
    p_spec = pl.BlockSpec((FFN_ROWS, D_MODEL), lambda i: (jnp.minimum(i, n_p - 1), 0))
    s_spec = pl.BlockSpec((FFN_ROWS, D_MODEL), lambda i: (jnp.maximum(i - n_p, 0), 0))
    return pl.pallas_call(
        functools.partial(_ffn_kernel, final_norm=final_norm, n_p=n_p),
        grid=(n_p + n_s,),
        in_specs=[p_spec, s_spec, _layer_spec((1, D_MODEL), layer), _layer_spec((D_MODEL, D_FF), layer),
                  _layer_spec((D_FF, D_MODEL), layer), _const_spec((1, D_MODEL))],
        out_specs=[p_spec, s_spec],
        out_shape=[jax.ShapeDtypeStruct(x_p.shape, F32), jax.ShapeDtypeStruct(x_s.shape, F32)],
        compiler_params=_params(),
        name="ffn",
    )(x_p, x_s, g, w1, w2, gf)


def _sgu_front(x, g_ref, win_ref, sg_ref):
    h = _rms(x, g_ref[...]).astype(BF16)
    uv = jax.nn.gelu(_dot(h, win_ref[...]))
    return uv[:, :D_MODEL], _rms(uv[:, D_MODEL:], sg_ref[...])


def _rows_to_batch_major(x, nb):
    steps = x.shape[0] // nb
    return jnp.swapaxes(x.reshape(steps, nb, x.shape[1]), 0, 1).reshape(x.shape)


def _rows_to_time_major(x, nb):
    steps = x.shape[0] // nb
    return jnp.swapaxes(x.reshape(nb, steps, x.shape[1]), 0, 1).reshape(x.shape)


def _sgu_prompt_kernel(x_ref, g_ref, win_ref, sg_ref, w_ref, bias_ref, wout_ref, o_ref, y_s, *, nb, in_tm, out_tm):
    rows = nb * CHUNK
    if in_tm:
        x = _rows_to_batch_major(x_ref[...], nb)
    else:
        x = x_ref[...].reshape(rows, D_MODEL)
    u, v = _sgu_front(x, g_ref, win_ref, sg_ref)
    vb = v.astype(BF16)
    for b in range(nb):
        rs = slice(b * CHUNK, (b + 1) * CHUNK)
        for g in range(HEADS_A):
            cs = slice(g * HD_A, (g + 1) * HD_A)
            mixed = _dot(w_ref[g], vb[rs, cs]) + bias_ref[:, cs]
            y_s[rs, cs] = (u[rs, cs] * mixed).astype(BF16)
    o = x + _dot(y_s[...], wout_ref[...])
    if out_tm:
        o_ref[...] = _rows_to_time_major(o, nb)
    else:
        o_ref[...] = o.reshape(nb, CHUNK, D_MODEL)


def _sgu_sample_kernel(x_ref, g_ref, win_ref, sg_ref, wl_ref, bl_ref, wout_ref, o_ref, v_ref, *, nb, steps):
    x = x_ref[...]
    u, v = _sgu_front(x, g_ref, win_ref, sg_ref)
    v_ref[...] = v
    mixed = []
    for t in range(steps):
        m = None
        for s in range(t + 1):
            term = wl_ref[t * steps + s:t * steps + s + 1, :] * v[s * nb:(s + 1) * nb, :]
            m = term if m is None else m + term
        mixed.append(m + bl_ref[t:t + 1, :])
    y = (u * jnp.concatenate(mixed, axis=0)).astype(BF16)
    o_ref[...] = x + _dot(y, wout_ref[...])


def _sgu_prompt(x, p, nb, n_steps, in_tm, out_tm):
    rows = nb * CHUNK
    tm_spec = pl.BlockSpec((rows, D_MODEL), lambda i: (i, 0))
    bm_spec = pl.BlockSpec((nb, CHUNK, D_MODEL), lambda i: (0, i, 0))
    out_shape = (jax.ShapeDtypeStruct((nb * n_steps, D_MODEL), F32) if out_tm
                 else jax.ShapeDtypeStruct((nb, n_steps, D_MODEL), F32))
    return pl.pallas_call(
        functools.partial(_sgu_prompt_kernel, nb=nb, in_tm=in_tm, out_tm=out_tm),
        grid=(n_steps // CHUNK,),
        in_specs=[tm_spec if in_tm else bm_spec, _const_spec((1, D_MODEL)), _const_spec((D_MODEL, 2 * D_MODEL)),
                  _const_spec((1, D_MODEL)), _const_spec((HEADS_A, CHUNK, CHUNK)),
                  _const_spec((CHUNK, D_MODEL)), _const_spec((D_MODEL, D_MODEL))],
        out_specs=tm_spec if out_tm else bm_spec,
        out_shape=out_shape,
        scratch_shapes=[pltpu.VMEM((rows, D_MODEL), BF16)],
        compiler_params=_params(),
        name="sgu_prompt",
    )(x, p["g"], p["win"], p["sg"], p["w_tril"], p["bias"], p["wout"])


def _sgu_sample(x, p, nb, steps):
    tile = steps * nb
    tile_spec = pl.BlockSpec((tile, D_MODEL), lambda i: (0, 0))
    return pl.pallas_call(
        functools.partial(_sgu_sample_kernel, nb=nb, steps=steps),
        grid=(1,),
        in_specs=[tile_spec, _const_spec((1, D_MODEL)), _const_spec((D_MODEL, 2 * D_MODEL)),
                  _const_spec((1, D_MODEL)), _const_spec((steps * steps, D_MODEL)),
                  _const_spec((steps, D_MODEL)), _const_spec((D_MODEL, D_MODEL))],
        out_specs=[tile_spec, tile_spec],
        out_shape=[jax.ShapeDtypeStruct((tile, D_MODEL), F32), jax.ShapeDtypeStruct((tile, D_MODEL), F32)],
        compiler_params=_params(),
        name="sgu_sample",
    )(x, p["g"], p["win"], p["sg"], p["wl"], p["bl"], p["wout"])


def _rglru_kernel(x_ref, g_ref, win_ref, cw_ref, cb_ref, wa_ref, ba_ref, wx_ref, bx_ref, lam_ref, wout_ref,
                  conv0_ref, h0_ref, o_ref, conv_ref, h_ref, xext_s, gate_s, a_s, b_s, *, nb, steps, sub_steps):
    rows = nb * steps
    sub = nb * sub_steps
    halo = (CONV_W - 1) * nb

    @pl.when(pl.program_id(0) == 0)
    def _():
        xext_s[0:halo, :] = conv0_ref[...]
        h_ref[...] = h0_ref[...]

    def project(k):
        rs = slice(k * sub, (k + 1) * sub)
        gx = _dot(_rms(x_ref[rs, :], g_ref[...]).astype(BF16), win_ref[...])
        gate_s[rs, :] = jax.nn.gelu(gx[:, :D_MODEL])
        xext_s[halo + k * sub:halo + (k + 1) * sub, :] = gx[:, D_MODEL:]

    def gates(k):
        rs = slice(k * sub, (k + 1) * sub)
        conv = None
        for w in range(CONV_W):
            term = xext_s[k * sub + w * nb:(k + 1) * sub + w * nb, :] * cw_ref[w:w + 1, :]
            conv = term if conv is None else conv + term
        xc = cb_ref[...] + conv
        xcb = xc.astype(BF16)
        ra, ia = [], []
        for j in range(N_BLK):
            cols = slice(j * V7X_MXU_DIM, (j + 1) * V7X_MXU_DIM)
            ra.append(_dot(xcb[:, cols], wa_ref[j]))
            ia.append(_dot(xcb[:, cols], wx_ref[j]))
        r = jax.nn.sigmoid(jnp.concatenate(ra, axis=1) + ba_ref[...])
        ig = jax.nn.sigmoid(jnp.concatenate(ia, axis=1) + bx_ref[...])
        log_a = (-LRU_C * r) * jax.nn.softplus(-lam_ref[...])
        a = jnp.exp(log_a)
        a_s[rs, :] = a
        mult = jnp.sqrt(jnp.maximum(-(jnp.tanh(log_a) * (a * a + 1.0)), 0.0))
        b_s[rs, :] = mult * (ig * xc)

    def recur(k, hcur):
        rs = slice(k * sub, (k + 1) * sub)
        for t in range(k * sub_steps, (k + 1) * sub_steps):
            sl = slice(t * nb, (t + 1) * nb)
            hcur = a_s[sl, :] * hcur + b_s[sl, :]
            b_s[sl, :] = hcur
        y = (b_s[rs, :] * gate_s[rs, :]).astype(BF16)
        o_ref[rs, :] = x_ref[rs, :] + _dot(y, wout_ref[...])
        return hcur

    n_sub = steps // sub_steps
    hcur = h_ref[...]
    for k in range(n_sub + 2):
        if k < n_sub:
            project(k)
        if 0 <= k - 1 < n_sub:
            gates(k - 1)
        if 0 <= k - 2 < n_sub:
            hcur = recur(k - 2, hcur)
    h_ref[...] = hcur
    tail = xext_s[rows:rows + halo, :]
    conv_ref[...] = tail
    xext_s[0:halo, :] = tail


def _rglru(x, p, conv0, h0, nb, steps, sub_steps):
    rows = nb * steps
    halo = (CONV_W - 1) * nb
    row_spec = pl.BlockSpec((rows, D_MODEL), lambda i: (i, 0))
    vec = _const_spec((1, D_MODEL))
    blk = _const_spec((N_BLK, V7X_MXU_DIM, V7X_MXU_DIM))
    return pl.pallas_call(
        functools.partial(_rglru_kernel, nb=nb, steps=steps, sub_steps=sub_steps),
        grid=(x.shape[0] // rows,),
        in_specs=[row_spec, vec, _const_spec((D_MODEL, 2 * D_MODEL)), _const_spec((CONV_W, D_MODEL)), vec,
                  blk, vec, blk, vec, vec, _const_spec((D_MODEL, D_MODEL)),
                  _const_spec((halo, D_MODEL)), _const_spec((nb, D_MODEL))],
        out_specs=[row_spec, pl.BlockSpec((halo, D_MODEL), lambda i: (0, 0)),
                   pl.BlockSpec((nb, D_MODEL), lambda i: (0, 0))],
        out_shape=[jax.ShapeDtypeStruct(x.shape, F32), jax.ShapeDtypeStruct((halo, D_MODEL), F32),
                   jax.ShapeDtypeStruct((nb, D_MODEL), F32)],
        scratch_shapes=[pltpu.VMEM((rows + halo, D_MODEL), F32), pltpu.VMEM((rows, D_MODEL), F32),
                        pltpu.VMEM((rows, D_MODEL), F32), pltpu.VMEM((rows, D_MODEL), F32)],
        compiler_params=_params(),
        name="rglru",
    )(x, p["g"], p["win"], p["cw"], p["cb"], p["wa"], p["ba"], p["wx"], p["bx"], p["lam"], p["wout"], conv0, h0)


def _s5_disc_kernel(lre_ref, lim_ref, ldt_ref, bre_ref, bim_ref, are_ref, aim_ref, bbre_ref, bbim_ref):
    lr, li = lre_ref[...], lim_ref[...]
    dt = jnp.exp(ldt_ref[...])
    mag = jnp.exp(lr * dt)
    ab_re, ab_im = mag * jnp.cos(li * dt), mag * jnp.sin(li * dt)
    zr, zi = ab_re - 1.0, ab_im
    den = lr * lr + li * li
    q_re = (zr * lr + zi * li) / den
    q_im = (zi * lr - zr * li) / den
    are_ref[...] = ab_re
    aim_ref[...] = ab_im
    br, bi = bre_ref[...], bim_ref[...]
    bbre_ref[...] = q_re[:, None, :] * br - q_im[:, None, :] * bi
    bbim_ref[...] = q_re[:, None, :] * bi + q_im[:, None, :] * br


def _s5_disc(lam_re, lam_im, log_dt, b_re_t, b_im_t):
    gp = jax.ShapeDtypeStruct((G_C, P_C), F32)
    ghp = jax.ShapeDtypeStruct((G_C, GROUP_C, P_C), F32)
    return pl.pallas_call(_s5_disc_kernel, out_shape=[gp, gp, ghp, ghp], name="s5_disc")(
        lam_re, lam_im, log_dt.reshape(G_C, 1), b_re_t, b_im_t)


def _s5_kernel(x_ref, g_ref, win_ref, bre_ref, bim_ref, are_ref, aim_ref, cre_ref, cim_ref, dskip_ref, wglu_ref,
               sre0_ref, sim0_ref, o_ref, sre_ref, sim_ref, xre_s, xim_s, *, nb, steps, lane_chunk):
    @pl.when(pl.program_id(0) == 0)
    def _():
        sre_ref[...] = sre0_ref[...]
        sim_ref[...] = sim0_ref[...]

    x = x_ref[...]
    h = _rms(x, g_ref[...]).astype(BF16)
    u = _dot(h, win_ref[...])
    ub = u.astype(BF16)

    def expand(j):
        ucols = slice(j * V7X_MXU_DIM, (j + 1) * V7X_MXU_DIM)
        scols = slice(j * STATE_PER_BLK, (j + 1) * STATE_PER_BLK)
        xre_s[:, scols] = _dot(ub[:, ucols], bre_ref[j])
        xim_s[:, scols] = _dot(ub[:, ucols], bim_ref[j])

    def scan(j):
        for c in range(STATE_PER_BLK // lane_chunk):
            cols = slice(j * STATE_PER_BLK + c * lane_chunk, j * STATE_PER_BLK + (c + 1) * lane_chunk)
            a_re = jnp.broadcast_to(are_ref[:, cols], (nb, lane_chunk))
            a_im = jnp.broadcast_to(aim_ref[:, cols], (nb, lane_chunk))
            s_re, s_im = sre_ref[:, cols], sim_ref[:, cols]
            for t in range(steps):
                sl = slice(t * nb, (t + 1) * nb)
                s_re, s_im = ((a_re * s_re - a_im * s_im) + xre_s[sl, cols],
                              (a_re * s_im + a_im * s_re) + xim_s[sl, cols])
                xre_s[sl, cols] = s_re
                xim_s[sl, cols] = s_im
            sre_ref[:, cols] = s_re
            sim_ref[:, cols] = s_im

    ys = []
    expand(0)
    for j in range(N_BLK):
        if j + 1 < N_BLK:
            expand(j + 1)
        scan(j)
        scols = slice(j * STATE_PER_BLK, (j + 1) * STATE_PER_BLK)
        ys.append(_dot(xre_s[:, scols].astype(BF16), cre_ref[j]) - _dot(xim_s[:, scols].astype(BF16), cim_ref[j]))
    y = jnp.concatenate(ys, axis=1) + dskip_ref[...] * u
    o = _dot(jax.nn.gelu(y).astype(BF16), wglu_ref[...])
    o_ref[...] = x + o[:, :D_MODEL] * jax.nn.sigmoid(o[:, D_MODEL:])


def _s5(x, p, sre0, sim0, nb, steps, lane_chunk):
    rows = nb * steps
    row_spec = pl.BlockSpec((rows, D_MODEL), lambda i: (i, 0))
    vec = _const_spec((1, D_MODEL))
    svec = _const_spec((1, D_STATE))
    state_spec = _const_spec((nb, D_STATE))
    bspec = _const_spec((N_BLK, V7X_MXU_DIM, STATE_PER_BLK))
    cspec = _const_spec((N_BLK, STATE_PER_BLK, V7X_MXU_DIM))
    state_out = pl.BlockSpec((nb, D_STATE), lambda i: (0, 0))
    return pl.pallas_call(
        functools.partial(_s5_kernel, nb=nb, steps=steps, lane_chunk=lane_chunk),
        grid=(x.shape[0] // rows,),
        in_specs=[row_spec, vec, _const_spec((D_MODEL, D_MODEL)), bspec, bspec, svec, svec, cspec, cspec, vec,
                  _const_spec((D_MODEL, 2 * D_MODEL)), state_spec, state_spec],
        out_specs=[row_spec, state_out, state_out],
        out_shape=[jax.ShapeDtypeStruct(x.shape, F32), jax.ShapeDtypeStruct((nb, D_STATE), F32),
                   jax.ShapeDtypeStruct((nb, D_STATE), F32)],
        scratch_shapes=[pltpu.VMEM((rows, D_STATE), F32), pltpu.VMEM((rows, D_STATE), F32)],
        compiler_params=_params(),
        name="s5",
    )(x, p["g"], p["win"], p["bre"], p["bim"], p["are"], p["aim"], p["cre"], p["cim"], p["dskip"], p["wglu"],
      sre0, sim0)


def _block_diag(w, n_per_blk):
    n, k_in, k_out = w.shape
    wb = w.reshape(n // n_per_blk, n_per_blk, k_in, k_out)
    eye = jnp.eye(n_per_blk, dtype=w.dtype)
    out = jnp.einsum("jgio,gk->jgiko", wb, eye)
    return out.reshape(n // n_per_blk, n_per_blk * k_in, n_per_blk * k_out)


def _to_rows(a):
    return jnp.swapaxes(a, 0, 1).reshape(a.shape[0] * a.shape[1], a.shape[2])


def _from_rows(a, nb):
    return jnp.swapaxes(a.reshape(a.shape[0] // nb, nb, a.shape[1]), 0, 1)


def kernel(x_prompt, x_sample, state_rglru_conv, state_rglru_h, state_s5_re, state_s5_im, norm_mix, norm_ffn, norm_f, w_ff1, w_ff2, w_in_a, sgu_g, w_s, b_s, w_out_a, w_in_b, conv_w, conv_b, w_a, b_a, w_x, b_x, lam, w_out_b, w_in_c, lam_re, lam_im, log_dt, b_re, b_im, c_re, c_im, d_skip, w_glu):
    bp, tp, _ = x_prompt.shape
    bs, ts, _ = x_sample.shape
    rows_p, rows_s = bp * tp, bs * ts
    assert DEPTH % N_MIXERS == 1 and DEPTH > 1, "first and last layers must be SGU layers"
    assert tp % CHUNK == 0 and ts < CHUNK
    assert rows_s % FFN_ROWS == 0 and rows_p % FFN_ROWS == 0

    row = lambda v: v.reshape(1, -1)
    norm_ffn3 = norm_ffn.reshape(DEPTH, 1, D_MODEL)
    w1, w2, gf = w_ff1.astype(BF16), w_ff2.astype(BF16), row(norm_f)

    x_p, x_s = x_prompt, _to_rows(x_sample)
    outs_v, conv_p, h_p, conv_s, h_s, sre_p, sim_p, sre_s, sim_s = [], [], [], [], [], [], [], [], []
    for layer in range(DEPTH):
        j, kind = layer // N_MIXERS, layer % N_MIXERS
        first, last = layer == 0, layer == DEPTH - 1
        g = row(norm_mix[layer])
        if kind == 0:
            p = dict(g=g, win=w_in_a[j].astype(BF16), wout=w_out_a[j].astype(BF16), sg=row(sgu_g[j]),
                     w_tril=jnp.where(jnp.tril(jnp.ones((CHUNK, CHUNK), dtype=bool)), w_s[j], 0.0).astype(BF16),
                     bias=jnp.repeat(b_s[j].T, HD_A, axis=1),
                     wl=jnp.repeat(w_s[j][:, :ts, :ts].reshape(HEADS_A, ts * ts).T, HD_A, axis=1),
                     bl=jnp.repeat(b_s[j][:, :ts].T, HD_A, axis=1))
            if first:
                x_p = _sgu_prompt(x_p, p, bp, tp, False, True)
            elif last:
                x_p = _sgu_prompt(x_p, p, bp, tp, True, False).reshape(rows_p, D_MODEL)
            else:
                raise NotImplementedError("interior SGU layers")
            x_s, v = _sgu_sample(x_s, p, bs, ts)
            outs_v.append(_from_rows(v, bs))
        elif kind == 1:
            p = dict(g=g, win=w_in_b[j].astype(BF16), cw=conv_w[j], cb=row(conv_b[j]),
                     wa=_block_diag(w_a[j], HEADS_PER_BLK).astype(BF16), ba=row(b_a[j]),
                     wx=_block_diag(w_x[j], HEADS_PER_BLK).astype(BF16), bx=row(b_x[j]),
                     lam=row(lam[j]), wout=w_out_b[j].astype(BF16))
            dt_s = state_rglru_h.dtype
            x_p, cp, hp = _rglru(x_p, p, jnp.zeros(((CONV_W - 1) * bp, D_MODEL), dt_s),
                                 jnp.zeros((bp, D_MODEL), dt_s), bp, RGLRU_PROMPT_STEPS, RGLRU_PROMPT_SUB_STEPS)
            x_s, cs, hs = _rglru(x_s, p, _to_rows(state_rglru_conv[j]), state_rglru_h[j], bs, ts, ts)
            conv_p.append(_from_rows(cp, bp)); h_p.append(hp)
            conv_s.append(_from_rows(cs, bs)); h_s.append(hs)
        else:
            are, aim, bbre, bbim = _s5_disc(lam_re[j], lam_im[j], log_dt[j],
                                            jnp.swapaxes(b_re[j], 1, 2), jnp.swapaxes(b_im[j], 1, 2))
            p = dict(g=g, win=w_in_c[j].astype(BF16),
                     bre=_block_diag(bbre, GROUPS_PER_BLK).astype(BF16),
                     bim=_block_diag(bbim, GROUPS_PER_BLK).astype(BF16),
                     are=are.reshape(1, D_STATE), aim=aim.reshape(1, D_STATE),
                     cre=_block_diag(jnp.swapaxes(c_re[j], 1, 2), GROUPS_PER_BLK).astype(BF16),
                     cim=_block_diag(jnp.swapaxes(c_im[j], 1, 2), GROUPS_PER_BLK).astype(BF16),
                     dskip=row(d_skip[j]), wglu=w_glu[j].astype(BF16))
            zs = jnp.zeros((bp, D_STATE), state_s5_re.dtype)
            x_p, rp, ip = _s5(x_p, p, zs, zs, bp, S5_PROMPT_STEPS, 4 * V7X_LANES)
            x_s, rs, is_ = _s5(x_s, p, state_s5_re[j].reshape(bs, D_STATE), state_s5_im[j].reshape(bs, D_STATE),
                               bs, ts, V7X_LANES)
            sre_p.append(rp.reshape(bp, G_C, P_C)); sim_p.append(ip.reshape(bp, G_C, P_C))
            sre_s.append(rs.reshape(bs, G_C, P_C)); sim_s.append(is_.reshape(bs, G_C, P_C))
        x_p, x_s = _ffn(x_p, x_s, norm_ffn3, w1, w2, gf, layer, last)

    y_prompt = x_p.reshape(bp, tp, D_MODEL)
    y_sample = _from_rows(x_s, bs)
    return (y_prompt, y_sample, jnp.stack(outs_v), jnp.stack(conv_p), jnp.stack(h_p), jnp.stack(conv_s),
            jnp.stack(h_s), jnp.stack(sre_p), jnp.stack(sim_p), jnp.stack(sre_s), jnp.stack(sim_s))
```

```python
import functools

import jax
import jax.numpy as jnp
from jax import lax
from jax.experimental import pallas as pl
from jax.experimental.pallas import tpu as pltpu

F32 = jnp.float32
BF16 = jnp.bfloat16

D_MODEL = 1024
DEPTH = 4
N_MIXERS = 3
EPS = 1e-6
CHUNK = 128
HEADS_A = 8
HD_A = D_MODEL // HEADS_A
HEADS_B = 16
HD_B = D_MODEL // HEADS_B
CONV_W = 4
LRU_C = 8.0
GROUP_C = 16
G_C = D_MODEL // GROUP_C
P_C = 64
D_STATE = G_C * P_C
D_FF = 4 * D_MODEL

V7X_LANES = 128
V7X_MXU_DIM = 256
V7X_VMEM_BYTES = 64 * 1024 * 1024
VMEM_LIMIT = V7X_VMEM_BYTES - 8 * 1024 * 1024

N_BLK = D_MODEL // V7X_MXU_DIM
HEADS_PER_BLK = V7X_MXU_DIM // HD_B
GROUPS_PER_BLK = V7X_MXU_DIM // GROUP_C
STATE_PER_BLK = GROUPS_PER_BLK * P_C

FFN_ROWS = 512
FFN_COL_CHUNK = 1024
S5_PROMPT_STEPS = 32
RGLRU_PROMPT_STEPS = 64


def _rms(x, g):
    return (x * lax.rsqrt(jnp.mean(x * x, axis=-1, keepdims=True) + EPS)) * g


GELU_C0 = 0.7978845608028654
GELU_C1 = GELU_C0 * 0.044715


def _gelu(x):
    return x * (0.5 + 0.5 * jnp.tanh(x * (GELU_C0 + GELU_C1 * (x * x))))


def _dot(a, b):
    return jnp.dot(a, b, preferred_element_type=F32)


def _const_spec(shape):
    zeros = (0,) * len(shape)
    return pl.BlockSpec(shape, lambda i: zeros, pipeline_mode=pl.Buffered(1))


def _layer_spec(shape, layer):
    idx = (layer,) + (0,) * len(shape)
    return pl.BlockSpec((None,) + tuple(shape), lambda i: idx, pipeline_mode=pl.Buffered(1))


def _params():
    return pltpu.CompilerParams(dimension_semantics=("arbitrary",), vmem_limit_bytes=VMEM_LIMIT)


def _ffn_tile(x_ref, g_ref, w1_ref, w2_ref, gf_ref, o_ref, final_norm):
    x = x_ref[...]
    h = _rms(x, g_ref[...]).astype(BF16)
    acc = None
    for j in range(D_FF // FFN_COL_CHUNK):
        cols = slice(j * FFN_COL_CHUNK, (j + 1) * FFN_COL_CHUNK)
        a = jnp.square(jnp.maximum(_dot(h, w1_ref[:, cols]), 0.0)).astype(BF16)
        part = _dot(a, w2_ref[cols, :])
        acc = part if acc is None else acc + part
    y = x + acc
    if final_norm:
        y = _rms(y, gf_ref[...])
    o_ref[...] = y


def _ffn_kernel(xp_ref, xs_ref, g_ref, w1_ref, w2_ref, gf_ref, op_ref, os_ref, *, final_norm, n_p):
    i = pl.program_id(0)

    @pl.when(i < n_p)
    def _():
        _ffn_tile(xp_ref, g_ref, w1_ref, w2_ref, gf_ref, op_ref, final_norm)

    @pl.when(i >= n_p)
    def _():
        _ffn_tile(xs_ref, g_ref, w1_ref, w2_ref, gf_ref, os_ref, final_norm)


def _ffn(x_p, x_s, g, w1, w2, gf, layer, final_norm):
    n_p, n_s = x_p.shape[0] // FFN_ROWS, x_s.shape[0] // FFN_ROWS
    p_spec = pl.BlockSpec((FFN_ROWS, D_MODEL), lambda i: (jnp.minimum(i, n_p - 1), 0))
    s_spec = pl.BlockSpec((FFN_ROWS, D_MODEL), lambda i: (jnp.maximum(i - n_p, 0), 0))
    return pl.pallas_call(
        functools.partial(_ffn_kernel, final_norm=final_norm, n_p=n_p),
        grid=(n_p + n_s,),
        in_specs=[p_spec, s_spec, _layer_spec((1, D_MODEL), layer), _layer_spec((D_MODEL, D_FF), layer),
                  _layer_spec((D_FF, D_MODEL), layer), _const_spec((1, D_MODEL))],
        out_specs=[p_spec, s_spec],
        out_shape=[jax.ShapeDtypeStruct(x_p.shape, F32), jax.ShapeDtypeStruct(x_s.shape, F32)],
        compiler_params=_params(),
        name="ffn",
    )(x_p, x_s, g, w1, w2, gf)


def _sgu_front(x, g_ref, win_ref, sg_ref):
    h = _rms(x, g_ref[...]).astype(BF16)
    uv = _gelu(_dot(h, win_ref[...]))
    return uv[:, :D_MODEL], _rms(uv[:, D_MODEL:], sg_ref[...])


def _rows_to_batch_major(x, nb):
    steps = x.shape[0] // nb
    return jnp.swapaxes(x.reshape(steps, nb, x.shape[1]), 0, 1).reshape(x.shape)


def _rows_to_time_major(x, nb):
    steps = x.shape[0] // nb
    return jnp.swapaxes(x.reshape(nb, steps, x.shape[1]), 0, 1).reshape(x.shape)


def _sgu_prompt_kernel(x_ref, g_ref, win_ref, sg_ref, w_ref, bias_ref, wout_ref, o_ref, v_s, y_s,
                       *, nb, in_tm, out_tm):
    rows = nb * CHUNK
    if in_tm:
        x = _rows_to_batch_major(x_ref[...], nb)
    else:
        x = x_ref[...].reshape(rows, D_MODEL)
    h = _rms(x, g_ref[...]).astype(BF16)
    sumsq = None
    for c in range(N_BLK):
        v_c = _gelu(_dot(h, win_ref[:, D_MODEL + c * V7X_MXU_DIM:D_MODEL + (c + 1) * V7X_MXU_DIM]))
        v_s[:, c * V7X_MXU_DIM:(c + 1) * V7X_MXU_DIM] = v_c
        part = jnp.sum(v_c * v_c, axis=-1, keepdims=True)
        sumsq = part if sumsq is None else sumsq + part
    inv = lax.rsqrt(sumsq * (1.0 / D_MODEL) + EPS)
    for c in range(N_BLK):
        cols = slice(c * V7X_MXU_DIM, (c + 1) * V7X_MXU_DIM)
        u_c = _gelu(_dot(h, win_ref[:, cols]))
        vb_c = ((v_s[:, cols] * inv) * sg_ref[:, cols]).astype(BF16)
        for k in range(V7X_MXU_DIM // HD_A):
            g = c * (V7X_MXU_DIM // HD_A) + k
            cs = slice(g * HD_A, (g + 1) * HD_A)
            ks = slice(k * HD_A, (k + 1) * HD_A)
            for b in range(nb):
                rs = slice(b * CHUNK, (b + 1) * CHUNK)
                mixed = _dot(w_ref[g], vb_c[rs, ks]) + bias_ref[:, cs]
                y_s[rs, cs] = (u_c[rs, ks] * mixed).astype(BF16)
    for c in range(N_BLK):
        cols = slice(c * V7X_MXU_DIM, (c + 1) * V7X_MXU_DIM)
        o_c = x[:, cols] + _dot(y_s[...], wout_ref[:, cols])
        if out_tm:
            o_ref[:, cols] = _rows_to_time_major(o_c, nb)
        else:
            o_ref[:, :, cols] = o_c.reshape(nb, CHUNK, V7X_MXU_DIM)


def _sgu_sample_kernel(x_ref, g_ref, win_ref, sg_ref, wl_ref, bl_ref, wout_ref, o_ref, v_ref, *, nb, steps):
    x = x_ref[...]
    u, v = _sgu_front(x, g_ref, win_ref, sg_ref)
    v_ref[...] = v
    mixed = []
    for t in range(steps):
        m = None
        for s in range(t + 1):
            term = wl_ref[t * steps + s:t * steps + s + 1, :] * v[s * nb:(s + 1) * nb, :]
            m = term if m is None else m + term
        mixed.append(m + bl_ref[t:t + 1, :])
    y = (u * jnp.concatenate(mixed, axis=0)).astype(BF16)
    o_ref[...] = x + _dot(y, wout_ref[...])


def _sgu_prompt(x, p, nb, n_steps, in_tm, out_tm):
    rows = nb * CHUNK
    tm_spec = pl.BlockSpec((rows, D_MODEL), lambda i: (i, 0))
    bm_spec = pl.BlockSpec((nb, CHUNK, D_MODEL), lambda i: (0, i, 0))
    out_shape = (jax.ShapeDtypeStruct((nb * n_steps, D_MODEL), F32) if out_tm
                 else jax.ShapeDtypeStruct((nb, n_steps, D_MODEL), F32))
    return pl.pallas_call(
        functools.partial(_sgu_prompt_kernel, nb=nb, in_tm=in_tm, out_tm=out_tm),
        grid=(n_steps // CHUNK,),
        in_specs=[tm_spec if in_tm else bm_spec, _const_spec((1, D_MODEL)), _const_spec((D_MODEL, 2 * D_MODEL)),
                  _const_spec((1, D_MODEL)), _const_spec((HEADS_A, CHUNK, CHUNK)),
                  _const_spec((CHUNK, D_MODEL)), _const_spec((D_MODEL, D_MODEL))],
        out_specs=tm_spec if out_tm else bm_spec,
        out_shape=out_shape,
        scratch_shapes=[pltpu.VMEM((rows, D_MODEL), F32), pltpu.VMEM((rows, D_MODEL), BF16)],
        compiler_params=_params(),
        name="sgu_prompt",
    )(x, p["g"], p["win"], p["sg"], p["w_tril"], p["bias"], p["wout"])


def _sgu_sample(x, p, nb, steps):
    tile = steps * nb
    tile_spec = pl.BlockSpec((tile, D_MODEL), lambda i: (0, 0))
    return pl.pallas_call(
        functools.partial(_sgu_sample_kernel, nb=nb, steps=steps),
        grid=(1,),
        in_specs=[tile_spec, _const_spec((1, D_MODEL)), _const_spec((D_MODEL, 2 * D_MODEL)),
                  _const_spec((1, D_MODEL)), _const_spec((steps * steps, D_MODEL)),
                  _const_spec((steps, D_MODEL)), _const_spec((D_MODEL, D_MODEL))],
        out_specs=[tile_spec, tile_spec],
        out_shape=[jax.ShapeDtypeStruct((tile, D_MODEL), F32), jax.ShapeDtypeStruct((tile, D_MODEL), F32)],
        compiler_params=_params(),
        name="sgu_sample",
    )(x, p["g"], p["win"], p["sg"], p["wl"], p["bl"], p["wout"])


def _rglru_kernel(x_ref, g_ref, win_ref, cw_ref, cb_ref, wa_ref, ba_ref, wx_ref, bx_ref, lam_ref, wout_ref,
                  conv0_ref, h0_ref, o_ref, conv_ref, h_ref, xext_s, a_s, b_s, y_s, *, nb, steps):
    rows = nb * steps
    halo = (CONV_W - 1) * nb

    @pl.when(pl.program_id(0) == 0)
    def _():
        conv_ref[...] = conv0_ref[...]
        h_ref[...] = h0_ref[...]

    def blk_cols(j):
        return slice(j * V7X_MXU_DIM, (j + 1) * V7X_MXU_DIM)

    x = x_ref[...]
    h = _rms(x, g_ref[...]).astype(BF16)
    xext_s[0:halo, :] = conv_ref[...]
    for j in range(N_BLK):
        cols = blk_cols(j)
        xext_s[halo:halo + rows, cols] = _dot(h, win_ref[:, blk_cols(N_BLK + j)])
        gate = _gelu(_dot(h, win_ref[:, cols]))
        conv = None
        for w in range(CONV_W):
            term = xext_s[w * nb:w * nb + rows, cols] * cw_ref[w:w + 1, cols]
            conv = term if conv is None else conv + term
        xc = cb_ref[:, cols] + conv
        xcb = xc.astype(BF16)
        r = jax.nn.sigmoid(_dot(xcb, wa_ref[j]) + ba_ref[:, cols])
        ig = jax.nn.sigmoid(_dot(xcb, wx_ref[j]) + bx_ref[:, cols])
        neg_log_a = (LRU_C * r) * jax.nn.softplus(-lam_ref[:, cols])
        a = jnp.exp(-neg_log_a)
        a_s[:, cols] = a
        w = jnp.tanh(neg_log_a) * (a * a + 1.0)
        mult = jnp.where(w > 0.0, w * lax.rsqrt(w), 0.0)
        b_s[:, cols] = mult * (ig * xc)
        hcur = h_ref[:, cols]
        for t in range(steps):
            sl = slice(t * nb, (t + 1) * nb)
            hcur = a_s[sl, cols] * hcur + b_s[sl, cols]
            b_s[sl, cols] = hcur
        h_ref[:, cols] = hcur
        y_s[:, cols] = (b_s[:, cols] * gate).astype(BF16)
    conv_ref[...] = xext_s[rows:rows + halo, :]
    for c in range(N_BLK):
        o_ref[:, blk_cols(c)] = x[:, blk_cols(c)] + _dot(y_s[...], wout_ref[:, blk_cols(c)])


def _rglru(x, p, conv0, h0, nb, steps):
    rows = nb * steps
    halo = (CONV_W - 1) * nb
    row_spec = pl.BlockSpec((rows, D_MODEL), lambda i: (i, 0))
    vec = _const_spec((1, D_MODEL))
    blk = _const_spec((N_BLK, V7X_MXU_DIM, V7X_MXU_DIM))
    return pl.pallas_call(
        functools.partial(_rglru_kernel, nb=nb, steps=steps),
        grid=(x.shape[0] // rows,),
        in_specs=[row_spec, vec, _const_spec((D_MODEL, 2 * D_MODEL)), _const_spec((CONV_W, D_MODEL)), vec,
                  blk, vec, blk, vec, vec, _const_spec((D_MODEL, D_MODEL)),
                  _const_spec((halo, D_MODEL)), _const_spec((nb, D_MODEL))],
        out_specs=[row_spec, pl.BlockSpec((halo, D_MODEL), lambda i: (0, 0)),
                   pl.BlockSpec((nb, D_MODEL), lambda i: (0, 0))],
        out_shape=[jax.ShapeDtypeStruct(x.shape, F32), jax.ShapeDtypeStruct((halo, D_MODEL), F32),
                   jax.ShapeDtypeStruct((nb, D_MODEL), F32)],
        scratch_shapes=[pltpu.VMEM((rows + halo, D_MODEL), F32), pltpu.VMEM((rows, D_MODEL), F32),
                        pltpu.VMEM((rows, D_MODEL), F32), pltpu.VMEM((rows, D_MODEL), BF16)],
        compiler_params=_params(),
        name="rglru",
    )(x, p["g"], p["win"], p["cw"], p["cb"], p["wa"], p["ba"], p["wx"], p["bx"], p["lam"], p["wout"], conv0, h0)


def _s5_disc_kernel(lre_ref, lim_ref, ldt_ref, bre_ref, bim_ref, are_ref, aim_ref, bbre_ref, bbim_ref):
    lr, li = lre_ref[...], lim_ref[...]
    dt = jnp.exp(ldt_ref[...])
    mag = jnp.exp(lr * dt)
    ab_re, ab_im = mag * jnp.cos(li * dt), mag * jnp.sin(li * dt)
    zr, zi = ab_re - 1.0, ab_im
    den = lr * lr + li * li
    q_re = (zr * lr + zi * li) / den
    q_im = (zi * lr - zr * li) / den
    are_ref[...] = ab_re
    aim_ref[...] = ab_im
    br, bi = bre_ref[...], bim_ref[...]
    bbre_ref[...] = q_re[:, None, :] * br - q_im[:, None, :] * bi
    bbim_ref[...] = q_re[:, None, :] * bi + q_im[:, None, :] * br


def _s5_disc(lam_re, lam_im, log_dt, b_re_t, b_im_t):
    gp = jax.ShapeDtypeStruct((G_C, P_C), F32)
    ghp = jax.ShapeDtypeStruct((G_C, GROUP_C, P_C), F32)
    return pl.pallas_call(_s5_disc_kernel, out_shape=[gp, gp, ghp, ghp], name="s5_disc")(
        lam_re, lam_im, log_dt.reshape(G_C, 1), b_re_t, b_im_t)


def _s5_kernel(x_ref, g_ref, win_ref, bre_ref, bim_ref, are_ref, aim_ref, cre_ref, cim_ref, dskip_ref, wglu_ref,
               sre0_ref, sim0_ref, o_ref, sre_ref, sim_ref, xre_s, xim_s, *, nb, steps, lane_chunk):
    @pl.when(pl.program_id(0) == 0)
    def _():
        sre_ref[...] = sre0_ref[...]
        sim_ref[...] = sim0_ref[...]

    x = x_ref[...]
    h = _rms(x, g_ref[...]).astype(BF16)
    u = _dot(h, win_ref[...])
    ub = u.astype(BF16)

    def expand(j):
        ucols = slice(j * V7X_MXU_DIM, (j + 1) * V7X_MXU_DIM)
        scols = slice(j * STATE_PER_BLK, (j + 1) * STATE_PER_BLK)
        xre_s[:, scols] = _dot(ub[:, ucols], bre_ref[j])
        xim_s[:, scols] = _dot(ub[:, ucols], bim_ref[j])

    def scan(j):
        for c in range(STATE_PER_BLK // lane_chunk):
            cols = slice(j * STATE_PER_BLK + c * lane_chunk, j * STATE_PER_BLK + (c + 1) * lane_chunk)
            a_re = jnp.broadcast_to(are_ref[:, cols], (nb, lane_chunk))
            a_im = jnp.broadcast_to(aim_ref[:, cols], (nb, lane_chunk))
            s_re, s_im = sre_ref[:, cols], sim_ref[:, cols]
            for t in range(steps):
                sl = slice(t * nb, (t + 1) * nb)
                s_re, s_im = ((a_re * s_re - a_im * s_im) + xre_s[sl, cols],
                              (a_re * s_im + a_im * s_re) + xim_s[sl, cols])
                xre_s[sl, cols] = s_re
                xim_s[sl, cols] = s_im
            sre_ref[:, cols] = s_re
            sim_ref[:, cols] = s_im

    ys = []
    expand(0)
    for j in range(N_BLK):
        if j + 1 < N_BLK:
            expand(j + 1)
        scan(j)
        scols = slice(j * STATE_PER_BLK, (j + 1) * STATE_PER_BLK)
        ys.append(_dot(xre_s[:, scols].astype(BF16), cre_ref[j]) - _dot(xim_s[:, scols].astype(BF16), cim_ref[j]))
    y = jnp.concatenate(ys, axis=1) + dskip_ref[...] * u
    o = _dot(_gelu(y).astype(BF16), wglu_ref[...])
    o_ref[...] = x + o[:, :D_MODEL] * jax.nn.sigmoid(o[:, D_MODEL:])


def _s5(x, p, sre0, sim0, nb, steps, lane_chunk):
    rows = nb * steps
    row_spec = pl.BlockSpec((rows, D_MODEL), lambda i: (i, 0))
    vec = _const_spec((1, D_MODEL))
    svec = _const_spec((1, D_STATE))
    state_spec = _const_spec((nb, D_STATE))
    bspec = _const_spec((N_BLK, V7X_MXU_DIM, STATE_PER_BLK))
    cspec = _const_spec((N_BLK, STATE_PER_BLK, V7X_MXU_DIM))
    state_out = pl.BlockSpec((nb, D_STATE), lambda i: (0, 0))
    return pl.pallas_call(
        functools.partial(_s5_kernel, nb=nb, steps=steps, lane_chunk=lane_chunk),
        grid=(x.shape[0] // rows,),
        in_specs=[row_spec, vec, _const_spec((D_MODEL, D_MODEL)), bspec, bspec, svec, svec, cspec, cspec, vec,
                  _const_spec((D_MODEL, 2 * D_MODEL)), state_spec, state_spec],
        out_specs=[row_spec, state_out, state_out],
        out_shape=[jax.ShapeDtypeStruct(x.shape, F32), jax.ShapeDtypeStruct((nb, D_STATE), F32),
                   jax.ShapeDtypeStruct((nb, D_STATE), F32)],
        scratch_shapes=[pltpu.VMEM((rows, D_STATE), F32), pltpu.VMEM((rows, D_STATE), F32)],
        compiler_params=_params(),
        name="s5",
    )(x, p["g"], p["win"], p["bre"], p["bim"], p["are"], p["aim"], p["cre"], p["cim"], p["dskip"], p["wglu"],
      sre0, sim0)


def _block_diag(w, n_per_blk):
    n, k_in, k_out = w.shape
    wb = w.reshape(n // n_per_blk, n_per_blk, k_in, k_out)
    eye = jnp.eye(n_per_blk, dtype=w.dtype)
    out = jnp.einsum("jgio,gk->jgiko", wb, eye)
    return out.reshape(n // n_per_blk, n_per_blk * k_in, n_per_blk * k_out)


def _to_rows(a):
    return jnp.swapaxes(a, 0, 1).reshape(a.shape[0] * a.shape[1], a.shape[2])


def _from_rows(a, nb):
    return jnp.swapaxes(a.reshape(a.shape[0] // nb, nb, a.shape[1]), 0, 1)


def kernel(x_prompt, x_sample, state_rglru_conv, state_rglru_h, state_s5_re, state_s5_im, norm_mix, norm_ffn, norm_f, w_ff1, w_ff2, w_in_a, sgu_g, w_s, b_s, w_out_a, w_in_b, conv_w, conv_b, w_a, b_a, w_x, b_x, lam, w_out_b, w_in_c, lam_re, lam_im, log_dt, b_re, b_im, c_re, c_im, d_skip, w_glu):
    bp, tp, _ = x_prompt.shape
    bs, ts, _ = x_sample.shape
    rows_p, rows_s = bp * tp, bs * ts
    assert DEPTH % N_MIXERS == 1 and DEPTH > 1, "first and last layers must be SGU layers"
    assert tp % CHUNK == 0 and ts < CHUNK
    assert rows_s % FFN_ROWS == 0 and rows_p % FFN_ROWS == 0

    row = lambda v: v.reshape(1, -1)
    norm_ffn3 = norm_ffn.reshape(DEPTH, 1, D_MODEL)
    w1, w2, gf = w_ff1.astype(BF16), w_ff2.astype(BF16), row(norm_f)

    x_p, x_s = x_prompt, _to_rows(x_sample)
    outs_v, conv_p, h_p, conv_s, h_s, sre_p, sim_p, sre_s, sim_s = [], [], [], [], [], [], [], [], []
    for layer in range(DEPTH):
        j, kind = layer // N_MIXERS, layer % N_MIXERS
        first, last = layer == 0, layer == DEPTH - 1
        g = row(norm_mix[layer])
        if kind == 0:
            p = dict(g=g, win=w_in_a[j].astype(BF16), wout=w_out_a[j].astype(BF16), sg=row(sgu_g[j]),
                     w_tril=jnp.where(jnp.tril(jnp.ones((CHUNK, CHUNK), dtype=bool)), w_s[j], 0.0).astype(BF16),
                     bias=jnp.repeat(b_s[j].T, HD_A, axis=1),
                     wl=jnp.repeat(w_s[j][:, :ts, :ts].reshape(HEADS_A, ts * ts).T, HD_A, axis=1),
                     bl=jnp.repeat(b_s[j][:, :ts].T, HD_A, axis=1))
            if first:
                x_p = _sgu_prompt(x_p, p, bp, tp, False, True)
            elif last:
                x_p = _sgu_prompt(x_p, p, bp, tp, True, False).reshape(rows_p, D_MODEL)
            else:
                raise NotImplementedError("interior SGU layers")
            x_s, v = _sgu_sample(x_s, p, bs, ts)
            outs_v.append(_from_rows(v, bs))
        elif kind == 1:
            p = dict(g=g, win=w_in_b[j].astype(BF16), cw=conv_w[j], cb=row(conv_b[j]),
                     wa=_block_diag(w_a[j], HEADS_PER_BLK).astype(BF16), ba=row(b_a[j]),
                     wx=_block_diag(w_x[j], HEADS_PER_BLK).astype(BF16), bx=row(b_x[j]),
                     lam=row(lam[j]), wout=w_out_b[j].astype(BF16))
            dt_s = state_rglru_h.dtype
            x_p, cp, hp = _rglru(x_p, p, jnp.zeros(((CONV_W - 1) * bp, D_MODEL), dt_s),
                                 jnp.zeros((bp, D_MODEL), dt_s), bp, RGLRU_PROMPT_STEPS)
            x_s, cs, hs = _rglru(x_s, p, _to_rows(state_rglru_conv[j]), state_rglru_h[j], bs, ts)
            conv_p.append(_from_rows(cp, bp)); h_p.append(hp)
            conv_s.append(_from_rows(cs, bs)); h_s.append(hs)
        else:
            are, aim, bbre, bbim = _s5_disc(lam_re[j], lam_im[j], log_dt[j],
                                            jnp.swapaxes(b_re[j], 1, 2), jnp.swapaxes(b_im[j], 1, 2))
            p = dict(g=g, win=w_in_c[j].astype(BF16),
                     bre=_block_diag(bbre, GROUPS_PER_BLK).astype(BF16),
                     bim=_block_diag(bbim, GROUPS_PER_BLK).astype(BF16),
                     are=are.reshape(1, D_STATE), aim=aim.reshape(1, D_STATE),
                     cre=_block_diag(jnp.swapaxes(c_re[j], 1, 2), GROUPS_PER_BLK).astype(BF16),
                     cim=_block_diag(jnp.swapaxes(c_im[j], 1, 2), GROUPS_PER_BLK).astype(BF16),
                     dskip=row(d_skip[j]), wglu=w_glu[j].astype(BF16))
            zs = jnp.zeros((bp, D_STATE), state_s5_re.dtype)
            x_p, rp, ip = _s5(x_p, p, zs, zs, bp, S5_PROMPT_STEPS, 4 * V7X_LANES)
            x_s, rs, is_ = _s5(x_s, p, state_s5_re[j].reshape(bs, D_STATE), state_s5_im[j].reshape(bs, D_STATE),
                               bs, ts, V7X_LANES)
            sre_p.append(rp.reshape(bp, G_C, P_C)); sim_p.append(ip.reshape(bp, G_C, P_C))
            sre_s.append(rs.reshape(bs, G_C, P_C)); sim_s.append(is_.reshape(bs, G_C, P_C))
        x_p, x_s = _ffn(x_p, x_s, norm_ffn3, w1, w2, gf, layer, last)

    y_prompt = x_p.reshape(bp, tp, D_MODEL)
    y_sample = _from_rows(x_s, bs)
    return (y_prompt, y_sample, jnp.stack(outs_v), jnp.stack(conv_p), jnp.stack(h_p), jnp.stack(conv_s),
            jnp.stack(h_s), jnp.stack(sre_p), jnp.stack(sim_p), jnp.stack(sre_s), jnp.stack(sim_s))
```

```python
import functools

import jax
import jax.numpy as jnp
from jax import lax
from jax.experimental import pallas as pl
from jax.experimental.pallas import tpu as pltpu

F32 = jnp.float32
BF16 = jnp.bfloat16

D_MODEL = 1024
DEPTH = 4
N_MIXERS = 3
EPS = 1e-6
CHUNK = 128
HEADS_A = 8
HD_A = D_MODEL // HEADS_A
HEADS_B = 16
HD_B = D_MODEL // HEADS_B
CONV_W = 4
LRU_C = 8.0
GROUP_C = 16
G_C = D_MODEL // GROUP_C
P_C = 64
D_STATE = G_C * P_C
D_FF = 4 * D_MODEL

V7X_LANES = 128
V7X_MXU_DIM = 256
V7X_VMEM_BYTES = 64 * 1024 * 1024
VMEM_LIMIT = V7X_VMEM_BYTES - 8 * 1024 * 1024

N_BLK = D_MODEL // V7X_MXU_DIM
HEADS_PER_BLK = V7X_MXU_DIM // HD_B
GROUPS_PER_BLK = V7X_MXU_DIM // GROUP_C
STATE_PER_BLK = GROUPS_PER_BLK * P_C

FFN_ROWS = 512
FFN_COL_CHUNK = 1024
S5_PROMPT_STEPS = 32
RGLRU_PROMPT_STEPS = 64


def _rms(x, g):
    return (x * lax.rsqrt(jnp.mean(x * x, axis=-1, keepdims=True) + EPS)) * g


GELU_C0 = 0.7978845608028654
GELU_C1 = GELU_C0 * 0.044715


def _gelu(x):
    return x * (0.5 + 0.5 * jnp.tanh(x * (GELU_C0 + GELU_C1 * (x * x))))


def _dot(a, b):
    return jnp.dot(a, b, preferred_element_type=F32)


def _const_spec(shape):
    zeros = (0,) * len(shape)
    return pl.BlockSpec(shape, lambda i: zeros, pipeline_mode=pl.Buffered(1))


def _layer_spec(shape, layer):
    idx = (layer,) + (0,) * len(shape)
    return pl.BlockSpec((None,) + tuple(shape), lambda i: idx, pipeline_mode=pl.Buffered(1))


def _params():
    return pltpu.CompilerParams(dimension_semantics=("arbitrary",), vmem_limit_bytes=VMEM_LIMIT)


def _ffn_tile(x_ref, g_ref, w1_ref, w2_ref, gf_ref, o_ref, final_norm):
    x = x_ref[...]
    h = _rms(x, g_ref[...]).astype(BF16)
    acc = None
    for j in range(D_FF // FFN_COL_CHUNK):
        cols = slice(j * FFN_COL_CHUNK, (j + 1) * FFN_COL_CHUNK)
        a = jnp.square(jnp.maximum(_dot(h, w1_ref[:, cols]), 0.0)).astype(BF16)
        part = _dot(a, w2_ref[cols, :])
        acc = part if acc is None else acc + part
    y = x + acc
    if final_norm:
        y = _rms(y, gf_ref[...])
    o_ref[...] = y


def _ffn_kernel(xp_ref, xs_ref, g_ref, w1_ref, w2_ref, gf_ref, op_ref, os_ref, *, final_norm, n_p):
    i = pl.program_id(0)

    @pl.when(i < n_p)
    def _():
        _ffn_tile(xp_ref, g_ref, w1_ref, w2_ref, gf_ref, op_ref, final_norm)

    @pl.when(i >= n_p)
    def _():
        _ffn_tile(xs_ref, g_ref, w1_ref, w2_ref, gf_ref, os_ref, final_norm)


def _ffn(x_p, x_s, g, w1, w2, gf, layer, final_norm):
    n_p, n_s = x_p.shape[0] // FFN_ROWS, x_s.shape[0] // FFN_ROWS
    p_spec = pl.BlockSpec((FFN_ROWS, D_MODEL), lambda i: (jnp.minimum(i, n_p - 1), 0))
    s_spec = pl.BlockSpec((FFN_ROWS, D_MODEL), lambda i: (jnp.maximum(i - n_p, 0), 0))
    return pl.pallas_call(
        functools.partial(_ffn_kernel, final_norm=final_norm, n_p=n_p),
        grid=(n_p + n_s,),
        in_specs=[p_spec, s_spec, _layer_spec((1, D_MODEL), layer), _layer_spec((D_MODEL, D_FF), layer),
                  _layer_spec((D_FF, D_MODEL), layer), _const_spec((1, D_MODEL))],
        out_specs=[p_spec, s_spec],
        out_shape=[jax.ShapeDtypeStruct(x_p.shape, F32), jax.ShapeDtypeStruct(x_s.shape, F32)],
        compiler_params=_params(),
        name="ffn",
    )(x_p, x_s, g, w1, w2, gf)


def _sgu_front(x, g_ref, win_ref, sg_ref):
    h = _rms(x, g_ref[...]).astype(BF16)
    uv = _gelu(_dot(h, win_ref[...]))
    return uv[:, :D_MODEL], _rms(uv[:, D_MODEL:], sg_ref[...])


def _rows_to_batch_major(x, nb):
    steps = x.shape[0] // nb
    return jnp.swapaxes(x.reshape(steps, nb, x.shape[1]), 0, 1).reshape(x.shape)


def _rows_to_time_major(x, nb):
    steps = x.shape[0] // nb
    return jnp.swapaxes(x.reshape(nb, steps, x.shape[1]), 0, 1).reshape(x.shape)


def _sgu_prompt_kernel(x_ref, g_ref, win_ref, sg_ref, w_ref, bias_ref, wout_ref, o_ref, y_s, *, nb, in_tm, out_tm):
    rows = nb * CHUNK
    if in_tm:
        x = _rows_to_batch_major(x_ref[...], nb)
    else:
        x = x_ref[...].reshape(rows, D_MODEL)
    u, v = _sgu_front(x, g_ref, win_ref, sg_ref)
    vb = v.astype(BF16)
    for b in range(nb):
        rs = slice(b * CHUNK, (b + 1) * CHUNK)
        for g in range(HEADS_A):
            cs = slice(g * HD_A, (g + 1) * HD_A)
            mixed = _dot(w_ref[g], vb[rs, cs]) + bias_ref[:, cs]
            y_s[rs, cs] = (u[rs, cs] * mixed).astype(BF16)
    o = x + _dot(y_s[...], wout_ref[...])
    if out_tm:
        o_ref[...] = _rows_to_time_major(o, nb)
    else:
        o_ref[...] = o.reshape(nb, CHUNK, D_MODEL)


def _sgu_sample_kernel(x_ref, g_ref, win_ref, sg_ref, wl_ref, bl_ref, wout_ref, o_ref, v_ref, *, nb, steps):
    x = x_ref[...]
    u, v = _sgu_front(x, g_ref, win_ref, sg_ref)
    v_ref[...] = v
    mixed = []
    for t in range(steps):
        m = None
        for s in range(t + 1):
            term = wl_ref[t * steps + s:t * steps + s + 1, :] * v[s * nb:(s + 1) * nb, :]
            m = term if m is None else m + term
        mixed.append(m + bl_ref[t:t + 1, :])
    y = (u * jnp.concatenate(mixed, axis=0)).astype(BF16)
    o_ref[...] = x + _dot(y, wout_ref[...])


def _sgu_prompt(x, p, nb, n_steps, in_tm, out_tm):
    rows = nb * CHUNK
    tm_spec = pl.BlockSpec((rows, D_MODEL), lambda i: (i, 0))
    bm_spec = pl.BlockSpec((nb, CHUNK, D_MODEL), lambda i: (0, i, 0))
    out_shape = (jax.ShapeDtypeStruct((nb * n_steps, D_MODEL), F32) if out_tm
                 else jax.ShapeDtypeStruct((nb, n_steps, D_MODEL), F32))
    return pl.pallas_call(
        functools.partial(_sgu_prompt_kernel, nb=nb, in_tm=in_tm, out_tm=out_tm),
        grid=(n_steps // CHUNK,),
        in_specs=[tm_spec if in_tm else bm_spec, _const_spec((1, D_MODEL)), _const_spec((D_MODEL, 2 * D_MODEL)),
                  _const_spec((1, D_MODEL)), _const_spec((HEADS_A, CHUNK, CHUNK)),
                  _const_spec((CHUNK, D_MODEL)), _const_spec((D_MODEL, D_MODEL))],
        out_specs=tm_spec if out_tm else bm_spec,
        out_shape=out_shape,
        scratch_shapes=[pltpu.VMEM((rows, D_MODEL), BF16)],
        compiler_params=_params(),
        name="sgu_prompt",
    )(x, p["g"], p["win"], p["sg"], p["w_tril"], p["bias"], p["wout"])


def _sgu_sample(x, p, nb, steps):
    tile = steps * nb
    tile_spec = pl.BlockSpec((tile, D_MODEL), lambda i: (0, 0))
    return pl.pallas_call(
        functools.partial(_sgu_sample_kernel, nb=nb, steps=steps),
        grid=(1,),
        in_specs=[tile_spec, _const_spec((1, D_MODEL)), _const_spec((D_MODEL, 2 * D_MODEL)),
                  _const_spec((1, D_MODEL)), _const_spec((steps * steps, D_MODEL)),
                  _const_spec((steps, D_MODEL)), _const_spec((D_MODEL, D_MODEL))],
        out_specs=[tile_spec, tile_spec],
        out_shape=[jax.ShapeDtypeStruct((tile, D_MODEL), F32), jax.ShapeDtypeStruct((tile, D_MODEL), F32)],
        compiler_params=_params(),
        name="sgu_sample",
    )(x, p["g"], p["win"], p["sg"], p["wl"], p["bl"], p["wout"])


def _rglru_kernel(x_ref, g_ref, win_ref, cw_ref, cb_ref, wa_ref, ba_ref, wx_ref, bx_ref, lam_ref, wout_ref,
                  conv0_ref, h0_ref, o_ref, conv_ref, h_ref, xext_s, a_s, b_s, y_s, *, nb, steps):
    rows = nb * steps
    halo = (CONV_W - 1) * nb

    @pl.when(pl.program_id(0) == 0)
    def _():
        conv_ref[...] = conv0_ref[...]
        h_ref[...] = h0_ref[...]

    def blk_cols(j):
        return slice(j * V7X_MXU_DIM, (j + 1) * V7X_MXU_DIM)

    x = x_ref[...]
    h = _rms(x, g_ref[...]).astype(BF16)
    for j in range(N_BLK):
        cols = blk_cols(j)
        xext, a_j, b_j = xext_s.at[j], a_s.at[j], b_s.at[j]
        xext[0:halo, :] = conv_ref[:, cols]
        xext[halo:halo + rows, :] = _dot(h, win_ref[:, blk_cols(N_BLK + j)])
        gate = _gelu(_dot(h, win_ref[:, cols]))
        conv = None
        for w in range(CONV_W):
            term = xext[w * nb:w * nb + rows, :] * cw_ref[w:w + 1, cols]
            conv = term if conv is None else conv + term
        conv_ref[:, cols] = xext[rows:rows + halo, :]
        xc = cb_ref[:, cols] + conv
        xcb = xc.astype(BF16)
        r = jax.nn.sigmoid(_dot(xcb, wa_ref[j]) + ba_ref[:, cols])
        ig = jax.nn.sigmoid(_dot(xcb, wx_ref[j]) + bx_ref[:, cols])
        neg_log_a = (LRU_C * r) * jax.nn.softplus(-lam_ref[:, cols])
        a = jnp.exp(-neg_log_a)
        a_j[...] = a
        w = jnp.tanh(neg_log_a) * (a * a + 1.0)
        mult = jnp.where(w > 0.0, w * lax.rsqrt(w), 0.0)
        b_j[...] = mult * (ig * xc)
        hcur = h_ref[:, cols]
        for t in range(steps):
            sl = slice(t * nb, (t + 1) * nb)
            hcur = a_j[sl, :] * hcur + b_j[sl, :]
            b_j[sl, :] = hcur
        h_ref[:, cols] = hcur
        y_s[:, cols] = (b_j[...] * gate).astype(BF16)
    o_ref[...] = x + _dot(y_s[...], wout_ref[...])


def _rglru(x, p, conv0, h0, nb, steps):
    rows = nb * steps
    halo = (CONV_W - 1) * nb
    row_spec = pl.BlockSpec((rows, D_MODEL), lambda i: (i, 0))
    vec = _const_spec((1, D_MODEL))
    blk = _const_spec((N_BLK, V7X_MXU_DIM, V7X_MXU_DIM))
    return pl.pallas_call(
        functools.partial(_rglru_kernel, nb=nb, steps=steps),
        grid=(x.shape[0] // rows,),
        in_specs=[row_spec, vec, _const_spec((D_MODEL, 2 * D_MODEL)), _const_spec((CONV_W, D_MODEL)), vec,
                  blk, vec, blk, vec, vec, _const_spec((D_MODEL, D_MODEL)),
                  _const_spec((halo, D_MODEL)), _const_spec((nb, D_MODEL))],
        out_specs=[row_spec, pl.BlockSpec((halo, D_MODEL), lambda i: (0, 0)),
                   pl.BlockSpec((nb, D_MODEL), lambda i: (0, 0))],
        out_shape=[jax.ShapeDtypeStruct(x.shape, F32), jax.ShapeDtypeStruct((halo, D_MODEL), F32),
                   jax.ShapeDtypeStruct((nb, D_MODEL), F32)],
        scratch_shapes=[pltpu.VMEM((N_BLK, rows + halo, V7X_MXU_DIM), F32), pltpu.VMEM((N_BLK, rows, V7X_MXU_DIM), F32),
                        pltpu.VMEM((N_BLK, rows, V7X_MXU_DIM), F32), pltpu.VMEM((rows, D_MODEL), BF16)],
        compiler_params=_params(),
        name="rglru",
    )(x, p["g"], p["win"], p["cw"], p["cb"], p["wa"], p["ba"], p["wx"], p["bx"], p["lam"], p["wout"], conv0, h0)


def _s5_disc_kernel(lre_ref, lim_ref, ldt_ref, bre_ref, bim_ref, are_ref, aim_ref, bbre_ref, bbim_ref):
    lr, li = lre_ref[...], lim_ref[...]
    dt = jnp.exp(ldt_ref[...])
    mag = jnp.exp(lr * dt)
    ab_re, ab_im = mag * jnp.cos(li * dt), mag * jnp.sin(li * dt)
    zr, zi = ab_re - 1.0, ab_im
    den = lr * lr + li * li
    q_re = (zr * lr + zi * li) / den
    q_im = (zi * lr - zr * li) / den
    are_ref[...] = ab_re
    aim_ref[...] = ab_im
    br, bi = bre_ref[...], bim_ref[...]
    bbre_ref[...] = q_re[:, None, :] * br - q_im[:, None, :] * bi
    bbim_ref[...] = q_re[:, None, :] * bi + q_im[:, None, :] * br


def _s5_disc(lam_re, lam_im, log_dt, b_re_t, b_im_t):
    gp = jax.ShapeDtypeStruct((G_C, P_C), F32)
    ghp = jax.ShapeDtypeStruct((G_C, GROUP_C, P_C), F32)
    return pl.pallas_call(_s5_disc_kernel, out_shape=[gp, gp, ghp, ghp], name="s5_disc")(
        lam_re, lam_im, log_dt.reshape(G_C, 1), b_re_t, b_im_t)


def _s5_kernel(x_ref, g_ref, win_ref, bre_ref, bim_ref, are_ref, aim_ref, cre_ref, cim_ref, dskip_ref, wglu_ref,
               sre0_ref, sim0_ref, o_ref, sre_ref, sim_ref, xre_s, xim_s, *, nb, steps, lane_chunk):
    @pl.when(pl.program_id(0) == 0)
    def _():
        sre_ref[...] = sre0_ref[...]
        sim_ref[...] = sim0_ref[...]

    x = x_ref[...]
    h = _rms(x, g_ref[...]).astype(BF16)
    u = _dot(h, win_ref[...])
    ub = u.astype(BF16)

    def expand(j):
        ucols = slice(j * V7X_MXU_DIM, (j + 1) * V7X_MXU_DIM)
        scols = slice(j * STATE_PER_BLK, (j + 1) * STATE_PER_BLK)
        xre_s[:, scols] = _dot(ub[:, ucols], bre_ref[j])
        xim_s[:, scols] = _dot(ub[:, ucols], bim_ref[j])

    def scan(j):
        for c in range(STATE_PER_BLK // lane_chunk):
            cols = slice(j * STATE_PER_BLK + c * lane_chunk, j * STATE_PER_BLK + (c + 1) * lane_chunk)
            a_re = jnp.broadcast_to(are_ref[:, cols], (nb, lane_chunk))
            a_im = jnp.broadcast_to(aim_ref[:, cols], (nb, lane_chunk))
            s_re, s_im = sre_ref[:, cols], sim_ref[:, cols]
            for t in range(steps):
                sl = slice(t * nb, (t + 1) * nb)
                s_re, s_im = ((a_re * s_re - a_im * s_im) + xre_s[sl, cols],
                              (a_re * s_im + a_im * s_re) + xim_s[sl, cols])
                xre_s[sl, cols] = s_re
                xim_s[sl, cols] = s_im
            sre_ref[:, cols] = s_re
            sim_ref[:, cols] = s_im

    ys = []
    expand(0)
    for j in range(N_BLK):
        if j + 1 < N_BLK:
            expand(j + 1)
        scan(j)
        scols = slice(j * STATE_PER_BLK, (j + 1) * STATE_PER_BLK)
        ys.append(_dot(xre_s[:, scols].astype(BF16), cre_ref[j]) - _dot(xim_s[:, scols].astype(BF16), cim_ref[j]))
    y = jnp.concatenate(ys, axis=1) + dskip_ref[...] * u
    o = _dot(_gelu(y).astype(BF16), wglu_ref[...])
    o_ref[...] = x + o[:, :D_MODEL] * jax.nn.sigmoid(o[:, D_MODEL:])


def _s5(x, p, sre0, sim0, nb, steps, lane_chunk):
    rows = nb * steps
    row_spec = pl.BlockSpec((rows, D_MODEL), lambda i: (i, 0))
    vec = _const_spec((1, D_MODEL))
    svec = _const_spec((1, D_STATE))
    state_spec = _const_spec((nb, D_STATE))
    bspec = _const_spec((N_BLK, V7X_MXU_DIM, STATE_PER_BLK))
    cspec = _const_spec((N_BLK, STATE_PER_BLK, V7X_MXU_DIM))
    state_out = pl.BlockSpec((nb, D_STATE), lambda i: (0, 0))
    return pl.pallas_call(
        functools.partial(_s5_kernel, nb=nb, steps=steps, lane_chunk=lane_chunk),
        grid=(x.shape[0] // rows,),
        in_specs=[row_spec, vec, _const_spec((D_MODEL, D_MODEL)), bspec, bspec, svec, svec, cspec, cspec, vec,
                  _const_spec((D_MODEL, 2 * D_MODEL)), state_spec, state_spec],
        out_specs=[row_spec, state_out, state_out],
        out_shape=[jax.ShapeDtypeStruct(x.shape, F32), jax.ShapeDtypeStruct((nb, D_STATE), F32),
                   jax.ShapeDtypeStruct((nb, D_STATE), F32)],
        scratch_shapes=[pltpu.VMEM((rows, D_STATE), F32), pltpu.VMEM((rows, D_STATE), F32)],
        compiler_params=_params(),
        name="s5",
    )(x, p["g"], p["win"], p["bre"], p["bim"], p["are"], p["aim"], p["cre"], p["cim"], p["dskip"], p["wglu"],
      sre0, sim0)


def _block_diag(w, n_per_blk):
    n, k_in, k_out = w.shape
    wb = w.reshape(n // n_per_blk, n_per_blk, k_in, k_out)
    eye = jnp.eye(n_per_blk, dtype=w.dtype)
    out = jnp.einsum("jgio,gk->jgiko", wb, eye)
    return out.reshape(n // n_per_blk, n_per_blk * k_in, n_per_blk * k_out)


def _to_rows(a):
    return jnp.swapaxes(a, 0, 1).reshape(a.shape[0] * a.shape[1], a.shape[2])


def _from_rows(a, nb):
    return jnp.swapaxes(a.reshape(a.shape[0] // nb, nb, a.shape[1]), 0, 1)


def kernel(x_prompt, x_sample, state_rglru_conv, state_rglru_h, state_s5_re, state_s5_im, norm_mix, norm_ffn, norm_f, w_ff1, w_ff2, w_in_a, sgu_g, w_s, b_s, w_out_a, w_in_b, conv_w, conv_b, w_a, b_a, w_x, b_x, lam, w_out_b, w_in_c, lam_re, lam_im, log_dt, b_re, b_im, c_re, c_im, d_skip, w_glu):
    bp, tp, _ = x_prompt.shape
    bs, ts, _ = x_sample.shape
    rows_p, rows_s = bp * tp, bs * ts
    assert DEPTH % N_MIXERS == 1 and DEPTH > 1, "first and last layers must be SGU layers"
    assert tp % CHUNK == 0 and ts < CHUNK
    assert rows_s % FFN_ROWS == 0 and rows_p % FFN_ROWS == 0

    row = lambda v: v.reshape(1, -1)
    norm_ffn3 = norm_ffn.reshape(DEPTH, 1, D_MODEL)
    w1, w2, gf = w_ff1.astype(BF16), w_ff2.astype(BF16), row(norm_f)

    x_p, x_s = x_prompt, _to_rows(x_sample)
    outs_v, conv_p, h_p, conv_s, h_s, sre_p, sim_p, sre_s, sim_s = [], [], [], [], [], [], [], [], []
    for layer in range(DEPTH):
        j, kind = layer // N_MIXERS, layer % N_MIXERS
        first, last = layer == 0, layer == DEPTH - 1
        g = row(norm_mix[layer])
        if kind == 0:
            p = dict(g=g, win=w_in_a[j].astype(BF16), wout=w_out_a[j].astype(BF16), sg=row(sgu_g[j]),
                     w_tril=jnp.where(jnp.tril(jnp.ones((CHUNK, CHUNK), dtype=bool)), w_s[j], 0.0).astype(BF16),
                     bias=jnp.repeat(b_s[j].T, HD_A, axis=1),
                     wl=jnp.repeat(w_s[j][:, :ts, :ts].reshape(HEADS_A, ts * ts).T, HD_A, axis=1),
                     bl=jnp.repeat(b_s[j][:, :ts].T, HD_A, axis=1))
            if first:
                x_p = _sgu_prompt(x_p, p, bp, tp, False, True)
            elif last:
                x_p = _sgu_prompt(x_p, p, bp, tp, True, False).reshape(rows_p, D_MODEL)
            else:
                raise NotImplementedError("interior SGU layers")
            x_s, v = _sgu_sample(x_s, p, bs, ts)
            outs_v.append(_from_rows(v, bs))
        elif kind == 1:
            p = dict(g=g, win=w_in_b[j].astype(BF16), cw=conv_w[j], cb=row(conv_b[j]),
                     wa=_block_diag(w_a[j], HEADS_PER_BLK).astype(BF16), ba=row(b_a[j]),
                     wx=_block_diag(w_x[j], HEADS_PER_BLK).astype(BF16), bx=row(b_x[j]),
                     lam=row(lam[j]), wout=w_out_b[j].astype(BF16))
            dt_s = state_rglru_h.dtype
            x_p, cp, hp = _rglru(x_p, p, jnp.zeros(((CONV_W - 1) * bp, D_MODEL), dt_s),
                                 jnp.zeros((bp, D_MODEL), dt_s), bp, RGLRU_PROMPT_STEPS)
            x_s, cs, hs = _rglru(x_s, p, _to_rows(state_rglru_conv[j]), state_rglru_h[j], bs, ts)
            conv_p.append(_from_rows(cp, bp)); h_p.append(hp)
            conv_s.append(_from_rows(cs, bs)); h_s.append(hs)
        else:
            are, aim, bbre, bbim = _s5_disc(lam_re[j], lam_im[j], log_dt[j],
                                            jnp.swapaxes(b_re[j], 1, 2), jnp.swapaxes(b_im[j], 1, 2))
            p = dict(g=g, win=w_in_c[j].astype(BF16),
                     bre=_block_diag(bbre, GROUPS_PER_BLK).astype(BF16),
                     bim=_block_diag(bbim, GROUPS_PER_BLK).astype(BF16),
                     are=are.reshape(1, D_STATE), aim=aim.reshape(1, D_STATE),
                     cre=_block_diag(jnp.swapaxes(c_re[j], 1, 2), GROUPS_PER_BLK).astype(BF16),
                     cim=_block_diag(jnp.swapaxes(c_im[j], 1, 2), GROUPS_PER_BLK).astype(BF16),
                     dskip=row(d_skip[j]), wglu=w_glu[j].astype(BF16))
            zs = jnp.zeros((bp, D_STATE), state_s5_re.dtype)
            x_p, rp, ip = _s5(x_p, p, zs, zs, bp, S5_PROMPT_STEPS, 4 * V7X_LANES)
            x_s, rs, is_ = _s5(x_s, p, state_s5_re[j].reshape(bs, D_STATE), state_s5_im[j].reshape(bs, D_STATE),
                               bs, ts, V7X_LANES)
            sre_p.append(rp.reshape(bp, G_C, P_C)); sim_p.append(ip.reshape(bp, G_C, P_C))
            sre_s.append(rs.reshape(bs, G_C, P_C)); sim_s.append(is_.reshape(bs, G_C, P_C))
        x_p, x_s = _ffn(x_p, x_s, norm_ffn3, w1, w2, gf, layer, last)

    y_prompt = x_p.reshape(bp, tp, D_MODEL)
    y_sample = _from_rows(x_s, bs)
    return (y_prompt, y_sample, jnp.stack(outs_v), jnp.stack(conv_p), jnp.stack(h_p), jnp.stack(conv_s),
            jnp.stack(h_s), jnp.stack(sre_p), jnp.stack(sim_p), jnp.stack(sre_s), jnp.stack(sim_s))
```

```python
import functools

import jax
import jax.numpy as jnp
from jax import lax
from jax.experimental import pallas as pl
from jax.experimental.pallas import tpu as pltpu

F32 = jnp.float32
BF16 = jnp.bfloat16

D_MODEL = 1024
DEPTH = 4
N_MIXERS = 3
EPS = 1e-6
CHUNK = 128
HEADS_A = 8
HD_A = D_MODEL // HEADS_A
HEADS_B = 16
HD_B = D_MODEL // HEADS_B
CONV_W = 4
LRU_C = 8.0
GROUP_C = 16
G_C = D_MODEL // GROUP_C
P_C = 64
D_STATE = G_C * P_C
D_FF = 4 * D_MODEL

V7X_LANES = 128
V7X_MXU_DIM = 256
V7X_VMEM_BYTES = 64 * 1024 * 1024
VMEM_LIMIT = V7X_VMEM_BYTES - 8 * 1024 * 1024

N_BLK = D_MODEL // V7X_MXU_DIM
HEADS_PER_BLK = V7X_MXU_DIM // HD_B
GROUPS_PER_BLK = V7X_MXU_DIM // GROUP_C
STATE_PER_BLK = GROUPS_PER_BLK * P_C

FFN_ROWS = 512
FFN_COL_CHUNK = 1024
S5_L = 16
S5_TILE_BLOCKS = 4
S5_PAIRS_PER_STEP = 2
GRANULES_PER_VREG = V7X_LANES // GROUP_C
RGLRU_PROMPT_STEPS = 64


def _rms(x, g):
    return (x * lax.rsqrt(jnp.mean(x * x, axis=-1, keepdims=True) + EPS)) * g


GELU_C0 = 0.7978845608028654
GELU_C1 = GELU_C0 * 0.044715


def _gelu(x):
    return x * (0.5 + 0.5 * jnp.tanh(x * (GELU_C0 + GELU_C1 * (x * x))))


def _dot(a, b):
    return jnp.dot(a, b, preferred_element_type=F32)


def _const_spec(shape):
    zeros = (0,) * len(shape)
    return pl.BlockSpec(shape, lambda i: zeros, pipeline_mode=pl.Buffered(1))


def _layer_spec(shape, layer):
    idx = (layer,) + (0,) * len(shape)
    return pl.BlockSpec((None,) + tuple(shape), lambda i: idx, pipeline_mode=pl.Buffered(1))


def _params():
    return pltpu.CompilerParams(dimension_semantics=("arbitrary",), vmem_limit_bytes=VMEM_LIMIT)


def _ffn_tile(x_ref, g_ref, w1_ref, w2_ref, gf_ref, o_ref, final_norm):
    x = x_ref[...]
    h = _rms(x, g_ref[...]).astype(BF16)
    acc = None
    for j in range(D_FF // FFN_COL_CHUNK):
        cols = slice(j * FFN_COL_CHUNK, (j + 1) * FFN_COL_CHUNK)
        a = jnp.square(jnp.maximum(_dot(h, w1_ref[:, cols]), 0.0)).astype(BF16)
        part = _dot(a, w2_ref[cols, :])
        acc = part if acc is None else acc + part
    y = x + acc
    if final_norm:
        y = _rms(y, gf_ref[...])
    o_ref[...] = y


def _ffn_kernel(xp_ref, xs_ref, g_ref, w1_ref, w2_ref, gf_ref, op_ref, os_ref, *, final_norm, n_p):
    i = pl.program_id(0)

    @pl.when(i < n_p)
    def _():
        _ffn_tile(xp_ref, g_ref, w1_ref, w2_ref, gf_ref, op_ref, final_norm)

    @pl.when(i >= n_p)
    def _():
        _ffn_tile(xs_ref, g_ref, w1_ref, w2_ref, gf_ref, os_ref, final_norm)


def _ffn(x_p, x_s, g, w1, w2, gf, layer, final_norm):
    n_p, n_s = x_p.shape[0] // FFN_ROWS, x_s.shape[0] // FFN_ROWS
    p_spec = pl.BlockSpec((FFN_ROWS, D_MODEL), lambda i: (jnp.minimum(i, n_p - 1), 0))
    s_spec = pl.BlockSpec((FFN_ROWS, D_MODEL), lambda i: (jnp.maximum(i - n_p, 0), 0))
    return pl.pallas_call(
        functools.partial(_ffn_kernel, final_norm=final_norm, n_p=n_p),
        grid=(n_p + n_s,),
        in_specs=[p_spec, s_spec, _layer_spec((1, D_MODEL), layer), _layer_spec((D_MODEL, D_FF), layer),
                  _layer_spec((D_FF, D_MODEL), layer), _const_spec((1, D_MODEL))],
        out_specs=[p_spec, s_spec],
        out_shape=[jax.ShapeDtypeStruct(x_p.shape, F32), jax.ShapeDtypeStruct(x_s.shape, F32)],
        compiler_params=_params(),
        name="ffn",
    )(x_p, x_s, g, w1, w2, gf)


def _sgu_front(x, g_ref, win_ref, sg_ref):
    h = _rms(x, g_ref[...]).astype(BF16)
    uv = _gelu(_dot(h, win_ref[...]))
    return uv[:, :D_MODEL], _rms(uv[:, D_MODEL:], sg_ref[...])


def _rows_to_batch_major(x, nb):
    steps = x.shape[0] // nb
    return jnp.swapaxes(x.reshape(steps, nb, x.shape[1]), 0, 1).reshape(x.shape)


def _rows_to_time_major(x, nb):
    steps = x.shape[0] // nb
    return jnp.swapaxes(x.reshape(nb, steps, x.shape[1]), 0, 1).reshape(x.shape)


def _sgu_prompt_kernel(x_ref, g_ref, win_ref, sg_ref, w_ref, bias_ref, wout_ref, o_ref, y_s, *, nb, in_tm, out_tm):
    rows = nb * CHUNK
    if in_tm:
        x = _rows_to_batch_major(x_ref[...], nb)
    else:
        x = x_ref[...].reshape(rows, D_MODEL)
    u, v = _sgu_front(x, g_ref, win_ref, sg_ref)
    vb = v.astype(BF16)
    for b in range(nb):
        rs = slice(b * CHUNK, (b + 1) * CHUNK)
        for g in range(HEADS_A):
            cs = slice(g * HD_A, (g + 1) * HD_A)
            mixed = _dot(w_ref[g], vb[rs, cs]) + bias_ref[:, cs]
            y_s[rs, cs] = (u[rs, cs] * mixed).astype(BF16)
    o = x + _dot(y_s[...], wout_ref[...])
    if out_tm:
        o_ref[...] = _rows_to_time_major(o, nb)
    else:
        o_ref[...] = o.reshape(nb, CHUNK, D_MODEL)


def _sgu_sample_kernel(x_ref, g_ref, win_ref, sg_ref, wl_ref, bl_ref, wout_ref, o_ref, v_ref, *, nb, steps):
    x = x_ref[...]
    u, v = _sgu_front(x, g_ref, win_ref, sg_ref)
    v_ref[...] = v
    mixed = []
    for t in range(steps):
        m = None
        for s in range(t + 1):
            term = wl_ref[t * steps + s:t * steps + s + 1, :] * v[s * nb:(s + 1) * nb, :]
            m = term if m is None else m + term
        mixed.append(m + bl_ref[t:t + 1, :])
    y = (u * jnp.concatenate(mixed, axis=0)).astype(BF16)
    o_ref[...] = x + _dot(y, wout_ref[...])


def _sgu_prompt(x, p, nb, n_steps, in_tm, out_tm):
    rows = nb * CHUNK
    tm_spec = pl.BlockSpec((rows, D_MODEL), lambda i: (i, 0))
    bm_spec = pl.BlockSpec((nb, CHUNK, D_MODEL), lambda i: (0, i, 0))
    out_shape = (jax.ShapeDtypeStruct((nb * n_steps, D_MODEL), F32) if out_tm
                 else jax.ShapeDtypeStruct((nb, n_steps, D_MODEL), F32))
    return pl.pallas_call(
        functools.partial(_sgu_prompt_kernel, nb=nb, in_tm=in_tm, out_tm=out_tm),
        grid=(n_steps // CHUNK,),
        in_specs=[tm_spec if in_tm else bm_spec, _const_spec((1, D_MODEL)), _const_spec((D_MODEL, 2 * D_MODEL)),
                  _const_spec((1, D_MODEL)), _const_spec((HEADS_A, CHUNK, CHUNK)),
                  _const_spec((CHUNK, D_MODEL)), _const_spec((D_MODEL, D_MODEL))],
        out_specs=tm_spec if out_tm else bm_spec,
        out_shape=out_shape,
        scratch_shapes=[pltpu.VMEM((rows, D_MODEL), BF16)],
        compiler_params=_params(),
        name="sgu_prompt",
    )(x, p["g"], p["win"], p["sg"], p["w_tril"], p["bias"], p["wout"])


def _sgu_sample(x, p, nb, steps):
    tile = steps * nb
    tile_spec = pl.BlockSpec((tile, D_MODEL), lambda i: (0, 0))
    return pl.pallas_call(
        functools.partial(_sgu_sample_kernel, nb=nb, steps=steps),
        grid=(1,),
        in_specs=[tile_spec, _const_spec((1, D_MODEL)), _const_spec((D_MODEL, 2 * D_MODEL)),
                  _const_spec((1, D_MODEL)), _const_spec((steps * steps, D_MODEL)),
                  _const_spec((steps, D_MODEL)), _const_spec((D_MODEL, D_MODEL))],
        out_specs=[tile_spec, tile_spec],
        out_shape=[jax.ShapeDtypeStruct((tile, D_MODEL), F32), jax.ShapeDtypeStruct((tile, D_MODEL), F32)],
        compiler_params=_params(),
        name="sgu_sample",
    )(x, p["g"], p["win"], p["sg"], p["wl"], p["bl"], p["wout"])


def _rglru_kernel(x_ref, g_ref, win_ref, cw_ref, cb_ref, wa_ref, ba_ref, wx_ref, bx_ref, lam_ref, wout_ref,
                  conv0_ref, h0_ref, o_ref, conv_ref, h_ref, xext_s, a_s, b_s, y_s, *, nb, steps):
    rows = nb * steps
    halo = (CONV_W - 1) * nb

    @pl.when(pl.program_id(0) == 0)
    def _():
        conv_ref[...] = conv0_ref[...]
        h_ref[...] = h0_ref[...]

    def blk_cols(j):
        return slice(j * V7X_MXU_DIM, (j + 1) * V7X_MXU_DIM)

    x = x_ref[...]
    h = _rms(x, g_ref[...]).astype(BF16)
    for j in range(N_BLK):
        cols = blk_cols(j)
        xext, a_j, b_j = xext_s.at[j], a_s.at[j], b_s.at[j]
        xext[0:halo, :] = conv_ref[:, cols]
        xext[halo:halo + rows, :] = _dot(h, win_ref[:, blk_cols(N_BLK + j)])
        gate = _gelu(_dot(h, win_ref[:, cols]))
        conv = None
        for w in range(CONV_W):
            term = xext[w * nb:w * nb + rows, :] * cw_ref[w:w + 1, cols]
            conv = term if conv is None else conv + term
        conv_ref[:, cols] = xext[rows:rows + halo, :]
        xc = cb_ref[:, cols] + conv
        xcb = xc.astype(BF16)
        r = jax.nn.sigmoid(_dot(xcb, wa_ref[j]) + ba_ref[:, cols])
        ig = jax.nn.sigmoid(_dot(xcb, wx_ref[j]) + bx_ref[:, cols])
        neg_log_a = (LRU_C * r) * jax.nn.softplus(-lam_ref[:, cols])
        a = jnp.exp(-neg_log_a)
        a_j[...] = a
        w = jnp.tanh(neg_log_a) * (a * a + 1.0)
        mult = jnp.where(w > 0.0, w * lax.rsqrt(w), 0.0)
        b_j[...] = mult * (ig * xc)
        hcur = h_ref[:, cols]
        for t in range(steps):
            sl = slice(t * nb, (t + 1) * nb)
            hcur = a_j[sl, :] * hcur + b_j[sl, :]
            b_j[sl, :] = hcur
        h_ref[:, cols] = hcur
        y_s[:, cols] = (b_j[...] * gate).astype(BF16)
    o_ref[...] = x + _dot(y_s[...], wout_ref[...])


def _rglru(x, p, conv0, h0, nb, steps):
    rows = nb * steps
    halo = (CONV_W - 1) * nb
    row_spec = pl.BlockSpec((rows, D_MODEL), lambda i: (i, 0))
    vec = _const_spec((1, D_MODEL))
    blk = _const_spec((N_BLK, V7X_MXU_DIM, V7X_MXU_DIM))
    return pl.pallas_call(
        functools.partial(_rglru_kernel, nb=nb, steps=steps),
        grid=(x.shape[0] // rows,),
        in_specs=[row_spec, vec, _const_spec((D_MODEL, 2 * D_MODEL)), _const_spec((CONV_W, D_MODEL)), vec,
                  blk, vec, blk, vec, vec, _const_spec((D_MODEL, D_MODEL)),
                  _const_spec((halo, D_MODEL)), _const_spec((nb, D_MODEL))],
        out_specs=[row_spec, pl.BlockSpec((halo, D_MODEL), lambda i: (0, 0)),
                   pl.BlockSpec((nb, D_MODEL), lambda i: (0, 0))],
        out_shape=[jax.ShapeDtypeStruct(x.shape, F32), jax.ShapeDtypeStruct((halo, D_MODEL), F32),
                   jax.ShapeDtypeStruct((nb, D_MODEL), F32)],
        scratch_shapes=[pltpu.VMEM((N_BLK, rows + halo, V7X_MXU_DIM), F32), pltpu.VMEM((N_BLK, rows, V7X_MXU_DIM), F32),
                        pltpu.VMEM((N_BLK, rows, V7X_MXU_DIM), F32), pltpu.VMEM((rows, D_MODEL), BF16)],
        compiler_params=_params(),
        name="rglru",
    )(x, p["g"], p["win"], p["cw"], p["cb"], p["wa"], p["ba"], p["wx"], p["bx"], p["lam"], p["wout"], conv0, h0)


def _cmul(a_re, a_im, b_re, b_im):
    return a_re * b_re - a_im * b_im, a_re * b_im + a_im * b_re


def _s5_disc_kernel(lre_ref, lim_ref, ldt_ref, bre_ref, bim_ref, are_ref, aim_ref, bbre_ref, bbim_ref, pw_ref):
    lr, li = lre_ref[...], lim_ref[...]
    dt = jnp.exp(ldt_ref[...])
    mag = jnp.exp(lr * dt)
    ab_re, ab_im = mag * jnp.cos(li * dt), mag * jnp.sin(li * dt)
    zr, zi = ab_re - 1.0, ab_im
    den = lr * lr + li * li
    q_re = (zr * lr + zi * li) / den
    q_im = (zi * lr - zr * li) / den
    are_ref[...] = ab_re
    aim_ref[...] = ab_im
    br, bi = bre_ref[...], bim_ref[...]
    bbre_ref[...] = q_re[:, None, :] * br - q_im[:, None, :] * bi
    bbim_ref[...] = q_re[:, None, :] * bi + q_im[:, None, :] * br
    p_re, p_im = jnp.ones_like(ab_re), jnp.zeros_like(ab_re)
    for k in range(S5_L + 1):
        pw_ref[0, k] = p_re
        pw_ref[1, k] = p_im
        p_re, p_im = _cmul(p_re, p_im, ab_re, ab_im)


def _s5_disc(lam_re, lam_im, log_dt, b_re_t, b_im_t):
    gp = jax.ShapeDtypeStruct((G_C, P_C), F32)
    ghp = jax.ShapeDtypeStruct((G_C, GROUP_C, P_C), F32)
    pw = jax.ShapeDtypeStruct((2, S5_L + 1, G_C, P_C), F32)
    return pl.pallas_call(_s5_disc_kernel, out_shape=[gp, gp, ghp, ghp, pw], name="s5_disc")(
        lam_re, lam_im, log_dt.reshape(G_C, 1), b_re_t, b_im_t)


def _s5_taps_kernel(pw0_ref, pw1_ref, pwf_ref, c_ref, b_ref, kconv_ref, wst_ref, kin_ref, *, groups):
    n = S5_L * GROUP_C

    def rep(t):
        return jnp.broadcast_to(t[:, None, :], (S5_L, GROUP_C, P_C)).reshape(n, P_C)

    def til(c):
        return jnp.broadcast_to(c[None, :, :], (S5_L, GROUP_C, P_C)).reshape(n, P_C)

    def dot_t(a, b):
        return lax.dot_general(a, b, (((1,), (1,)), ((), ())), precision=lax.Precision.HIGHEST,
                               preferred_element_type=F32)

    lane = lax.broadcasted_iota(jnp.int32, (GROUP_C, n), 1)
    for gi in range(groups):
        cr, ci = til(c_ref[gi, 0]), til(c_ref[gi, 1])
        b_re, b_im = b_ref[gi, 0], b_ref[gi, 1]
        x_re, x_im = _cmul(rep(pw0_ref[gi, 0]), rep(pw0_ref[gi, 1]), cr, ci)
        mrow = dot_t(b_re, x_re) - dot_t(b_im, x_im)
        for s in range(S5_L):
            blk = mrow if s == 0 else jnp.where(lane >= s * GROUP_C, pltpu.roll(mrow, s * GROUP_C, 1), 0.0)
            kconv_ref[gi, s * GROUP_C:(s + 1) * GROUP_C, :] = blk.astype(BF16)
        e_re, e_im = _cmul(rep(pw1_ref[gi, 0]), rep(pw1_ref[gi, 1]), cr, ci)
        kin_ref[gi, 0] = e_re.T.astype(BF16)
        kin_ref[gi, 1] = (-e_im).T.astype(BF16)
        f_re, f_im = _cmul(rep(pwf_ref[gi, 0]), rep(pwf_ref[gi, 1]), til(b_re), til(b_im))
        wst_ref[gi, 0] = f_re.astype(BF16)
        wst_ref[gi, 1] = f_im.astype(BF16)


def _s5_taps(pw0, pw1, pwf, c, bb):
    groups = 8
    n = S5_L * GROUP_C
    pspec = pl.BlockSpec((groups, 2, S5_L, P_C), lambda i: (i, 0, 0, 0))
    hspec = pl.BlockSpec((groups, 2, GROUP_C, P_C), lambda i: (i, 0, 0, 0))
    return pl.pallas_call(
        functools.partial(_s5_taps_kernel, groups=groups),
        grid=(G_C // groups,),
        in_specs=[pspec, pspec, pspec, hspec, hspec],
        out_specs=[pl.BlockSpec((groups, n, n), lambda i: (i, 0, 0)),
                   pl.BlockSpec((groups, 2, n, P_C), lambda i: (i, 0, 0, 0)),
                   pl.BlockSpec((groups, 2, P_C, n), lambda i: (i, 0, 0, 0))],
        out_shape=[jax.ShapeDtypeStruct((G_C, n, n), BF16), jax.ShapeDtypeStruct((G_C, 2, n, P_C), BF16),
                   jax.ShapeDtypeStruct((G_C, 2, P_C, n), BF16)],
        compiler_params=_params(),
        name="s5_taps",
    )(pw0, pw1, pwf, c, bb)


def _granule_transpose(vs):
    vs = list(vs)
    granule = lax.broadcasted_iota(jnp.int32, vs[0].shape, 1) // GROUP_C
    d = 1
    while d < len(vs):
        keep = (granule & d) == 0
        for i in range(len(vs)):
            if i & d == 0:
                a, b = vs[i], vs[i + d]
                vs[i] = jnp.where(keep, a, pltpu.roll(b, d * GROUP_C, 1))
                vs[i + d] = jnp.where(keep, pltpu.roll(a, V7X_LANES - d * GROUP_C, 1), b)
        d *= 2
    return vs


def _s5_in_kernel(x_ref, g_ref, win_ref, u_ref, u2_ref, *, nb, blocks):
    h = _rms(x_ref[...], g_ref[...]).astype(BF16)
    rows2 = blocks * nb
    for cc in range(N_BLK):
        cols = slice(cc * V7X_MXU_DIM, (cc + 1) * V7X_MXU_DIM)
        u = _dot(h, win_ref[:, cols])
        u_ref[:, cols] = u
        u4 = u.reshape(blocks, S5_L, nb, V7X_MXU_DIM)
        for sub in range(V7X_MXU_DIM // V7X_LANES):
            c = cc * (V7X_MXU_DIM // V7X_LANES) + sub
            for half in range(S5_L // GRANULES_PER_VREG):
                vs = []
                for kk in range(GRANULES_PER_VREG):
                    slab = u4[:, half * GRANULES_PER_VREG + kk, :, sub * V7X_LANES:(sub + 1) * V7X_LANES]
                    vs.append(pltpu.bitcast(slab.reshape(rows2, V7X_LANES).astype(BF16), jnp.uint32))
                ws = _granule_transpose(vs)
                for gi in range(GRANULES_PER_VREG):
                    lane0 = (c * GRANULES_PER_VREG + gi) * S5_L * GROUP_C + half * V7X_LANES
                    u2_ref[:, lane0:lane0 + V7X_LANES] = pltpu.bitcast(ws[gi], BF16)


def _s5_in(x, g, win, nb, blocks):
    rows = blocks * S5_L * nb
    n_blocks = x.shape[0] // (S5_L * nb)
    row_spec = pl.BlockSpec((rows, D_MODEL), lambda i: (i, 0))
    return pl.pallas_call(
        functools.partial(_s5_in_kernel, nb=nb, blocks=blocks),
        grid=(x.shape[0] // rows,),
        in_specs=[row_spec, _const_spec((1, D_MODEL)), _const_spec((D_MODEL, D_MODEL))],
        out_specs=[row_spec, pl.BlockSpec((blocks * nb, S5_L * D_MODEL), lambda i: (i, 0))],
        out_shape=[jax.ShapeDtypeStruct(x.shape, F32),
                   jax.ShapeDtypeStruct((n_blocks * nb, S5_L * D_MODEL), BF16)],
        compiler_params=_params(),
        name="s5_in",
    )(x, g, win)


def _s5_core_kernel(u_ref, wst_ref, kconv_ref, kin_ref, a_ref, s0_ref, y_ref, sf_ref, sc_s, sin_s, *, nb, n_blocks):
    pw = 2 * S5_L * GROUP_C
    sw = 2 * P_C
    ub = [u_ref[:, q * pw:(q + 1) * pw] for q in range(S5_PAIRS_PER_STEP)]
    for q in range(S5_PAIRS_PER_STEP):
        sc_s[q] = _dot(ub[q], wst_ref[q])
    a_re = [jnp.broadcast_to(a_ref[:, 2 * sw * q:2 * sw * q + sw], (nb, sw)) for q in range(S5_PAIRS_PER_STEP)]
    a_im = [jnp.broadcast_to(a_ref[:, 2 * sw * q + sw:2 * sw * (q + 1)], (nb, sw)) for q in range(S5_PAIRS_PER_STEP)]
    s_re = [s0_ref[:, 2 * sw * q:2 * sw * q + sw] for q in range(S5_PAIRS_PER_STEP)]
    s_im = [s0_ref[:, 2 * sw * q + sw:2 * sw * (q + 1)] for q in range(S5_PAIRS_PER_STEP)]
    for blk in range(n_blocks):
        rs = slice(blk * nb, (blk + 1) * nb)
        for q in range(S5_PAIRS_PER_STEP):
            sin_s[q, rs, 0:sw] = s_re[q]
            sin_s[q, rs, sw:2 * sw] = s_im[q]
            n_re, n_im = _cmul(a_re[q], a_im[q], s_re[q], s_im[q])
            s_re[q], s_im[q] = n_re + sc_s[q, rs, 0:sw], n_im + sc_s[q, rs, sw:2 * sw]
    for q in range(S5_PAIRS_PER_STEP):
        sf_ref[:, 2 * sw * q:2 * sw * q + sw] = s_re[q]
        sf_ref[:, 2 * sw * q + sw:2 * sw * (q + 1)] = s_im[q]
        half = S5_L * GROUP_C
        conv = jnp.concatenate([_dot(ub[q][:, :half], kconv_ref[2 * q]),
                                _dot(ub[q][:, half:], kconv_ref[2 * q + 1])], axis=1)
        y_ref[:, q * pw:(q + 1) * pw] = conv + _dot(sin_s[q].astype(BF16), kin_ref[q])


def _s5_core(u2, wst, kconv, kin, a_pair, s0, nb):
    rows = u2.shape[0]
    pw = 2 * S5_L * GROUP_C
    sw = 2 * P_C
    cols = S5_PAIRS_PER_STEP * pw
    scols = S5_PAIRS_PER_STEP * 2 * sw
    return pl.pallas_call(
        functools.partial(_s5_core_kernel, nb=nb, n_blocks=rows // nb),
        grid=(u2.shape[1] // cols,),
        in_specs=[pl.BlockSpec((rows, cols), lambda i: (0, i)),
                  pl.BlockSpec((S5_PAIRS_PER_STEP, pw, 2 * sw), lambda i: (i, 0, 0)),
                  pl.BlockSpec((2 * S5_PAIRS_PER_STEP, pw // 2, pw // 2), lambda i: (i, 0, 0)),
                  pl.BlockSpec((S5_PAIRS_PER_STEP, 2 * sw, pw), lambda i: (i, 0, 0)),
                  pl.BlockSpec((1, scols), lambda i: (0, i)), pl.BlockSpec((nb, scols), lambda i: (0, i))],
        out_specs=[pl.BlockSpec((rows, cols), lambda i: (0, i)), pl.BlockSpec((nb, scols), lambda i: (0, i))],
        out_shape=[jax.ShapeDtypeStruct(u2.shape, F32), jax.ShapeDtypeStruct((nb, 2 * D_STATE), F32)],
        scratch_shapes=[pltpu.VMEM((S5_PAIRS_PER_STEP, rows, 2 * sw), F32),
                        pltpu.VMEM((S5_PAIRS_PER_STEP, rows, 2 * sw), F32)],
        compiler_params=_params(),
        name="s5_core",
    )(u2, wst, kconv, kin, a_pair, s0)


def _s5_out_kernel(x_ref, u_ref, y2_ref, dskip_ref, wglu_ref, o_ref, y_s, *, nb, blocks):
    rows = blocks * S5_L * nb
    o = None
    for cc in range(N_BLK):
        cols = slice(cc * V7X_MXU_DIM, (cc + 1) * V7X_MXU_DIM)
        for sub in range(V7X_MXU_DIM // V7X_LANES):
            c = cc * (V7X_MXU_DIM // V7X_LANES) + sub
            for half in range(S5_L // GRANULES_PER_VREG):
                lanes = [(c * GRANULES_PER_VREG + gi) * S5_L * GROUP_C + half * V7X_LANES
                         for gi in range(GRANULES_PER_VREG)]
                ws = _granule_transpose([y2_ref[:, l0:l0 + V7X_LANES] for l0 in lanes])
                for kk in range(GRANULES_PER_VREG):
                    y_s[cc, :, half * GRANULES_PER_VREG + kk, :, sub * V7X_LANES:(sub + 1) * V7X_LANES] = (
                        ws[kk].reshape(blocks, nb, V7X_LANES))
        y = y_s[cc].reshape(rows, V7X_MXU_DIM) + dskip_ref[:, cols] * u_ref[:, cols]
        part = _dot(_gelu(y).astype(BF16), wglu_ref[cols, :])
        o = part if o is None else o + part
    o_ref[...] = x_ref[...] + o[:, :D_MODEL] * jax.nn.sigmoid(o[:, D_MODEL:])


def _s5_out(x, u, y2, dskip, wglu, nb, blocks):
    rows = blocks * S5_L * nb
    row_spec = pl.BlockSpec((rows, D_MODEL), lambda i: (i, 0))
    return pl.pallas_call(
        functools.partial(_s5_out_kernel, nb=nb, blocks=blocks),
        grid=(x.shape[0] // rows,),
        in_specs=[row_spec, row_spec, pl.BlockSpec((blocks * nb, S5_L * D_MODEL), lambda i: (i, 0)),
                  _const_spec((1, D_MODEL)), _const_spec((D_MODEL, 2 * D_MODEL))],
        out_specs=row_spec,
        out_shape=jax.ShapeDtypeStruct(x.shape, F32),
        scratch_shapes=[pltpu.VMEM((N_BLK, blocks, S5_L, nb, V7X_MXU_DIM), F32)],
        compiler_params=_params(),
        name="s5_out",
    )(x, u, y2, dskip, wglu)


def _pair_cols(re, im):
    z = jnp.zeros_like(re[0::2])
    top = jnp.concatenate([re[0::2], z, im[0::2], z], axis=-1)
    bot = jnp.concatenate([z, re[1::2], z, im[1::2]], axis=-1)
    return jnp.concatenate([top, bot], axis=-2)


def _pair_rows(re, im):
    z = jnp.zeros_like(re[0::2])
    left = jnp.concatenate([re[0::2], z, im[0::2], z], axis=-2)
    right = jnp.concatenate([z, re[1::2], z, im[1::2]], axis=-2)
    return jnp.concatenate([left, right], axis=-1)


def _pair_state(re, im):
    lead = re.shape[:-2]
    r = re.reshape(lead + (G_C // 2, 2 * P_C))
    i = im.reshape(lead + (G_C // 2, 2 * P_C))
    return jnp.concatenate([r, i], axis=-1).reshape(lead + (2 * D_STATE,))


def _unpair_state(s, nb):
    s4 = s.reshape(nb, G_C // 2, 2, 2 * P_C)
    return s4[:, :, 0].reshape(nb, G_C, P_C), s4[:, :, 1].reshape(nb, G_C, P_C)


def _s5_kernel(x_ref, g_ref, win_ref, bre_ref, bim_ref, are_ref, aim_ref, cre_ref, cim_ref, dskip_ref, wglu_ref,
               sre0_ref, sim0_ref, o_ref, sre_ref, sim_ref, xre_s, xim_s, *, nb, steps, lane_chunk):
    @pl.when(pl.program_id(0) == 0)
    def _():
        sre_ref[...] = sre0_ref[...]
        sim_ref[...] = sim0_ref[...]

    x = x_ref[...]
    h = _rms(x, g_ref[...]).astype(BF16)
    u = _dot(h, win_ref[...])
    ub = u.astype(BF16)

    def expand(j):
        ucols = slice(j * V7X_MXU_DIM, (j + 1) * V7X_MXU_DIM)
        scols = slice(j * STATE_PER_BLK, (j + 1) * STATE_PER_BLK)
        xre_s[:, scols] = _dot(ub[:, ucols], bre_ref[j])
        xim_s[:, scols] = _dot(ub[:, ucols], bim_ref[j])

    def scan(j):
        for c in range(STATE_PER_BLK // lane_chunk):
            cols = slice(j * STATE_PER_BLK + c * lane_chunk, j * STATE_PER_BLK + (c + 1) * lane_chunk)
            a_re = jnp.broadcast_to(are_ref[:, cols], (nb, lane_chunk))
            a_im = jnp.broadcast_to(aim_ref[:, cols], (nb, lane_chunk))
            s_re, s_im = sre_ref[:, cols], sim_ref[:, cols]
            for t in range(steps):
                sl = slice(t * nb, (t + 1) * nb)
                s_re, s_im = ((a_re * s_re - a_im * s_im) + xre_s[sl, cols],
                              (a_re * s_im + a_im * s_re) + xim_s[sl, cols])
                xre_s[sl, cols] = s_re
                xim_s[sl, cols] = s_im
            sre_ref[:, cols] = s_re
            sim_ref[:, cols] = s_im

    ys = []
    expand(0)
    for j in range(N_BLK):
        if j + 1 < N_BLK:
            expand(j + 1)
        scan(j)
        scols = slice(j * STATE_PER_BLK, (j + 1) * STATE_PER_BLK)
        ys.append(_dot(xre_s[:, scols].astype(BF16), cre_ref[j]) - _dot(xim_s[:, scols].astype(BF16), cim_ref[j]))
    y = jnp.concatenate(ys, axis=1) + dskip_ref[...] * u
    o = _dot(_gelu(y).astype(BF16), wglu_ref[...])
    o_ref[...] = x + o[:, :D_MODEL] * jax.nn.sigmoid(o[:, D_MODEL:])


def _s5(x, p, sre0, sim0, nb, steps, lane_chunk):
    rows = nb * steps
    row_spec = pl.BlockSpec((rows, D_MODEL), lambda i: (i, 0))
    vec = _const_spec((1, D_MODEL))
    svec = _const_spec((1, D_STATE))
    state_spec = _const_spec((nb, D_STATE))
    bspec = _const_spec((N_BLK, V7X_MXU_DIM, STATE_PER_BLK))
    cspec = _const_spec((N_BLK, STATE_PER_BLK, V7X_MXU_DIM))
    state_out = pl.BlockSpec((nb, D_STATE), lambda i: (0, 0))
    return pl.pallas_call(
        functools.partial(_s5_kernel, nb=nb, steps=steps, lane_chunk=lane_chunk),
        grid=(x.shape[0] // rows,),
        in_specs=[row_spec, vec, _const_spec((D_MODEL, D_MODEL)), bspec, bspec, svec, svec, cspec, cspec, vec,
                  _const_spec((D_MODEL, 2 * D_MODEL)), state_spec, state_spec],
        out_specs=[row_spec, state_out, state_out],
        out_shape=[jax.ShapeDtypeStruct(x.shape, F32), jax.ShapeDtypeStruct((nb, D_STATE), F32),
                   jax.ShapeDtypeStruct((nb, D_STATE), F32)],
        scratch_shapes=[pltpu.VMEM((rows, D_STATE), F32), pltpu.VMEM((rows, D_STATE), F32)],
        compiler_params=_params(),
        name="s5",
    )(x, p["g"], p["win"], p["bre"], p["bim"], p["are"], p["aim"], p["cre"], p["cim"], p["dskip"], p["wglu"],
      sre0, sim0)


def _block_diag(w, n_per_blk):
    n, k_in, k_out = w.shape
    wb = w.reshape(n // n_per_blk, n_per_blk, k_in, k_out)
    eye = jnp.eye(n_per_blk, dtype=w.dtype)
    out = jnp.einsum("jgio,gk->jgiko", wb, eye)
    return out.reshape(n // n_per_blk, n_per_blk * k_in, n_per_blk * k_out)


def _to_rows(a):
    return jnp.swapaxes(a, 0, 1).reshape(a.shape[0] * a.shape[1], a.shape[2])


def _from_rows(a, nb):
    return jnp.swapaxes(a.reshape(a.shape[0] // nb, nb, a.shape[1]), 0, 1)


def kernel(x_prompt, x_sample, state_rglru_conv, state_rglru_h, state_s5_re, state_s5_im, norm_mix, norm_ffn, norm_f, w_ff1, w_ff2, w_in_a, sgu_g, w_s, b_s, w_out_a, w_in_b, conv_w, conv_b, w_a, b_a, w_x, b_x, lam, w_out_b, w_in_c, lam_re, lam_im, log_dt, b_re, b_im, c_re, c_im, d_skip, w_glu):
    bp, tp, _ = x_prompt.shape
    bs, ts, _ = x_sample.shape
    rows_p, rows_s = bp * tp, bs * ts
    assert DEPTH % N_MIXERS == 1 and DEPTH > 1, "first and last layers must be SGU layers"
    assert tp % CHUNK == 0 and ts < CHUNK
    assert rows_s % FFN_ROWS == 0 and rows_p % FFN_ROWS == 0

    row = lambda v: v.reshape(1, -1)
    norm_ffn3 = norm_ffn.reshape(DEPTH, 1, D_MODEL)
    w1, w2, gf = w_ff1.astype(BF16), w_ff2.astype(BF16), row(norm_f)

    x_p, x_s = x_prompt, _to_rows(x_sample)
    outs_v, conv_p, h_p, conv_s, h_s, sre_p, sim_p, sre_s, sim_s = [], [], [], [], [], [], [], [], []
    for layer in range(DEPTH):
        j, kind = layer // N_MIXERS, layer % N_MIXERS
        first, last = layer == 0, layer == DEPTH - 1
        g = row(norm_mix[layer])
        if kind == 0:
            p = dict(g=g, win=w_in_a[j].astype(BF16), wout=w_out_a[j].astype(BF16), sg=row(sgu_g[j]),
                     w_tril=jnp.where(jnp.tril(jnp.ones((CHUNK, CHUNK), dtype=bool)), w_s[j], 0.0).astype(BF16),
                     bias=jnp.repeat(b_s[j].T, HD_A, axis=1),
                     wl=jnp.repeat(w_s[j][:, :ts, :ts].reshape(HEADS_A, ts * ts).T, HD_A, axis=1),
                     bl=jnp.repeat(b_s[j][:, :ts].T, HD_A, axis=1))
            if first:
                x_p = _sgu_prompt(x_p, p, bp, tp, False, True)
            elif last:
                x_p = _sgu_prompt(x_p, p, bp, tp, True, False).reshape(rows_p, D_MODEL)
            else:
                raise NotImplementedError("interior SGU layers")
            x_s, v = _sgu_sample(x_s, p, bs, ts)
            outs_v.append(_from_rows(v, bs))
        elif kind == 1:
            p = dict(g=g, win=w_in_b[j].astype(BF16), cw=conv_w[j], cb=row(conv_b[j]),
                     wa=_block_diag(w_a[j], HEADS_PER_BLK).astype(BF16), ba=row(b_a[j]),
                     wx=_block_diag(w_x[j], HEADS_PER_BLK).astype(BF16), bx=row(b_x[j]),
                     lam=row(lam[j]), wout=w_out_b[j].astype(BF16))
            dt_s = state_rglru_h.dtype
            x_p, cp, hp = _rglru(x_p, p, jnp.zeros(((CONV_W - 1) * bp, D_MODEL), dt_s),
                                 jnp.zeros((bp, D_MODEL), dt_s), bp, RGLRU_PROMPT_STEPS)
            x_s, cs, hs = _rglru(x_s, p, _to_rows(state_rglru_conv[j]), state_rglru_h[j], bs, ts)
            conv_p.append(_from_rows(cp, bp)); h_p.append(hp)
            conv_s.append(_from_rows(cs, bs)); h_s.append(hs)
        else:
            are, aim, bbre, bbim, pw = _s5_disc(lam_re[j], lam_im[j], log_dt[j],
                                                jnp.swapaxes(b_re[j], 1, 2), jnp.swapaxes(b_im[j], 1, 2))
            p = dict(g=g, win=w_in_c[j].astype(BF16),
                     bre=_block_diag(bbre, GROUPS_PER_BLK).astype(BF16),
                     bim=_block_diag(bbim, GROUPS_PER_BLK).astype(BF16),
                     are=are.reshape(1, D_STATE), aim=aim.reshape(1, D_STATE),
                     cre=_block_diag(jnp.swapaxes(c_re[j], 1, 2), GROUPS_PER_BLK).astype(BF16),
                     cim=_block_diag(jnp.swapaxes(c_im[j], 1, 2), GROUPS_PER_BLK).astype(BF16),
                     dskip=row(d_skip[j]), wglu=w_glu[j].astype(BF16))
            pwg = jnp.transpose(pw, (2, 0, 1, 3))
            kconv, wst, kin = _s5_taps(pwg[:, :, :S5_L], pwg[:, :, 1:], pwg[:, :, S5_L - 1::-1],
                                       jnp.stack([c_re[j], c_im[j]], axis=1), jnp.stack([bbre, bbim], axis=1))
            zs = jnp.zeros((bp, G_C, P_C), state_s5_re.dtype)
            u, u2 = _s5_in(x_p, g, p["win"], bp, S5_TILE_BLOCKS)
            y2, sf = _s5_core(u2, _pair_cols(wst[:, 0], wst[:, 1]), kconv, _pair_rows(kin[:, 0], kin[:, 1]),
                              _pair_state(pw[0, S5_L], pw[1, S5_L]).reshape(1, -1), _pair_state(zs, zs), bp)
            x_p = _s5_out(x_p, u, y2, p["dskip"], p["wglu"], bp, S5_TILE_BLOCKS)
            rp, ip = _unpair_state(sf, bp)
            x_s, rs, is_ = _s5(x_s, p, state_s5_re[j].reshape(bs, D_STATE), state_s5_im[j].reshape(bs, D_STATE),
                               bs, ts, V7X_LANES)
            sre_p.append(rp); sim_p.append(ip)
            sre_s.append(rs.reshape(bs, G_C, P_C)); sim_s.append(is_.reshape(bs, G_C, P_C))
        x_p, x_s = _ffn(x_p, x_s, norm_ffn3, w1, w2, gf, layer, last)

    y_prompt = x_p.reshape(bp, tp, D_MODEL)
    y_sample = _from_rows(x_s, bs)
    return (y_prompt, y_sample, jnp.stack(outs_v), jnp.stack(conv_p), jnp.stack(h_p), jnp.stack(conv_s),
            jnp.stack(h_s), jnp.stack(sre_p), jnp.stack(sim_p), jnp.stack(sre_s), jnp.stack(sim_s))
```

```python
import functools

import jax
import jax.numpy as jnp
from jax import lax
from jax.experimental import pallas as pl
from jax.experimental.pallas import tpu as pltpu

F32 = jnp.float32
BF16 = jnp.bfloat16

D_MODEL = 1024
DEPTH = 4
N_MIXERS = 3
EPS = 1e-6
CHUNK = 128
HEADS_A = 8
HD_A = D_MODEL // HEADS_A
HEADS_B = 16
HD_B = D_MODEL // HEADS_B
CONV_W = 4
LRU_C = 8.0
GROUP_C = 16
G_C = D_MODEL // GROUP_C
P_C = 64
D_STATE = G_C * P_C
D_FF = 4 * D_MODEL

V7X_LANES = 128
V7X_MXU_DIM = 256
V7X_VMEM_BYTES = 64 * 1024 * 1024
VMEM_LIMIT = V7X_VMEM_BYTES - 8 * 1024 * 1024

N_BLK = D_MODEL // V7X_MXU_DIM
HEADS_PER_BLK = V7X_MXU_DIM // HD_B
GROUPS_PER_BLK = V7X_MXU_DIM // GROUP_C
STATE_PER_BLK = GROUPS_PER_BLK * P_C

FFN_ROWS = 512
FFN_COL_CHUNK = 1024
S5_L = 16
S5_TILE_BLOCKS = 4
S5_PAIRS_PER_STEP = 2
GRANULES_PER_VREG = V7X_LANES // GROUP_C
RGLRU_PROMPT_STEPS = 64


def _rms(x, g):
    return (x * lax.rsqrt(jnp.mean(x * x, axis=-1, keepdims=True) + EPS)) * g


GELU_C0 = 0.7978845608028654
GELU_C1 = GELU_C0 * 0.044715


def _gelu(x):
    return x * (0.5 + 0.5 * jnp.tanh(x * (GELU_C0 + GELU_C1 * (x * x))))


def _dot(a, b):
    return jnp.dot(a, b, preferred_element_type=F32)


def _const_spec(shape):
    zeros = (0,) * len(shape)
    return pl.BlockSpec(shape, lambda i: zeros, pipeline_mode=pl.Buffered(1))


def _layer_spec(shape, layer):
    idx = (layer,) + (0,) * len(shape)
    return pl.BlockSpec((None,) + tuple(shape), lambda i: idx, pipeline_mode=pl.Buffered(1))


def _params():
    return pltpu.CompilerParams(dimension_semantics=("arbitrary",), vmem_limit_bytes=VMEM_LIMIT)


def _ffn_tile(x_ref, g_ref, w1_ref, w2_ref, gf_ref, o_ref, final_norm):
    x = x_ref[...]
    h = _rms(x, g_ref[...]).astype(BF16)
    acc = None
    for j in range(D_FF // FFN_COL_CHUNK):
        cols = slice(j * FFN_COL_CHUNK, (j + 1) * FFN_COL_CHUNK)
        a = jnp.square(jnp.maximum(_dot(h, w1_ref[:, cols]), 0.0)).astype(BF16)
        part = _dot(a, w2_ref[cols, :])
        acc = part if acc is None else acc + part
    y = x + acc
    if final_norm:
        y = _rms(y, gf_ref[...])
    o_ref[...] = y


def _ffn_kernel(xp_ref, xs_ref, g_ref, w1_ref, w2_ref, gf_ref, op_ref, os_ref, *, final_norm, n_p):
    i = pl.program_id(0)

    @pl.when(i < n_p)
    def _():
        _ffn_tile(xp_ref, g_ref, w1_ref, w2_ref, gf_ref, op_ref, final_norm)

    @pl.when(i >= n_p)
    def _():
        _ffn_tile(xs_ref, g_ref, w1_ref, w2_ref, gf_ref, os_ref, final_norm)


def _ffn(x_p, x_s, g, w1, w2, gf, layer, final_norm):
    n_p, n_s = x_p.shape[0] // FFN_ROWS, x_s.shape[0] // FFN_ROWS
    p_spec = pl.BlockSpec((FFN_ROWS, D_MODEL), lambda i: (jnp.minimum(i, n_p - 1), 0))
    s_spec = pl.BlockSpec((FFN_ROWS, D_MODEL), lambda i: (jnp.maximum(i - n_p, 0), 0))
    return pl.pallas_call(
        functools.partial(_ffn_kernel, final_norm=final_norm, n_p=n_p),
        grid=(n_p + n_s,),
        in_specs=[p_spec, s_spec, _layer_spec((1, D_MODEL), layer), _layer_spec((D_MODEL, D_FF), layer),
                  _layer_spec((D_FF, D_MODEL), layer), _const_spec((1, D_MODEL))],
        out_specs=[p_spec, s_spec],
        out_shape=[jax.ShapeDtypeStruct(x_p.shape, F32), jax.ShapeDtypeStruct(x_s.shape, F32)],
        compiler_params=_params(),
        name="ffn",
    )(x_p, x_s, g, w1, w2, gf)


def _sgu_front(x, g_ref, win_ref, sg_ref):
    h = _rms(x, g_ref[...]).astype(BF16)
    uv = _gelu(_dot(h, win_ref[...]))
    return uv[:, :D_MODEL], _rms(uv[:, D_MODEL:], sg_ref[...])


def _rows_to_batch_major(x, nb):
    steps = x.shape[0] // nb
    return jnp.swapaxes(x.reshape(steps, nb, x.shape[1]), 0, 1).reshape(x.shape)


def _rows_to_time_major(x, nb):
    steps = x.shape[0] // nb
    return jnp.swapaxes(x.reshape(nb, steps, x.shape[1]), 0, 1).reshape(x.shape)


def _sgu_prompt_kernel(x_ref, g_ref, win_ref, sg_ref, w_ref, bias_ref, wout_ref, o_ref, y_s, *, nb, in_tm, out_tm):
    rows = nb * CHUNK
    if in_tm:
        x = _rows_to_batch_major(x_ref[...], nb)
    else:
        x = x_ref[...].reshape(rows, D_MODEL)
    u, v = _sgu_front(x, g_ref, win_ref, sg_ref)
    vb = v.astype(BF16)
    for b in range(nb):
        rs = slice(b * CHUNK, (b + 1) * CHUNK)
        for g in range(HEADS_A):
            cs = slice(g * HD_A, (g + 1) * HD_A)
            mixed = _dot(w_ref[g], vb[rs, cs]) + bias_ref[:, cs]
            y_s[rs, cs] = (u[rs, cs] * mixed).astype(BF16)
    o = x + _dot(y_s[...], wout_ref[...])
    if out_tm:
        o_ref[...] = _rows_to_time_major(o, nb)
    else:
        o_ref[...] = o.reshape(nb, CHUNK, D_MODEL)


def _sgu_sample_kernel(x_ref, g_ref, win_ref, sg_ref, wl_ref, bl_ref, wout_ref, o_ref, v_ref, *, nb, steps):
    x = x_ref[...]
    u, v = _sgu_front(x, g_ref, win_ref, sg_ref)
    v_ref[...] = v
    mixed = []
    for t in range(steps):
        m = None
        for s in range(t + 1):
            term = wl_ref[t * steps + s:t * steps + s + 1, :] * v[s * nb:(s + 1) * nb, :]
            m = term if m is None else m + term
        mixed.append(m + bl_ref[t:t + 1, :])
    y = (u * jnp.concatenate(mixed, axis=0)).astype(BF16)
    o_ref[...] = x + _dot(y, wout_ref[...])


def _sgu_prompt(x, p, nb, n_steps, in_tm, out_tm):
    rows = nb * CHUNK
    tm_spec = pl.BlockSpec((rows, D_MODEL), lambda i: (i, 0))
    bm_spec = pl.BlockSpec((nb, CHUNK, D_MODEL), lambda i: (0, i, 0))
    out_shape = (jax.ShapeDtypeStruct((nb * n_steps, D_MODEL), F32) if out_tm
                 else jax.ShapeDtypeStruct((nb, n_steps, D_MODEL), F32))
    return pl.pallas_call(
        functools.partial(_sgu_prompt_kernel, nb=nb, in_tm=in_tm, out_tm=out_tm),
        grid=(n_steps // CHUNK,),
        in_specs=[tm_spec if in_tm else bm_spec, _const_spec((1, D_MODEL)), _const_spec((D_MODEL, 2 * D_MODEL)),
                  _const_spec((1, D_MODEL)), _const_spec((HEADS_A, CHUNK, CHUNK)),
                  _const_spec((CHUNK, D_MODEL)), _const_spec((D_MODEL, D_MODEL))],
        out_specs=tm_spec if out_tm else bm_spec,
        out_shape=out_shape,
        scratch_shapes=[pltpu.VMEM((rows, D_MODEL), BF16)],
        compiler_params=_params(),
        name="sgu_prompt",
    )(x, p["g"], p["win"], p["sg"], p["w_tril"], p["bias"], p["wout"])


def _sgu_sample(x, p, nb, steps):
    tile = steps * nb
    tile_spec = pl.BlockSpec((tile, D_MODEL), lambda i: (0, 0))
    return pl.pallas_call(
        functools.partial(_sgu_sample_kernel, nb=nb, steps=steps),
        grid=(1,),
        in_specs=[tile_spec, _const_spec((1, D_MODEL)), _const_spec((D_MODEL, 2 * D_MODEL)),
                  _const_spec((1, D_MODEL)), _const_spec((steps * steps, D_MODEL)),
                  _const_spec((steps, D_MODEL)), _const_spec((D_MODEL, D_MODEL))],
        out_specs=[tile_spec, tile_spec],
        out_shape=[jax.ShapeDtypeStruct((tile, D_MODEL), F32), jax.ShapeDtypeStruct((tile, D_MODEL), F32)],
        compiler_params=_params(),
        name="sgu_sample",
    )(x, p["g"], p["win"], p["sg"], p["wl"], p["bl"], p["wout"])


def _rglru_kernel(x_ref, g_ref, win_ref, cw_ref, cb_ref, wa_ref, ba_ref, wx_ref, bx_ref, lam_ref, wout_ref,
                  conv0_ref, h0_ref, o_ref, conv_ref, h_ref, xext_s, a_s, b_s, y_s, *, nb, steps):
    rows = nb * steps
    halo = (CONV_W - 1) * nb

    @pl.when(pl.program_id(0) == 0)
    def _():
        conv_ref[...] = conv0_ref[...]
        h_ref[...] = h0_ref[...]

    def blk_cols(j):
        return slice(j * V7X_MXU_DIM, (j + 1) * V7X_MXU_DIM)

    x = x_ref[...]
    h = _rms(x, g_ref[...]).astype(BF16)
    for j in range(N_BLK):
        cols = blk_cols(j)
        xext, a_j, b_j = xext_s.at[j], a_s.at[j], b_s.at[j]
        xext[0:halo, :] = conv_ref[:, cols]
        xext[halo:halo + rows, :] = _dot(h, win_ref[:, blk_cols(N_BLK + j)])
        gate = _gelu(_dot(h, win_ref[:, cols]))
        conv = None
        for w in range(CONV_W):
            term = xext[w * nb:w * nb + rows, :] * cw_ref[w:w + 1, cols]
            conv = term if conv is None else conv + term
        conv_ref[:, cols] = xext[rows:rows + halo, :]
        xc = cb_ref[:, cols] + conv
        xcb = xc.astype(BF16)
        r = jax.nn.sigmoid(_dot(xcb, wa_ref[j]) + ba_ref[:, cols])
        ig = jax.nn.sigmoid(_dot(xcb, wx_ref[j]) + bx_ref[:, cols])
        neg_log_a = (LRU_C * r) * jax.nn.softplus(-lam_ref[:, cols])
        a = jnp.exp(-neg_log_a)
        a_j[...] = a
        w = jnp.tanh(neg_log_a) * (a * a + 1.0)
        mult = jnp.where(w > 0.0, w * lax.rsqrt(w), 0.0)
        b_j[...] = mult * (ig * xc)
        hcur = h_ref[:, cols]
        for t in range(steps):
            sl = slice(t * nb, (t + 1) * nb)
            hcur = a_j[sl, :] * hcur + b_j[sl, :]
            b_j[sl, :] = hcur
        h_ref[:, cols] = hcur
        y_s[:, cols] = (b_j[...] * gate).astype(BF16)
    o_ref[...] = x + _dot(y_s[...], wout_ref[...])


def _rglru(x, p, conv0, h0, nb, steps):
    rows = nb * steps
    halo = (CONV_W - 1) * nb
    row_spec = pl.BlockSpec((rows, D_MODEL), lambda i: (i, 0))
    vec = _const_spec((1, D_MODEL))
    blk = _const_spec((N_BLK, V7X_MXU_DIM, V7X_MXU_DIM))
    return pl.pallas_call(
        functools.partial(_rglru_kernel, nb=nb, steps=steps),
        grid=(x.shape[0] // rows,),
        in_specs=[row_spec, vec, _const_spec((D_MODEL, 2 * D_MODEL)), _const_spec((CONV_W, D_MODEL)), vec,
                  blk, vec, blk, vec, vec, _const_spec((D_MODEL, D_MODEL)),
                  _const_spec((halo, D_MODEL)), _const_spec((nb, D_MODEL))],
        out_specs=[row_spec, pl.BlockSpec((halo, D_MODEL), lambda i: (0, 0)),
                   pl.BlockSpec((nb, D_MODEL), lambda i: (0, 0))],
        out_shape=[jax.ShapeDtypeStruct(x.shape, F32), jax.ShapeDtypeStruct((halo, D_MODEL), F32),
                   jax.ShapeDtypeStruct((nb, D_MODEL), F32)],
        scratch_shapes=[pltpu.VMEM((N_BLK, rows + halo, V7X_MXU_DIM), F32), pltpu.VMEM((N_BLK, rows, V7X_MXU_DIM), F32),
                        pltpu.VMEM((N_BLK, rows, V7X_MXU_DIM), F32), pltpu.VMEM((rows, D_MODEL), BF16)],
        compiler_params=_params(),
        name="rglru",
    )(x, p["g"], p["win"], p["cw"], p["cb"], p["wa"], p["ba"], p["wx"], p["bx"], p["lam"], p["wout"], conv0, h0)


def _cmul(a_re, a_im, b_re, b_im):
    return a_re * b_re - a_im * b_im, a_re * b_im + a_im * b_re


def _s5_disc_kernel(lre_ref, lim_ref, ldt_ref, bre_ref, bim_ref, are_ref, aim_ref, bbre_ref, bbim_ref, pw_ref):
    lr, li = lre_ref[...], lim_ref[...]
    dt = jnp.exp(ldt_ref[...])
    mag = jnp.exp(lr * dt)
    ab_re, ab_im = mag * jnp.cos(li * dt), mag * jnp.sin(li * dt)
    zr, zi = ab_re - 1.0, ab_im
    den = lr * lr + li * li
    q_re = (zr * lr + zi * li) / den
    q_im = (zi * lr - zr * li) / den
    are_ref[...] = ab_re
    aim_ref[...] = ab_im
    br, bi = bre_ref[...], bim_ref[...]
    bbre_ref[...] = q_re[:, None, :] * br - q_im[:, None, :] * bi
    bbim_ref[...] = q_re[:, None, :] * bi + q_im[:, None, :] * br
    p_re, p_im = jnp.ones_like(ab_re), jnp.zeros_like(ab_re)
    for k in range(S5_L + 1):
        pw_ref[0, k] = p_re
        pw_ref[1, k] = p_im
        p_re, p_im = _cmul(p_re, p_im, ab_re, ab_im)


def _s5_disc(lam_re, lam_im, log_dt, b_re_t, b_im_t):
    gp = jax.ShapeDtypeStruct((G_C, P_C), F32)
    ghp = jax.ShapeDtypeStruct((G_C, GROUP_C, P_C), F32)
    pw = jax.ShapeDtypeStruct((2, S5_L + 1, G_C, P_C), F32)
    return pl.pallas_call(_s5_disc_kernel, out_shape=[gp, gp, ghp, ghp, pw], name="s5_disc")(
        lam_re, lam_im, log_dt.reshape(G_C, 1), b_re_t, b_im_t)


def _s5_taps_kernel(pw0_ref, pw1_ref, pwf_ref, c_ref, b_ref, kconv_ref, wst_ref, kin_ref, *, groups):
    n = S5_L * GROUP_C

    def rep(t):
        return jnp.broadcast_to(t[:, None, :], (S5_L, GROUP_C, P_C)).reshape(n, P_C)

    def til(c):
        return jnp.broadcast_to(c[None, :, :], (S5_L, GROUP_C, P_C)).reshape(n, P_C)

    def dot_t(a, b):
        return lax.dot_general(a, b, (((1,), (1,)), ((), ())), precision=lax.Precision.HIGHEST,
                               preferred_element_type=F32)

    lane = lax.broadcasted_iota(jnp.int32, (GROUP_C, n), 1)
    for gi in range(groups):
        cr, ci = til(c_ref[gi, 0]), til(c_ref[gi, 1])
        b_re, b_im = b_ref[gi, 0], b_ref[gi, 1]
        x_re, x_im = _cmul(rep(pw0_ref[gi, 0]), rep(pw0_ref[gi, 1]), cr, ci)
        mrow = dot_t(b_re, x_re) - dot_t(b_im, x_im)
        for s in range(S5_L):
            blk = mrow if s == 0 else jnp.where(lane >= s * GROUP_C, pltpu.roll(mrow, s * GROUP_C, 1), 0.0)
            kconv_ref[gi, s * GROUP_C:(s + 1) * GROUP_C, :] = blk.astype(BF16)
        e_re, e_im = _cmul(rep(pw1_ref[gi, 0]), rep(pw1_ref[gi, 1]), cr, ci)
        kin_ref[gi, 0] = e_re.T.astype(BF16)
        kin_ref[gi, 1] = (-e_im).T.astype(BF16)
        f_re, f_im = _cmul(rep(pwf_ref[gi, 0]), rep(pwf_ref[gi, 1]), til(b_re), til(b_im))
        wst_ref[gi, 0] = f_re.astype(BF16)
        wst_ref[gi, 1] = f_im.astype(BF16)


def _s5_taps(pw0, pw1, pwf, c, bb):
    groups = 8
    n = S5_L * GROUP_C
    pspec = pl.BlockSpec((groups, 2, S5_L, P_C), lambda i: (i, 0, 0, 0))
    hspec = pl.BlockSpec((groups, 2, GROUP_C, P_C), lambda i: (i, 0, 0, 0))
    return pl.pallas_call(
        functools.partial(_s5_taps_kernel, groups=groups),
        grid=(G_C // groups,),
        in_specs=[pspec, pspec, pspec, hspec, hspec],
        out_specs=[pl.BlockSpec((groups, n, n), lambda i: (i, 0, 0)),
                   pl.BlockSpec((groups, 2, n, P_C), lambda i: (i, 0, 0, 0)),
                   pl.BlockSpec((groups, 2, P_C, n), lambda i: (i, 0, 0, 0))],
        out_shape=[jax.ShapeDtypeStruct((G_C, n, n), BF16), jax.ShapeDtypeStruct((G_C, 2, n, P_C), BF16),
                   jax.ShapeDtypeStruct((G_C, 2, P_C, n), BF16)],
        compiler_params=_params(),
        name="s5_taps",
    )(pw0, pw1, pwf, c, bb)


def _granule_transpose(vs):
    vs = list(vs)
    granule = lax.broadcasted_iota(jnp.int32, vs[0].shape, 1) // GROUP_C
    d = 1
    while d < len(vs):
        keep = (granule & d) == 0
        for i in range(len(vs)):
            if i & d == 0:
                a, b = vs[i], vs[i + d]
                vs[i] = jnp.where(keep, a, pltpu.roll(b, d * GROUP_C, 1))
                vs[i + d] = jnp.where(keep, pltpu.roll(a, V7X_LANES - d * GROUP_C, 1), b)
        d *= 2
    return vs


def _s5_in_kernel(x_ref, g_ref, win_ref, u_ref, u2_ref, *, nb, blocks):
    h = _rms(x_ref[...], g_ref[...]).astype(BF16)
    rows2 = blocks * nb
    for cc in range(N_BLK):
        cols = slice(cc * V7X_MXU_DIM, (cc + 1) * V7X_MXU_DIM)
        u = _dot(h, win_ref[:, cols])
        u_ref[:, cols] = u
        u4 = u.reshape(blocks, S5_L, nb, V7X_MXU_DIM)
        for sub in range(V7X_MXU_DIM // V7X_LANES):
            c = cc * (V7X_MXU_DIM // V7X_LANES) + sub
            for half in range(S5_L // GRANULES_PER_VREG):
                vs = []
                for kk in range(GRANULES_PER_VREG):
                    slab = u4[:, half * GRANULES_PER_VREG + kk, :, sub * V7X_LANES:(sub + 1) * V7X_LANES]
                    vs.append(slab.reshape(rows2, V7X_LANES))
                ws = _granule_transpose(vs)
                for gi in range(GRANULES_PER_VREG):
                    lane0 = (c * GRANULES_PER_VREG + gi) * S5_L * GROUP_C + half * V7X_LANES
                    u2_ref[:, lane0:lane0 + V7X_LANES] = ws[gi].astype(BF16)


def _s5_in(x, g, win, nb, blocks):
    rows = blocks * S5_L * nb
    n_blocks = x.shape[0] // (S5_L * nb)
    row_spec = pl.BlockSpec((rows, D_MODEL), lambda i: (i, 0))
    return pl.pallas_call(
        functools.partial(_s5_in_kernel, nb=nb, blocks=blocks),
        grid=(x.shape[0] // rows,),
        in_specs=[row_spec, _const_spec((1, D_MODEL)), _const_spec((D_MODEL, D_MODEL))],
        out_specs=[row_spec, pl.BlockSpec((blocks * nb, S5_L * D_MODEL), lambda i: (i, 0))],
        out_shape=[jax.ShapeDtypeStruct(x.shape, F32),
                   jax.ShapeDtypeStruct((n_blocks * nb, S5_L * D_MODEL), BF16)],
        compiler_params=_params(),
        name="s5_in",
    )(x, g, win)


def _s5_core_kernel(u_ref, wst_ref, kconv_ref, kin_ref, a_ref, s0_ref, y_ref, sf_ref, sc_s, sin_s, *, nb, n_blocks):
    pw = 2 * S5_L * GROUP_C
    sw = 2 * P_C
    ub = [u_ref[:, q * pw:(q + 1) * pw] for q in range(S5_PAIRS_PER_STEP)]
    for q in range(S5_PAIRS_PER_STEP):
        sc_s[q] = _dot(ub[q], wst_ref[q])
    a_re = [jnp.broadcast_to(a_ref[:, 2 * sw * q:2 * sw * q + sw], (nb, sw)) for q in range(S5_PAIRS_PER_STEP)]
    a_im = [jnp.broadcast_to(a_ref[:, 2 * sw * q + sw:2 * sw * (q + 1)], (nb, sw)) for q in range(S5_PAIRS_PER_STEP)]
    s_re = [s0_ref[:, 2 * sw * q:2 * sw * q + sw] for q in range(S5_PAIRS_PER_STEP)]
    s_im = [s0_ref[:, 2 * sw * q + sw:2 * sw * (q + 1)] for q in range(S5_PAIRS_PER_STEP)]
    for blk in range(n_blocks):
        rs = slice(blk * nb, (blk + 1) * nb)
        for q in range(S5_PAIRS_PER_STEP):
            sin_s[q, rs, 0:sw] = s_re[q]
            sin_s[q, rs, sw:2 * sw] = s_im[q]
            n_re, n_im = _cmul(a_re[q], a_im[q], s_re[q], s_im[q])
            s_re[q], s_im[q] = n_re + sc_s[q, rs, 0:sw], n_im + sc_s[q, rs, sw:2 * sw]
    for q in range(S5_PAIRS_PER_STEP):
        sf_ref[:, 2 * sw * q:2 * sw * q + sw] = s_re[q]
        sf_ref[:, 2 * sw * q + sw:2 * sw * (q + 1)] = s_im[q]
        half = S5_L * GROUP_C
        conv = jnp.concatenate([_dot(ub[q][:, :half], kconv_ref[2 * q]),
                                _dot(ub[q][:, half:], kconv_ref[2 * q + 1])], axis=1)
        y_ref[:, q * pw:(q + 1) * pw] = conv + _dot(sin_s[q].astype(BF16), kin_ref[q])


def _s5_core(u2, wst, kconv, kin, a_pair, s0, nb):
    rows = u2.shape[0]
    pw = 2 * S5_L * GROUP_C
    sw = 2 * P_C
    cols = S5_PAIRS_PER_STEP * pw
    scols = S5_PAIRS_PER_STEP * 2 * sw
    return pl.pallas_call(
        functools.partial(_s5_core_kernel, nb=nb, n_blocks=rows // nb),
        grid=(u2.shape[1] // cols,),
        in_specs=[pl.BlockSpec((rows, cols), lambda i: (0, i)),
                  pl.BlockSpec((S5_PAIRS_PER_STEP, pw, 2 * sw), lambda i: (i, 0, 0)),
                  pl.BlockSpec((2 * S5_PAIRS_PER_STEP, pw // 2, pw // 2), lambda i: (i, 0, 0)),
                  pl.BlockSpec((S5_PAIRS_PER_STEP, 2 * sw, pw), lambda i: (i, 0, 0)),
                  pl.BlockSpec((1, scols), lambda i: (0, i)), pl.BlockSpec((nb, scols), lambda i: (0, i))],
        out_specs=[pl.BlockSpec((rows, cols), lambda i: (0, i)), pl.BlockSpec((nb, scols), lambda i: (0, i))],
        out_shape=[jax.ShapeDtypeStruct(u2.shape, F32), jax.ShapeDtypeStruct((nb, 2 * D_STATE), F32)],
        scratch_shapes=[pltpu.VMEM((S5_PAIRS_PER_STEP, rows, 2 * sw), F32),
                        pltpu.VMEM((S5_PAIRS_PER_STEP, rows, 2 * sw), F32)],
        compiler_params=_params(),
        name="s5_core",
    )(u2, wst, kconv, kin, a_pair, s0)


def _s5_out_kernel(x_ref, u_ref, y2_ref, dskip_ref, wglu_ref, o_ref, y_s, *, nb, blocks):
    rows = blocks * S5_L * nb
    o = None
    for cc in range(N_BLK):
        cols = slice(cc * V7X_MXU_DIM, (cc + 1) * V7X_MXU_DIM)
        for sub in range(V7X_MXU_DIM // V7X_LANES):
            c = cc * (V7X_MXU_DIM // V7X_LANES) + sub
            for half in range(S5_L // GRANULES_PER_VREG):
                lanes = [(c * GRANULES_PER_VREG + gi) * S5_L * GROUP_C + half * V7X_LANES
                         for gi in range(GRANULES_PER_VREG)]
                ws = _granule_transpose([y2_ref[:, l0:l0 + V7X_LANES] for l0 in lanes])
                for kk in range(GRANULES_PER_VREG):
                    y_s[cc, :, half * GRANULES_PER_VREG + kk, :, sub * V7X_LANES:(sub + 1) * V7X_LANES] = (
                        ws[kk].reshape(blocks, nb, V7X_LANES))
        y = y_s[cc].reshape(rows, V7X_MXU_DIM) + dskip_ref[:, cols] * u_ref[:, cols]
        part = _dot(_gelu(y).astype(BF16), wglu_ref[cols, :])
        o = part if o is None else o + part
    o_ref[...] = x_ref[...] + o[:, :D_MODEL] * jax.nn.sigmoid(o[:, D_MODEL:])


def _s5_out(x, u, y2, dskip, wglu, nb, blocks):
    rows = blocks * S5_L * nb
    row_spec = pl.BlockSpec((rows, D_MODEL), lambda i: (i, 0))
    return pl.pallas_call(
        functools.partial(_s5_out_kernel, nb=nb, blocks=blocks),
        grid=(x.shape[0] // rows,),
        in_specs=[row_spec, row_spec, pl.BlockSpec((blocks * nb, S5_L * D_MODEL), lambda i: (i, 0)),
                  _const_spec((1, D_MODEL)), _const_spec((D_MODEL, 2 * D_MODEL))],
        out_specs=row_spec,
        out_shape=jax.ShapeDtypeStruct(x.shape, F32),
        scratch_shapes=[pltpu.VMEM((N_BLK, blocks, S5_L, nb, V7X_MXU_DIM), F32)],
        compiler_params=_params(),
        name="s5_out",
    )(x, u, y2, dskip, wglu)


def _pair_cols(re, im):
    z = jnp.zeros_like(re[0::2])
    top = jnp.concatenate([re[0::2], z, im[0::2], z], axis=-1)
    bot = jnp.concatenate([z, re[1::2], z, im[1::2]], axis=-1)
    return jnp.concatenate([top, bot], axis=-2)


def _pair_rows(re, im):
    z = jnp.zeros_like(re[0::2])
    left = jnp.concatenate([re[0::2], z, im[0::2], z], axis=-2)
    right = jnp.concatenate([z, re[1::2], z, im[1::2]], axis=-2)
    return jnp.concatenate([left, right], axis=-1)


def _pair_state(re, im):
    lead = re.shape[:-2]
    r = re.reshape(lead + (G_C // 2, 2 * P_C))
    i = im.reshape(lead + (G_C // 2, 2 * P_C))
    return jnp.concatenate([r, i], axis=-1).reshape(lead + (2 * D_STATE,))


def _unpair_state(s, nb):
    s4 = s.reshape(nb, G_C // 2, 2, 2 * P_C)
    return s4[:, :, 0].reshape(nb, G_C, P_C), s4[:, :, 1].reshape(nb, G_C, P_C)


def _s5_kernel(x_ref, g_ref, win_ref, bre_ref, bim_ref, are_ref, aim_ref, cre_ref, cim_ref, dskip_ref, wglu_ref,
               sre0_ref, sim0_ref, o_ref, sre_ref, sim_ref, xre_s, xim_s, *, nb, steps, lane_chunk):
    @pl.when(pl.program_id(0) == 0)
    def _():
        sre_ref[...] = sre0_ref[...]
        sim_ref[...] = sim0_ref[...]

    x = x_ref[...]
    h = _rms(x, g_ref[...]).astype(BF16)
    u = _dot(h, win_ref[...])
    ub = u.astype(BF16)

    def expand(j):
        ucols = slice(j * V7X_MXU_DIM, (j + 1) * V7X_MXU_DIM)
        scols = slice(j * STATE_PER_BLK, (j + 1) * STATE_PER_BLK)
        xre_s[:, scols] = _dot(ub[:, ucols], bre_ref[j])
        xim_s[:, scols] = _dot(ub[:, ucols], bim_ref[j])

    def scan(j):
        for c in range(STATE_PER_BLK // lane_chunk):
            cols = slice(j * STATE_PER_BLK + c * lane_chunk, j * STATE_PER_BLK + (c + 1) * lane_chunk)
            a_re = jnp.broadcast_to(are_ref[:, cols], (nb, lane_chunk))
            a_im = jnp.broadcast_to(aim_ref[:, cols], (nb, lane_chunk))
            s_re, s_im = sre_ref[:, cols], sim_ref[:, cols]
            for t in range(steps):
                sl = slice(t * nb, (t + 1) * nb)
                s_re, s_im = ((a_re * s_re - a_im * s_im) + xre_s[sl, cols],
                              (a_re * s_im + a_im * s_re) + xim_s[sl, cols])
                xre_s[sl, cols] = s_re
                xim_s[sl, cols] = s_im
            sre_ref[:, cols] = s_re
            sim_ref[:, cols] = s_im

    ys = []
    expand(0)
    for j in range(N_BLK):
        if j + 1 < N_BLK:
            expand(j + 1)
        scan(j)
        scols = slice(j * STATE_PER_BLK, (j + 1) * STATE_PER_BLK)
        ys.append(_dot(xre_s[:, scols].astype(BF16), cre_ref[j]) - _dot(xim_s[:, scols].astype(BF16), cim_ref[j]))
    y = jnp.concatenate(ys, axis=1) + dskip_ref[...] * u
    o = _dot(_gelu(y).astype(BF16), wglu_ref[...])
    o_ref[...] = x + o[:, :D_MODEL] * jax.nn.sigmoid(o[:, D_MODEL:])


def _s5(x, p, sre0, sim0, nb, steps, lane_chunk):
    rows = nb * steps
    row_spec = pl.BlockSpec((rows, D_MODEL), lambda i: (i, 0))
    vec = _const_spec((1, D_MODEL))
    svec = _const_spec((1, D_STATE))
    state_spec = _const_spec((nb, D_STATE))
    bspec = _const_spec((N_BLK, V7X_MXU_DIM, STATE_PER_BLK))
    cspec = _const_spec((N_BLK, STATE_PER_BLK, V7X_MXU_DIM))
    state_out = pl.BlockSpec((nb, D_STATE), lambda i: (0, 0))
    return pl.pallas_call(
        functools.partial(_s5_kernel, nb=nb, steps=steps, lane_chunk=lane_chunk),
        grid=(x.shape[0] // rows,),
        in_specs=[row_spec, vec, _const_spec((D_MODEL, D_MODEL)), bspec, bspec, svec, svec, cspec, cspec, vec,
                  _const_spec((D_MODEL, 2 * D_MODEL)), state_spec, state_spec],
        out_specs=[row_spec, state_out, state_out],
        out_shape=[jax.ShapeDtypeStruct(x.shape, F32), jax.ShapeDtypeStruct((nb, D_STATE), F32),
                   jax.ShapeDtypeStruct((nb, D_STATE), F32)],
        scratch_shapes=[pltpu.VMEM((rows, D_STATE), F32), pltpu.VMEM((rows, D_STATE), F32)],
        compiler_params=_params(),
        name="s5",
    )(x, p["g"], p["win"], p["bre"], p["bim"], p["are"], p["aim"], p["cre"], p["cim"], p["dskip"], p["wglu"],
      sre0, sim0)


def _block_diag(w, n_per_blk):
    n, k_in, k_out = w.shape
    wb = w.reshape(n // n_per_blk, n_per_blk, k_in, k_out)
    eye = jnp.eye(n_per_blk, dtype=w.dtype)
    out = jnp.einsum("jgio,gk->jgiko", wb, eye)
    return out.reshape(n // n_per_blk, n_per_blk * k_in, n_per_blk * k_out)


def _to_rows(a):
    return jnp.swapaxes(a, 0, 1).reshape(a.shape[0] * a.shape[1], a.shape[2])


def _from_rows(a, nb):
    return jnp.swapaxes(a.reshape(a.shape[0] // nb, nb, a.shape[1]), 0, 1)


def kernel(x_prompt, x_sample, state_rglru_conv, state_rglru_h, state_s5_re, state_s5_im, norm_mix, norm_ffn, norm_f, w_ff1, w_ff2, w_in_a, sgu_g, w_s, b_s, w_out_a, w_in_b, conv_w, conv_b, w_a, b_a, w_x, b_x, lam, w_out_b, w_in_c, lam_re, lam_im, log_dt, b_re, b_im, c_re, c_im, d_skip, w_glu):
    bp, tp, _ = x_prompt.shape
    bs, ts, _ = x_sample.shape
    rows_p, rows_s = bp * tp, bs * ts
    assert DEPTH % N_MIXERS == 1 and DEPTH > 1, "first and last layers must be SGU layers"
    assert tp % CHUNK == 0 and ts < CHUNK
    assert rows_s % FFN_ROWS == 0 and rows_p % FFN_ROWS == 0

    row = lambda v: v.reshape(1, -1)
    norm_ffn3 = norm_ffn.reshape(DEPTH, 1, D_MODEL)
    w1, w2, gf = w_ff1.astype(BF16), w_ff2.astype(BF16), row(norm_f)

    x_p, x_s = x_prompt, _to_rows(x_sample)
    outs_v, conv_p, h_p, conv_s, h_s, sre_p, sim_p, sre_s, sim_s = [], [], [], [], [], [], [], [], []
    for layer in range(DEPTH):
        j, kind = layer // N_MIXERS, layer % N_MIXERS
        first, last = layer == 0, layer == DEPTH - 1
        g = row(norm_mix[layer])
        if kind == 0:
            p = dict(g=g, win=w_in_a[j].astype(BF16), wout=w_out_a[j].astype(BF16), sg=row(sgu_g[j]),
                     w_tril=jnp.where(jnp.tril(jnp.ones((CHUNK, CHUNK), dtype=bool)), w_s[j], 0.0).astype(BF16),
                     bias=jnp.repeat(b_s[j].T, HD_A, axis=1),
                     wl=jnp.repeat(w_s[j][:, :ts, :ts].reshape(HEADS_A, ts * ts).T, HD_A, axis=1),
                     bl=jnp.repeat(b_s[j][:, :ts].T, HD_A, axis=1))
            if first:
                x_p = _sgu_prompt(x_p, p, bp, tp, False, True)
            elif last:
                x_p = _sgu_prompt(x_p, p, bp, tp, True, False).reshape(rows_p, D_MODEL)
            else:
                raise NotImplementedError("interior SGU layers")
            x_s, v = _sgu_sample(x_s, p, bs, ts)
            outs_v.append(_from_rows(v, bs))
        elif kind == 1:
            p = dict(g=g, win=w_in_b[j].astype(BF16), cw=conv_w[j], cb=row(conv_b[j]),
                     wa=_block_diag(w_a[j], HEADS_PER_BLK).astype(BF16), ba=row(b_a[j]),
                     wx=_block_diag(w_x[j], HEADS_PER_BLK).astype(BF16), bx=row(b_x[j]),
                     lam=row(lam[j]), wout=w_out_b[j].astype(BF16))
            dt_s = state_rglru_h.dtype
            x_p, cp, hp = _rglru(x_p, p, jnp.zeros(((CONV_W - 1) * bp, D_MODEL), dt_s),
                                 jnp.zeros((bp, D_MODEL), dt_s), bp, RGLRU_PROMPT_STEPS)
            x_s, cs, hs = _rglru(x_s, p, _to_rows(state_rglru_conv[j]), state_rglru_h[j], bs, ts)
            conv_p.append(_from_rows(cp, bp)); h_p.append(hp)
            conv_s.append(_from_rows(cs, bs)); h_s.append(hs)
        else:
            are, aim, bbre, bbim, pw = _s5_disc(lam_re[j], lam_im[j], log_dt[j],
                                                jnp.swapaxes(b_re[j], 1, 2), jnp.swapaxes(b_im[j], 1, 2))
            p = dict(g=g, win=w_in_c[j].astype(BF16),
                     bre=_block_diag(bbre, GROUPS_PER_BLK).astype(BF16),
                     bim=_block_diag(bbim, GROUPS_PER_BLK).astype(BF16),
                     are=are.reshape(1, D_STATE), aim=aim.reshape(1, D_STATE),
                     cre=_block_diag(jnp.swapaxes(c_re[j], 1, 2), GROUPS_PER_BLK).astype(BF16),
                     cim=_block_diag(jnp.swapaxes(c_im[j], 1, 2), GROUPS_PER_BLK).astype(BF16),
                     dskip=row(d_skip[j]), wglu=w_glu[j].astype(BF16))
            pwg = jnp.transpose(pw, (2, 0, 1, 3))
            kconv, wst, kin = _s5_taps(pwg[:, :, :S5_L], pwg[:, :, 1:], pwg[:, :, S5_L - 1::-1],
                                       jnp.stack([c_re[j], c_im[j]], axis=1), jnp.stack([bbre, bbim], axis=1))
            zs = jnp.zeros((bp, G_C, P_C), state_s5_re.dtype)
            u, u2 = _s5_in(x_p, g, p["win"], bp, S5_TILE_BLOCKS)
            y2, sf = _s5_core(u2, _pair_cols(wst[:, 0], wst[:, 1]), kconv, _pair_rows(kin[:, 0], kin[:, 1]),
                              _pair_state(pw[0, S5_L], pw[1, S5_L]).reshape(1, -1), _pair_state(zs, zs), bp)
            x_p = _s5_out(x_p, u, y2, p["dskip"], p["wglu"], bp, S5_TILE_BLOCKS)
            rp, ip = _unpair_state(sf, bp)
            x_s, rs, is_ = _s5(x_s, p, state_s5_re[j].reshape(bs, D_STATE), state_s5_im[j].reshape(bs, D_STATE),
                               bs, ts, V7X_LANES)
            sre_p.append(rp); sim_p.append(ip)
            sre_s.append(rs.reshape(bs, G_C, P_C)); sim_s.append(is_.reshape(bs, G_C, P_C))
        x_p, x_s = _ffn(x_p, x_s, norm_ffn3, w1, w2, gf, layer, last)

    y_prompt = x_p.reshape(bp, tp, D_MODEL)
    y_sample = _from_rows(x_s, bs)
    return (y_prompt, y_sample, jnp.stack(outs_v), jnp.stack(conv_p), jnp.stack(h_p), jnp.stack(conv_s),
            jnp.stack(h_s), jnp.stack(sre_p), jnp.stack(sim_p), jnp.stack(sre_s), jnp.stack(sim_s))
```

```python
import functools

import jax
import jax.numpy as jnp
from jax import lax
from jax.experimental import pallas as pl
from jax.experimental.pallas import tpu as pltpu

F32 = jnp.float32
BF16 = jnp.bfloat16

D_MODEL = 1024
DEPTH = 4
N_MIXERS = 3
EPS = 1e-6
CHUNK = 128
HEADS_A = 8
HD_A = D_MODEL // HEADS_A
HEADS_B = 16
HD_B = D_MODEL // HEADS_B
CONV_W = 4
LRU_C = 8.0
GROUP_C = 16
G_C = D_MODEL // GROUP_C
P_C = 64
D_STATE = G_C * P_C
D_FF = 4 * D_MODEL

V7X_LANES = 128
V7X_MXU_DIM = 256
V7X_VMEM_BYTES = 64 * 1024 * 1024
VMEM_LIMIT = V7X_VMEM_BYTES - 8 * 1024 * 1024

N_BLK = D_MODEL // V7X_MXU_DIM
HEADS_PER_BLK = V7X_MXU_DIM // HD_B
GROUPS_PER_BLK = V7X_MXU_DIM // GROUP_C
STATE_PER_BLK = GROUPS_PER_BLK * P_C

FFN_ROWS = 512
FFN_COL_CHUNK = 1024
FFN_STAGE_BYTES = 2 * 1024 * 1024
S5_L = 16
S5_TILE_BLOCKS = 4
S5_PAIRS_PER_STEP = 2
GRANULES_PER_VREG = V7X_LANES // GROUP_C
RGLRU_PROMPT_STEPS = 64


def _rms(x, g):
    return (x * lax.rsqrt(jnp.mean(x * x, axis=-1, keepdims=True) + EPS)) * g


GELU_C0 = 0.7978845608028654
GELU_C1 = GELU_C0 * 0.044715


def _gelu(x):
    return x * (0.5 + 0.5 * jnp.tanh(x * (GELU_C0 + GELU_C1 * (x * x))))


def _dot(a, b):
    return jnp.dot(a, b, preferred_element_type=F32)


def _const_spec(shape):
    zeros = (0,) * len(shape)
    return pl.BlockSpec(shape, lambda i: zeros, pipeline_mode=pl.Buffered(1))


def _layer_spec(shape, layer):
    idx = (layer,) + (0,) * len(shape)
    return pl.BlockSpec((None,) + tuple(shape), lambda i: idx, pipeline_mode=pl.Buffered(1))


def _params():
    return pltpu.CompilerParams(dimension_semantics=("arbitrary",), vmem_limit_bytes=VMEM_LIMIT)


def _ffn_tile(x_ref, g_ref, w1_ref, w2_ref, gf_ref, o_ref, final_norm):
    x = x_ref[...]
    h = _rms(x, g_ref[...]).astype(BF16)
    acc = None
    for j in range(D_FF // FFN_COL_CHUNK):
        cols = slice(j * FFN_COL_CHUNK, (j + 1) * FFN_COL_CHUNK)
        a = jnp.square(jnp.maximum(_dot(h, w1_ref[:, cols]), 0.0)).astype(BF16)
        part = _dot(a, w2_ref[cols, :])
        acc = part if acc is None else acc + part
    y = x + acc
    if final_norm:
        y = _rms(y, gf_ref[...])
    o_ref[...] = y


def _load_as_bf16(src, dst, stage, sem, chunk_rows):
    n = src.shape[0] // chunk_rows
    copies = [pltpu.make_async_copy(src.at[pl.ds(k * chunk_rows, chunk_rows)], stage.at[k % 2], sem.at[k % 2])
              for k in range(n)]
    copies[0].start()
    for k in range(n):
        if k + 1 < n:
            copies[k + 1].start()
        copies[k].wait()
        dst[k * chunk_rows:(k + 1) * chunk_rows, :] = stage[k % 2].astype(BF16)


def _ffn_kernel(xp_ref, xs_ref, g_ref, w1_hbm, w2_hbm, gf_ref, op_ref, os_ref, w1_s, w2_s, stage1, stage2, sem,
                *, final_norm, n_p, layer):
    i = pl.program_id(0)

    @pl.when(i == 0)
    def _():
        _load_as_bf16(w1_hbm.at[layer], w1_s, stage1, sem.at[0], FFN_STAGE_BYTES // (4 * D_FF))
        _load_as_bf16(w2_hbm.at[layer], w2_s, stage2, sem.at[1], FFN_STAGE_BYTES // (4 * D_MODEL))

    @pl.when(i < n_p)
    def _():
        _ffn_tile(xp_ref, g_ref, w1_s, w2_s, gf_ref, op_ref, final_norm)

    @pl.when(i >= n_p)
    def _():
        _ffn_tile(xs_ref, g_ref, w1_s, w2_s, gf_ref, os_ref, final_norm)


def _ffn(x_p, x_s, g, w1, w2, gf, layer, final_norm):
    n_p, n_s = x_p.shape[0] // FFN_ROWS, x_s.shape[0] // FFN_ROWS
    p_spec = pl.BlockSpec((FFN_ROWS, D_MODEL), lambda i: (jnp.minimum(i, n_p - 1), 0))
    s_spec = pl.BlockSpec((FFN_ROWS, D_MODEL), lambda i: (jnp.maximum(i - n_p, 0), 0))
    hbm = pl.BlockSpec(memory_space=pl.ANY)
    return pl.pallas_call(
        functools.partial(_ffn_kernel, final_norm=final_norm, n_p=n_p, layer=layer),
        grid=(n_p + n_s,),
        in_specs=[p_spec, s_spec, _layer_spec((1, D_MODEL), layer), hbm, hbm, _const_spec((1, D_MODEL))],
        out_specs=[p_spec, s_spec],
        out_shape=[jax.ShapeDtypeStruct(x_p.shape, F32), jax.ShapeDtypeStruct(x_s.shape, F32)],
        scratch_shapes=[pltpu.VMEM((D_MODEL, D_FF), BF16), pltpu.VMEM((D_FF, D_MODEL), BF16),
                        pltpu.VMEM((2, FFN_STAGE_BYTES // (4 * D_FF), D_FF), F32),
                        pltpu.VMEM((2, FFN_STAGE_BYTES // (4 * D_MODEL), D_MODEL), F32),
                        pltpu.SemaphoreType.DMA((2, 2))],
        compiler_params=_params(),
        name="ffn",
    )(x_p, x_s, g, w1, w2, gf)


def _sgu_front(x, g_ref, win_ref, sg_ref):
    h = _rms(x, g_ref[...]).astype(BF16)
    uv = _gelu(_dot(h, win_ref[...]))
    return uv[:, :D_MODEL], _rms(uv[:, D_MODEL:], sg_ref[...])


def _rows_to_batch_major(x, nb):
    steps = x.shape[0] // nb
    return jnp.swapaxes(x.reshape(steps, nb, x.shape[1]), 0, 1).reshape(x.shape)


def _rows_to_time_major(x, nb):
    steps = x.shape[0] // nb
    return jnp.swapaxes(x.reshape(nb, steps, x.shape[1]), 0, 1).reshape(x.shape)


def _sgu_prompt_kernel(x_ref, g_ref, win_ref, sg_ref, w_ref, bias_ref, wout_ref, o_ref, y_s, *, nb, in_tm, out_tm):
    rows = nb * CHUNK
    if in_tm:
        x = _rows_to_batch_major(x_ref[...], nb)
    else:
        x = x_ref[...].reshape(rows, D_MODEL)
    u, v = _sgu_front(x, g_ref, win_ref, sg_ref)
    vb = v.astype(BF16)
    for b in range(nb):
        rs = slice(b * CHUNK, (b + 1) * CHUNK)
        for g in range(HEADS_A):
            cs = slice(g * HD_A, (g + 1) * HD_A)
            mixed = _dot(w_ref[g], vb[rs, cs]) + bias_ref[:, cs]
            y_s[rs, cs] = (u[rs, cs] * mixed).astype(BF16)
    o = x + _dot(y_s[...], wout_ref[...])
    if out_tm:
        o_ref[...] = _rows_to_time_major(o, nb)
    else:
        o_ref[...] = o.reshape(nb, CHUNK, D_MODEL)


def _sgu_sample_kernel(x_ref, g_ref, win_ref, sg_ref, wl_ref, bl_ref, wout_ref, o_ref, v_ref, *, nb, steps):
    x = x_ref[...]
    u, v = _sgu_front(x, g_ref, win_ref, sg_ref)
    v_ref[...] = v
    mixed = []
    for t in range(steps):
        m = None
        for s in range(t + 1):
            term = wl_ref[t * steps + s:t * steps + s + 1, :] * v[s * nb:(s + 1) * nb, :]
            m = term if m is None else m + term
        mixed.append(m + bl_ref[t:t + 1, :])
    y = (u * jnp.concatenate(mixed, axis=0)).astype(BF16)
    o_ref[...] = x + _dot(y, wout_ref[...])


def _sgu_prompt(x, p, nb, n_steps, in_tm, out_tm):
    rows = nb * CHUNK
    tm_spec = pl.BlockSpec((rows, D_MODEL), lambda i: (i, 0))
    bm_spec = pl.BlockSpec((nb, CHUNK, D_MODEL), lambda i: (0, i, 0))
    out_shape = (jax.ShapeDtypeStruct((nb * n_steps, D_MODEL), F32) if out_tm
                 else jax.ShapeDtypeStruct((nb, n_steps, D_MODEL), F32))
    return pl.pallas_call(
        functools.partial(_sgu_prompt_kernel, nb=nb, in_tm=in_tm, out_tm=out_tm),
        grid=(n_steps // CHUNK,),
        in_specs=[tm_spec if in_tm else bm_spec, _const_spec((1, D_MODEL)), _const_spec((D_MODEL, 2 * D_MODEL)),
                  _const_spec((1, D_MODEL)), _const_spec((HEADS_A, CHUNK, CHUNK)),
                  _const_spec((CHUNK, D_MODEL)), _const_spec((D_MODEL, D_MODEL))],
        out_specs=tm_spec if out_tm else bm_spec,
        out_shape=out_shape,
        scratch_shapes=[pltpu.VMEM((rows, D_MODEL), BF16)],
        compiler_params=_params(),
        name="sgu_prompt",
    )(x, p["g"], p["win"], p["sg"], p["w_tril"], p["bias"], p["wout"])


def _sgu_sample(x, p, nb, steps):
    tile = steps * nb
    tile_spec = pl.BlockSpec((tile, D_MODEL), lambda i: (0, 0))
    return pl.pallas_call(
        functools.partial(_sgu_sample_kernel, nb=nb, steps=steps),
        grid=(1,),
        in_specs=[tile_spec, _const_spec((1, D_MODEL)), _const_spec((D_MODEL, 2 * D_MODEL)),
                  _const_spec((1, D_MODEL)), _const_spec((steps * steps, D_MODEL)),
                  _const_spec((steps, D_MODEL)), _const_spec((D_MODEL, D_MODEL))],
        out_specs=[tile_spec, tile_spec],
        out_shape=[jax.ShapeDtypeStruct((tile, D_MODEL), F32), jax.ShapeDtypeStruct((tile, D_MODEL), F32)],
        compiler_params=_params(),
        name="sgu_sample",
    )(x, p["g"], p["win"], p["sg"], p["wl"], p["bl"], p["wout"])


def _rglru_kernel(x_ref, g_ref, win_ref, cw_ref, cb_ref, wa_ref, ba_ref, wx_ref, bx_ref, lam_ref, wout_ref,
                  conv0_ref, h0_ref, o_ref, conv_ref, h_ref, xext_s, a_s, b_s, y_s, *, nb, steps):
    rows = nb * steps
    halo = (CONV_W - 1) * nb

    @pl.when(pl.program_id(0) == 0)
    def _():
        conv_ref[...] = conv0_ref[...]
        h_ref[...] = h0_ref[...]

    def blk_cols(j):
        return slice(j * V7X_MXU_DIM, (j + 1) * V7X_MXU_DIM)

    x = x_ref[...]
    h = _rms(x, g_ref[...]).astype(BF16)
    for j in range(N_BLK):
        cols = blk_cols(j)
        xext, a_j, b_j = xext_s.at[j], a_s.at[j], b_s.at[j]
        xext[0:halo, :] = conv_ref[:, cols]
        xext[halo:halo + rows, :] = _dot(h, win_ref[:, blk_cols(N_BLK + j)])
        gate = _gelu(_dot(h, win_ref[:, cols]))
        conv = None
        for w in range(CONV_W):
            term = xext[w * nb:w * nb + rows, :] * cw_ref[w:w + 1, cols]
            conv = term if conv is None else conv + term
        conv_ref[:, cols] = xext[rows:rows + halo, :]
        xc = cb_ref[:, cols] + conv
        xcb = xc.astype(BF16)
        r = jax.nn.sigmoid(_dot(xcb, wa_ref[j]) + ba_ref[:, cols])
        ig = jax.nn.sigmoid(_dot(xcb, wx_ref[j]) + bx_ref[:, cols])
        neg_log_a = (LRU_C * r) * jax.nn.softplus(-lam_ref[:, cols])
        a = jnp.exp(-neg_log_a)
        a_j[...] = a
        w = jnp.tanh(neg_log_a) * (a * a + 1.0)
        mult = jnp.where(w > 0.0, w * lax.rsqrt(w), 0.0)
        b_j[...] = mult * (ig * xc)
        hcur = h_ref[:, cols]
        for t in range(steps):
            sl = slice(t * nb, (t + 1) * nb)
            hcur = a_j[sl, :] * hcur + b_j[sl, :]
            b_j[sl, :] = hcur
        h_ref[:, cols] = hcur
        y_s[:, cols] = (b_j[...] * gate).astype(BF16)
    o_ref[...] = x + _dot(y_s[...], wout_ref[...])


def _rglru(x, p, conv0, h0, nb, steps):
    rows = nb * steps
    halo = (CONV_W - 1) * nb
    row_spec = pl.BlockSpec((rows, D_MODEL), lambda i: (i, 0))
    vec = _const_spec((1, D_MODEL))
    blk = _const_spec((N_BLK, V7X_MXU_DIM, V7X_MXU_DIM))
    return pl.pallas_call(
        functools.partial(_rglru_kernel, nb=nb, steps=steps),
        grid=(x.shape[0] // rows,),
        in_specs=[row_spec, vec, _const_spec((D_MODEL, 2 * D_MODEL)), _const_spec((CONV_W, D_MODEL)), vec,
                  blk, vec, blk, vec, vec, _const_spec((D_MODEL, D_MODEL)),
                  _const_spec((halo, D_MODEL)), _const_spec((nb, D_MODEL))],
        out_specs=[row_spec, pl.BlockSpec((halo, D_MODEL), lambda i: (0, 0)),
                   pl.BlockSpec((nb, D_MODEL), lambda i: (0, 0))],
        out_shape=[jax.ShapeDtypeStruct(x.shape, F32), jax.ShapeDtypeStruct((halo, D_MODEL), F32),
                   jax.ShapeDtypeStruct((nb, D_MODEL), F32)],
        scratch_shapes=[pltpu.VMEM((N_BLK, rows + halo, V7X_MXU_DIM), F32), pltpu.VMEM((N_BLK, rows, V7X_MXU_DIM), F32),
                        pltpu.VMEM((N_BLK, rows, V7X_MXU_DIM), F32), pltpu.VMEM((rows, D_MODEL), BF16)],
        compiler_params=_params(),
        name="rglru",
    )(x, p["g"], p["win"], p["cw"], p["cb"], p["wa"], p["ba"], p["wx"], p["bx"], p["lam"], p["wout"], conv0, h0)


def _cmul(a_re, a_im, b_re, b_im):
    return a_re * b_re - a_im * b_im, a_re * b_im + a_im * b_re


def _s5_disc_kernel(lre_ref, lim_ref, ldt_ref, bre_ref, bim_ref, are_ref, aim_ref, bbre_ref, bbim_ref, pw_ref):
    lr, li = lre_ref[...], lim_ref[...]
    dt = jnp.exp(ldt_ref[...])
    mag = jnp.exp(lr * dt)
    ab_re, ab_im = mag * jnp.cos(li * dt), mag * jnp.sin(li * dt)
    zr, zi = ab_re - 1.0, ab_im
    den = lr * lr + li * li
    q_re = (zr * lr + zi * li) / den
    q_im = (zi * lr - zr * li) / den
    are_ref[...] = ab_re
    aim_ref[...] = ab_im
    br, bi = bre_ref[...], bim_ref[...]
    bbre_ref[...] = q_re[:, None, :] * br - q_im[:, None, :] * bi
    bbim_ref[...] = q_re[:, None, :] * bi + q_im[:, None, :] * br
    p_re, p_im = jnp.ones_like(ab_re), jnp.zeros_like(ab_re)
    for k in range(S5_L + 1):
        pw_ref[0, k] = p_re
        pw_ref[1, k] = p_im
        p_re, p_im = _cmul(p_re, p_im, ab_re, ab_im)


def _s5_disc(lam_re, lam_im, log_dt, b_re_t, b_im_t):
    gp = jax.ShapeDtypeStruct((G_C, P_C), F32)
    ghp = jax.ShapeDtypeStruct((G_C, GROUP_C, P_C), F32)
    pw = jax.ShapeDtypeStruct((2, S5_L + 1, G_C, P_C), F32)
    return pl.pallas_call(_s5_disc_kernel, out_shape=[gp, gp, ghp, ghp, pw], name="s5_disc")(
        lam_re, lam_im, log_dt.reshape(G_C, 1), b_re_t, b_im_t)


def _s5_taps_kernel(pw0_ref, pw1_ref, pwf_ref, c_ref, b_ref, kconv_ref, wst_ref, kin_ref, *, groups):
    n = S5_L * GROUP_C

    def rep(t):
        return jnp.broadcast_to(t[:, None, :], (S5_L, GROUP_C, P_C)).reshape(n, P_C)

    def til(c):
        return jnp.broadcast_to(c[None, :, :], (S5_L, GROUP_C, P_C)).reshape(n, P_C)

    def dot_t(a, b):
        return lax.dot_general(a, b, (((1,), (1,)), ((), ())), precision=lax.Precision.HIGHEST,
                               preferred_element_type=F32)

    lane = lax.broadcasted_iota(jnp.int32, (GROUP_C, n), 1)
    for gi in range(groups):
        cr, ci = til(c_ref[gi, 0]), til(c_ref[gi, 1])
        b_re, b_im = b_ref[gi, 0], b_ref[gi, 1]
        x_re, x_im = _cmul(rep(pw0_ref[gi, 0]), rep(pw0_ref[gi, 1]), cr, ci)
        mrow = dot_t(b_re, x_re) - dot_t(b_im, x_im)
        for s in range(S5_L):
            blk = mrow if s == 0 else jnp.where(lane >= s * GROUP_C, pltpu.roll(mrow, s * GROUP_C, 1), 0.0)
            kconv_ref[gi, s * GROUP_C:(s + 1) * GROUP_C, :] = blk.astype(BF16)
        e_re, e_im = _cmul(rep(pw1_ref[gi, 0]), rep(pw1_ref[gi, 1]), cr, ci)
        kin_ref[gi, 0] = e_re.T.astype(BF16)
        kin_ref[gi, 1] = (-e_im).T.astype(BF16)
        f_re, f_im = _cmul(rep(pwf_ref[gi, 0]), rep(pwf_ref[gi, 1]), til(b_re), til(b_im))
        wst_ref[gi, 0] = f_re.astype(BF16)
        wst_ref[gi, 1] = f_im.astype(BF16)


def _s5_taps(pw0, pw1, pwf, c, bb):
    groups = 8
    n = S5_L * GROUP_C
    pspec = pl.BlockSpec((groups, 2, S5_L, P_C), lambda i: (i, 0, 0, 0))
    hspec = pl.BlockSpec((groups, 2, GROUP_C, P_C), lambda i: (i, 0, 0, 0))
    return pl.pallas_call(
        functools.partial(_s5_taps_kernel, groups=groups),
        grid=(G_C // groups,),
        in_specs=[pspec, pspec, pspec, hspec, hspec],
        out_specs=[pl.BlockSpec((groups, n, n), lambda i: (i, 0, 0)),
                   pl.BlockSpec((groups, 2, n, P_C), lambda i: (i, 0, 0, 0)),
                   pl.BlockSpec((groups, 2, P_C, n), lambda i: (i, 0, 0, 0))],
        out_shape=[jax.ShapeDtypeStruct((G_C, n, n), BF16), jax.ShapeDtypeStruct((G_C, 2, n, P_C), BF16),
                   jax.ShapeDtypeStruct((G_C, 2, P_C, n), BF16)],
        compiler_params=_params(),
        name="s5_taps",
    )(pw0, pw1, pwf, c, bb)


def _granule_transpose(vs):
    vs = list(vs)
    granule = lax.broadcasted_iota(jnp.int32, vs[0].shape, 1) // GROUP_C
    d = 1
    while d < len(vs):
        keep = (granule & d) == 0
        for i in range(len(vs)):
            if i & d == 0:
                a, b = vs[i], vs[i + d]
                vs[i] = jnp.where(keep, a, pltpu.roll(b, d * GROUP_C, 1))
                vs[i + d] = jnp.where(keep, pltpu.roll(a, V7X_LANES - d * GROUP_C, 1), b)
        d *= 2
    return vs


def _s5_in_kernel(x_ref, g_ref, win_ref, u_ref, u2_ref, *, nb, blocks):
    h = _rms(x_ref[...], g_ref[...]).astype(BF16)
    rows2 = blocks * nb
    for cc in range(N_BLK):
        cols = slice(cc * V7X_MXU_DIM, (cc + 1) * V7X_MXU_DIM)
        u = _dot(h, win_ref[:, cols])
        u_ref[:, cols] = u
        u4 = u.reshape(blocks, S5_L, nb, V7X_MXU_DIM)
        for sub in range(V7X_MXU_DIM // V7X_LANES):
            c = cc * (V7X_MXU_DIM // V7X_LANES) + sub
            for half in range(S5_L // GRANULES_PER_VREG):
                vs = []
                for kk in range(GRANULES_PER_VREG):
                    slab = u4[:, half * GRANULES_PER_VREG + kk, :, sub * V7X_LANES:(sub + 1) * V7X_LANES]
                    vs.append(slab.reshape(rows2, V7X_LANES))
                ws = _granule_transpose(vs)
                for gi in range(GRANULES_PER_VREG):
                    lane0 = (c * GRANULES_PER_VREG + gi) * S5_L * GROUP_C + half * V7X_LANES
                    u2_ref[:, lane0:lane0 + V7X_LANES] = ws[gi].astype(BF16)


def _s5_in(x, g, win, nb, blocks):
    rows = blocks * S5_L * nb
    n_blocks = x.shape[0] // (S5_L * nb)
    row_spec = pl.BlockSpec((rows, D_MODEL), lambda i: (i, 0))
    return pl.pallas_call(
        functools.partial(_s5_in_kernel, nb=nb, blocks=blocks),
        grid=(x.shape[0] // rows,),
        in_specs=[row_spec, _const_spec((1, D_MODEL)), _const_spec((D_MODEL, D_MODEL))],
        out_specs=[row_spec, pl.BlockSpec((blocks * nb, S5_L * D_MODEL), lambda i: (i, 0))],
        out_shape=[jax.ShapeDtypeStruct(x.shape, F32),
                   jax.ShapeDtypeStruct((n_blocks * nb, S5_L * D_MODEL), BF16)],
        compiler_params=_params(),
        name="s5_in",
    )(x, g, win)


def _s5_core_kernel(u_ref, wst_ref, kconv_ref, kin_ref, a_ref, s0_ref, y_ref, sf_ref, sc_s, sin_s, *, nb, n_blocks):
    pw = 2 * S5_L * GROUP_C
    sw = 2 * P_C
    ub = [u_ref[:, q * pw:(q + 1) * pw] for q in range(S5_PAIRS_PER_STEP)]
    for q in range(S5_PAIRS_PER_STEP):
        sc_s[q] = _dot(ub[q], wst_ref[q])
    a_re = [jnp.broadcast_to(a_ref[:, 2 * sw * q:2 * sw * q + sw], (nb, sw)) for q in range(S5_PAIRS_PER_STEP)]
    a_im = [jnp.broadcast_to(a_ref[:, 2 * sw * q + sw:2 * sw * (q + 1)], (nb, sw)) for q in range(S5_PAIRS_PER_STEP)]
    s_re = [s0_ref[:, 2 * sw * q:2 * sw * q + sw] for q in range(S5_PAIRS_PER_STEP)]
    s_im = [s0_ref[:, 2 * sw * q + sw:2 * sw * (q + 1)] for q in range(S5_PAIRS_PER_STEP)]
    for blk in range(n_blocks):
        rs = slice(blk * nb, (blk + 1) * nb)
        for q in range(S5_PAIRS_PER_STEP):
            sin_s[q, rs, 0:sw] = s_re[q]
            sin_s[q, rs, sw:2 * sw] = s_im[q]
            n_re, n_im = _cmul(a_re[q], a_im[q], s_re[q], s_im[q])
            s_re[q], s_im[q] = n_re + sc_s[q, rs, 0:sw], n_im + sc_s[q, rs, sw:2 * sw]
    for q in range(S5_PAIRS_PER_STEP):
        sf_ref[:, 2 * sw * q:2 * sw * q + sw] = s_re[q]
        sf_ref[:, 2 * sw * q + sw:2 * sw * (q + 1)] = s_im[q]
        half = S5_L * GROUP_C
        conv = jnp.concatenate([_dot(ub[q][:, :half], kconv_ref[2 * q]),
                                _dot(ub[q][:, half:], kconv_ref[2 * q + 1])], axis=1)
        y_ref[:, q * pw:(q + 1) * pw] = conv + _dot(sin_s[q].astype(BF16), kin_ref[q])


def _s5_core(u2, wst, kconv, kin, a_pair, s0, nb):
    rows = u2.shape[0]
    pw = 2 * S5_L * GROUP_C
    sw = 2 * P_C
    cols = S5_PAIRS_PER_STEP * pw
    scols = S5_PAIRS_PER_STEP * 2 * sw
    return pl.pallas_call(
        functools.partial(_s5_core_kernel, nb=nb, n_blocks=rows // nb),
        grid=(u2.shape[1] // cols,),
        in_specs=[pl.BlockSpec((rows, cols), lambda i: (0, i)),
                  pl.BlockSpec((S5_PAIRS_PER_STEP, pw, 2 * sw), lambda i: (i, 0, 0)),
                  pl.BlockSpec((2 * S5_PAIRS_PER_STEP, pw // 2, pw // 2), lambda i: (i, 0, 0)),
                  pl.BlockSpec((S5_PAIRS_PER_STEP, 2 * sw, pw), lambda i: (i, 0, 0)),
                  pl.BlockSpec((1, scols), lambda i: (0, i)), pl.BlockSpec((nb, scols), lambda i: (0, i))],
        out_specs=[pl.BlockSpec((rows, cols), lambda i: (0, i)), pl.BlockSpec((nb, scols), lambda i: (0, i))],
        out_shape=[jax.ShapeDtypeStruct(u2.shape, F32), jax.ShapeDtypeStruct((nb, 2 * D_STATE), F32)],
        scratch_shapes=[pltpu.VMEM((S5_PAIRS_PER_STEP, rows, 2 * sw), F32),
                        pltpu.VMEM((S5_PAIRS_PER_STEP, rows, 2 * sw), F32)],
        compiler_params=_params(),
        name="s5_core",
    )(u2, wst, kconv, kin, a_pair, s0)


def _s5_out_kernel(x_ref, u_ref, y2_ref, dskip_ref, wglu_ref, o_ref, y_s, *, nb, blocks):
    rows = blocks * S5_L * nb
    o = None
    for cc in range(N_BLK):
        cols = slice(cc * V7X_MXU_DIM, (cc + 1) * V7X_MXU_DIM)
        for sub in range(V7X_MXU_DIM // V7X_LANES):
            c = cc * (V7X_MXU_DIM // V7X_LANES) + sub
            for half in range(S5_L // GRANULES_PER_VREG):
                lanes = [(c * GRANULES_PER_VREG + gi) * S5_L * GROUP_C + half * V7X_LANES
                         for gi in range(GRANULES_PER_VREG)]
                ws = _granule_transpose([y2_ref[:, l0:l0 + V7X_LANES] for l0 in lanes])
                for kk in range(GRANULES_PER_VREG):
                    y_s[cc, :, half * GRANULES_PER_VREG + kk, :, sub * V7X_LANES:(sub + 1) * V7X_LANES] = (
                        ws[kk].reshape(blocks, nb, V7X_LANES))
        y = y_s[cc].reshape(rows, V7X_MXU_DIM) + dskip_ref[:, cols] * u_ref[:, cols]
        part = _dot(_gelu(y).astype(BF16), wglu_ref[cols, :])
        o = part if o is None else o + part
    o_ref[...] = x_ref[...] + o[:, :D_MODEL] * jax.nn.sigmoid(o[:, D_MODEL:])


def _s5_out(x, u, y2, dskip, wglu, nb, blocks):
    rows = blocks * S5_L * nb
    row_spec = pl.BlockSpec((rows, D_MODEL), lambda i: (i, 0))
    return pl.pallas_call(
        functools.partial(_s5_out_kernel, nb=nb, blocks=blocks),
        grid=(x.shape[0] // rows,),
        in_specs=[row_spec, row_spec, pl.BlockSpec((blocks * nb, S5_L * D_MODEL), lambda i: (i, 0)),
                  _const_spec((1, D_MODEL)), _const_spec((D_MODEL, 2 * D_MODEL))],
        out_specs=row_spec,
        out_shape=jax.ShapeDtypeStruct(x.shape, F32),
        scratch_shapes=[pltpu.VMEM((N_BLK, blocks, S5_L, nb, V7X_MXU_DIM), F32)],
        compiler_params=_params(),
        name="s5_out",
    )(x, u, y2, dskip, wglu)


def _pair_cols(re, im):
    z = jnp.zeros_like(re[0::2])
    top = jnp.concatenate([re[0::2], z, im[0::2], z], axis=-1)
    bot = jnp.concatenate([z, re[1::2], z, im[1::2]], axis=-1)
    return jnp.concatenate([top, bot], axis=-2)


def _pair_rows(re, im):
    z = jnp.zeros_like(re[0::2])
    left = jnp.concatenate([re[0::2], z, im[0::2], z], axis=-2)
    right = jnp.concatenate([z, re[1::2], z, im[1::2]], axis=-2)
    return jnp.concatenate([left, right], axis=-1)


def _pair_state(re, im):
    lead = re.shape[:-2]
    r = re.reshape(lead + (G_C // 2, 2 * P_C))
    i = im.reshape(lead + (G_C // 2, 2 * P_C))
    return jnp.concatenate([r, i], axis=-1).reshape(lead + (2 * D_STATE,))


def _unpair_state(s, nb):
    s4 = s.reshape(nb, G_C // 2, 2, 2 * P_C)
    return s4[:, :, 0].reshape(nb, G_C, P_C), s4[:, :, 1].reshape(nb, G_C, P_C)


def _s5_kernel(x_ref, g_ref, win_ref, bre_ref, bim_ref, are_ref, aim_ref, cre_ref, cim_ref, dskip_ref, wglu_ref,
               sre0_ref, sim0_ref, o_ref, sre_ref, sim_ref, xre_s, xim_s, *, nb, steps, lane_chunk):
    @pl.when(pl.program_id(0) == 0)
    def _():
        sre_ref[...] = sre0_ref[...]
        sim_ref[...] = sim0_ref[...]

    x = x_ref[...]
    h = _rms(x, g_ref[...]).astype(BF16)
    u = _dot(h, win_ref[...])
    ub = u.astype(BF16)

    def expand(j):
        ucols = slice(j * V7X_MXU_DIM, (j + 1) * V7X_MXU_DIM)
        scols = slice(j * STATE_PER_BLK, (j + 1) * STATE_PER_BLK)
        xre_s[:, scols] = _dot(ub[:, ucols], bre_ref[j])
        xim_s[:, scols] = _dot(ub[:, ucols], bim_ref[j])

    def scan(j):
        for c in range(STATE_PER_BLK // lane_chunk):
            cols = slice(j * STATE_PER_BLK + c * lane_chunk, j * STATE_PER_BLK + (c + 1) * lane_chunk)
            a_re = jnp.broadcast_to(are_ref[:, cols], (nb, lane_chunk))
            a_im = jnp.broadcast_to(aim_ref[:, cols], (nb, lane_chunk))
            s_re, s_im = sre_ref[:, cols], sim_ref[:, cols]
            for t in range(steps):
                sl = slice(t * nb, (t + 1) * nb)
                s_re, s_im = ((a_re * s_re - a_im * s_im) + xre_s[sl, cols],
                              (a_re * s_im + a_im * s_re) + xim_s[sl, cols])
                xre_s[sl, cols] = s_re
                xim_s[sl, cols] = s_im
            sre_ref[:, cols] = s_re
            sim_ref[:, cols] = s_im

    ys = []
    expand(0)
    for j in range(N_BLK):
        if j + 1 < N_BLK:
            expand(j + 1)
        scan(j)
        scols = slice(j * STATE_PER_BLK, (j + 1) * STATE_PER_BLK)
        ys.append(_dot(xre_s[:, scols].astype(BF16), cre_ref[j]) - _dot(xim_s[:, scols].astype(BF16), cim_ref[j]))
    y = jnp.concatenate(ys, axis=1) + dskip_ref[...] * u
    o = _dot(_gelu(y).astype(BF16), wglu_ref[...])
    o_ref[...] = x + o[:, :D_MODEL] * jax.nn.sigmoid(o[:, D_MODEL:])


def _s5(x, p, sre0, sim0, nb, steps, lane_chunk):
    rows = nb * steps
    row_spec = pl.BlockSpec((rows, D_MODEL), lambda i: (i, 0))
    vec = _const_spec((1, D_MODEL))
    svec = _const_spec((1, D_STATE))
    state_spec = _const_spec((nb, D_STATE))
    bspec = _const_spec((N_BLK, V7X_MXU_DIM, STATE_PER_BLK))
    cspec = _const_spec((N_BLK, STATE_PER_BLK, V7X_MXU_DIM))
    state_out = pl.BlockSpec((nb, D_STATE), lambda i: (0, 0))
    return pl.pallas_call(
        functools.partial(_s5_kernel, nb=nb, steps=steps, lane_chunk=lane_chunk),
        grid=(x.shape[0] // rows,),
        in_specs=[row_spec, vec, _const_spec((D_MODEL, D_MODEL)), bspec, bspec, svec, svec, cspec, cspec, vec,
                  _const_spec((D_MODEL, 2 * D_MODEL)), state_spec, state_spec],
        out_specs=[row_spec, state_out, state_out],
        out_shape=[jax.ShapeDtypeStruct(x.shape, F32), jax.ShapeDtypeStruct((nb, D_STATE), F32),
                   jax.ShapeDtypeStruct((nb, D_STATE), F32)],
        scratch_shapes=[pltpu.VMEM((rows, D_STATE), F32), pltpu.VMEM((rows, D_STATE), F32)],
        compiler_params=_params(),
        name="s5",
    )(x, p["g"], p["win"], p["bre"], p["bim"], p["are"], p["aim"], p["cre"], p["cim"], p["dskip"], p["wglu"],
      sre0, sim0)


def _block_diag(w, n_per_blk):
    n, k_in, k_out = w.shape
    wb = w.reshape(n // n_per_blk, n_per_blk, k_in, k_out)
    eye = jnp.eye(n_per_blk, dtype=w.dtype)
    out = jnp.einsum("jgio,gk->jgiko", wb, eye)
    return out.reshape(n // n_per_blk, n_per_blk * k_in, n_per_blk * k_out)


def _to_rows(a):
    return jnp.swapaxes(a, 0, 1).reshape(a.shape[0] * a.shape[1], a.shape[2])


def _from_rows(a, nb):
    return jnp.swapaxes(a.reshape(a.shape[0] // nb, nb, a.shape[1]), 0, 1)


def kernel(x_prompt, x_sample, state_rglru_conv, state_rglru_h, state_s5_re, state_s5_im, norm_mix, norm_ffn, norm_f, w_ff1, w_ff2, w_in_a, sgu_g, w_s, b_s, w_out_a, w_in_b, conv_w, conv_b, w_a, b_a, w_x, b_x, lam, w_out_b, w_in_c, lam_re, lam_im, log_dt, b_re, b_im, c_re, c_im, d_skip, w_glu):
    bp, tp, _ = x_prompt.shape
    bs, ts, _ = x_sample.shape
    rows_p, rows_s = bp * tp, bs * ts
    assert DEPTH % N_MIXERS == 1 and DEPTH > 1, "first and last layers must be SGU layers"
    assert tp % CHUNK == 0 and ts < CHUNK
    assert rows_s % FFN_ROWS == 0 and rows_p % FFN_ROWS == 0

    row = lambda v: v.reshape(1, -1)
    norm_ffn3 = norm_ffn.reshape(DEPTH, 1, D_MODEL)
    w1, w2, gf = w_ff1, w_ff2, row(norm_f)

    x_p, x_s = x_prompt, _to_rows(x_sample)
    outs_v, conv_p, h_p, conv_s, h_s, sre_p, sim_p, sre_s, sim_s = [], [], [], [], [], [], [], [], []
    for layer in range(DEPTH):
        j, kind = layer // N_MIXERS, layer % N_MIXERS
        first, last = layer == 0, layer == DEPTH - 1
        g = row(norm_mix[layer])
        if kind == 0:
            p = dict(g=g, win=w_in_a[j].astype(BF16), wout=w_out_a[j].astype(BF16), sg=row(sgu_g[j]),
                     w_tril=jnp.where(jnp.tril(jnp.ones((CHUNK, CHUNK), dtype=bool)), w_s[j], 0.0).astype(BF16),
                     bias=jnp.repeat(b_s[j].T, HD_A, axis=1),
                     wl=jnp.repeat(w_s[j][:, :ts, :ts].reshape(HEADS_A, ts * ts).T, HD_A, axis=1),
                     bl=jnp.repeat(b_s[j][:, :ts].T, HD_A, axis=1))
            if first:
                x_p = _sgu_prompt(x_p, p, bp, tp, False, True)
            elif last:
                x_p = _sgu_prompt(x_p, p, bp, tp, True, False).reshape(rows_p, D_MODEL)
            else:
                raise NotImplementedError("interior SGU layers")
            x_s, v = _sgu_sample(x_s, p, bs, ts)
            outs_v.append(_from_rows(v, bs))
        elif kind == 1:
            p = dict(g=g, win=w_in_b[j].astype(BF16), cw=conv_w[j], cb=row(conv_b[j]),
                     wa=_block_diag(w_a[j], HEADS_PER_BLK).astype(BF16), ba=row(b_a[j]),
                     wx=_block_diag(w_x[j], HEADS_PER_BLK).astype(BF16), bx=row(b_x[j]),
                     lam=row(lam[j]), wout=w_out_b[j].astype(BF16))
            dt_s = state_rglru_h.dtype
            x_p, cp, hp = _rglru(x_p, p, jnp.zeros(((CONV_W - 1) * bp, D_MODEL), dt_s),
                                 jnp.zeros((bp, D_MODEL), dt_s), bp, RGLRU_PROMPT_STEPS)
            x_s, cs, hs = _rglru(x_s, p, _to_rows(state_rglru_conv[j]), state_rglru_h[j], bs, ts)
            conv_p.append(_from_rows(cp, bp)); h_p.append(hp)
            conv_s.append(_from_rows(cs, bs)); h_s.append(hs)
        else:
            are, aim, bbre, bbim, pw = _s5_disc(lam_re[j], lam_im[j], log_dt[j],
                                                jnp.swapaxes(b_re[j], 1, 2), jnp.swapaxes(b_im[j], 1, 2))
            p = dict(g=g, win=w_in_c[j].astype(BF16),
                     bre=_block_diag(bbre, GROUPS_PER_BLK).astype(BF16),
                     bim=_block_diag(bbim, GROUPS_PER_BLK).astype(BF16),
                     are=are.reshape(1, D_STATE), aim=aim.reshape(1, D_STATE),
                     cre=_block_diag(jnp.swapaxes(c_re[j], 1, 2), GROUPS_PER_BLK).astype(BF16),
                     cim=_block_diag(jnp.swapaxes(c_im[j], 1, 2), GROUPS_PER_BLK).astype(BF16),
                     dskip=row(d_skip[j]), wglu=w_glu[j].astype(BF16))
            pwg = jnp.transpose(pw, (2, 0, 1, 3))
            kconv, wst, kin = _s5_taps(pwg[:, :, :S5_L], pwg[:, :, 1:], pwg[:, :, S5_L - 1::-1],
                                       jnp.stack([c_re[j], c_im[j]], axis=1), jnp.stack([bbre, bbim], axis=1))
            zs = jnp.zeros((bp, G_C, P_C), state_s5_re.dtype)
            u, u2 = _s5_in(x_p, g, p["win"], bp, S5_TILE_BLOCKS)
            y2, sf = _s5_core(u2, _pair_cols(wst[:, 0], wst[:, 1]), kconv, _pair_rows(kin[:, 0], kin[:, 1]),
                              _pair_state(pw[0, S5_L], pw[1, S5_L]).reshape(1, -1), _pair_state(zs, zs), bp)
            x_p = _s5_out(x_p, u, y2, p["dskip"], p["wglu"], bp, S5_TILE_BLOCKS)
            rp, ip = _unpair_state(sf, bp)
            x_s, rs, is_ = _s5(x_s, p, state_s5_re[j].reshape(bs, D_STATE), state_s5_im[j].reshape(bs, D_STATE),
                               bs, ts, V7X_LANES)
            sre_p.append(rp); sim_p.append(ip)
            sre_s.append(rs.reshape(bs, G_C, P_C)); sim_s.append(is_.reshape(bs, G_C, P_C))
        x_p, x_s = _ffn(x_p, x_s, norm_ffn3, w1, w2, gf, layer, last)

    y_prompt = x_p.reshape(bp, tp, D_MODEL)
    y_sample = _from_rows(x_s, bs)
    return (y_prompt, y_sample, jnp.stack(outs_v), jnp.stack(conv_p), jnp.stack(h_p), jnp.stack(conv_s),
            jnp.stack(h_s), jnp.stack(sre_p), jnp.stack(sim_p), jnp.stack(sre_s), jnp.stack(sim_s))
```

```python
import functools

import jax
import jax.numpy as jnp
from jax import lax
from jax.experimental import pallas as pl
from jax.experimental.pallas import tpu as pltpu

F32 = jnp.float32
BF16 = jnp.bfloat16

D_MODEL = 1024
DEPTH = 4
N_MIXERS = 3
EPS = 1e-6
CHUNK = 128
HEADS_A = 8
HD_A = D_MODEL // HEADS_A
HEADS_B = 16
HD_B = D_MODEL // HEADS_B
CONV_W = 4
LRU_C = 8.0
GROUP_C = 16
G_C = D_MODEL // GROUP_C
P_C = 64
D_STATE = G_C * P_C
D_FF = 4 * D_MODEL

V7X_LANES = 128
V7X_MXU_DIM = 256
V7X_VMEM_BYTES = 64 * 1024 * 1024
VMEM_LIMIT = V7X_VMEM_BYTES - 8 * 1024 * 1024

N_BLK = D_MODEL // V7X_MXU_DIM
HEADS_PER_BLK = V7X_MXU_DIM // HD_B
GROUPS_PER_BLK = V7X_MXU_DIM // GROUP_C
STATE_PER_BLK = GROUPS_PER_BLK * P_C

FFN_ROWS = 512
FFN_COL_CHUNK = 1024
FFN_STAGE_BYTES = 2 * 1024 * 1024
S5_L = 16
S5_TILE_BLOCKS = 4
S5_PAIRS_PER_STEP = 2
GRANULES_PER_VREG = V7X_LANES // GROUP_C
RGLRU_PROMPT_STEPS = 64


def _rms(x, g):
    return (x * lax.rsqrt(jnp.mean(x * x, axis=-1, keepdims=True) + EPS)) * g


GELU_C0 = 0.7978845608028654
GELU_C1 = GELU_C0 * 0.044715


def _gelu(x):
    return x * (0.5 + 0.5 * jnp.tanh(x * (GELU_C0 + GELU_C1 * (x * x))))


def _dot(a, b):
    return jnp.dot(a, b, preferred_element_type=F32)


def _const_spec(shape):
    zeros = (0,) * len(shape)
    return pl.BlockSpec(shape, lambda i: zeros, pipeline_mode=pl.Buffered(1))


def _layer_spec(shape, layer):
    idx = (layer,) + (0,) * len(shape)
    return pl.BlockSpec((None,) + tuple(shape), lambda i: idx, pipeline_mode=pl.Buffered(1))


def _params():
    return pltpu.CompilerParams(dimension_semantics=("arbitrary",), vmem_limit_bytes=VMEM_LIMIT)


def _ffn_tile(x_ref, g_ref, w1_ref, w2_ref, gf_ref, o_ref, final_norm):
    x = x_ref[...]
    h = _rms(x, g_ref[...]).astype(BF16)
    acc = None
    for j in range(D_FF // FFN_COL_CHUNK):
        cols = slice(j * FFN_COL_CHUNK, (j + 1) * FFN_COL_CHUNK)
        a = jnp.square(jnp.maximum(_dot(h, w1_ref[:, cols]), 0.0)).astype(BF16)
        part = _dot(a, w2_ref[cols, :])
        acc = part if acc is None else acc + part
    y = x + acc
    if final_norm:
        y = _rms(y, gf_ref[...])
    o_ref[...] = y


def _load_as_bf16(src, dst, stage, sem, chunk_rows):
    n = src.shape[0] // chunk_rows
    copies = [pltpu.make_async_copy(src.at[pl.ds(k * chunk_rows, chunk_rows)], stage.at[k % 2], sem.at[k % 2])
              for k in range(n)]
    copies[0].start()
    for k in range(n):
        if k + 1 < n:
            copies[k + 1].start()
        copies[k].wait()
        dst[k * chunk_rows:(k + 1) * chunk_rows, :] = stage[k % 2].astype(BF16)


def _ffn_kernel(xp_ref, xs_ref, g_ref, w1_hbm, w2_hbm, gf_ref, op_ref, os_ref, w1_s, w2_s, stage1, stage2,
                sem1, sem2, *, final_norm, n_p, layer):
    i = pl.program_id(0)

    @pl.when(i == 0)
    def _():
        _load_as_bf16(w1_hbm.at[layer], w1_s, stage1, sem1, FFN_STAGE_BYTES // (4 * D_FF))
        _load_as_bf16(w2_hbm.at[layer], w2_s, stage2, sem2, FFN_STAGE_BYTES // (4 * D_MODEL))

    @pl.when(i < n_p)
    def _():
        _ffn_tile(xp_ref, g_ref, w1_s, w2_s, gf_ref, op_ref, final_norm)

    @pl.when(i >= n_p)
    def _():
        _ffn_tile(xs_ref, g_ref, w1_s, w2_s, gf_ref, os_ref, final_norm)


def _ffn(x_p, x_s, g, w1, w2, gf, layer, final_norm):
    n_p, n_s = x_p.shape[0] // FFN_ROWS, x_s.shape[0] // FFN_ROWS
    p_spec = pl.BlockSpec((FFN_ROWS, D_MODEL), lambda i: (jnp.minimum(i, n_p - 1), 0))
    s_spec = pl.BlockSpec((FFN_ROWS, D_MODEL), lambda i: (jnp.maximum(i - n_p, 0), 0))
    hbm = pl.BlockSpec(memory_space=pl.ANY)
    return pl.pallas_call(
        functools.partial(_ffn_kernel, final_norm=final_norm, n_p=n_p, layer=layer),
        grid=(n_p + n_s,),
        in_specs=[p_spec, s_spec, _layer_spec((1, D_MODEL), layer), hbm, hbm, _const_spec((1, D_MODEL))],
        out_specs=[p_spec, s_spec],
        out_shape=[jax.ShapeDtypeStruct(x_p.shape, F32), jax.ShapeDtypeStruct(x_s.shape, F32)],
        scratch_shapes=[pltpu.VMEM((D_MODEL, D_FF), BF16), pltpu.VMEM((D_FF, D_MODEL), BF16),
                        pltpu.VMEM((2, FFN_STAGE_BYTES // (4 * D_FF), D_FF), F32),
                        pltpu.VMEM((2, FFN_STAGE_BYTES // (4 * D_MODEL), D_MODEL), F32),
                        pltpu.SemaphoreType.DMA((2,)), pltpu.SemaphoreType.DMA((2,))],
        compiler_params=_params(),
        name="ffn",
    )(x_p, x_s, g, w1, w2, gf)


def _sgu_front(x, g_ref, win_ref, sg_ref):
    h = _rms(x, g_ref[...]).astype(BF16)
    uv = _gelu(_dot(h, win_ref[...]))
    return uv[:, :D_MODEL], _rms(uv[:, D_MODEL:], sg_ref[...])


def _rows_to_batch_major(x, nb):
    steps = x.shape[0] // nb
    return jnp.swapaxes(x.reshape(steps, nb, x.shape[1]), 0, 1).reshape(x.shape)


def _rows_to_time_major(x, nb):
    steps = x.shape[0] // nb
    return jnp.swapaxes(x.reshape(nb, steps, x.shape[1]), 0, 1).reshape(x.shape)


def _sgu_prompt_kernel(x_ref, g_ref, win_ref, sg_ref, w_ref, bias_ref, wout_ref, o_ref, y_s, *, nb, in_tm, out_tm):
    rows = nb * CHUNK
    if in_tm:
        x = _rows_to_batch_major(x_ref[...], nb)
    else:
        x = x_ref[...].reshape(rows, D_MODEL)
    u, v = _sgu_front(x, g_ref, win_ref, sg_ref)
    vb = v.astype(BF16)
    for b in range(nb):
        rs = slice(b * CHUNK, (b + 1) * CHUNK)
        for g in range(HEADS_A):
            cs = slice(g * HD_A, (g + 1) * HD_A)
            mixed = _dot(w_ref[g], vb[rs, cs]) + bias_ref[:, cs]
            y_s[rs, cs] = (u[rs, cs] * mixed).astype(BF16)
    o = x + _dot(y_s[...], wout_ref[...])
    if out_tm:
        o_ref[...] = _rows_to_time_major(o, nb)
    else:
        o_ref[...] = o.reshape(nb, CHUNK, D_MODEL)


def _sgu_sample_kernel(x_ref, g_ref, win_ref, sg_ref, wl_ref, bl_ref, wout_ref, o_ref, v_ref, *, nb, steps):
    x = x_ref[...]
    u, v = _sgu_front(x, g_ref, win_ref, sg_ref)
    v_ref[...] = v
    mixed = []
    for t in range(steps):
        m = None
        for s in range(t + 1):
            term = wl_ref[t * steps + s:t * steps + s + 1, :] * v[s * nb:(s + 1) * nb, :]
            m = term if m is None else m + term
        mixed.append(m + bl_ref[t:t + 1, :])
    y = (u * jnp.concatenate(mixed, axis=0)).astype(BF16)
    o_ref[...] = x + _dot(y, wout_ref[...])


def _sgu_prompt(x, p, nb, n_steps, in_tm, out_tm):
    rows = nb * CHUNK
    tm_spec = pl.BlockSpec((rows, D_MODEL), lambda i: (i, 0))
    bm_spec = pl.BlockSpec((nb, CHUNK, D_MODEL), lambda i: (0, i, 0))
    out_shape = (jax.ShapeDtypeStruct((nb * n_steps, D_MODEL), F32) if out_tm
                 else jax.ShapeDtypeStruct((nb, n_steps, D_MODEL), F32))
    return pl.pallas_call(
        functools.partial(_sgu_prompt_kernel, nb=nb, in_tm=in_tm, out_tm=out_tm),
        grid=(n_steps // CHUNK,),
        in_specs=[tm_spec if in_tm else bm_spec, _const_spec((1, D_MODEL)), _const_spec((D_MODEL, 2 * D_MODEL)),
                  _const_spec((1, D_MODEL)), _const_spec((HEADS_A, CHUNK, CHUNK)),
                  _const_spec((CHUNK, D_MODEL)), _const_spec((D_MODEL, D_MODEL))],
        out_specs=tm_spec if out_tm else bm_spec,
        out_shape=out_shape,
        scratch_shapes=[pltpu.VMEM((rows, D_MODEL), BF16)],
        compiler_params=_params(),
        name="sgu_prompt",
    )(x, p["g"], p["win"], p["sg"], p["w_tril"], p["bias"], p["wout"])


def _sgu_sample(x, p, nb, steps):
    tile = steps * nb
    tile_spec = pl.BlockSpec((tile, D_MODEL), lambda i: (0, 0))
    return pl.pallas_call(
        functools.partial(_sgu_sample_kernel, nb=nb, steps=steps),
        grid=(1,),
        in_specs=[tile_spec, _const_spec((1, D_MODEL)), _const_spec((D_MODEL, 2 * D_MODEL)),
                  _const_spec((1, D_MODEL)), _const_spec((steps * steps, D_MODEL)),
                  _const_spec((steps, D_MODEL)), _const_spec((D_MODEL, D_MODEL))],
        out_specs=[tile_spec, tile_spec],
        out_shape=[jax.ShapeDtypeStruct((tile, D_MODEL), F32), jax.ShapeDtypeStruct((tile, D_MODEL), F32)],
        compiler_params=_params(),
        name="sgu_sample",
    )(x, p["g"], p["win"], p["sg"], p["wl"], p["bl"], p["wout"])


def _rglru_kernel(x_ref, g_ref, win_ref, cw_ref, cb_ref, wa_ref, ba_ref, wx_ref, bx_ref, lam_ref, wout_ref,
                  conv0_ref, h0_ref, o_ref, conv_ref, h_ref, xext_s, a_s, b_s, y_s, *, nb, steps):
    rows = nb * steps
    halo = (CONV_W - 1) * nb

    @pl.when(pl.program_id(0) == 0)
    def _():
        conv_ref[...] = conv0_ref[...]
        h_ref[...] = h0_ref[...]

    def blk_cols(j):
        return slice(j * V7X_MXU_DIM, (j + 1) * V7X_MXU_DIM)

    x = x_ref[...]
    h = _rms(x, g_ref[...]).astype(BF16)
    for j in range(N_BLK):
        cols = blk_cols(j)
        xext, a_j, b_j = xext_s.at[j], a_s.at[j], b_s.at[j]
        xext[0:halo, :] = conv_ref[:, cols]
        xext[halo:halo + rows, :] = _dot(h, win_ref[:, blk_cols(N_BLK + j)])
        gate = _gelu(_dot(h, win_ref[:, cols]))
        conv = None
        for w in range(CONV_W):
            term = xext[w * nb:w * nb + rows, :] * cw_ref[w:w + 1, cols]
            conv = term if conv is None else conv + term
        conv_ref[:, cols] = xext[rows:rows + halo, :]
        xc = cb_ref[:, cols] + conv
        xcb = xc.astype(BF16)
        r = jax.nn.sigmoid(_dot(xcb, wa_ref[j]) + ba_ref[:, cols])
        ig = jax.nn.sigmoid(_dot(xcb, wx_ref[j]) + bx_ref[:, cols])
        neg_log_a = (LRU_C * r) * jax.nn.softplus(-lam_ref[:, cols])
        a = jnp.exp(-neg_log_a)
        a_j[...] = a
        w = jnp.tanh(neg_log_a) * (a * a + 1.0)
        mult = jnp.where(w > 0.0, w * lax.rsqrt(w), 0.0)
        b_j[...] = mult * (ig * xc)
        hcur = h_ref[:, cols]
        for t in range(steps):
            sl = slice(t * nb, (t + 1) * nb)
            hcur = a_j[sl, :] * hcur + b_j[sl, :]
            b_j[sl, :] = hcur
        h_ref[:, cols] = hcur
        y_s[:, cols] = (b_j[...] * gate).astype(BF16)
    o_ref[...] = x + _dot(y_s[...], wout_ref[...])


def _rglru(x, p, conv0, h0, nb, steps):
    rows = nb * steps
    halo = (CONV_W - 1) * nb
    row_spec = pl.BlockSpec((rows, D_MODEL), lambda i: (i, 0))
    vec = _const_spec((1, D_MODEL))
    blk = _const_spec((N_BLK, V7X_MXU_DIM, V7X_MXU_DIM))
    return pl.pallas_call(
        functools.partial(_rglru_kernel, nb=nb, steps=steps),
        grid=(x.shape[0] // rows,),
        in_specs=[row_spec, vec, _const_spec((D_MODEL, 2 * D_MODEL)), _const_spec((CONV_W, D_MODEL)), vec,
                  blk, vec, blk, vec, vec, _const_spec((D_MODEL, D_MODEL)),
                  _const_spec((halo, D_MODEL)), _const_spec((nb, D_MODEL))],
        out_specs=[row_spec, pl.BlockSpec((halo, D_MODEL), lambda i: (0, 0)),
                   pl.BlockSpec((nb, D_MODEL), lambda i: (0, 0))],
        out_shape=[jax.ShapeDtypeStruct(x.shape, F32), jax.ShapeDtypeStruct((halo, D_MODEL), F32),
                   jax.ShapeDtypeStruct((nb, D_MODEL), F32)],
        scratch_shapes=[pltpu.VMEM((N_BLK, rows + halo, V7X_MXU_DIM), F32), pltpu.VMEM((N_BLK, rows, V7X_MXU_DIM), F32),
                        pltpu.VMEM((N_BLK, rows, V7X_MXU_DIM), F32), pltpu.VMEM((rows, D_MODEL), BF16)],
        compiler_params=_params(),
        name="rglru",
    )(x, p["g"], p["win"], p["cw"], p["cb"], p["wa"], p["ba"], p["wx"], p["bx"], p["lam"], p["wout"], conv0, h0)


def _cmul(a_re, a_im, b_re, b_im):
    return a_re * b_re - a_im * b_im, a_re * b_im + a_im * b_re


def _s5_disc_kernel(lre_ref, lim_ref, ldt_ref, bre_ref, bim_ref, are_ref, aim_ref, bbre_ref, bbim_ref, pw_ref):
    lr, li = lre_ref[...], lim_ref[...]
    dt = jnp.exp(ldt_ref[...])
    mag = jnp.exp(lr * dt)
    ab_re, ab_im = mag * jnp.cos(li * dt), mag * jnp.sin(li * dt)
    zr, zi = ab_re - 1.0, ab_im
    den = lr * lr + li * li
    q_re = (zr * lr + zi * li) / den
    q_im = (zi * lr - zr * li) / den
    are_ref[...] = ab_re
    aim_ref[...] = ab_im
    br, bi = bre_ref[...], bim_ref[...]
    bbre_ref[...] = q_re[:, None, :] * br - q_im[:, None, :] * bi
    bbim_ref[...] = q_re[:, None, :] * bi + q_im[:, None, :] * br
    p_re, p_im = jnp.ones_like(ab_re), jnp.zeros_like(ab_re)
    for k in range(S5_L + 1):
        pw_ref[0, k] = p_re
        pw_ref[1, k] = p_im
        p_re, p_im = _cmul(p_re, p_im, ab_re, ab_im)


def _s5_disc(lam_re, lam_im, log_dt, b_re_t, b_im_t):
    gp = jax.ShapeDtypeStruct((G_C, P_C), F32)
    ghp = jax.ShapeDtypeStruct((G_C, GROUP_C, P_C), F32)
    pw = jax.ShapeDtypeStruct((2, S5_L + 1, G_C, P_C), F32)
    return pl.pallas_call(_s5_disc_kernel, out_shape=[gp, gp, ghp, ghp, pw], name="s5_disc")(
        lam_re, lam_im, log_dt.reshape(G_C, 1), b_re_t, b_im_t)


def _s5_taps_kernel(pw0_ref, pw1_ref, pwf_ref, c_ref, b_ref, kconv_ref, wst_ref, kin_ref, *, groups):
    n = S5_L * GROUP_C

    def rep(t):
        return jnp.broadcast_to(t[:, None, :], (S5_L, GROUP_C, P_C)).reshape(n, P_C)

    def til(c):
        return jnp.broadcast_to(c[None, :, :], (S5_L, GROUP_C, P_C)).reshape(n, P_C)

    def dot_t(a, b):
        return lax.dot_general(a, b, (((1,), (1,)), ((), ())), precision=lax.Precision.HIGHEST,
                               preferred_element_type=F32)

    lane = lax.broadcasted_iota(jnp.int32, (GROUP_C, n), 1)
    for gi in range(groups):
        cr, ci = til(c_ref[gi, 0]), til(c_ref[gi, 1])
        b_re, b_im = b_ref[gi, 0], b_ref[gi, 1]
        x_re, x_im = _cmul(rep(pw0_ref[gi, 0]), rep(pw0_ref[gi, 1]), cr, ci)
        mrow = dot_t(b_re, x_re) - dot_t(b_im, x_im)
        for s in range(S5_L):
            blk = mrow if s == 0 else jnp.where(lane >= s * GROUP_C, pltpu.roll(mrow, s * GROUP_C, 1), 0.0)
            kconv_ref[gi, s * GROUP_C:(s + 1) * GROUP_C, :] = blk.astype(BF16)
        e_re, e_im = _cmul(rep(pw1_ref[gi, 0]), rep(pw1_ref[gi, 1]), cr, ci)
        kin_ref[gi, 0] = e_re.T.astype(BF16)
        kin_ref[gi, 1] = (-e_im).T.astype(BF16)
        f_re, f_im = _cmul(rep(pwf_ref[gi, 0]), rep(pwf_ref[gi, 1]), til(b_re), til(b_im))
        wst_ref[gi, 0] = f_re.astype(BF16)
        wst_ref[gi, 1] = f_im.astype(BF16)


def _s5_taps(pw0, pw1, pwf, c, bb):
    groups = 8
    n = S5_L * GROUP_C
    pspec = pl.BlockSpec((groups, 2, S5_L, P_C), lambda i: (i, 0, 0, 0))
    hspec = pl.BlockSpec((groups, 2, GROUP_C, P_C), lambda i: (i, 0, 0, 0))
    return pl.pallas_call(
        functools.partial(_s5_taps_kernel, groups=groups),
        grid=(G_C // groups,),
        in_specs=[pspec, pspec, pspec, hspec, hspec],
        out_specs=[pl.BlockSpec((groups, n, n), lambda i: (i, 0, 0)),
                   pl.BlockSpec((groups, 2, n, P_C), lambda i: (i, 0, 0, 0)),
                   pl.BlockSpec((groups, 2, P_C, n), lambda i: (i, 0, 0, 0))],
        out_shape=[jax.ShapeDtypeStruct((G_C, n, n), BF16), jax.ShapeDtypeStruct((G_C, 2, n, P_C), BF16),
                   jax.ShapeDtypeStruct((G_C, 2, P_C, n), BF16)],
        compiler_params=_params(),
        name="s5_taps",
    )(pw0, pw1, pwf, c, bb)


def _granule_transpose(vs):
    vs = list(vs)
    granule = lax.broadcasted_iota(jnp.int32, vs[0].shape, 1) // GROUP_C
    d = 1
    while d < len(vs):
        keep = (granule & d) == 0
        for i in range(len(vs)):
            if i & d == 0:
                a, b = vs[i], vs[i + d]
                vs[i] = jnp.where(keep, a, pltpu.roll(b, d * GROUP_C, 1))
                vs[i + d] = jnp.where(keep, pltpu.roll(a, V7X_LANES - d * GROUP_C, 1), b)
        d *= 2
    return vs


def _s5_in_kernel(x_ref, g_ref, win_ref, u_ref, u2_ref, *, nb, blocks):
    h = _rms(x_ref[...], g_ref[...]).astype(BF16)
    rows2 = blocks * nb
    for cc in range(N_BLK):
        cols = slice(cc * V7X_MXU_DIM, (cc + 1) * V7X_MXU_DIM)
        u = _dot(h, win_ref[:, cols])
        u_ref[:, cols] = u
        u4 = u.reshape(blocks, S5_L, nb, V7X_MXU_DIM)
        for sub in range(V7X_MXU_DIM // V7X_LANES):
            c = cc * (V7X_MXU_DIM // V7X_LANES) + sub
            for half in range(S5_L // GRANULES_PER_VREG):
                vs = []
                for kk in range(GRANULES_PER_VREG):
                    slab = u4[:, half * GRANULES_PER_VREG + kk, :, sub * V7X_LANES:(sub + 1) * V7X_LANES]
                    vs.append(slab.reshape(rows2, V7X_LANES))
                ws = _granule_transpose(vs)
                for gi in range(GRANULES_PER_VREG):
                    lane0 = (c * GRANULES_PER_VREG + gi) * S5_L * GROUP_C + half * V7X_LANES
                    u2_ref[:, lane0:lane0 + V7X_LANES] = ws[gi].astype(BF16)


def _s5_in(x, g, win, nb, blocks):
    rows = blocks * S5_L * nb
    n_blocks = x.shape[0] // (S5_L * nb)
    row_spec = pl.BlockSpec((rows, D_MODEL), lambda i: (i, 0))
    return pl.pallas_call(
        functools.partial(_s5_in_kernel, nb=nb, blocks=blocks),
        grid=(x.shape[0] // rows,),
        in_specs=[row_spec, _const_spec((1, D_MODEL)), _const_spec((D_MODEL, D_MODEL))],
        out_specs=[row_spec, pl.BlockSpec((blocks * nb, S5_L * D_MODEL), lambda i: (i, 0))],
        out_shape=[jax.ShapeDtypeStruct(x.shape, F32),
                   jax.ShapeDtypeStruct((n_blocks * nb, S5_L * D_MODEL), BF16)],
        compiler_params=_params(),
        name="s5_in",
    )(x, g, win)


def _s5_core_kernel(u_ref, wst_ref, kconv_ref, kin_ref, a_ref, s0_ref, y_ref, sf_ref, sc_s, sin_s, *, nb, n_blocks):
    pw = 2 * S5_L * GROUP_C
    sw = 2 * P_C
    ub = [u_ref[:, q * pw:(q + 1) * pw] for q in range(S5_PAIRS_PER_STEP)]
    for q in range(S5_PAIRS_PER_STEP):
        sc_s[q] = _dot(ub[q], wst_ref[q])
    a_re = [jnp.broadcast_to(a_ref[:, 2 * sw * q:2 * sw * q + sw], (nb, sw)) for q in range(S5_PAIRS_PER_STEP)]
    a_im = [jnp.broadcast_to(a_ref[:, 2 * sw * q + sw:2 * sw * (q + 1)], (nb, sw)) for q in range(S5_PAIRS_PER_STEP)]
    s_re = [s0_ref[:, 2 * sw * q:2 * sw * q + sw] for q in range(S5_PAIRS_PER_STEP)]
    s_im = [s0_ref[:, 2 * sw * q + sw:2 * sw * (q + 1)] for q in range(S5_PAIRS_PER_STEP)]
    for blk in range(n_blocks):
        rs = slice(blk * nb, (blk + 1) * nb)
        for q in range(S5_PAIRS_PER_STEP):
            sin_s[q, rs, 0:sw] = s_re[q]
            sin_s[q, rs, sw:2 * sw] = s_im[q]
            n_re, n_im = _cmul(a_re[q], a_im[q], s_re[q], s_im[q])
            s_re[q], s_im[q] = n_re + sc_s[q, rs, 0:sw], n_im + sc_s[q, rs, sw:2 * sw]
    for q in range(S5_PAIRS_PER_STEP):
        sf_ref[:, 2 * sw * q:2 * sw * q + sw] = s_re[q]
        sf_ref[:, 2 * sw * q + sw:2 * sw * (q + 1)] = s_im[q]
        half = S5_L * GROUP_C
        conv = jnp.concatenate([_dot(ub[q][:, :half], kconv_ref[2 * q]),
                                _dot(ub[q][:, half:], kconv_ref[2 * q + 1])], axis=1)
        y_ref[:, q * pw:(q + 1) * pw] = conv + _dot(sin_s[q].astype(BF16), kin_ref[q])


def _s5_core(u2, wst, kconv, kin, a_pair, s0, nb):
    rows = u2.shape[0]
    pw = 2 * S5_L * GROUP_C
    sw = 2 * P_C
    cols = S5_PAIRS_PER_STEP * pw
    scols = S5_PAIRS_PER_STEP * 2 * sw
    return pl.pallas_call(
        functools.partial(_s5_core_kernel, nb=nb, n_blocks=rows // nb),
        grid=(u2.shape[1] // cols,),
        in_specs=[pl.BlockSpec((rows, cols), lambda i: (0, i)),
                  pl.BlockSpec((S5_PAIRS_PER_STEP, pw, 2 * sw), lambda i: (i, 0, 0)),
                  pl.BlockSpec((2 * S5_PAIRS_PER_STEP, pw // 2, pw // 2), lambda i: (i, 0, 0)),
                  pl.BlockSpec((S5_PAIRS_PER_STEP, 2 * sw, pw), lambda i: (i, 0, 0)),
                  pl.BlockSpec((1, scols), lambda i: (0, i)), pl.BlockSpec((nb, scols), lambda i: (0, i))],
        out_specs=[pl.BlockSpec((rows, cols), lambda i: (0, i)), pl.BlockSpec((nb, scols), lambda i: (0, i))],
        out_shape=[jax.ShapeDtypeStruct(u2.shape, F32), jax.ShapeDtypeStruct((nb, 2 * D_STATE), F32)],
        scratch_shapes=[pltpu.VMEM((S5_PAIRS_PER_STEP, rows, 2 * sw), F32),
                        pltpu.VMEM((S5_PAIRS_PER_STEP, rows, 2 * sw), F32)],
        compiler_params=_params(),
        name="s5_core",
    )(u2, wst, kconv, kin, a_pair, s0)


def _s5_out_kernel(x_ref, u_ref, y2_ref, dskip_ref, wglu_ref, o_ref, y_s, *, nb, blocks):
    rows = blocks * S5_L * nb
    o = None
    for cc in range(N_BLK):
        cols = slice(cc * V7X_MXU_DIM, (cc + 1) * V7X_MXU_DIM)
        for sub in range(V7X_MXU_DIM // V7X_LANES):
            c = cc * (V7X_MXU_DIM // V7X_LANES) + sub
            for half in range(S5_L // GRANULES_PER_VREG):
                lanes = [(c * GRANULES_PER_VREG + gi) * S5_L * GROUP_C + half * V7X_LANES
                         for gi in range(GRANULES_PER_VREG)]
                ws = _granule_transpose([y2_ref[:, l0:l0 + V7X_LANES] for l0 in lanes])
                for kk in range(GRANULES_PER_VREG):
                    y_s[cc, :, half * GRANULES_PER_VREG + kk, :, sub * V7X_LANES:(sub + 1) * V7X_LANES] = (
                        ws[kk].reshape(blocks, nb, V7X_LANES))
        y = y_s[cc].reshape(rows, V7X_MXU_DIM) + dskip_ref[:, cols] * u_ref[:, cols]
        part = _dot(_gelu(y).astype(BF16), wglu_ref[cols, :])
        o = part if o is None else o + part
    o_ref[...] = x_ref[...] + o[:, :D_MODEL] * jax.nn.sigmoid(o[:, D_MODEL:])


def _s5_out(x, u, y2, dskip, wglu, nb, blocks):
    rows = blocks * S5_L * nb
    row_spec = pl.BlockSpec((rows, D_MODEL), lambda i: (i, 0))
    return pl.pallas_call(
        functools.partial(_s5_out_kernel, nb=nb, blocks=blocks),
        grid=(x.shape[0] // rows,),
        in_specs=[row_spec, row_spec, pl.BlockSpec((blocks * nb, S5_L * D_MODEL), lambda i: (i, 0)),
                  _const_spec((1, D_MODEL)), _const_spec((D_MODEL, 2 * D_MODEL))],
        out_specs=row_spec,
        out_shape=jax.ShapeDtypeStruct(x.shape, F32),
        scratch_shapes=[pltpu.VMEM((N_BLK, blocks, S5_L, nb, V7X_MXU_DIM), F32)],
        compiler_params=_params(),
        name="s5_out",
    )(x, u, y2, dskip, wglu)


def _pair_cols(re, im):
    z = jnp.zeros_like(re[0::2])
    top = jnp.concatenate([re[0::2], z, im[0::2], z], axis=-1)
    bot = jnp.concatenate([z, re[1::2], z, im[1::2]], axis=-1)
    return jnp.concatenate([top, bot], axis=-2)


def _pair_rows(re, im):
    z = jnp.zeros_like(re[0::2])
    left = jnp.concatenate([re[0::2], z, im[0::2], z], axis=-2)
    right = jnp.concatenate([z, re[1::2], z, im[1::2]], axis=-2)
    return jnp.concatenate([left, right], axis=-1)


def _pair_state(re, im):
    lead = re.shape[:-2]
    r = re.reshape(lead + (G_C // 2, 2 * P_C))
    i = im.reshape(lead + (G_C // 2, 2 * P_C))
    return jnp.concatenate([r, i], axis=-1).reshape(lead + (2 * D_STATE,))


def _unpair_state(s, nb):
    s4 = s.reshape(nb, G_C // 2, 2, 2 * P_C)
    return s4[:, :, 0].reshape(nb, G_C, P_C), s4[:, :, 1].reshape(nb, G_C, P_C)


def _s5_kernel(x_ref, g_ref, win_ref, bre_ref, bim_ref, are_ref, aim_ref, cre_ref, cim_ref, dskip_ref, wglu_ref,
               sre0_ref, sim0_ref, o_ref, sre_ref, sim_ref, xre_s, xim_s, *, nb, steps, lane_chunk):
    @pl.when(pl.program_id(0) == 0)
    def _():
        sre_ref[...] = sre0_ref[...]
        sim_ref[...] = sim0_ref[...]

    x = x_ref[...]
    h = _rms(x, g_ref[...]).astype(BF16)
    u = _dot(h, win_ref[...])
    ub = u.astype(BF16)

    def expand(j):
        ucols = slice(j * V7X_MXU_DIM, (j + 1) * V7X_MXU_DIM)
        scols = slice(j * STATE_PER_BLK, (j + 1) * STATE_PER_BLK)
        xre_s[:, scols] = _dot(ub[:, ucols], bre_ref[j])
        xim_s[:, scols] = _dot(ub[:, ucols], bim_ref[j])

    def scan(j):
        for c in range(STATE_PER_BLK // lane_chunk):
            cols = slice(j * STATE_PER_BLK + c * lane_chunk, j * STATE_PER_BLK + (c + 1) * lane_chunk)
            a_re = jnp.broadcast_to(are_ref[:, cols], (nb, lane_chunk))
            a_im = jnp.broadcast_to(aim_ref[:, cols], (nb, lane_chunk))
            s_re, s_im = sre_ref[:, cols], sim_ref[:, cols]
            for t in range(steps):
                sl = slice(t * nb, (t + 1) * nb)
                s_re, s_im = ((a_re * s_re - a_im * s_im) + xre_s[sl, cols],
                              (a_re * s_im + a_im * s_re) + xim_s[sl, cols])
                xre_s[sl, cols] = s_re
                xim_s[sl, cols] = s_im
            sre_ref[:, cols] = s_re
            sim_ref[:, cols] = s_im

    ys = []
    expand(0)
    for j in range(N_BLK):
        if j + 1 < N_BLK:
            expand(j + 1)
        scan(j)
        scols = slice(j * STATE_PER_BLK, (j + 1) * STATE_PER_BLK)
        ys.append(_dot(xre_s[:, scols].astype(BF16), cre_ref[j]) - _dot(xim_s[:, scols].astype(BF16), cim_ref[j]))
    y = jnp.concatenate(ys, axis=1) + dskip_ref[...] * u
    o = _dot(_gelu(y).astype(BF16), wglu_ref[...])
    o_ref[...] = x + o[:, :D_MODEL] * jax.nn.sigmoid(o[:, D_MODEL:])


def _s5(x, p, sre0, sim0, nb, steps, lane_chunk):
    rows = nb * steps
    row_spec = pl.BlockSpec((rows, D_MODEL), lambda i: (i, 0))
    vec = _const_spec((1, D_MODEL))
    svec = _const_spec((1, D_STATE))
    state_spec = _const_spec((nb, D_STATE))
    bspec = _const_spec((N_BLK, V7X_MXU_DIM, STATE_PER_BLK))
    cspec = _const_spec((N_BLK, STATE_PER_BLK, V7X_MXU_DIM))
    state_out = pl.BlockSpec((nb, D_STATE), lambda i: (0, 0))
    return pl.pallas_call(
        functools.partial(_s5_kernel, nb=nb, steps=steps, lane_chunk=lane_chunk),
        grid=(x.shape[0] // rows,),
        in_specs=[row_spec, vec, _const_spec((D_MODEL, D_MODEL)), bspec, bspec, svec, svec, cspec, cspec, vec,
                  _const_spec((D_MODEL, 2 * D_MODEL)), state_spec, state_spec],
        out_specs=[row_spec, state_out, state_out],
        out_shape=[jax.ShapeDtypeStruct(x.shape, F32), jax.ShapeDtypeStruct((nb, D_STATE), F32),
                   jax.ShapeDtypeStruct((nb, D_STATE), F32)],
        scratch_shapes=[pltpu.VMEM((rows, D_STATE), F32), pltpu.VMEM((rows, D_STATE), F32)],
        compiler_params=_params(),
        name="s5",
    )(x, p["g"], p["win"], p["bre"], p["bim"], p["are"], p["aim"], p["cre"], p["cim"], p["dskip"], p["wglu"],
      sre0, sim0)


def _block_diag(w, n_per_blk):
    n, k_in, k_out = w.shape
    wb = w.reshape(n // n_per_blk, n_per_blk, k_in, k_out)
    eye = jnp.eye(n_per_blk, dtype=w.dtype)
    out = jnp.einsum("jgio,gk->jgiko", wb, eye)
    return out.reshape(n // n_per_blk, n_per_blk * k_in, n_per_blk * k_out)


def _to_rows(a):
    return jnp.swapaxes(a, 0, 1).reshape(a.shape[0] * a.shape[1], a.shape[2])


def _from_rows(a, nb):
    return jnp.swapaxes(a.reshape(a.shape[0] // nb, nb, a.shape[1]), 0, 1)


def kernel(x_prompt, x_sample, state_rglru_conv, state_rglru_h, state_s5_re, state_s5_im, norm_mix, norm_ffn, norm_f, w_ff1, w_ff2, w_in_a, sgu_g, w_s, b_s, w_out_a, w_in_b, conv_w, conv_b, w_a, b_a, w_x, b_x, lam, w_out_b, w_in_c, lam_re, lam_im, log_dt, b_re, b_im, c_re, c_im, d_skip, w_glu):
    bp, tp, _ = x_prompt.shape
    bs, ts, _ = x_sample.shape
    rows_p, rows_s = bp * tp, bs * ts
    assert DEPTH % N_MIXERS == 1 and DEPTH > 1, "first and last layers must be SGU layers"
    assert tp % CHUNK == 0 and ts < CHUNK
    assert rows_s % FFN_ROWS == 0 and rows_p % FFN_ROWS == 0

    row = lambda v: v.reshape(1, -1)
    norm_ffn3 = norm_ffn.reshape(DEPTH, 1, D_MODEL)
    w1, w2, gf = w_ff1, w_ff2, row(norm_f)

    x_p, x_s = x_prompt, _to_rows(x_sample)
    outs_v, conv_p, h_p, conv_s, h_s, sre_p, sim_p, sre_s, sim_s = [], [], [], [], [], [], [], [], []
    for layer in range(DEPTH):
        j, kind = layer // N_MIXERS, layer % N_MIXERS
        first, last = layer == 0, layer == DEPTH - 1
        g = row(norm_mix[layer])
        if kind == 0:
            p = dict(g=g, win=w_in_a[j].astype(BF16), wout=w_out_a[j].astype(BF16), sg=row(sgu_g[j]),
                     w_tril=jnp.where(jnp.tril(jnp.ones((CHUNK, CHUNK), dtype=bool)), w_s[j], 0.0).astype(BF16),
                     bias=jnp.repeat(b_s[j].T, HD_A, axis=1),
                     wl=jnp.repeat(w_s[j][:, :ts, :ts].reshape(HEADS_A, ts * ts).T, HD_A, axis=1),
                     bl=jnp.repeat(b_s[j][:, :ts].T, HD_A, axis=1))
            if first:
                x_p = _sgu_prompt(x_p, p, bp, tp, False, True)
            elif last:
                x_p = _sgu_prompt(x_p, p, bp, tp, True, False).reshape(rows_p, D_MODEL)
            else:
                raise NotImplementedError("interior SGU layers")
            x_s, v = _sgu_sample(x_s, p, bs, ts)
            outs_v.append(_from_rows(v, bs))
        elif kind == 1:
            p = dict(g=g, win=w_in_b[j].astype(BF16), cw=conv_w[j], cb=row(conv_b[j]),
                     wa=_block_diag(w_a[j], HEADS_PER_BLK).astype(BF16), ba=row(b_a[j]),
                     wx=_block_diag(w_x[j], HEADS_PER_BLK).astype(BF16), bx=row(b_x[j]),
                     lam=row(lam[j]), wout=w_out_b[j].astype(BF16))
            dt_s = state_rglru_h.dtype
            x_p, cp, hp = _rglru(x_p, p, jnp.zeros(((CONV_W - 1) * bp, D_MODEL), dt_s),
                                 jnp.zeros((bp, D_MODEL), dt_s), bp, RGLRU_PROMPT_STEPS)
            x_s, cs, hs = _rglru(x_s, p, _to_rows(state_rglru_conv[j]), state_rglru_h[j], bs, ts)
            conv_p.append(_from_rows(cp, bp)); h_p.append(hp)
            conv_s.append(_from_rows(cs, bs)); h_s.append(hs)
        else:
            are, aim, bbre, bbim, pw = _s5_disc(lam_re[j], lam_im[j], log_dt[j],
                                                jnp.swapaxes(b_re[j], 1, 2), jnp.swapaxes(b_im[j], 1, 2))
            p = dict(g=g, win=w_in_c[j].astype(BF16),
                     bre=_block_diag(bbre, GROUPS_PER_BLK).astype(BF16),
                     bim=_block_diag(bbim, GROUPS_PER_BLK).astype(BF16),
                     are=are.reshape(1, D_STATE), aim=aim.reshape(1, D_STATE),
                     cre=_block_diag(jnp.swapaxes(c_re[j], 1, 2), GROUPS_PER_BLK).astype(BF16),
                     cim=_block_diag(jnp.swapaxes(c_im[j], 1, 2), GROUPS_PER_BLK).astype(BF16),
                     dskip=row(d_skip[j]), wglu=w_glu[j].astype(BF16))
            pwg = jnp.transpose(pw, (2, 0, 1, 3))
            kconv, wst, kin = _s5_taps(pwg[:, :, :S5_L], pwg[:, :, 1:], pwg[:, :, S5_L - 1::-1],
                                       jnp.stack([c_re[j], c_im[j]], axis=1), jnp.stack([bbre, bbim], axis=1))
            zs = jnp.zeros((bp, G_C, P_C), state_s5_re.dtype)
            u, u2 = _s5_in(x_p, g, p["win"], bp, S5_TILE_BLOCKS)
            y2, sf = _s5_core(u2, _pair_cols(wst[:, 0], wst[:, 1]), kconv, _pair_rows(kin[:, 0], kin[:, 1]),
                              _pair_state(pw[0, S5_L], pw[1, S5_L]).reshape(1, -1), _pair_state(zs, zs), bp)
            x_p = _s5_out(x_p, u, y2, p["dskip"], p["wglu"], bp, S5_TILE_BLOCKS)
            rp, ip = _unpair_state(sf, bp)
            x_s, rs, is_ = _s5(x_s, p, state_s5_re[j].reshape(bs, D_STATE), state_s5_im[j].reshape(bs, D_STATE),
                               bs, ts, V7X_LANES)
            sre_p.append(rp); sim_p.append(ip)
            sre_s.append(rs.reshape(bs, G_C, P_C)); sim_s.append(is_.reshape(bs, G_C, P_C))
        x_p, x_s = _ffn(x_p, x_s, norm_ffn3, w1, w2, gf, layer, last)

    y_prompt = x_p.reshape(bp, tp, D_MODEL)
    y_sample = _from_rows(x_s, bs)
    return (y_prompt, y_sample, jnp.stack(outs_v), jnp.stack(conv_p), jnp.stack(h_p), jnp.stack(conv_s),
            jnp.stack(h_s), jnp.stack(sre_p), jnp.stack(sim_p), jnp.stack(sre_s), jnp.stack(sim_s))
```

```python
import functools

import jax
import jax.numpy as jnp
from jax import lax
from jax.experimental import pallas as pl
from jax.experimental.pallas import tpu as pltpu

F32 = jnp.float32
BF16 = jnp.bfloat16

D_MODEL = 1024
DEPTH = 4
N_MIXERS = 3
EPS = 1e-6
CHUNK = 128
HEADS_A = 8
HD_A = D_MODEL // HEADS_A
HEADS_B = 16
HD_B = D_MODEL // HEADS_B
CONV_W = 4
LRU_C = 8.0
GROUP_C = 16
G_C = D_MODEL // GROUP_C
P_C = 64
D_STATE = G_C * P_C
D_FF = 4 * D_MODEL

V7X_LANES = 128
V7X_MXU_DIM = 256
V7X_VMEM_BYTES = 64 * 1024 * 1024
VMEM_LIMIT = V7X_VMEM_BYTES - 8 * 1024 * 1024

N_BLK = D_MODEL // V7X_MXU_DIM
HEADS_PER_BLK = V7X_MXU_DIM // HD_B
GROUPS_PER_BLK = V7X_MXU_DIM // GROUP_C
STATE_PER_BLK = GROUPS_PER_BLK * P_C

FFN_ROWS = 512
FFN_COL_CHUNK = 1024
FFN_STAGE_BYTES = 2 * 1024 * 1024
S5_L = 16
S5_TILE_BLOCKS = 4
S5_PAIRS_PER_STEP = 2
GRANULES_PER_VREG = V7X_LANES // GROUP_C
RGLRU_PROMPT_STEPS = 64


def _rms(x, g):
    return (x * lax.rsqrt(jnp.mean(x * x, axis=-1, keepdims=True) + EPS)) * g


GELU_C0 = 0.7978845608028654
GELU_C1 = GELU_C0 * 0.044715


def _gelu(x):
    return x * (0.5 + 0.5 * jnp.tanh(x * (GELU_C0 + GELU_C1 * (x * x))))


def _dot(a, b):
    return jnp.dot(a, b, preferred_element_type=F32)


def _const_spec(shape):
    zeros = (0,) * len(shape)
    return pl.BlockSpec(shape, lambda i: zeros, pipeline_mode=pl.Buffered(1))


def _layer_spec(shape, layer):
    idx = (layer,) + (0,) * len(shape)
    return pl.BlockSpec((None,) + tuple(shape), lambda i: idx, pipeline_mode=pl.Buffered(1))


def _params():
    return pltpu.CompilerParams(dimension_semantics=("arbitrary",), vmem_limit_bytes=VMEM_LIMIT)


def _ffn_tile(x_ref, g_ref, w1_ref, w2_ref, gf_ref, o_ref, final_norm):
    x = x_ref[...]
    h = _rms(x, g_ref[...]).astype(BF16)
    acc = None
    for j in range(D_FF // FFN_COL_CHUNK):
        cols = slice(j * FFN_COL_CHUNK, (j + 1) * FFN_COL_CHUNK)
        a = jnp.square(jnp.maximum(_dot(h, w1_ref[:, cols]), 0.0)).astype(BF16)
        part = _dot(a, w2_ref[cols, :])
        acc = part if acc is None else acc + part
    y = x + acc
    if final_norm:
        y = _rms(y, gf_ref[...])
    o_ref[...] = y


def _load_as_bf16(jobs):
    def copies_of(src, stage, sem):
        rows = stage.shape[1]
        return [pltpu.make_async_copy(src.at[pl.ds(k * rows, rows)], stage.at[k % 2], sem.at[k % 2])
                for k in range(src.shape[0] // rows)]

    plans = [copies_of(src, stage, sem) for src, _, stage, sem in jobs]
    n = len(plans[0])
    assert all(len(p) == n for p in plans)
    for p in plans:
        p[0].start()
    for k in range(n):
        if k + 1 < n:
            for p in plans:
                p[k + 1].start()
        for p, (_, dst, stage, _) in zip(plans, jobs):
            p[k].wait()
            rows = stage.shape[1]
            dst[k * rows:(k + 1) * rows, :] = stage[k % 2].astype(BF16)


def _ffn_kernel(xp_ref, xs_ref, g_ref, w1_hbm, w2_hbm, gf_ref, op_ref, os_ref, w1_s, w2_s, stage1, stage2,
                sem1, sem2, *, final_norm, n_p, layer):
    i = pl.program_id(0)

    @pl.when(i == 0)
    def _():
        _load_as_bf16([(w1_hbm.at[layer], w1_s, stage1, sem1), (w2_hbm.at[layer], w2_s, stage2, sem2)])

    @pl.when(i < n_p)
    def _():
        _ffn_tile(xp_ref, g_ref, w1_s, w2_s, gf_ref, op_ref, final_norm)

    @pl.when(i >= n_p)
    def _():
        _ffn_tile(xs_ref, g_ref, w1_s, w2_s, gf_ref, os_ref, final_norm)


def _ffn(x_p, x_s, g, w1, w2, gf, layer, final_norm):
    n_p, n_s = x_p.shape[0] // FFN_ROWS, x_s.shape[0] // FFN_ROWS
    p_spec = pl.BlockSpec((FFN_ROWS, D_MODEL), lambda i: (jnp.minimum(i, n_p - 1), 0))
    s_spec = pl.BlockSpec((FFN_ROWS, D_MODEL), lambda i: (jnp.maximum(i - n_p, 0), 0))
    hbm = pl.BlockSpec(memory_space=pl.ANY)
    return pl.pallas_call(
        functools.partial(_ffn_kernel, final_norm=final_norm, n_p=n_p, layer=layer),
        grid=(n_p + n_s,),
        in_specs=[p_spec, s_spec, _layer_spec((1, D_MODEL), layer), hbm, hbm, _const_spec((1, D_MODEL))],
        out_specs=[p_spec, s_spec],
        out_shape=[jax.ShapeDtypeStruct(x_p.shape, F32), jax.ShapeDtypeStruct(x_s.shape, F32)],
        scratch_shapes=[pltpu.VMEM((D_MODEL, D_FF), BF16), pltpu.VMEM((D_FF, D_MODEL), BF16),
                        pltpu.VMEM((2, FFN_STAGE_BYTES // (4 * D_FF), D_FF), F32),
                        pltpu.VMEM((2, FFN_STAGE_BYTES // (4 * D_MODEL), D_MODEL), F32),
                        pltpu.SemaphoreType.DMA((2,)), pltpu.SemaphoreType.DMA((2,))],
        compiler_params=_params(),
        name="ffn",
    )(x_p, x_s, g, w1, w2, gf)


def _sgu_front(x, g_ref, win_ref, sg_ref):
    h = _rms(x, g_ref[...]).astype(BF16)
    uv = _gelu(_dot(h, win_ref[...]))
    return uv[:, :D_MODEL], _rms(uv[:, D_MODEL:], sg_ref[...])


def _rows_to_batch_major(x, nb):
    steps = x.shape[0] // nb
    return jnp.swapaxes(x.reshape(steps, nb, x.shape[1]), 0, 1).reshape(x.shape)


def _rows_to_time_major(x, nb):
    steps = x.shape[0] // nb
    return jnp.swapaxes(x.reshape(nb, steps, x.shape[1]), 0, 1).reshape(x.shape)


def _sgu_prompt_kernel(x_ref, g_ref, win_ref, sg_ref, w_ref, bias_ref, wout_ref, o_ref, y_s, *, nb, in_tm, out_tm):
    rows = nb * CHUNK
    if in_tm:
        x = _rows_to_batch_major(x_ref[...], nb)
    else:
        x = x_ref[...].reshape(rows, D_MODEL)
    u, v = _sgu_front(x, g_ref, win_ref, sg_ref)
    vb = v.astype(BF16)
    for b in range(nb):
        rs = slice(b * CHUNK, (b + 1) * CHUNK)
        for g in range(HEADS_A):
            cs = slice(g * HD_A, (g + 1) * HD_A)
            mixed = _dot(w_ref[g], vb[rs, cs]) + bias_ref[:, cs]
            y_s[rs, cs] = (u[rs, cs] * mixed).astype(BF16)
    o = x + _dot(y_s[...], wout_ref[...])
    if out_tm:
        o_ref[...] = _rows_to_time_major(o, nb)
    else:
        o_ref[...] = o.reshape(nb, CHUNK, D_MODEL)


def _sgu_sample_kernel(x_ref, g_ref, win_ref, sg_ref, wl_ref, bl_ref, wout_ref, o_ref, v_ref, *, nb, steps):
    x = x_ref[...]
    u, v = _sgu_front(x, g_ref, win_ref, sg_ref)
    v_ref[...] = v
    mixed = []
    for t in range(steps):
        m = None
        for s in range(t + 1):
            term = wl_ref[t * steps + s:t * steps + s + 1, :] * v[s * nb:(s + 1) * nb, :]
            m = term if m is None else m + term
        mixed.append(m + bl_ref[t:t + 1, :])
    y = (u * jnp.concatenate(mixed, axis=0)).astype(BF16)
    o_ref[...] = x + _dot(y, wout_ref[...])


def _sgu_prompt(x, p, nb, n_steps, in_tm, out_tm):
    rows = nb * CHUNK
    tm_spec = pl.BlockSpec((rows, D_MODEL), lambda i: (i, 0))
    bm_spec = pl.BlockSpec((nb, CHUNK, D_MODEL), lambda i: (0, i, 0))
    out_shape = (jax.ShapeDtypeStruct((nb * n_steps, D_MODEL), F32) if out_tm
                 else jax.ShapeDtypeStruct((nb, n_steps, D_MODEL), F32))
    return pl.pallas_call(
        functools.partial(_sgu_prompt_kernel, nb=nb, in_tm=in_tm, out_tm=out_tm),
        grid=(n_steps // CHUNK,),
        in_specs=[tm_spec if in_tm else bm_spec, _const_spec((1, D_MODEL)), _const_spec((D_MODEL, 2 * D_MODEL)),
                  _const_spec((1, D_MODEL)), _const_spec((HEADS_A, CHUNK, CHUNK)),
                  _const_spec((CHUNK, D_MODEL)), _const_spec((D_MODEL, D_MODEL))],
        out_specs=tm_spec if out_tm else bm_spec,
        out_shape=out_shape,
        scratch_shapes=[pltpu.VMEM((rows, D_MODEL), BF16)],
        compiler_params=_params(),
        name="sgu_prompt",
    )(x, p["g"], p["win"], p["sg"], p["w_tril"], p["bias"], p["wout"])


def _sgu_sample(x, p, nb, steps):
    tile = steps * nb
    tile_spec = pl.BlockSpec((tile, D_MODEL), lambda i: (0, 0))
    return pl.pallas_call(
        functools.partial(_sgu_sample_kernel, nb=nb, steps=steps),
        grid=(1,),
        in_specs=[tile_spec, _const_spec((1, D_MODEL)), _const_spec((D_MODEL, 2 * D_MODEL)),
                  _const_spec((1, D_MODEL)), _const_spec((steps * steps, D_MODEL)),
                  _const_spec((steps, D_MODEL)), _const_spec((D_MODEL, D_MODEL))],
        out_specs=[tile_spec, tile_spec],
        out_shape=[jax.ShapeDtypeStruct((tile, D_MODEL), F32), jax.ShapeDtypeStruct((tile, D_MODEL), F32)],
        compiler_params=_params(),
        name="sgu_sample",
    )(x, p["g"], p["win"], p["sg"], p["wl"], p["bl"], p["wout"])


def _rglru_kernel(x_ref, g_ref, win_ref, cw_ref, cb_ref, wa_ref, ba_ref, wx_ref, bx_ref, lam_ref, wout_ref,
                  conv0_ref, h0_ref, o_ref, conv_ref, h_ref, xext_s, a_s, b_s, y_s, *, nb, steps):
    rows = nb * steps
    halo = (CONV_W - 1) * nb

    @pl.when(pl.program_id(0) == 0)
    def _():
        conv_ref[...] = conv0_ref[...]
        h_ref[...] = h0_ref[...]

    def blk_cols(j):
        return slice(j * V7X_MXU_DIM, (j + 1) * V7X_MXU_DIM)

    x = x_ref[...]
    h = _rms(x, g_ref[...]).astype(BF16)
    for j in range(N_BLK):
        cols = blk_cols(j)
        xext, a_j, b_j = xext_s.at[j], a_s.at[j], b_s.at[j]
        xext[0:halo, :] = conv_ref[:, cols]
        xext[halo:halo + rows, :] = _dot(h, win_ref[:, blk_cols(N_BLK + j)])
        gate = _gelu(_dot(h, win_ref[:, cols]))
        conv = None
        for w in range(CONV_W):
            term = xext[w * nb:w * nb + rows, :] * cw_ref[w:w + 1, cols]
            conv = term if conv is None else conv + term
        conv_ref[:, cols] = xext[rows:rows + halo, :]
        xc = cb_ref[:, cols] + conv
        xcb = xc.astype(BF16)
        r = jax.nn.sigmoid(_dot(xcb, wa_ref[j]) + ba_ref[:, cols])
        ig = jax.nn.sigmoid(_dot(xcb, wx_ref[j]) + bx_ref[:, cols])
        neg_log_a = (LRU_C * r) * jax.nn.softplus(-lam_ref[:, cols])
        a = jnp.exp(-neg_log_a)
        a_j[...] = a
        w = jnp.tanh(neg_log_a) * (a * a + 1.0)
        mult = jnp.where(w > 0.0, w * lax.rsqrt(w), 0.0)
        b_j[...] = mult * (ig * xc)
        hcur = h_ref[:, cols]
        for t in range(steps):
            sl = slice(t * nb, (t + 1) * nb)
            hcur = a_j[sl, :] * hcur + b_j[sl, :]
            b_j[sl, :] = hcur
        h_ref[:, cols] = hcur
        y_s[:, cols] = (b_j[...] * gate).astype(BF16)
    o_ref[...] = x + _dot(y_s[...], wout_ref[...])


def _rglru(x, p, conv0, h0, nb, steps):
    rows = nb * steps
    halo = (CONV_W - 1) * nb
    row_spec = pl.BlockSpec((rows, D_MODEL), lambda i: (i, 0))
    vec = _const_spec((1, D_MODEL))
    blk = _const_spec((N_BLK, V7X_MXU_DIM, V7X_MXU_DIM))
    return pl.pallas_call(
        functools.partial(_rglru_kernel, nb=nb, steps=steps),
        grid=(x.shape[0] // rows,),
        in_specs=[row_spec, vec, _const_spec((D_MODEL, 2 * D_MODEL)), _const_spec((CONV_W, D_MODEL)), vec,
                  blk, vec, blk, vec, vec, _const_spec((D_MODEL, D_MODEL)),
                  _const_spec((halo, D_MODEL)), _const_spec((nb, D_MODEL))],
        out_specs=[row_spec, pl.BlockSpec((halo, D_MODEL), lambda i: (0, 0)),
                   pl.BlockSpec((nb, D_MODEL), lambda i: (0, 0))],
        out_shape=[jax.ShapeDtypeStruct(x.shape, F32), jax.ShapeDtypeStruct((halo, D_MODEL), F32),
                   jax.ShapeDtypeStruct((nb, D_MODEL), F32)],
        scratch_shapes=[pltpu.VMEM((N_BLK, rows + halo, V7X_MXU_DIM), F32), pltpu.VMEM((N_BLK, rows, V7X_MXU_DIM), F32),
                        pltpu.VMEM((N_BLK, rows, V7X_MXU_DIM), F32), pltpu.VMEM((rows, D_MODEL), BF16)],
        compiler_params=_params(),
        name="rglru",
    )(x, p["g"], p["win"], p["cw"], p["cb"], p["wa"], p["ba"], p["wx"], p["bx"], p["lam"], p["wout"], conv0, h0)


def _cmul(a_re, a_im, b_re, b_im):
    return a_re * b_re - a_im * b_im, a_re * b_im + a_im * b_re


def _s5_disc_kernel(lre_ref, lim_ref, ldt_ref, bre_ref, bim_ref, are_ref, aim_ref, bbre_ref, bbim_ref, pw_ref):
    lr, li = lre_ref[...], lim_ref[...]
    dt = jnp.exp(ldt_ref[...])
    mag = jnp.exp(lr * dt)
    ab_re, ab_im = mag * jnp.cos(li * dt), mag * jnp.sin(li * dt)
    zr, zi = ab_re - 1.0, ab_im
    den = lr * lr + li * li
    q_re = (zr * lr + zi * li) / den
    q_im = (zi * lr - zr * li) / den
    are_ref[...] = ab_re
    aim_ref[...] = ab_im
    br, bi = bre_ref[...], bim_ref[...]
    bbre_ref[...] = q_re[:, None, :] * br - q_im[:, None, :] * bi
    bbim_ref[...] = q_re[:, None, :] * bi + q_im[:, None, :] * br
    p_re, p_im = jnp.ones_like(ab_re), jnp.zeros_like(ab_re)
    for k in range(S5_L + 1):
        pw_ref[0, k] = p_re
        pw_ref[1, k] = p_im
        p_re, p_im = _cmul(p_re, p_im, ab_re, ab_im)


def _s5_disc(lam_re, lam_im, log_dt, b_re_t, b_im_t):
    gp = jax.ShapeDtypeStruct((G_C, P_C), F32)
    ghp = jax.ShapeDtypeStruct((G_C, GROUP_C, P_C), F32)
    pw = jax.ShapeDtypeStruct((2, S5_L + 1, G_C, P_C), F32)
    return pl.pallas_call(_s5_disc_kernel, out_shape=[gp, gp, ghp, ghp, pw], name="s5_disc")(
        lam_re, lam_im, log_dt.reshape(G_C, 1), b_re_t, b_im_t)


def _s5_taps_kernel(pw0_ref, pw1_ref, pwf_ref, c_ref, b_ref, kconv_ref, wst_ref, kin_ref, *, pairs):
    n = S5_L * GROUP_C
    pair_lanes = 2 * P_C

    def rep(t):
        return jnp.broadcast_to(t[:, None, :], (S5_L, GROUP_C, pair_lanes)).reshape(n, pair_lanes)

    def til(c):
        return jnp.broadcast_to(c[None, :, :], (S5_L, GROUP_C, pair_lanes)).reshape(n, pair_lanes)

    def dot_t(a, b):
        return lax.dot_general(a, b, (((1,), (1,)), ((), ())), precision=lax.Precision.HIGHEST,
                               preferred_element_type=F32)

    def first(shape, axis):
        return lax.broadcasted_iota(jnp.int32, shape, axis) < P_C

    lane = lax.broadcasted_iota(jnp.int32, (GROUP_C, n), 1)
    for q in range(pairs):
        cr, ci = til(c_ref[q, 0]), til(c_ref[q, 1])
        b_re, b_im = b_ref[q, 0], b_ref[q, 1]
        x_re, x_im = _cmul(rep(pw0_ref[q, 0]), rep(pw0_ref[q, 1]), cr, ci)
        for gi in range(2):
            mine = first(b_re.shape, 1) == (gi == 0)
            mrow = (dot_t(jnp.where(mine, b_re, 0.0), x_re)
                    - dot_t(jnp.where(mine, b_im, 0.0), x_im))
            for s in range(S5_L):
                blk = mrow if s == 0 else jnp.where(lane >= s * GROUP_C, pltpu.roll(mrow, s * GROUP_C, 1), 0.0)
                kconv_ref[2 * q + gi, s * GROUP_C:(s + 1) * GROUP_C, :] = blk.astype(BF16)
        e_re, e_im = _cmul(rep(pw1_ref[q, 0]), rep(pw1_ref[q, 1]), cr, ci)
        et_re, et_im = e_re.T, (-e_im).T
        top = first(et_re.shape, 0)
        kin_ref[q] = jnp.concatenate(
            [jnp.concatenate([jnp.where(top, et_re, 0.0), jnp.where(top, 0.0, et_re)], axis=1),
             jnp.concatenate([jnp.where(top, et_im, 0.0), jnp.where(top, 0.0, et_im)], axis=1)], axis=0).astype(BF16)
        f_re, f_im = _cmul(rep(pwf_ref[q, 0]), rep(pwf_ref[q, 1]), til(b_re), til(b_im))
        left = first(f_re.shape, 1)
        wst_ref[q] = jnp.concatenate(
            [jnp.concatenate([jnp.where(left, f_re, 0.0), jnp.where(left, f_im, 0.0)], axis=1),
             jnp.concatenate([jnp.where(left, 0.0, f_re), jnp.where(left, 0.0, f_im)], axis=1)], axis=0).astype(BF16)


def _s5_taps(pw0, pw1, pwf, c, bb):
    pairs = 4
    n = S5_L * GROUP_C
    pspec = pl.BlockSpec((pairs, 2, S5_L, 2 * P_C), lambda i: (i, 0, 0, 0))
    hspec = pl.BlockSpec((pairs, 2, GROUP_C, 2 * P_C), lambda i: (i, 0, 0, 0))
    return pl.pallas_call(
        functools.partial(_s5_taps_kernel, pairs=pairs),
        grid=(G_C // (2 * pairs),),
        in_specs=[pspec, pspec, pspec, hspec, hspec],
        out_specs=[pl.BlockSpec((2 * pairs, n, n), lambda i: (i, 0, 0)),
                   pl.BlockSpec((pairs, 2 * n, 4 * P_C), lambda i: (i, 0, 0)),
                   pl.BlockSpec((pairs, 4 * P_C, 2 * n), lambda i: (i, 0, 0))],
        out_shape=[jax.ShapeDtypeStruct((G_C, n, n), BF16), jax.ShapeDtypeStruct((G_C // 2, 2 * n, 4 * P_C), BF16),
                   jax.ShapeDtypeStruct((G_C // 2, 4 * P_C, 2 * n), BF16)],
        compiler_params=_params(),
        name="s5_taps",
    )(pw0, pw1, pwf, c, bb)


def _pair_lanes(a):
    a2 = a.reshape((G_C // 2, 2) + a.shape[1:])
    return jnp.concatenate([a2[:, 0], a2[:, 1]], axis=-1)


def _granule_transpose(vs):
    vs = list(vs)
    granule = lax.broadcasted_iota(jnp.int32, vs[0].shape, 1) // GROUP_C
    d = 1
    while d < len(vs):
        keep = (granule & d) == 0
        for i in range(len(vs)):
            if i & d == 0:
                a, b = vs[i], vs[i + d]
                vs[i] = jnp.where(keep, a, pltpu.roll(b, d * GROUP_C, 1))
                vs[i + d] = jnp.where(keep, pltpu.roll(a, V7X_LANES - d * GROUP_C, 1), b)
        d *= 2
    return vs


def _s5_in_kernel(x_ref, g_ref, win_ref, u_ref, u2_ref, *, nb, blocks):
    h = _rms(x_ref[...], g_ref[...]).astype(BF16)
    rows2 = blocks * nb
    for cc in range(N_BLK):
        cols = slice(cc * V7X_MXU_DIM, (cc + 1) * V7X_MXU_DIM)
        u = _dot(h, win_ref[:, cols])
        u_ref[:, cols] = u
        u4 = u.reshape(blocks, S5_L, nb, V7X_MXU_DIM)
        for sub in range(V7X_MXU_DIM // V7X_LANES):
            c = cc * (V7X_MXU_DIM // V7X_LANES) + sub
            for half in range(S5_L // GRANULES_PER_VREG):
                vs = []
                for kk in range(GRANULES_PER_VREG):
                    slab = u4[:, half * GRANULES_PER_VREG + kk, :, sub * V7X_LANES:(sub + 1) * V7X_LANES]
                    vs.append(slab.reshape(rows2, V7X_LANES))
                ws = _granule_transpose(vs)
                for gi in range(GRANULES_PER_VREG):
                    lane0 = (c * GRANULES_PER_VREG + gi) * S5_L * GROUP_C + half * V7X_LANES
                    u2_ref[:, lane0:lane0 + V7X_LANES] = ws[gi].astype(BF16)


def _s5_in(x, g, win, nb, blocks):
    rows = blocks * S5_L * nb
    n_blocks = x.shape[0] // (S5_L * nb)
    row_spec = pl.BlockSpec((rows, D_MODEL), lambda i: (i, 0))
    return pl.pallas_call(
        functools.partial(_s5_in_kernel, nb=nb, blocks=blocks),
        grid=(x.shape[0] // rows,),
        in_specs=[row_spec, _const_spec((1, D_MODEL)), _const_spec((D_MODEL, D_MODEL))],
        out_specs=[row_spec, pl.BlockSpec((blocks * nb, S5_L * D_MODEL), lambda i: (i, 0))],
        out_shape=[jax.ShapeDtypeStruct(x.shape, F32),
                   jax.ShapeDtypeStruct((n_blocks * nb, S5_L * D_MODEL), BF16)],
        compiler_params=_params(),
        name="s5_in",
    )(x, g, win)


def _s5_core_kernel(u_ref, wst_ref, kconv_ref, kin_ref, a_ref, s0_ref, y_ref, sf_ref, sc_s, sin_s, *, nb, n_blocks):
    pw = 2 * S5_L * GROUP_C
    sw = 2 * P_C
    ub = [u_ref[:, q * pw:(q + 1) * pw] for q in range(S5_PAIRS_PER_STEP)]
    for q in range(S5_PAIRS_PER_STEP):
        sc_s[q] = _dot(ub[q], wst_ref[q])
    a_re = [jnp.broadcast_to(a_ref[:, 2 * sw * q:2 * sw * q + sw], (nb, sw)) for q in range(S5_PAIRS_PER_STEP)]
    a_im = [jnp.broadcast_to(a_ref[:, 2 * sw * q + sw:2 * sw * (q + 1)], (nb, sw)) for q in range(S5_PAIRS_PER_STEP)]
    s_re = [s0_ref[:, 2 * sw * q:2 * sw * q + sw] for q in range(S5_PAIRS_PER_STEP)]
    s_im = [s0_ref[:, 2 * sw * q + sw:2 * sw * (q + 1)] for q in range(S5_PAIRS_PER_STEP)]
    for blk in range(n_blocks):
        rs = slice(blk * nb, (blk + 1) * nb)
        for q in range(S5_PAIRS_PER_STEP):
            sin_s[q, rs, 0:sw] = s_re[q]
            sin_s[q, rs, sw:2 * sw] = s_im[q]
            n_re, n_im = _cmul(a_re[q], a_im[q], s_re[q], s_im[q])
            s_re[q], s_im[q] = n_re + sc_s[q, rs, 0:sw], n_im + sc_s[q, rs, sw:2 * sw]
    for q in range(S5_PAIRS_PER_STEP):
        sf_ref[:, 2 * sw * q:2 * sw * q + sw] = s_re[q]
        sf_ref[:, 2 * sw * q + sw:2 * sw * (q + 1)] = s_im[q]
        half = S5_L * GROUP_C
        conv = jnp.concatenate([_dot(ub[q][:, :half], kconv_ref[2 * q]),
                                _dot(ub[q][:, half:], kconv_ref[2 * q + 1])], axis=1)
        y_ref[:, q * pw:(q + 1) * pw] = conv + _dot(sin_s[q].astype(BF16), kin_ref[q])


def _s5_core(u2, wst, kconv, kin, a_pair, s0, nb):
    rows = u2.shape[0]
    pw = 2 * S5_L * GROUP_C
    sw = 2 * P_C
    cols = S5_PAIRS_PER_STEP * pw
    scols = S5_PAIRS_PER_STEP * 2 * sw
    return pl.pallas_call(
        functools.partial(_s5_core_kernel, nb=nb, n_blocks=rows // nb),
        grid=(u2.shape[1] // cols,),
        in_specs=[pl.BlockSpec((rows, cols), lambda i: (0, i)),
                  pl.BlockSpec((S5_PAIRS_PER_STEP, pw, 2 * sw), lambda i: (i, 0, 0)),
                  pl.BlockSpec((2 * S5_PAIRS_PER_STEP, pw // 2, pw // 2), lambda i: (i, 0, 0)),
                  pl.BlockSpec((S5_PAIRS_PER_STEP, 2 * sw, pw), lambda i: (i, 0, 0)),
                  pl.BlockSpec((1, scols), lambda i: (0, i)), pl.BlockSpec((nb, scols), lambda i: (0, i))],
        out_specs=[pl.BlockSpec((rows, cols), lambda i: (0, i)), pl.BlockSpec((nb, scols), lambda i: (0, i))],
        out_shape=[jax.ShapeDtypeStruct(u2.shape, F32), jax.ShapeDtypeStruct((nb, 2 * D_STATE), F32)],
        scratch_shapes=[pltpu.VMEM((S5_PAIRS_PER_STEP, rows, 2 * sw), F32),
                        pltpu.VMEM((S5_PAIRS_PER_STEP, rows, 2 * sw), F32)],
        compiler_params=_params(),
        name="s5_core",
    )(u2, wst, kconv, kin, a_pair, s0)


def _s5_out_kernel(x_ref, u_ref, y2_ref, dskip_ref, wglu_ref, o_ref, y_s, *, nb, blocks):
    rows = blocks * S5_L * nb
    o = None
    for cc in range(N_BLK):
        cols = slice(cc * V7X_MXU_DIM, (cc + 1) * V7X_MXU_DIM)
        for sub in range(V7X_MXU_DIM // V7X_LANES):
            c = cc * (V7X_MXU_DIM // V7X_LANES) + sub
            for half in range(S5_L // GRANULES_PER_VREG):
                lanes = [(c * GRANULES_PER_VREG + gi) * S5_L * GROUP_C + half * V7X_LANES
                         for gi in range(GRANULES_PER_VREG)]
                ws = _granule_transpose([y2_ref[:, l0:l0 + V7X_LANES] for l0 in lanes])
                for kk in range(GRANULES_PER_VREG):
                    y_s[cc, :, half * GRANULES_PER_VREG + kk, :, sub * V7X_LANES:(sub + 1) * V7X_LANES] = (
                        ws[kk].reshape(blocks, nb, V7X_LANES))
        y = y_s[cc].reshape(rows, V7X_MXU_DIM) + dskip_ref[:, cols] * u_ref[:, cols]
        part = _dot(_gelu(y).astype(BF16), wglu_ref[cols, :])
        o = part if o is None else o + part
    o_ref[...] = x_ref[...] + o[:, :D_MODEL] * jax.nn.sigmoid(o[:, D_MODEL:])


def _s5_out(x, u, y2, dskip, wglu, nb, blocks):
    rows = blocks * S5_L * nb
    row_spec = pl.BlockSpec((rows, D_MODEL), lambda i: (i, 0))
    return pl.pallas_call(
        functools.partial(_s5_out_kernel, nb=nb, blocks=blocks),
        grid=(x.shape[0] // rows,),
        in_specs=[row_spec, row_spec, pl.BlockSpec((blocks * nb, S5_L * D_MODEL), lambda i: (i, 0)),
                  _const_spec((1, D_MODEL)), _const_spec((D_MODEL, 2 * D_MODEL))],
        out_specs=row_spec,
        out_shape=jax.ShapeDtypeStruct(x.shape, F32),
        scratch_shapes=[pltpu.VMEM((N_BLK, blocks, S5_L, nb, V7X_MXU_DIM), F32)],
        compiler_params=_params(),
        name="s5_out",
    )(x, u, y2, dskip, wglu)


def _pair_state(re, im):
    lead = re.shape[:-2]
    r = re.reshape(lead + (G_C // 2, 2 * P_C))
    i = im.reshape(lead + (G_C // 2, 2 * P_C))
    return jnp.concatenate([r, i], axis=-1).reshape(lead + (2 * D_STATE,))


def _unpair_state(s, nb):
    s4 = s.reshape(nb, G_C // 2, 2, 2 * P_C)
    return s4[:, :, 0].reshape(nb, G_C, P_C), s4[:, :, 1].reshape(nb, G_C, P_C)


def _s5_kernel(x_ref, g_ref, win_ref, bre_ref, bim_ref, are_ref, aim_ref, cre_ref, cim_ref, dskip_ref, wglu_ref,
               sre0_ref, sim0_ref, o_ref, sre_ref, sim_ref, xre_s, xim_s, *, nb, steps, lane_chunk):
    @pl.when(pl.program_id(0) == 0)
    def _():
        sre_ref[...] = sre0_ref[...]
        sim_ref[...] = sim0_ref[...]

    x = x_ref[...]
    h = _rms(x, g_ref[...]).astype(BF16)
    u = _dot(h, win_ref[...])
    ub = u.astype(BF16)

    def expand(j):
        ucols = slice(j * V7X_MXU_DIM, (j + 1) * V7X_MXU_DIM)
        scols = slice(j * STATE_PER_BLK, (j + 1) * STATE_PER_BLK)
        xre_s[:, scols] = _dot(ub[:, ucols], bre_ref[j])
        xim_s[:, scols] = _dot(ub[:, ucols], bim_ref[j])

    def scan(j):
        for c in range(STATE_PER_BLK // lane_chunk):
            cols = slice(j * STATE_PER_BLK + c * lane_chunk, j * STATE_PER_BLK + (c + 1) * lane_chunk)
            a_re = jnp.broadcast_to(are_ref[:, cols], (nb, lane_chunk))
            a_im = jnp.broadcast_to(aim_ref[:, cols], (nb, lane_chunk))
            s_re, s_im = sre_ref[:, cols], sim_ref[:, cols]
            for t in range(steps):
                sl = slice(t * nb, (t + 1) * nb)
                s_re, s_im = ((a_re * s_re - a_im * s_im) + xre_s[sl, cols],
                              (a_re * s_im + a_im * s_re) + xim_s[sl, cols])
                xre_s[sl, cols] = s_re
                xim_s[sl, cols] = s_im
            sre_ref[:, cols] = s_re
            sim_ref[:, cols] = s_im

    ys = []
    expand(0)
    for j in range(N_BLK):
        if j + 1 < N_BLK:
            expand(j + 1)
        scan(j)
        scols = slice(j * STATE_PER_BLK, (j + 1) * STATE_PER_BLK)
        ys.append(_dot(xre_s[:, scols].astype(BF16), cre_ref[j]) - _dot(xim_s[:, scols].astype(BF16), cim_ref[j]))
    y = jnp.concatenate(ys, axis=1) + dskip_ref[...] * u
    o = _dot(_gelu(y).astype(BF16), wglu_ref[...])
    o_ref[...] = x + o[:, :D_MODEL] * jax.nn.sigmoid(o[:, D_MODEL:])


def _s5(x, p, sre0, sim0, nb, steps, lane_chunk):
    rows = nb * steps
    row_spec = pl.BlockSpec((rows, D_MODEL), lambda i: (i, 0))
    vec = _const_spec((1, D_MODEL))
    svec = _const_spec((1, D_STATE))
    state_spec = _const_spec((nb, D_STATE))
    bspec = _const_spec((N_BLK, V7X_MXU_DIM, STATE_PER_BLK))
    cspec = _const_spec((N_BLK, STATE_PER_BLK, V7X_MXU_DIM))
    state_out = pl.BlockSpec((nb, D_STATE), lambda i: (0, 0))
    return pl.pallas_call(
        functools.partial(_s5_kernel, nb=nb, steps=steps, lane_chunk=lane_chunk),
        grid=(x.shape[0] // rows,),
        in_specs=[row_spec, vec, _const_spec((D_MODEL, D_MODEL)), bspec, bspec, svec, svec, cspec, cspec, vec,
                  _const_spec((D_MODEL, 2 * D_MODEL)), state_spec, state_spec],
        out_specs=[row_spec, state_out, state_out],
        out_shape=[jax.ShapeDtypeStruct(x.shape, F32), jax.ShapeDtypeStruct((nb, D_STATE), F32),
                   jax.ShapeDtypeStruct((nb, D_STATE), F32)],
        scratch_shapes=[pltpu.VMEM((rows, D_STATE), F32), pltpu.VMEM((rows, D_STATE), F32)],
        compiler_params=_params(),
        name="s5",
    )(x, p["g"], p["win"], p["bre"], p["bim"], p["are"], p["aim"], p["cre"], p["cim"], p["dskip"], p["wglu"],
      sre0, sim0)


def _block_diag(w, n_per_blk):
    n, k_in, k_out = w.shape
    wb = w.reshape(n // n_per_blk, n_per_blk, k_in, k_out)
    eye = jnp.eye(n_per_blk, dtype=w.dtype)
    out = jnp.einsum("jgio,gk->jgiko", wb, eye)
    return out.reshape(n // n_per_blk, n_per_blk * k_in, n_per_blk * k_out)


def _to_rows(a):
    return jnp.swapaxes(a, 0, 1).reshape(a.shape[0] * a.shape[1], a.shape[2])


def _from_rows(a, nb):
    return jnp.swapaxes(a.reshape(a.shape[0] // nb, nb, a.shape[1]), 0, 1)


def kernel(x_prompt, x_sample, state_rglru_conv, state_rglru_h, state_s5_re, state_s5_im, norm_mix, norm_ffn, norm_f, w_ff1, w_ff2, w_in_a, sgu_g, w_s, b_s, w_out_a, w_in_b, conv_w, conv_b, w_a, b_a, w_x, b_x, lam, w_out_b, w_in_c, lam_re, lam_im, log_dt, b_re, b_im, c_re, c_im, d_skip, w_glu):
    bp, tp, _ = x_prompt.shape
    bs, ts, _ = x_sample.shape
    rows_p, rows_s = bp * tp, bs * ts
    assert DEPTH % N_MIXERS == 1 and DEPTH > 1, "first and last layers must be SGU layers"
    assert tp % CHUNK == 0 and ts < CHUNK
    assert rows_s % FFN_ROWS == 0 and rows_p % FFN_ROWS == 0

    row = lambda v: v.reshape(1, -1)
    norm_ffn3 = norm_ffn.reshape(DEPTH, 1, D_MODEL)
    w1, w2, gf = w_ff1, w_ff2, row(norm_f)

    x_p, x_s = x_prompt, _to_rows(x_sample)
    outs_v, conv_p, h_p, conv_s, h_s, sre_p, sim_p, sre_s, sim_s = [], [], [], [], [], [], [], [], []
    for layer in range(DEPTH):
        j, kind = layer // N_MIXERS, layer % N_MIXERS
        first, last = layer == 0, layer == DEPTH - 1
        g = row(norm_mix[layer])
        if kind == 0:
            p = dict(g=g, win=w_in_a[j].astype(BF16), wout=w_out_a[j].astype(BF16), sg=row(sgu_g[j]),
                     w_tril=jnp.where(jnp.tril(jnp.ones((CHUNK, CHUNK), dtype=bool)), w_s[j], 0.0).astype(BF16),
                     bias=jnp.repeat(b_s[j].T, HD_A, axis=1),
                     wl=jnp.repeat(w_s[j][:, :ts, :ts].reshape(HEADS_A, ts * ts).T, HD_A, axis=1),
                     bl=jnp.repeat(b_s[j][:, :ts].T, HD_A, axis=1))
            if first:
                x_p = _sgu_prompt(x_p, p, bp, tp, False, True)
            elif last:
                x_p = _sgu_prompt(x_p, p, bp, tp, True, False).reshape(rows_p, D_MODEL)
            else:
                raise NotImplementedError("interior SGU layers")
            x_s, v = _sgu_sample(x_s, p, bs, ts)
            outs_v.append(_from_rows(v, bs))
        elif kind == 1:
            p = dict(g=g, win=w_in_b[j].astype(BF16), cw=conv_w[j], cb=row(conv_b[j]),
                     wa=_block_diag(w_a[j], HEADS_PER_BLK).astype(BF16), ba=row(b_a[j]),
                     wx=_block_diag(w_x[j], HEADS_PER_BLK).astype(BF16), bx=row(b_x[j]),
                     lam=row(lam[j]), wout=w_out_b[j].astype(BF16))
            dt_s = state_rglru_h.dtype
            x_p, cp, hp = _rglru(x_p, p, jnp.zeros(((CONV_W - 1) * bp, D_MODEL), dt_s),
                                 jnp.zeros((bp, D_MODEL), dt_s), bp, RGLRU_PROMPT_STEPS)
            x_s, cs, hs = _rglru(x_s, p, _to_rows(state_rglru_conv[j]), state_rglru_h[j], bs, ts)
            conv_p.append(_from_rows(cp, bp)); h_p.append(hp)
            conv_s.append(_from_rows(cs, bs)); h_s.append(hs)
        else:
            are, aim, bbre, bbim, pw = _s5_disc(lam_re[j], lam_im[j], log_dt[j],
                                                jnp.swapaxes(b_re[j], 1, 2), jnp.swapaxes(b_im[j], 1, 2))
            p = dict(g=g, win=w_in_c[j].astype(BF16),
                     bre=_block_diag(bbre, GROUPS_PER_BLK).astype(BF16),
                     bim=_block_diag(bbim, GROUPS_PER_BLK).astype(BF16),
                     are=are.reshape(1, D_STATE), aim=aim.reshape(1, D_STATE),
                     cre=_block_diag(jnp.swapaxes(c_re[j], 1, 2), GROUPS_PER_BLK).astype(BF16),
                     cim=_block_diag(jnp.swapaxes(c_im[j], 1, 2), GROUPS_PER_BLK).astype(BF16),
                     dskip=row(d_skip[j]), wglu=w_glu[j].astype(BF16))
            pwg = _pair_lanes(jnp.transpose(pw, (2, 0, 1, 3)))
            kconv, wst, kin = _s5_taps(pwg[:, :, :S5_L], pwg[:, :, 1:], pwg[:, :, S5_L - 1::-1],
                                       _pair_lanes(jnp.stack([c_re[j], c_im[j]], axis=1)),
                                       _pair_lanes(jnp.stack([bbre, bbim], axis=1)))
            zs = jnp.zeros((bp, G_C, P_C), state_s5_re.dtype)
            u, u2 = _s5_in(x_p, g, p["win"], bp, S5_TILE_BLOCKS)
            y2, sf = _s5_core(u2, wst, kconv, kin, _pair_state(pw[0, S5_L], pw[1, S5_L]).reshape(1, -1),
                              _pair_state(zs, zs), bp)
            x_p = _s5_out(x_p, u, y2, p["dskip"], p["wglu"], bp, S5_TILE_BLOCKS)
            rp, ip = _unpair_state(sf, bp)
            x_s, rs, is_ = _s5(x_s, p, state_s5_re[j].reshape(bs, D_STATE), state_s5_im[j].reshape(bs, D_STATE),
                               bs, ts, V7X_LANES)
            sre_p.append(rp); sim_p.append(ip)
            sre_s.append(rs.reshape(bs, G_C, P_C)); sim_s.append(is_.reshape(bs, G_C, P_C))
        x_p, x_s = _ffn(x_p, x_s, norm_ffn3, w1, w2, gf, layer, last)

    y_prompt = x_p.reshape(bp, tp, D_MODEL)
    y_sample = _from_rows(x_s, bs)
    return (y_prompt, y_sample, jnp.stack(outs_v), jnp.stack(conv_p), jnp.stack(h_p), jnp.stack(conv_s),
            jnp.stack(h_s), jnp.stack(sre_p), jnp.stack(sim_p), jnp.stack(sre_s), jnp.stack(sim_s))
```

```python
import functools

import jax
import jax.numpy as jnp
from jax import lax
from jax.experimental import pallas as pl
from jax.experimental.pallas import tpu as pltpu

F32 = jnp.float32
BF16 = jnp.bfloat16

D_MODEL = 1024
DEPTH = 4
N_MIXERS = 3
EPS = 1e-6
CHUNK = 128
HEADS_A = 8
HD_A = D_MODEL // HEADS_A
HEADS_B = 16
HD_B = D_MODEL // HEADS_B
CONV_W = 4
LRU_C = 8.0
GROUP_C = 16
G_C = D_MODEL // GROUP_C
P_C = 64
D_STATE = G_C * P_C
D_FF = 4 * D_MODEL

V7X_LANES = 128
V7X_MXU_DIM = 256
V7X_VMEM_BYTES = 64 * 1024 * 1024
VMEM_LIMIT = V7X_VMEM_BYTES - 8 * 1024 * 1024

N_BLK = D_MODEL // V7X_MXU_DIM
HEADS_PER_BLK = V7X_MXU_DIM // HD_B
GROUPS_PER_BLK = V7X_MXU_DIM // GROUP_C
STATE_PER_BLK = GROUPS_PER_BLK * P_C

FFN_ROWS = 512
FFN_COL_CHUNK = 1024
FFN_STAGE_BYTES = 2 * 1024 * 1024
S5_L = 16
S5_TILE_BLOCKS = 4
S5_PAIRS_PER_STEP = 2
GRANULES_PER_VREG = V7X_LANES // GROUP_C
RGLRU_PROMPT_STEPS = 128


def _rms(x, g):
    return (x * lax.rsqrt(jnp.mean(x * x, axis=-1, keepdims=True) + EPS)) * g


GELU_C0 = 0.7978845608028654
GELU_C1 = GELU_C0 * 0.044715


def _gelu(x):
    return x * (0.5 + 0.5 * jnp.tanh(x * (GELU_C0 + GELU_C1 * (x * x))))


def _dot(a, b):
    return jnp.dot(a, b, preferred_element_type=F32)


def _const_spec(shape):
    zeros = (0,) * len(shape)
    return pl.BlockSpec(shape, lambda i: zeros, pipeline_mode=pl.Buffered(1))


def _layer_spec(shape, layer):
    idx = (layer,) + (0,) * len(shape)
    return pl.BlockSpec((None,) + tuple(shape), lambda i: idx, pipeline_mode=pl.Buffered(1))


def _params():
    return pltpu.CompilerParams(dimension_semantics=("arbitrary",), vmem_limit_bytes=VMEM_LIMIT)


def _ffn_tile(x_ref, g_ref, w1_ref, w2_ref, gf_ref, o_ref, final_norm):
    x = x_ref[...]
    h = _rms(x, g_ref[...]).astype(BF16)
    acc = None
    for j in range(D_FF // FFN_COL_CHUNK):
        cols = slice(j * FFN_COL_CHUNK, (j + 1) * FFN_COL_CHUNK)
        a = jnp.square(jnp.maximum(_dot(h, w1_ref[:, cols]), 0.0)).astype(BF16)
        part = _dot(a, w2_ref[cols, :])
        acc = part if acc is None else acc + part
    y = x + acc
    if final_norm:
        y = _rms(y, gf_ref[...])
    o_ref[...] = y


def _load_as_bf16(jobs):
    def copies_of(src, stage, sem):
        rows = stage.shape[1]
        return [pltpu.make_async_copy(src.at[pl.ds(k * rows, rows)], stage.at[k % 2], sem.at[k % 2])
                for k in range(src.shape[0] // rows)]

    plans = [copies_of(src, stage, sem) for src, _, stage, sem in jobs]
    n = len(plans[0])
    assert all(len(p) == n for p in plans)
    for p in plans:
        p[0].start()
    for k in range(n):
        if k + 1 < n:
            for p in plans:
                p[k + 1].start()
        for p, (_, dst, stage, _) in zip(plans, jobs):
            p[k].wait()
            rows = stage.shape[1]
            dst[k * rows:(k + 1) * rows, :] = stage[k % 2].astype(BF16)


def _ffn_kernel(xp_ref, xs_ref, g_ref, w1_hbm, w2_hbm, gf_ref, op_ref, os_ref, w1_s, w2_s, stage1, stage2,
                sem1, sem2, *, final_norm, n_p, layer):
    i = pl.program_id(0)

    @pl.when(i == 0)
    def _():
        _load_as_bf16([(w1_hbm.at[layer], w1_s, stage1, sem1), (w2_hbm.at[layer], w2_s, stage2, sem2)])

    @pl.when(i < n_p)
    def _():
        _ffn_tile(xp_ref, g_ref, w1_s, w2_s, gf_ref, op_ref, final_norm)

    @pl.when(i >= n_p)
    def _():
        _ffn_tile(xs_ref, g_ref, w1_s, w2_s, gf_ref, os_ref, final_norm)


def _ffn(x_p, x_s, g, w1, w2, gf, layer, final_norm):
    n_p, n_s = x_p.shape[0] // FFN_ROWS, x_s.shape[0] // FFN_ROWS
    p_spec = pl.BlockSpec((FFN_ROWS, D_MODEL), lambda i: (jnp.minimum(i, n_p - 1), 0))
    s_spec = pl.BlockSpec((FFN_ROWS, D_MODEL), lambda i: (jnp.maximum(i - n_p, 0), 0))
    hbm = pl.BlockSpec(memory_space=pl.ANY)
    return pl.pallas_call(
        functools.partial(_ffn_kernel, final_norm=final_norm, n_p=n_p, layer=layer),
        grid=(n_p + n_s,),
        in_specs=[p_spec, s_spec, _layer_spec((1, D_MODEL), layer), hbm, hbm, _const_spec((1, D_MODEL))],
        out_specs=[p_spec, s_spec],
        out_shape=[jax.ShapeDtypeStruct(x_p.shape, F32), jax.ShapeDtypeStruct(x_s.shape, F32)],
        scratch_shapes=[pltpu.VMEM((D_MODEL, D_FF), BF16), pltpu.VMEM((D_FF, D_MODEL), BF16),
                        pltpu.VMEM((2, FFN_STAGE_BYTES // (4 * D_FF), D_FF), F32),
                        pltpu.VMEM((2, FFN_STAGE_BYTES // (4 * D_MODEL), D_MODEL), F32),
                        pltpu.SemaphoreType.DMA((2,)), pltpu.SemaphoreType.DMA((2,))],
        compiler_params=_params(),
        name="ffn",
    )(x_p, x_s, g, w1, w2, gf)


def _sgu_front(x, g_ref, win_ref, sg_ref):
    h = _rms(x, g_ref[...]).astype(BF16)
    uv = _gelu(_dot(h, win_ref[...]))
    return uv[:, :D_MODEL], _rms(uv[:, D_MODEL:], sg_ref[...])


def _rows_to_batch_major(x, nb):
    steps = x.shape[0] // nb
    return jnp.swapaxes(x.reshape(steps, nb, x.shape[1]), 0, 1).reshape(x.shape)


def _rows_to_time_major(x, nb):
    steps = x.shape[0] // nb
    return jnp.swapaxes(x.reshape(nb, steps, x.shape[1]), 0, 1).reshape(x.shape)


def _sgu_prompt_kernel(x_ref, g_ref, win_ref, sg_ref, w_ref, bias_ref, wout_ref, o_ref, y_s, *, nb, in_tm, out_tm):
    rows = nb * CHUNK
    if in_tm:
        x = _rows_to_batch_major(x_ref[...], nb)
    else:
        x = x_ref[...].reshape(rows, D_MODEL)
    u, v = _sgu_front(x, g_ref, win_ref, sg_ref)
    vb = v.astype(BF16)
    for b in range(nb):
        rs = slice(b * CHUNK, (b + 1) * CHUNK)
        for g in range(HEADS_A):
            cs = slice(g * HD_A, (g + 1) * HD_A)
            mixed = _dot(w_ref[g], vb[rs, cs]) + bias_ref[:, cs]
            y_s[rs, cs] = (u[rs, cs] * mixed).astype(BF16)
    o = x + _dot(y_s[...], wout_ref[...])
    if out_tm:
        o_ref[...] = _rows_to_time_major(o, nb)
    else:
        o_ref[...] = o.reshape(nb, CHUNK, D_MODEL)


def _sgu_sample_kernel(x_ref, g_ref, win_ref, sg_ref, wl_ref, bl_ref, wout_ref, o_ref, v_ref, *, nb, steps):
    x = x_ref[...]
    u, v = _sgu_front(x, g_ref, win_ref, sg_ref)
    v_ref[...] = v
    mixed = []
    for t in range(steps):
        m = None
        for s in range(t + 1):
            term = wl_ref[t * steps + s:t * steps + s + 1, :] * v[s * nb:(s + 1) * nb, :]
            m = term if m is None else m + term
        mixed.append(m + bl_ref[t:t + 1, :])
    y = (u * jnp.concatenate(mixed, axis=0)).astype(BF16)
    o_ref[...] = x + _dot(y, wout_ref[...])


def _sgu_prompt(x, p, nb, n_steps, in_tm, out_tm):
    rows = nb * CHUNK
    tm_spec = pl.BlockSpec((rows, D_MODEL), lambda i: (i, 0))
    bm_spec = pl.BlockSpec((nb, CHUNK, D_MODEL), lambda i: (0, i, 0))
    out_shape = (jax.ShapeDtypeStruct((nb * n_steps, D_MODEL), F32) if out_tm
                 else jax.ShapeDtypeStruct((nb, n_steps, D_MODEL), F32))
    return pl.pallas_call(
        functools.partial(_sgu_prompt_kernel, nb=nb, in_tm=in_tm, out_tm=out_tm),
        grid=(n_steps // CHUNK,),
        in_specs=[tm_spec if in_tm else bm_spec, _const_spec((1, D_MODEL)), _const_spec((D_MODEL, 2 * D_MODEL)),
                  _const_spec((1, D_MODEL)), _const_spec((HEADS_A, CHUNK, CHUNK)),
                  _const_spec((CHUNK, D_MODEL)), _const_spec((D_MODEL, D_MODEL))],
        out_specs=tm_spec if out_tm else bm_spec,
        out_shape=out_shape,
        scratch_shapes=[pltpu.VMEM((rows, D_MODEL), BF16)],
        compiler_params=_params(),
        name="sgu_prompt",
    )(x, p["g"], p["win"], p["sg"], p["w_tril"], p["bias"], p["wout"])


def _sgu_sample(x, p, nb, steps):
    tile = steps * nb
    tile_spec = pl.BlockSpec((tile, D_MODEL), lambda i: (0, 0))
    return pl.pallas_call(
        functools.partial(_sgu_sample_kernel, nb=nb, steps=steps),
        grid=(1,),
        in_specs=[tile_spec, _const_spec((1, D_MODEL)), _const_spec((D_MODEL, 2 * D_MODEL)),
                  _const_spec((1, D_MODEL)), _const_spec((steps * steps, D_MODEL)),
                  _const_spec((steps, D_MODEL)), _const_spec((D_MODEL, D_MODEL))],
        out_specs=[tile_spec, tile_spec],
        out_shape=[jax.ShapeDtypeStruct((tile, D_MODEL), F32), jax.ShapeDtypeStruct((tile, D_MODEL), F32)],
        compiler_params=_params(),
        name="sgu_sample",
    )(x, p["g"], p["win"], p["sg"], p["wl"], p["bl"], p["wout"])


def _rglru_kernel(x_ref, g_ref, win_ref, cw_ref, cb_ref, wa_ref, ba_ref, wx_ref, bx_ref, lam_ref, wout_ref,
                  conv0_ref, h0_ref, o_ref, conv_ref, h_ref, xext_s, a_s, b_s, y_s, *, nb, steps):
    rows = nb * steps
    halo = (CONV_W - 1) * nb

    @pl.when(pl.program_id(0) == 0)
    def _():
        conv_ref[...] = conv0_ref[...]
        h_ref[...] = h0_ref[...]

    def blk_cols(j):
        return slice(j * V7X_MXU_DIM, (j + 1) * V7X_MXU_DIM)

    x = x_ref[...]
    h = _rms(x, g_ref[...]).astype(BF16)
    for j in range(N_BLK):
        cols = blk_cols(j)
        xext, a_j, b_j = xext_s.at[j], a_s.at[j], b_s.at[j]
        xext[0:halo, :] = conv_ref[:, cols]
        xext[halo:halo + rows, :] = _dot(h, win_ref[:, blk_cols(N_BLK + j)])
        gate = _gelu(_dot(h, win_ref[:, cols]))
        conv = None
        for w in range(CONV_W):
            term = xext[w * nb:w * nb + rows, :] * cw_ref[w:w + 1, cols]
            conv = term if conv is None else conv + term
        conv_ref[:, cols] = xext[rows:rows + halo, :]
        xc = cb_ref[:, cols] + conv
        xcb = xc.astype(BF16)
        r = jax.nn.sigmoid(_dot(xcb, wa_ref[j]) + ba_ref[:, cols])
        ig = jax.nn.sigmoid(_dot(xcb, wx_ref[j]) + bx_ref[:, cols])
        neg_log_a = (LRU_C * r) * jax.nn.softplus(-lam_ref[:, cols])
        a = jnp.exp(-neg_log_a)
        a_j[...] = a
        w = jnp.tanh(neg_log_a) * (a * a + 1.0)
        mult = jnp.where(w > 0.0, w * lax.rsqrt(w), 0.0)
        b_j[...] = mult * (ig * xc)
        hcur = h_ref[:, cols]
        for t in range(steps):
            sl = slice(t * nb, (t + 1) * nb)
            hcur = a_j[sl, :] * hcur + b_j[sl, :]
            b_j[sl, :] = hcur
        h_ref[:, cols] = hcur
        y_s[:, cols] = (b_j[...] * gate).astype(BF16)
    o_ref[...] = x + _dot(y_s[...], wout_ref[...])


def _rglru(x, p, conv0, h0, nb, steps):
    rows = nb * steps
    halo = (CONV_W - 1) * nb
    row_spec = pl.BlockSpec((rows, D_MODEL), lambda i: (i, 0))
    vec = _const_spec((1, D_MODEL))
    blk = _const_spec((N_BLK, V7X_MXU_DIM, V7X_MXU_DIM))
    return pl.pallas_call(
        functools.partial(_rglru_kernel, nb=nb, steps=steps),
        grid=(x.shape[0] // rows,),
        in_specs=[row_spec, vec, _const_spec((D_MODEL, 2 * D_MODEL)), _const_spec((CONV_W, D_MODEL)), vec,
                  blk, vec, blk, vec, vec, _const_spec((D_MODEL, D_MODEL)),
                  _const_spec((halo, D_MODEL)), _const_spec((nb, D_MODEL))],
        out_specs=[row_spec, pl.BlockSpec((halo, D_MODEL), lambda i: (0, 0)),
                   pl.BlockSpec((nb, D_MODEL), lambda i: (0, 0))],
        out_shape=[jax.ShapeDtypeStruct(x.shape, F32), jax.ShapeDtypeStruct((halo, D_MODEL), F32),
                   jax.ShapeDtypeStruct((nb, D_MODEL), F32)],
        scratch_shapes=[pltpu.VMEM((N_BLK, rows + halo, V7X_MXU_DIM), F32), pltpu.VMEM((N_BLK, rows, V7X_MXU_DIM), F32),
                        pltpu.VMEM((N_BLK, rows, V7X_MXU_DIM), F32), pltpu.VMEM((rows, D_MODEL), BF16)],
        compiler_params=_params(),
        name="rglru",
    )(x, p["g"], p["win"], p["cw"], p["cb"], p["wa"], p["ba"], p["wx"], p["bx"], p["lam"], p["wout"], conv0, h0)


def _cmul(a_re, a_im, b_re, b_im):
    return a_re * b_re - a_im * b_im, a_re * b_im + a_im * b_re


def _s5_disc_kernel(lre_ref, lim_ref, ldt_ref, bre_ref, bim_ref, are_ref, aim_ref, bbre_ref, bbim_ref, pw_ref):
    lr, li = lre_ref[...], lim_ref[...]
    dt = jnp.exp(ldt_ref[...])
    mag = jnp.exp(lr * dt)
    ab_re, ab_im = mag * jnp.cos(li * dt), mag * jnp.sin(li * dt)
    zr, zi = ab_re - 1.0, ab_im
    den = lr * lr + li * li
    q_re = (zr * lr + zi * li) / den
    q_im = (zi * lr - zr * li) / den
    are_ref[...] = ab_re
    aim_ref[...] = ab_im
    br, bi = bre_ref[...], bim_ref[...]
    bbre_ref[...] = q_re[:, None, :] * br - q_im[:, None, :] * bi
    bbim_ref[...] = q_re[:, None, :] * bi + q_im[:, None, :] * br
    p_re, p_im = jnp.ones_like(ab_re), jnp.zeros_like(ab_re)
    for k in range(S5_L + 1):
        pw_ref[0, k] = p_re
        pw_ref[1, k] = p_im
        p_re, p_im = _cmul(p_re, p_im, ab_re, ab_im)


def _s5_disc(lam_re, lam_im, log_dt, b_re_t, b_im_t):
    gp = jax.ShapeDtypeStruct((G_C, P_C), F32)
    ghp = jax.ShapeDtypeStruct((G_C, GROUP_C, P_C), F32)
    pw = jax.ShapeDtypeStruct((2, S5_L + 1, G_C, P_C), F32)
    return pl.pallas_call(_s5_disc_kernel, out_shape=[gp, gp, ghp, ghp, pw], name="s5_disc")(
        lam_re, lam_im, log_dt.reshape(G_C, 1), b_re_t, b_im_t)


def _s5_taps_kernel(pw0_ref, pw1_ref, pwf_ref, c_ref, b_ref, kconv_ref, wst_ref, kin_ref, *, pairs):
    n = S5_L * GROUP_C
    pair_lanes = 2 * P_C

    def rep(t):
        return jnp.broadcast_to(t[:, None, :], (S5_L, GROUP_C, pair_lanes)).reshape(n, pair_lanes)

    def til(c):
        return jnp.broadcast_to(c[None, :, :], (S5_L, GROUP_C, pair_lanes)).reshape(n, pair_lanes)

    def dot_t(a, b):
        return lax.dot_general(a, b, (((1,), (1,)), ((), ())), precision=lax.Precision.HIGHEST,
                               preferred_element_type=F32)

    def first(shape, axis):
        return lax.broadcasted_iota(jnp.int32, shape, axis) < P_C

    lane = lax.broadcasted_iota(jnp.int32, (GROUP_C, n), 1)
    for q in range(pairs):
        cr, ci = til(c_ref[q, 0]), til(c_ref[q, 1])
        b_re, b_im = b_ref[q, 0], b_ref[q, 1]
        x_re, x_im = _cmul(rep(pw0_ref[q, 0]), rep(pw0_ref[q, 1]), cr, ci)
        for gi in range(2):
            mine = first(b_re.shape, 1) == (gi == 0)
            mrow = (dot_t(jnp.where(mine, b_re, 0.0), x_re)
                    - dot_t(jnp.where(mine, b_im, 0.0), x_im))
            for s in range(S5_L):
                blk = mrow if s == 0 else jnp.where(lane >= s * GROUP_C, pltpu.roll(mrow, s * GROUP_C, 1), 0.0)
                kconv_ref[2 * q + gi, s * GROUP_C:(s + 1) * GROUP_C, :] = blk.astype(BF16)
        e_re, e_im = _cmul(rep(pw1_ref[q, 0]), rep(pw1_ref[q, 1]), cr, ci)
        et_re, et_im = e_re.T, (-e_im).T
        top = first(et_re.shape, 0)
        kin_ref[q] = jnp.concatenate(
            [jnp.concatenate([jnp.where(top, et_re, 0.0), jnp.where(top, 0.0, et_re)], axis=1),
             jnp.concatenate([jnp.where(top, et_im, 0.0), jnp.where(top, 0.0, et_im)], axis=1)], axis=0).astype(BF16)
        f_re, f_im = _cmul(rep(pwf_ref[q, 0]), rep(pwf_ref[q, 1]), til(b_re), til(b_im))
        left = first(f_re.shape, 1)
        wst_ref[q] = jnp.concatenate(
            [jnp.concatenate([jnp.where(left, f_re, 0.0), jnp.where(left, f_im, 0.0)], axis=1),
             jnp.concatenate([jnp.where(left, 0.0, f_re), jnp.where(left, 0.0, f_im)], axis=1)], axis=0).astype(BF16)


def _s5_taps(pw0, pw1, pwf, c, bb):
    pairs = 4
    n = S5_L * GROUP_C
    pspec = pl.BlockSpec((pairs, 2, S5_L, 2 * P_C), lambda i: (i, 0, 0, 0))
    hspec = pl.BlockSpec((pairs, 2, GROUP_C, 2 * P_C), lambda i: (i, 0, 0, 0))
    return pl.pallas_call(
        functools.partial(_s5_taps_kernel, pairs=pairs),
        grid=(G_C // (2 * pairs),),
        in_specs=[pspec, pspec, pspec, hspec, hspec],
        out_specs=[pl.BlockSpec((2 * pairs, n, n), lambda i: (i, 0, 0)),
                   pl.BlockSpec((pairs, 2 * n, 4 * P_C), lambda i: (i, 0, 0)),
                   pl.BlockSpec((pairs, 4 * P_C, 2 * n), lambda i: (i, 0, 0))],
        out_shape=[jax.ShapeDtypeStruct((G_C, n, n), BF16), jax.ShapeDtypeStruct((G_C // 2, 2 * n, 4 * P_C), BF16),
                   jax.ShapeDtypeStruct((G_C // 2, 4 * P_C, 2 * n), BF16)],
        compiler_params=_params(),
        name="s5_taps",
    )(pw0, pw1, pwf, c, bb)


def _pair_lanes(a):
    a2 = a.reshape((G_C // 2, 2) + a.shape[1:])
    return jnp.concatenate([a2[:, 0], a2[:, 1]], axis=-1)


def _granule_transpose(vs):
    vs = list(vs)
    granule = lax.broadcasted_iota(jnp.int32, vs[0].shape, 1) // GROUP_C
    d = 1
    while d < len(vs):
        keep = (granule & d) == 0
        for i in range(len(vs)):
            if i & d == 0:
                a, b = vs[i], vs[i + d]
                vs[i] = jnp.where(keep, a, pltpu.roll(b, d * GROUP_C, 1))
                vs[i + d] = jnp.where(keep, pltpu.roll(a, V7X_LANES - d * GROUP_C, 1), b)
        d *= 2
    return vs


def _s5_in_kernel(x_ref, g_ref, win_ref, u_ref, u2_ref, *, nb, blocks):
    h = _rms(x_ref[...], g_ref[...]).astype(BF16)
    rows2 = blocks * nb
    for cc in range(N_BLK):
        cols = slice(cc * V7X_MXU_DIM, (cc + 1) * V7X_MXU_DIM)
        u = _dot(h, win_ref[:, cols])
        u_ref[:, cols] = u
        u4 = u.reshape(blocks, S5_L, nb, V7X_MXU_DIM)
        for sub in range(V7X_MXU_DIM // V7X_LANES):
            c = cc * (V7X_MXU_DIM // V7X_LANES) + sub
            for half in range(S5_L // GRANULES_PER_VREG):
                vs = []
                for kk in range(GRANULES_PER_VREG):
                    slab = u4[:, half * GRANULES_PER_VREG + kk, :, sub * V7X_LANES:(sub + 1) * V7X_LANES]
                    vs.append(slab.reshape(rows2, V7X_LANES))
                ws = _granule_transpose(vs)
                for gi in range(GRANULES_PER_VREG):
                    lane0 = (c * GRANULES_PER_VREG + gi) * S5_L * GROUP_C + half * V7X_LANES
                    u2_ref[:, lane0:lane0 + V7X_LANES] = ws[gi].astype(BF16)


def _s5_in(x, g, win, nb, blocks):
    rows = blocks * S5_L * nb
    n_blocks = x.shape[0] // (S5_L * nb)
    row_spec = pl.BlockSpec((rows, D_MODEL), lambda i: (i, 0))
    return pl.pallas_call(
        functools.partial(_s5_in_kernel, nb=nb, blocks=blocks),
        grid=(x.shape[0] // rows,),
        in_specs=[row_spec, _const_spec((1, D_MODEL)), _const_spec((D_MODEL, D_MODEL))],
        out_specs=[row_spec, pl.BlockSpec((blocks * nb, S5_L * D_MODEL), lambda i: (i, 0))],
        out_shape=[jax.ShapeDtypeStruct(x.shape, F32),
                   jax.ShapeDtypeStruct((n_blocks * nb, S5_L * D_MODEL), BF16)],
        compiler_params=_params(),
        name="s5_in",
    )(x, g, win)


def _s5_core_kernel(u_ref, wst_ref, kconv_ref, kin_ref, a_ref, s0_ref, y_ref, sf_ref, sc_s, sin_s, *, nb, n_blocks):
    pw = 2 * S5_L * GROUP_C
    sw = 2 * P_C
    ub = [u_ref[:, q * pw:(q + 1) * pw] for q in range(S5_PAIRS_PER_STEP)]
    for q in range(S5_PAIRS_PER_STEP):
        sc_s[q] = _dot(ub[q], wst_ref[q])
    a_re = [jnp.broadcast_to(a_ref[:, 2 * sw * q:2 * sw * q + sw], (nb, sw)) for q in range(S5_PAIRS_PER_STEP)]
    a_im = [jnp.broadcast_to(a_ref[:, 2 * sw * q + sw:2 * sw * (q + 1)], (nb, sw)) for q in range(S5_PAIRS_PER_STEP)]
    s_re = [s0_ref[:, 2 * sw * q:2 * sw * q + sw] for q in range(S5_PAIRS_PER_STEP)]
    s_im = [s0_ref[:, 2 * sw * q + sw:2 * sw * (q + 1)] for q in range(S5_PAIRS_PER_STEP)]
    for blk in range(n_blocks):
        rs = slice(blk * nb, (blk + 1) * nb)
        for q in range(S5_PAIRS_PER_STEP):
            sin_s[q, rs, 0:sw] = s_re[q]
            sin_s[q, rs, sw:2 * sw] = s_im[q]
            n_re, n_im = _cmul(a_re[q], a_im[q], s_re[q], s_im[q])
            s_re[q], s_im[q] = n_re + sc_s[q, rs, 0:sw], n_im + sc_s[q, rs, sw:2 * sw]
    for q in range(S5_PAIRS_PER_STEP):
        sf_ref[:, 2 * sw * q:2 * sw * q + sw] = s_re[q]
        sf_ref[:, 2 * sw * q + sw:2 * sw * (q + 1)] = s_im[q]
        half = S5_L * GROUP_C
        conv = jnp.concatenate([_dot(ub[q][:, :half], kconv_ref[2 * q]),
                                _dot(ub[q][:, half:], kconv_ref[2 * q + 1])], axis=1)
        y_ref[:, q * pw:(q + 1) * pw] = conv + _dot(sin_s[q].astype(BF16), kin_ref[q])


def _s5_core(u2, wst, kconv, kin, a_pair, s0, nb):
    rows = u2.shape[0]
    pw = 2 * S5_L * GROUP_C
    sw = 2 * P_C
    cols = S5_PAIRS_PER_STEP * pw
    scols = S5_PAIRS_PER_STEP * 2 * sw
    return pl.pallas_call(
        functools.partial(_s5_core_kernel, nb=nb, n_blocks=rows // nb),
        grid=(u2.shape[1] // cols,),
        in_specs=[pl.BlockSpec((rows, cols), lambda i: (0, i)),
                  pl.BlockSpec((S5_PAIRS_PER_STEP, pw, 2 * sw), lambda i: (i, 0, 0)),
                  pl.BlockSpec((2 * S5_PAIRS_PER_STEP, pw // 2, pw // 2), lambda i: (i, 0, 0)),
                  pl.BlockSpec((S5_PAIRS_PER_STEP, 2 * sw, pw), lambda i: (i, 0, 0)),
                  pl.BlockSpec((1, scols), lambda i: (0, i)), pl.BlockSpec((nb, scols), lambda i: (0, i))],
        out_specs=[pl.BlockSpec((rows, cols), lambda i: (0, i)), pl.BlockSpec((nb, scols), lambda i: (0, i))],
        out_shape=[jax.ShapeDtypeStruct(u2.shape, F32), jax.ShapeDtypeStruct((nb, 2 * D_STATE), F32)],
        scratch_shapes=[pltpu.VMEM((S5_PAIRS_PER_STEP, rows, 2 * sw), F32),
                        pltpu.VMEM((S5_PAIRS_PER_STEP, rows, 2 * sw), F32)],
        compiler_params=_params(),
        name="s5_core",
    )(u2, wst, kconv, kin, a_pair, s0)


def _s5_out_kernel(x_ref, u_ref, y2_ref, dskip_ref, wglu_ref, o_ref, y_s, *, nb, blocks):
    rows = blocks * S5_L * nb
    o = None
    for cc in range(N_BLK):
        cols = slice(cc * V7X_MXU_DIM, (cc + 1) * V7X_MXU_DIM)
        for sub in range(V7X_MXU_DIM // V7X_LANES):
            c = cc * (V7X_MXU_DIM // V7X_LANES) + sub
            for half in range(S5_L // GRANULES_PER_VREG):
                lanes = [(c * GRANULES_PER_VREG + gi) * S5_L * GROUP_C + half * V7X_LANES
                         for gi in range(GRANULES_PER_VREG)]
                ws = _granule_transpose([y2_ref[:, l0:l0 + V7X_LANES] for l0 in lanes])
                for kk in range(GRANULES_PER_VREG):
                    y_s[cc, :, half * GRANULES_PER_VREG + kk, :, sub * V7X_LANES:(sub + 1) * V7X_LANES] = (
                        ws[kk].reshape(blocks, nb, V7X_LANES))
        y = y_s[cc].reshape(rows, V7X_MXU_DIM) + dskip_ref[:, cols] * u_ref[:, cols]
        part = _dot(_gelu(y).astype(BF16), wglu_ref[cols, :])
        o = part if o is None else o + part
    o_ref[...] = x_ref[...] + o[:, :D_MODEL] * jax.nn.sigmoid(o[:, D_MODEL:])


def _s5_out(x, u, y2, dskip, wglu, nb, blocks):
    rows = blocks * S5_L * nb
    row_spec = pl.BlockSpec((rows, D_MODEL), lambda i: (i, 0))
    return pl.pallas_call(
        functools.partial(_s5_out_kernel, nb=nb, blocks=blocks),
        grid=(x.shape[0] // rows,),
        in_specs=[row_spec, row_spec, pl.BlockSpec((blocks * nb, S5_L * D_MODEL), lambda i: (i, 0)),
                  _const_spec((1, D_MODEL)), _const_spec((D_MODEL, 2 * D_MODEL))],
        out_specs=row_spec,
        out_shape=jax.ShapeDtypeStruct(x.shape, F32),
        scratch_shapes=[pltpu.VMEM((N_BLK, blocks, S5_L, nb, V7X_MXU_DIM), F32)],
        compiler_params=_params(),
        name="s5_out",
    )(x, u, y2, dskip, wglu)


def _pair_state(re, im):
    lead = re.shape[:-2]
    r = re.reshape(lead + (G_C // 2, 2 * P_C))
    i = im.reshape(lead + (G_C // 2, 2 * P_C))
    return jnp.concatenate([r, i], axis=-1).reshape(lead + (2 * D_STATE,))


def _unpair_state(s, nb):
    s4 = s.reshape(nb, G_C // 2, 2, 2 * P_C)
    return s4[:, :, 0].reshape(nb, G_C, P_C), s4[:, :, 1].reshape(nb, G_C, P_C)


def _s5_kernel(x_ref, g_ref, win_ref, bre_ref, bim_ref, are_ref, aim_ref, cre_ref, cim_ref, dskip_ref, wglu_ref,
               sre0_ref, sim0_ref, o_ref, sre_ref, sim_ref, xre_s, xim_s, *, nb, steps, lane_chunk):
    @pl.when(pl.program_id(0) == 0)
    def _():
        sre_ref[...] = sre0_ref[...]
        sim_ref[...] = sim0_ref[...]

    x = x_ref[...]
    h = _rms(x, g_ref[...]).astype(BF16)
    u = _dot(h, win_ref[...])
    ub = u.astype(BF16)

    def expand(j):
        ucols = slice(j * V7X_MXU_DIM, (j + 1) * V7X_MXU_DIM)
        scols = slice(j * STATE_PER_BLK, (j + 1) * STATE_PER_BLK)
        xre_s[:, scols] = _dot(ub[:, ucols], bre_ref[j])
        xim_s[:, scols] = _dot(ub[:, ucols], bim_ref[j])

    def scan(j):
        for c in range(STATE_PER_BLK // lane_chunk):
            cols = slice(j * STATE_PER_BLK + c * lane_chunk, j * STATE_PER_BLK + (c + 1) * lane_chunk)
            a_re = jnp.broadcast_to(are_ref[:, cols], (nb, lane_chunk))
            a_im = jnp.broadcast_to(aim_ref[:, cols], (nb, lane_chunk))
            s_re, s_im = sre_ref[:, cols], sim_ref[:, cols]
            for t in range(steps):
                sl = slice(t * nb, (t + 1) * nb)
                s_re, s_im = ((a_re * s_re - a_im * s_im) + xre_s[sl, cols],
                              (a_re * s_im + a_im * s_re) + xim_s[sl, cols])
                xre_s[sl, cols] = s_re
                xim_s[sl, cols] = s_im
            sre_ref[:, cols] = s_re
            sim_ref[:, cols] = s_im

    ys = []
    expand(0)
    for j in range(N_BLK):
        if j + 1 < N_BLK:
            expand(j + 1)
        scan(j)
        scols = slice(j * STATE_PER_BLK, (j + 1) * STATE_PER_BLK)
        ys.append(_dot(xre_s[:, scols].astype(BF16), cre_ref[j]) - _dot(xim_s[:, scols].astype(BF16), cim_ref[j]))
    y = jnp.concatenate(ys, axis=1) + dskip_ref[...] * u
    o = _dot(_gelu(y).astype(BF16), wglu_ref[...])
    o_ref[...] = x + o[:, :D_MODEL] * jax.nn.sigmoid(o[:, D_MODEL:])


def _s5(x, p, sre0, sim0, nb, steps, lane_chunk):
    rows = nb * steps
    row_spec = pl.BlockSpec((rows, D_MODEL), lambda i: (i, 0))
    vec = _const_spec((1, D_MODEL))
    svec = _const_spec((1, D_STATE))
    state_spec = _const_spec((nb, D_STATE))
    bspec = _const_spec((N_BLK, V7X_MXU_DIM, STATE_PER_BLK))
    cspec = _const_spec((N_BLK, STATE_PER_BLK, V7X_MXU_DIM))
    state_out = pl.BlockSpec((nb, D_STATE), lambda i: (0, 0))
    return pl.pallas_call(
        functools.partial(_s5_kernel, nb=nb, steps=steps, lane_chunk=lane_chunk),
        grid=(x.shape[0] // rows,),
        in_specs=[row_spec, vec, _const_spec((D_MODEL, D_MODEL)), bspec, bspec, svec, svec, cspec, cspec, vec,
                  _const_spec((D_MODEL, 2 * D_MODEL)), state_spec, state_spec],
        out_specs=[row_spec, state_out, state_out],
        out_shape=[jax.ShapeDtypeStruct(x.shape, F32), jax.ShapeDtypeStruct((nb, D_STATE), F32),
                   jax.ShapeDtypeStruct((nb, D_STATE), F32)],
        scratch_shapes=[pltpu.VMEM((rows, D_STATE), F32), pltpu.VMEM((rows, D_STATE), F32)],
        compiler_params=_params(),
        name="s5",
    )(x, p["g"], p["win"], p["bre"], p["bim"], p["are"], p["aim"], p["cre"], p["cim"], p["dskip"], p["wglu"],
      sre0, sim0)


def _block_diag(w, n_per_blk):
    n, k_in, k_out = w.shape
    wb = w.reshape(n // n_per_blk, n_per_blk, k_in, k_out)
    eye = jnp.eye(n_per_blk, dtype=w.dtype)
    out = jnp.einsum("jgio,gk->jgiko", wb, eye)
    return out.reshape(n // n_per_blk, n_per_blk * k_in, n_per_blk * k_out)


def _to_rows(a):
    return jnp.swapaxes(a, 0, 1).reshape(a.shape[0] * a.shape[1], a.shape[2])


def _from_rows(a, nb):
    return jnp.swapaxes(a.reshape(a.shape[0] // nb, nb, a.shape[1]), 0, 1)


def kernel(x_prompt, x_sample, state_rglru_conv, state_rglru_h, state_s5_re, state_s5_im, norm_mix, norm_ffn, norm_f, w_ff1, w_ff2, w_in_a, sgu_g, w_s, b_s, w_out_a, w_in_b, conv_w, conv_b, w_a, b_a, w_x, b_x, lam, w_out_b, w_in_c, lam_re, lam_im, log_dt, b_re, b_im, c_re, c_im, d_skip, w_glu):
    bp, tp, _ = x_prompt.shape
    bs, ts, _ = x_sample.shape
    rows_p, rows_s = bp * tp, bs * ts
    assert DEPTH % N_MIXERS == 1 and DEPTH > 1, "first and last layers must be SGU layers"
    assert tp % CHUNK == 0 and ts < CHUNK
    assert rows_s % FFN_ROWS == 0 and rows_p % FFN_ROWS == 0

    row = lambda v: v.reshape(1, -1)
    norm_ffn3 = norm_ffn.reshape(DEPTH, 1, D_MODEL)
    w1, w2, gf = w_ff1, w_ff2, row(norm_f)

    x_p, x_s = x_prompt, _to_rows(x_sample)
    outs_v, conv_p, h_p, conv_s, h_s, sre_p, sim_p, sre_s, sim_s = [], [], [], [], [], [], [], [], []
    for layer in range(DEPTH):
        j, kind = layer // N_MIXERS, layer % N_MIXERS
        first, last = layer == 0, layer == DEPTH - 1
        g = row(norm_mix[layer])
        if kind == 0:
            p = dict(g=g, win=w_in_a[j].astype(BF16), wout=w_out_a[j].astype(BF16), sg=row(sgu_g[j]),
                     w_tril=jnp.where(jnp.tril(jnp.ones((CHUNK, CHUNK), dtype=bool)), w_s[j], 0.0).astype(BF16),
                     bias=jnp.repeat(b_s[j].T, HD_A, axis=1),
                     wl=jnp.repeat(w_s[j][:, :ts, :ts].reshape(HEADS_A, ts * ts).T, HD_A, axis=1),
                     bl=jnp.repeat(b_s[j][:, :ts].T, HD_A, axis=1))
            if first:
                x_p = _sgu_prompt(x_p, p, bp, tp, False, True)
            elif last:
                x_p = _sgu_prompt(x_p, p, bp, tp, True, False).reshape(rows_p, D_MODEL)
            else:
                raise NotImplementedError("interior SGU layers")
            x_s, v = _sgu_sample(x_s, p, bs, ts)
            outs_v.append(_from_rows(v, bs))
        elif kind == 1:
            p = dict(g=g, win=w_in_b[j].astype(BF16), cw=conv_w[j], cb=row(conv_b[j]),
                     wa=_block_diag(w_a[j], HEADS_PER_BLK).astype(BF16), ba=row(b_a[j]),
                     wx=_block_diag(w_x[j], HEADS_PER_BLK).astype(BF16), bx=row(b_x[j]),
                     lam=row(lam[j]), wout=w_out_b[j].astype(BF16))
            dt_s = state_rglru_h.dtype
            x_p, cp, hp = _rglru(x_p, p, jnp.zeros(((CONV_W - 1) * bp, D_MODEL), dt_s),
                                 jnp.zeros((bp, D_MODEL), dt_s), bp, RGLRU_PROMPT_STEPS)
            x_s, cs, hs = _rglru(x_s, p, _to_rows(state_rglru_conv[j]), state_rglru_h[j], bs, ts)
            conv_p.append(_from_rows(cp, bp)); h_p.append(hp)
            conv_s.append(_from_rows(cs, bs)); h_s.append(hs)
        else:
            are, aim, bbre, bbim, pw = _s5_disc(lam_re[j], lam_im[j], log_dt[j],
                                                jnp.swapaxes(b_re[j], 1, 2), jnp.swapaxes(b_im[j], 1, 2))
            p = dict(g=g, win=w_in_c[j].astype(BF16),
                     bre=_block_diag(bbre, GROUPS_PER_BLK).astype(BF16),
                     bim=_block_diag(bbim, GROUPS_PER_BLK).astype(BF16),
                     are=are.reshape(1, D_STATE), aim=aim.reshape(1, D_STATE),
                     cre=_block_diag(jnp.swapaxes(c_re[j], 1, 2), GROUPS_PER_BLK).astype(BF16),
                     cim=_block_diag(jnp.swapaxes(c_im[j], 1, 2), GROUPS_PER_BLK).astype(BF16),
                     dskip=row(d_skip[j]), wglu=w_glu[j].astype(BF16))
            pwg = _pair_lanes(jnp.transpose(pw, (2, 0, 1, 3)))
            kconv, wst, kin = _s5_taps(pwg[:, :, :S5_L], pwg[:, :, 1:], pwg[:, :, S5_L - 1::-1],
                                       _pair_lanes(jnp.stack([c_re[j], c_im[j]], axis=1)),
                                       _pair_lanes(jnp.stack([bbre, bbim], axis=1)))
            zs = jnp.zeros((bp, G_C, P_C), state_s5_re.dtype)
            u, u2 = _s5_in(x_p, g, p["win"], bp, 2 * S5_TILE_BLOCKS)
            y2, sf = _s5_core(u2, wst, kconv, kin, _pair_state(pw[0, S5_L], pw[1, S5_L]).reshape(1, -1),
                              _pair_state(zs, zs), bp)
            x_p = _s5_out(x_p, u, y2, p["dskip"], p["wglu"], bp, S5_TILE_BLOCKS)
            rp, ip = _unpair_state(sf, bp)
            x_s, rs, is_ = _s5(x_s, p, state_s5_re[j].reshape(bs, D_STATE), state_s5_im[j].reshape(bs, D_STATE),
                               bs, ts, V7X_LANES)
            sre_p.append(rp); sim_p.append(ip)
            sre_s.append(rs.reshape(bs, G_C, P_C)); sim_s.append(is_.reshape(bs, G_C, P_C))
        x_p, x_s = _ffn(x_p, x_s, norm_ffn3, w1, w2, gf, layer, last)

    y_prompt = x_p.reshape(bp, tp, D_MODEL)
    y_sample = _from_rows(x_s, bs)
    return (y_prompt, y_sample, jnp.stack(outs_v), jnp.stack(conv_p), jnp.stack(h_p), jnp.stack(conv_s),
            jnp.stack(h_s), jnp.stack(sre_p), jnp.stack(sim_p), jnp.stack(sre_s), jnp.stack(sim_s))
```

```python
import functools

import jax
import jax.numpy as jnp
from jax import lax
from jax.experimental import pallas as pl
from jax.experimental.pallas import tpu as pltpu

F32 = jnp.float32
BF16 = jnp.bfloat16

D_MODEL = 1024
DEPTH = 4
N_MIXERS = 3
EPS = 1e-6
CHUNK = 128
HEADS_A = 8
HD_A = D_MODEL // HEADS_A
HEADS_B = 16
HD_B = D_MODEL // HEADS_B
CONV_W = 4
LRU_C = 8.0
GROUP_C = 16
G_C = D_MODEL // GROUP_C
P_C = 64
D_STATE = G_C * P_C
D_FF = 4 * D_MODEL

V7X_LANES = 128
V7X_MXU_DIM = 256
V7X_VMEM_BYTES = 64 * 1024 * 1024
VMEM_LIMIT = V7X_VMEM_BYTES - 8 * 1024 * 1024

N_BLK = D_MODEL // V7X_MXU_DIM
HEADS_PER_BLK = V7X_MXU_DIM // HD_B
GROUPS_PER_BLK = V7X_MXU_DIM // GROUP_C
STATE_PER_BLK = GROUPS_PER_BLK * P_C

FFN_ROWS = 1024
FFN_SAMPLE_ROWS = 512
FFN_COL_CHUNK = 1024
FFN_STAGE_BYTES = 1024 * 1024
S5_L = 16
S5_TILE_BLOCKS = 4
S5_PAIRS_PER_STEP = 2
GRANULES_PER_VREG = V7X_LANES // GROUP_C
RGLRU_PROMPT_STEPS = 128


def _rms(x, g):
    return (x * lax.rsqrt(jnp.mean(x * x, axis=-1, keepdims=True) + EPS)) * g


GELU_C0 = 0.7978845608028654
GELU_C1 = GELU_C0 * 0.044715


def _gelu(x):
    return x * (0.5 + 0.5 * jnp.tanh(x * (GELU_C0 + GELU_C1 * (x * x))))


def _dot(a, b):
    return jnp.dot(a, b, preferred_element_type=F32)


def _const_spec(shape):
    zeros = (0,) * len(shape)
    return pl.BlockSpec(shape, lambda i: zeros, pipeline_mode=pl.Buffered(1))


def _layer_spec(shape, layer):
    idx = (layer,) + (0,) * len(shape)
    return pl.BlockSpec((None,) + tuple(shape), lambda i: idx, pipeline_mode=pl.Buffered(1))


def _params():
    return pltpu.CompilerParams(dimension_semantics=("arbitrary",), vmem_limit_bytes=VMEM_LIMIT)


def _ffn_tile(x_ref, g_ref, w1_ref, w2_ref, gf_ref, o_ref, final_norm):
    h = _rms(x_ref[...], g_ref[...]).astype(BF16)
    for j in range(D_FF // FFN_COL_CHUNK):
        cols = slice(j * FFN_COL_CHUNK, (j + 1) * FFN_COL_CHUNK)
        a = jnp.square(jnp.maximum(_dot(h, w1_ref[:, cols]), 0.0)).astype(BF16)
        part = _dot(a, w2_ref[cols, :])
        if j == 0:
            o_ref[...] = part
        else:
            o_ref[...] += part
    y = x_ref[...] + o_ref[...]
    if final_norm:
        y = _rms(y, gf_ref[...])
    o_ref[...] = y


def _load_as_bf16(jobs):
    def copies_of(src, stage, sem):
        rows = stage.shape[1]
        return [pltpu.make_async_copy(src.at[pl.ds(k * rows, rows)], stage.at[k % 2], sem.at[k % 2])
                for k in range(src.shape[0] // rows)]

    plans = [copies_of(src, stage, sem) for src, _, stage, sem in jobs]
    n = len(plans[0])
    assert all(len(p) == n for p in plans)
    for p in plans:
        p[0].start()
    for k in range(n):
        if k + 1 < n:
            for p in plans:
                p[k + 1].start()
        for p, (_, dst, stage, _) in zip(plans, jobs):
            p[k].wait()
            rows = stage.shape[1]
            dst[k * rows:(k + 1) * rows, :] = stage[k % 2].astype(BF16)


def _ffn_kernel(xp_ref, xs_ref, g_ref, w1_hbm, w2_hbm, gf_ref, op_ref, os_ref, w1_s, w2_s, stage1, stage2,
                sem1, sem2, *, final_norm, n_p, layer):
    i = pl.program_id(0)

    @pl.when(i == 0)
    def _():
        _load_as_bf16([(w1_hbm.at[layer], w1_s, stage1, sem1), (w2_hbm.at[layer], w2_s, stage2, sem2)])

    @pl.when(i < n_p)
    def _():
        _ffn_tile(xp_ref, g_ref, w1_s, w2_s, gf_ref, op_ref, final_norm)

    @pl.when(i >= n_p)
    def _():
        _ffn_tile(xs_ref, g_ref, w1_s, w2_s, gf_ref, os_ref, final_norm)


def _ffn(x_p, x_s, g, w1, w2, gf, layer, final_norm):
    n_p, n_s = x_p.shape[0] // FFN_ROWS, x_s.shape[0] // FFN_SAMPLE_ROWS
    p_spec = pl.BlockSpec((FFN_ROWS, D_MODEL), lambda i: (jnp.minimum(i, n_p - 1), 0))
    s_spec = pl.BlockSpec((FFN_SAMPLE_ROWS, D_MODEL), lambda i: (jnp.maximum(i - n_p, 0), 0))
    hbm = pl.BlockSpec(memory_space=pl.ANY)
    return pl.pallas_call(
        functools.partial(_ffn_kernel, final_norm=final_norm, n_p=n_p, layer=layer),
        grid=(n_p + n_s,),
        in_specs=[p_spec, s_spec, _layer_spec((1, D_MODEL), layer), hbm, hbm, _const_spec((1, D_MODEL))],
        out_specs=[p_spec, s_spec],
        out_shape=[jax.ShapeDtypeStruct(x_p.shape, F32), jax.ShapeDtypeStruct(x_s.shape, F32)],
        scratch_shapes=[pltpu.VMEM((D_MODEL, D_FF), BF16), pltpu.VMEM((D_FF, D_MODEL), BF16),
                        pltpu.VMEM((2, FFN_STAGE_BYTES // (4 * D_FF), D_FF), F32),
                        pltpu.VMEM((2, FFN_STAGE_BYTES // (4 * D_MODEL), D_MODEL), F32),
                        pltpu.SemaphoreType.DMA((2,)), pltpu.SemaphoreType.DMA((2,))],
        compiler_params=_params(),
        name="ffn",
    )(x_p, x_s, g, w1, w2, gf)


def _sgu_front(x, g_ref, win_ref, sg_ref):
    h = _rms(x, g_ref[...]).astype(BF16)
    uv = _gelu(_dot(h, win_ref[...]))
    return uv[:, :D_MODEL], _rms(uv[:, D_MODEL:], sg_ref[...])


def _rows_to_batch_major(x, nb):
    steps = x.shape[0] // nb
    return jnp.swapaxes(x.reshape(steps, nb, x.shape[1]), 0, 1).reshape(x.shape)


def _rows_to_time_major(x, nb):
    steps = x.shape[0] // nb
    return jnp.swapaxes(x.reshape(nb, steps, x.shape[1]), 0, 1).reshape(x.shape)


def _sgu_prompt_kernel(x_ref, g_ref, win_ref, sg_ref, w_ref, bias_ref, wout_ref, o_ref, y_s, *, nb, in_tm, out_tm):
    rows = nb * CHUNK
    if in_tm:
        x = _rows_to_batch_major(x_ref[...], nb)
    else:
        x = x_ref[...].reshape(rows, D_MODEL)
    u, v = _sgu_front(x, g_ref, win_ref, sg_ref)
    vb = v.astype(BF16)
    for b in range(nb):
        rs = slice(b * CHUNK, (b + 1) * CHUNK)
        for g in range(HEADS_A):
            cs = slice(g * HD_A, (g + 1) * HD_A)
            mixed = _dot(w_ref[g], vb[rs, cs]) + bias_ref[:, cs]
            y_s[rs, cs] = (u[rs, cs] * mixed).astype(BF16)
    o = x + _dot(y_s[...], wout_ref[...])
    if out_tm:
        o_ref[...] = _rows_to_time_major(o, nb)
    else:
        o_ref[...] = o.reshape(nb, CHUNK, D_MODEL)


def _sgu_sample_kernel(x_ref, g_ref, win_ref, sg_ref, wl_ref, bl_ref, wout_ref, o_ref, v_ref, *, nb, steps):
    x = x_ref[...]
    u, v = _sgu_front(x, g_ref, win_ref, sg_ref)
    v_ref[...] = v
    mixed = []
    for t in range(steps):
        m = None
        for s in range(t + 1):
            term = wl_ref[t * steps + s:t * steps + s + 1, :] * v[s * nb:(s + 1) * nb, :]
            m = term if m is None else m + term
        mixed.append(m + bl_ref[t:t + 1, :])
    y = (u * jnp.concatenate(mixed, axis=0)).astype(BF16)
    o_ref[...] = x + _dot(y, wout_ref[...])


def _sgu_prompt(x, p, nb, n_steps, in_tm, out_tm):
    rows = nb * CHUNK
    tm_spec = pl.BlockSpec((rows, D_MODEL), lambda i: (i, 0))
    bm_spec = pl.BlockSpec((nb, CHUNK, D_MODEL), lambda i: (0, i, 0))
    out_shape = (jax.ShapeDtypeStruct((nb * n_steps, D_MODEL), F32) if out_tm
                 else jax.ShapeDtypeStruct((nb, n_steps, D_MODEL), F32))
    return pl.pallas_call(
        functools.partial(_sgu_prompt_kernel, nb=nb, in_tm=in_tm, out_tm=out_tm),
        grid=(n_steps // CHUNK,),
        in_specs=[tm_spec if in_tm else bm_spec, _const_spec((1, D_MODEL)), _const_spec((D_MODEL, 2 * D_MODEL)),
                  _const_spec((1, D_MODEL)), _const_spec((HEADS_A, CHUNK, CHUNK)),
                  _const_spec((CHUNK, D_MODEL)), _const_spec((D_MODEL, D_MODEL))],
        out_specs=tm_spec if out_tm else bm_spec,
        out_shape=out_shape,
        scratch_shapes=[pltpu.VMEM((rows, D_MODEL), BF16)],
        compiler_params=_params(),
        name="sgu_prompt",
    )(x, p["g"], p["win"], p["sg"], p["w_tril"], p["bias"], p["wout"])


def _sgu_sample(x, p, nb, steps):
    tile = steps * nb
    tile_spec = pl.BlockSpec((tile, D_MODEL), lambda i: (0, 0))
    return pl.pallas_call(
        functools.partial(_sgu_sample_kernel, nb=nb, steps=steps),
        grid=(1,),
        in_specs=[tile_spec, _const_spec((1, D_MODEL)), _const_spec((D_MODEL, 2 * D_MODEL)),
                  _const_spec((1, D_MODEL)), _const_spec((steps * steps, D_MODEL)),
                  _const_spec((steps, D_MODEL)), _const_spec((D_MODEL, D_MODEL))],
        out_specs=[tile_spec, tile_spec],
        out_shape=[jax.ShapeDtypeStruct((tile, D_MODEL), F32), jax.ShapeDtypeStruct((tile, D_MODEL), F32)],
        compiler_params=_params(),
        name="sgu_sample",
    )(x, p["g"], p["win"], p["sg"], p["wl"], p["bl"], p["wout"])


def _rglru_kernel(x_ref, g_ref, win_ref, cw_ref, cb_ref, wa_ref, ba_ref, wx_ref, bx_ref, lam_ref, wout_ref,
                  conv0_ref, h0_ref, o_ref, conv_ref, h_ref, xext_s, a_s, b_s, y_s, *, nb, steps):
    rows = nb * steps
    halo = (CONV_W - 1) * nb

    @pl.when(pl.program_id(0) == 0)
    def _():
        conv_ref[...] = conv0_ref[...]
        h_ref[...] = h0_ref[...]

    def blk_cols(j):
        return slice(j * V7X_MXU_DIM, (j + 1) * V7X_MXU_DIM)

    x = x_ref[...]
    h = _rms(x, g_ref[...]).astype(BF16)
    for j in range(N_BLK):
        cols = blk_cols(j)
        xext, a_j, b_j = xext_s.at[j], a_s.at[j], b_s.at[j]
        xext[0:halo, :] = conv_ref[:, cols]
        xext[halo:halo + rows, :] = _dot(h, win_ref[:, blk_cols(N_BLK + j)])
        gate = _gelu(_dot(h, win_ref[:, cols]))
        conv = None
        for w in range(CONV_W):
            term = xext[w * nb:w * nb + rows, :] * cw_ref[w:w + 1, cols]
            conv = term if conv is None else conv + term
        conv_ref[:, cols] = xext[rows:rows + halo, :]
        xc = cb_ref[:, cols] + conv
        xcb = xc.astype(BF16)
        r = jax.nn.sigmoid(_dot(xcb, wa_ref[j]) + ba_ref[:, cols])
        ig = jax.nn.sigmoid(_dot(xcb, wx_ref[j]) + bx_ref[:, cols])
        neg_log_a = (LRU_C * r) * jax.nn.softplus(-lam_ref[:, cols])
        a = jnp.exp(-neg_log_a)
        a_j[...] = a
        w = jnp.tanh(neg_log_a) * (a * a + 1.0)
        mult = jnp.where(w > 0.0, w * lax.rsqrt(w), 0.0)
        b_j[...] = mult * (ig * xc)
        hcur = h_ref[:, cols]
        for t in range(steps):
            sl = slice(t * nb, (t + 1) * nb)
            hcur = a_j[sl, :] * hcur + b_j[sl, :]
            b_j[sl, :] = hcur
        h_ref[:, cols] = hcur
        y_s[:, cols] = (b_j[...] * gate).astype(BF16)
    o_ref[...] = x + _dot(y_s[...], wout_ref[...])


def _rglru(x, p, conv0, h0, nb, steps):
    rows = nb * steps
    halo = (CONV_W - 1) * nb
    row_spec = pl.BlockSpec((rows, D_MODEL), lambda i: (i, 0))
    vec = _const_spec((1, D_MODEL))
    blk = _const_spec((N_BLK, V7X_MXU_DIM, V7X_MXU_DIM))
    return pl.pallas_call(
        functools.partial(_rglru_kernel, nb=nb, steps=steps),
        grid=(x.shape[0] // rows,),
        in_specs=[row_spec, vec, _const_spec((D_MODEL, 2 * D_MODEL)), _const_spec((CONV_W, D_MODEL)), vec,
                  blk, vec, blk, vec, vec, _const_spec((D_MODEL, D_MODEL)),
                  _const_spec((halo, D_MODEL)), _const_spec((nb, D_MODEL))],
        out_specs=[row_spec, pl.BlockSpec((halo, D_MODEL), lambda i: (0, 0)),
                   pl.BlockSpec((nb, D_MODEL), lambda i: (0, 0))],
        out_shape=[jax.ShapeDtypeStruct(x.shape, F32), jax.ShapeDtypeStruct((halo, D_MODEL), F32),
                   jax.ShapeDtypeStruct((nb, D_MODEL), F32)],
        scratch_shapes=[pltpu.VMEM((N_BLK, rows + halo, V7X_MXU_DIM), F32), pltpu.VMEM((N_BLK, rows, V7X_MXU_DIM), F32),
                        pltpu.VMEM((N_BLK, rows, V7X_MXU_DIM), F32), pltpu.VMEM((rows, D_MODEL), BF16)],
        compiler_params=_params(),
        name="rglru",
    )(x, p["g"], p["win"], p["cw"], p["cb"], p["wa"], p["ba"], p["wx"], p["bx"], p["lam"], p["wout"], conv0, h0)


def _cmul(a_re, a_im, b_re, b_im):
    return a_re * b_re - a_im * b_im, a_re * b_im + a_im * b_re


def _s5_disc_kernel(lre_ref, lim_ref, ldt_ref, bre_ref, bim_ref, are_ref, aim_ref, bbre_ref, bbim_ref, pw_ref):
    lr, li = lre_ref[...], lim_ref[...]
    dt = jnp.exp(ldt_ref[...])
    mag = jnp.exp(lr * dt)
    ab_re, ab_im = mag * jnp.cos(li * dt), mag * jnp.sin(li * dt)
    zr, zi = ab_re - 1.0, ab_im
    den = lr * lr + li * li
    q_re = (zr * lr + zi * li) / den
    q_im = (zi * lr - zr * li) / den
    are_ref[...] = ab_re
    aim_ref[...] = ab_im
    br, bi = bre_ref[...], bim_ref[...]
    bbre_ref[...] = q_re[:, None, :] * br - q_im[:, None, :] * bi
    bbim_ref[...] = q_re[:, None, :] * bi + q_im[:, None, :] * br
    p_re, p_im = jnp.ones_like(ab_re), jnp.zeros_like(ab_re)
    for k in range(S5_L + 1):
        pw_ref[0, k] = p_re
        pw_ref[1, k] = p_im
        p_re, p_im = _cmul(p_re, p_im, ab_re, ab_im)


def _s5_disc(lam_re, lam_im, log_dt, b_re_t, b_im_t):
    gp = jax.ShapeDtypeStruct((G_C, P_C), F32)
    ghp = jax.ShapeDtypeStruct((G_C, GROUP_C, P_C), F32)
    pw = jax.ShapeDtypeStruct((2, S5_L + 1, G_C, P_C), F32)
    return pl.pallas_call(_s5_disc_kernel, out_shape=[gp, gp, ghp, ghp, pw], name="s5_disc")(
        lam_re, lam_im, log_dt.reshape(G_C, 1), b_re_t, b_im_t)


def _s5_taps_kernel(pw0_ref, pw1_ref, pwf_ref, c_ref, b_ref, kconv_ref, wst_ref, kin_ref, *, pairs):
    n = S5_L * GROUP_C
    pair_lanes = 2 * P_C

    def rep(t):
        return jnp.broadcast_to(t[:, None, :], (S5_L, GROUP_C, pair_lanes)).reshape(n, pair_lanes)

    def til(c):
        return jnp.broadcast_to(c[None, :, :], (S5_L, GROUP_C, pair_lanes)).reshape(n, pair_lanes)

    def dot_t(a, b):
        return lax.dot_general(a, b, (((1,), (1,)), ((), ())), precision=lax.Precision.HIGHEST,
                               preferred_element_type=F32)

    def first(shape, axis):
        return lax.broadcasted_iota(jnp.int32, shape, axis) < P_C

    lane = lax.broadcasted_iota(jnp.int32, (GROUP_C, n), 1)
    for q in range(pairs):
        cr, ci = til(c_ref[q, 0]), til(c_ref[q, 1])
        b_re, b_im = b_ref[q, 0], b_ref[q, 1]
        x_re, x_im = _cmul(rep(pw0_ref[q, 0]), rep(pw0_ref[q, 1]), cr, ci)
        for gi in range(2):
            mine = first(b_re.shape, 1) == (gi == 0)
            mrow = (dot_t(jnp.where(mine, b_re, 0.0), x_re)
                    - dot_t(jnp.where(mine, b_im, 0.0), x_im))
            for s in range(S5_L):
                blk = mrow if s == 0 else jnp.where(lane >= s * GROUP_C, pltpu.roll(mrow, s * GROUP_C, 1), 0.0)
                kconv_ref[2 * q + gi, s * GROUP_C:(s + 1) * GROUP_C, :] = blk.astype(BF16)
        e_re, e_im = _cmul(rep(pw1_ref[q, 0]), rep(pw1_ref[q, 1]), cr, ci)
        et_re, et_im = e_re.T, (-e_im).T
        top = first(et_re.shape, 0)
        kin_ref[q] = jnp.concatenate(
            [jnp.concatenate([jnp.where(top, et_re, 0.0), jnp.where(top, 0.0, et_re)], axis=1),
             jnp.concatenate([jnp.where(top, et_im, 0.0), jnp.where(top, 0.0, et_im)], axis=1)], axis=0).astype(BF16)
        f_re, f_im = _cmul(rep(pwf_ref[q, 0]), rep(pwf_ref[q, 1]), til(b_re), til(b_im))
        left = first(f_re.shape, 1)
        wst_ref[q] = jnp.concatenate(
            [jnp.concatenate([jnp.where(left, f_re, 0.0), jnp.where(left, f_im, 0.0)], axis=1),
             jnp.concatenate([jnp.where(left, 0.0, f_re), jnp.where(left, 0.0, f_im)], axis=1)], axis=0).astype(BF16)


def _s5_taps(pw0, pw1, pwf, c, bb):
    pairs = 4
    n = S5_L * GROUP_C
    pspec = pl.BlockSpec((pairs, 2, S5_L, 2 * P_C), lambda i: (i, 0, 0, 0))
    hspec = pl.BlockSpec((pairs, 2, GROUP_C, 2 * P_C), lambda i: (i, 0, 0, 0))
    return pl.pallas_call(
        functools.partial(_s5_taps_kernel, pairs=pairs),
        grid=(G_C // (2 * pairs),),
        in_specs=[pspec, pspec, pspec, hspec, hspec],
        out_specs=[pl.BlockSpec((2 * pairs, n, n), lambda i: (i, 0, 0)),
                   pl.BlockSpec((pairs, 2 * n, 4 * P_C), lambda i: (i, 0, 0)),
                   pl.BlockSpec((pairs, 4 * P_C, 2 * n), lambda i: (i, 0, 0))],
        out_shape=[jax.ShapeDtypeStruct((G_C, n, n), BF16), jax.ShapeDtypeStruct((G_C // 2, 2 * n, 4 * P_C), BF16),
                   jax.ShapeDtypeStruct((G_C // 2, 4 * P_C, 2 * n), BF16)],
        compiler_params=_params(),
        name="s5_taps",
    )(pw0, pw1, pwf, c, bb)


def _pair_lanes(a):
    a2 = a.reshape((G_C // 2, 2) + a.shape[1:])
    return jnp.concatenate([a2[:, 0], a2[:, 1]], axis=-1)


def _granule_transpose(vs):
    vs = list(vs)
    granule = lax.broadcasted_iota(jnp.int32, vs[0].shape, 1) // GROUP_C
    d = 1
    while d < len(vs):
        keep = (granule & d) == 0
        for i in range(len(vs)):
            if i & d == 0:
                a, b = vs[i], vs[i + d]
                vs[i] = jnp.where(keep, a, pltpu.roll(b, d * GROUP_C, 1))
                vs[i + d] = jnp.where(keep, pltpu.roll(a, V7X_LANES - d * GROUP_C, 1), b)
        d *= 2
    return vs


def _s5_in_kernel(x_ref, g_ref, win_ref, u_ref, u2_ref, *, nb, blocks):
    h = _rms(x_ref[...], g_ref[...]).astype(BF16)
    rows2 = blocks * nb
    for cc in range(N_BLK):
        cols = slice(cc * V7X_MXU_DIM, (cc + 1) * V7X_MXU_DIM)
        u = _dot(h, win_ref[:, cols])
        u_ref[:, cols] = u
        u4 = u.reshape(blocks, S5_L, nb, V7X_MXU_DIM)
        for sub in range(V7X_MXU_DIM // V7X_LANES):
            c = cc * (V7X_MXU_DIM // V7X_LANES) + sub
            for half in range(S5_L // GRANULES_PER_VREG):
                vs = []
                for kk in range(GRANULES_PER_VREG):
                    slab = u4[:, half * GRANULES_PER_VREG + kk, :, sub * V7X_LANES:(sub + 1) * V7X_LANES]
                    vs.append(slab.reshape(rows2, V7X_LANES))
                ws = _granule_transpose(vs)
                for gi in range(GRANULES_PER_VREG):
                    lane0 = (c * GRANULES_PER_VREG + gi) * S5_L * GROUP_C + half * V7X_LANES
                    u2_ref[:, lane0:lane0 + V7X_LANES] = ws[gi].astype(BF16)


def _s5_in(x, g, win, nb, blocks):
    rows = blocks * S5_L * nb
    n_blocks = x.shape[0] // (S5_L * nb)
    row_spec = pl.BlockSpec((rows, D_MODEL), lambda i: (i, 0))
    return pl.pallas_call(
        functools.partial(_s5_in_kernel, nb=nb, blocks=blocks),
        grid=(x.shape[0] // rows,),
        in_specs=[row_spec, _const_spec((1, D_MODEL)), _const_spec((D_MODEL, D_MODEL))],
        out_specs=[row_spec, pl.BlockSpec((blocks * nb, S5_L * D_MODEL), lambda i: (i, 0))],
        out_shape=[jax.ShapeDtypeStruct(x.shape, F32),
                   jax.ShapeDtypeStruct((n_blocks * nb, S5_L * D_MODEL), BF16)],
        compiler_params=_params(),
        name="s5_in",
    )(x, g, win)


def _s5_core_kernel(u_ref, wst_ref, kconv_ref, kin_ref, a_ref, s0_ref, y_ref, sf_ref, sc_s, sin_s, *, nb, n_blocks):
    pw = 2 * S5_L * GROUP_C
    sw = 2 * P_C
    ub = [u_ref[:, q * pw:(q + 1) * pw] for q in range(S5_PAIRS_PER_STEP)]
    for q in range(S5_PAIRS_PER_STEP):
        sc_s[q] = _dot(ub[q], wst_ref[q])
    a_re = [jnp.broadcast_to(a_ref[:, 2 * sw * q:2 * sw * q + sw], (nb, sw)) for q in range(S5_PAIRS_PER_STEP)]
    a_im = [jnp.broadcast_to(a_ref[:, 2 * sw * q + sw:2 * sw * (q + 1)], (nb, sw)) for q in range(S5_PAIRS_PER_STEP)]
    s_re = [s0_ref[:, 2 * sw * q:2 * sw * q + sw] for q in range(S5_PAIRS_PER_STEP)]
    s_im = [s0_ref[:, 2 * sw * q + sw:2 * sw * (q + 1)] for q in range(S5_PAIRS_PER_STEP)]
    for blk in range(n_blocks):
        rs = slice(blk * nb, (blk + 1) * nb)
        for q in range(S5_PAIRS_PER_STEP):
            sin_s[q, rs, 0:sw] = s_re[q]
            sin_s[q, rs, sw:2 * sw] = s_im[q]
            n_re, n_im = _cmul(a_re[q], a_im[q], s_re[q], s_im[q])
            s_re[q], s_im[q] = n_re + sc_s[q, rs, 0:sw], n_im + sc_s[q, rs, sw:2 * sw]
    for q in range(S5_PAIRS_PER_STEP):
        sf_ref[:, 2 * sw * q:2 * sw * q + sw] = s_re[q]
        sf_ref[:, 2 * sw * q + sw:2 * sw * (q + 1)] = s_im[q]
        half = S5_L * GROUP_C
        conv = jnp.concatenate([_dot(ub[q][:, :half], kconv_ref[2 * q]),
                                _dot(ub[q][:, half:], kconv_ref[2 * q + 1])], axis=1)
        y_ref[:, q * pw:(q + 1) * pw] = conv + _dot(sin_s[q].astype(BF16), kin_ref[q])


def _s5_core(u2, wst, kconv, kin, a_pair, s0, nb):
    rows = u2.shape[0]
    pw = 2 * S5_L * GROUP_C
    sw = 2 * P_C
    cols = S5_PAIRS_PER_STEP * pw
    scols = S5_PAIRS_PER_STEP * 2 * sw
    return pl.pallas_call(
        functools.partial(_s5_core_kernel, nb=nb, n_blocks=rows // nb),
        grid=(u2.shape[1] // cols,),
        in_specs=[pl.BlockSpec((rows, cols), lambda i: (0, i)),
                  pl.BlockSpec((S5_PAIRS_PER_STEP, pw, 2 * sw), lambda i: (i, 0, 0)),
                  pl.BlockSpec((2 * S5_PAIRS_PER_STEP, pw // 2, pw // 2), lambda i: (i, 0, 0)),
                  pl.BlockSpec((S5_PAIRS_PER_STEP, 2 * sw, pw), lambda i: (i, 0, 0)),
                  pl.BlockSpec((1, scols), lambda i: (0, i)), pl.BlockSpec((nb, scols), lambda i: (0, i))],
        out_specs=[pl.BlockSpec((rows, cols), lambda i: (0, i)), pl.BlockSpec((nb, scols), lambda i: (0, i))],
        out_shape=[jax.ShapeDtypeStruct(u2.shape, F32), jax.ShapeDtypeStruct((nb, 2 * D_STATE), F32)],
        scratch_shapes=[pltpu.VMEM((S5_PAIRS_PER_STEP, rows, 2 * sw), F32),
                        pltpu.VMEM((S5_PAIRS_PER_STEP, rows, 2 * sw), F32)],
        compiler_params=_params(),
        name="s5_core",
    )(u2, wst, kconv, kin, a_pair, s0)


def _s5_out_kernel(x_ref, u_ref, y2_ref, dskip_ref, wglu_ref, o_ref, y_s, *, nb, blocks):
    rows = blocks * S5_L * nb
    o = None
    for cc in range(N_BLK):
        cols = slice(cc * V7X_MXU_DIM, (cc + 1) * V7X_MXU_DIM)
        for sub in range(V7X_MXU_DIM // V7X_LANES):
            c = cc * (V7X_MXU_DIM // V7X_LANES) + sub
            for half in range(S5_L // GRANULES_PER_VREG):
                lanes = [(c * GRANULES_PER_VREG + gi) * S5_L * GROUP_C + half * V7X_LANES
                         for gi in range(GRANULES_PER_VREG)]
                ws = _granule_transpose([y2_ref[:, l0:l0 + V7X_LANES] for l0 in lanes])
                for kk in range(GRANULES_PER_VREG):
                    y_s[cc, :, half * GRANULES_PER_VREG + kk, :, sub * V7X_LANES:(sub + 1) * V7X_LANES] = (
                        ws[kk].reshape(blocks, nb, V7X_LANES))
        y = y_s[cc].reshape(rows, V7X_MXU_DIM) + dskip_ref[:, cols] * u_ref[:, cols]
        part = _dot(_gelu(y).astype(BF16), wglu_ref[cols, :])
        o = part if o is None else o + part
    o_ref[...] = x_ref[...] + o[:, :D_MODEL] * jax.nn.sigmoid(o[:, D_MODEL:])


def _s5_out(x, u, y2, dskip, wglu, nb, blocks):
    rows = blocks * S5_L * nb
    row_spec = pl.BlockSpec((rows, D_MODEL), lambda i: (i, 0))
    return pl.pallas_call(
        functools.partial(_s5_out_kernel, nb=nb, blocks=blocks),
        grid=(x.shape[0] // rows,),
        in_specs=[row_spec, row_spec, pl.BlockSpec((blocks * nb, S5_L * D_MODEL), lambda i: (i, 0)),
                  _const_spec((1, D_MODEL)), _const_spec((D_MODEL, 2 * D_MODEL))],
        out_specs=row_spec,
        out_shape=jax.ShapeDtypeStruct(x.shape, F32),
        scratch_shapes=[pltpu.VMEM((N_BLK, blocks, S5_L, nb, V7X_MXU_DIM), F32)],
        compiler_params=_params(),
        name="s5_out",
    )(x, u, y2, dskip, wglu)


def _pair_state(re, im):
    lead = re.shape[:-2]
    r = re.reshape(lead + (G_C // 2, 2 * P_C))
    i = im.reshape(lead + (G_C // 2, 2 * P_C))
    return jnp.concatenate([r, i], axis=-1).reshape(lead + (2 * D_STATE,))


def _unpair_state(s, nb):
    s4 = s.reshape(nb, G_C // 2, 2, 2 * P_C)
    return s4[:, :, 0].reshape(nb, G_C, P_C), s4[:, :, 1].reshape(nb, G_C, P_C)


def _s5_kernel(x_ref, g_ref, win_ref, bre_ref, bim_ref, are_ref, aim_ref, cre_ref, cim_ref, dskip_ref, wglu_ref,
               sre0_ref, sim0_ref, o_ref, sre_ref, sim_ref, xre_s, xim_s, *, nb, steps, lane_chunk):
    @pl.when(pl.program_id(0) == 0)
    def _():
        sre_ref[...] = sre0_ref[...]
        sim_ref[...] = sim0_ref[...]

    x = x_ref[...]
    h = _rms(x, g_ref[...]).astype(BF16)
    u = _dot(h, win_ref[...])
    ub = u.astype(BF16)

    def expand(j):
        ucols = slice(j * V7X_MXU_DIM, (j + 1) * V7X_MXU_DIM)
        scols = slice(j * STATE_PER_BLK, (j + 1) * STATE_PER_BLK)
        xre_s[:, scols] = _dot(ub[:, ucols], bre_ref[j])
        xim_s[:, scols] = _dot(ub[:, ucols], bim_ref[j])

    def scan(j):
        for c in range(STATE_PER_BLK // lane_chunk):
            cols = slice(j * STATE_PER_BLK + c * lane_chunk, j * STATE_PER_BLK + (c + 1) * lane_chunk)
            a_re = jnp.broadcast_to(are_ref[:, cols], (nb, lane_chunk))
            a_im = jnp.broadcast_to(aim_ref[:, cols], (nb, lane_chunk))
            s_re, s_im = sre_ref[:, cols], sim_ref[:, cols]
            for t in range(steps):
                sl = slice(t * nb, (t + 1) * nb)
                s_re, s_im = ((a_re * s_re - a_im * s_im) + xre_s[sl, cols],
                              (a_re * s_im + a_im * s_re) + xim_s[sl, cols])
                xre_s[sl, cols] = s_re
                xim_s[sl, cols] = s_im
            sre_ref[:, cols] = s_re
            sim_ref[:, cols] = s_im

    ys = []
    expand(0)
    for j in range(N_BLK):
        if j + 1 < N_BLK:
            expand(j + 1)
        scan(j)
        scols = slice(j * STATE_PER_BLK, (j + 1) * STATE_PER_BLK)
        ys.append(_dot(xre_s[:, scols].astype(BF16), cre_ref[j]) - _dot(xim_s[:, scols].astype(BF16), cim_ref[j]))
    y = jnp.concatenate(ys, axis=1) + dskip_ref[...] * u
    o = _dot(_gelu(y).astype(BF16), wglu_ref[...])
    o_ref[...] = x + o[:, :D_MODEL] * jax.nn.sigmoid(o[:, D_MODEL:])


def _s5(x, p, sre0, sim0, nb, steps, lane_chunk):
    rows = nb * steps
    row_spec = pl.BlockSpec((rows, D_MODEL), lambda i: (i, 0))
    vec = _const_spec((1, D_MODEL))
    svec = _const_spec((1, D_STATE))
    state_spec = _const_spec((nb, D_STATE))
    bspec = _const_spec((N_BLK, V7X_MXU_DIM, STATE_PER_BLK))
    cspec = _const_spec((N_BLK, STATE_PER_BLK, V7X_MXU_DIM))
    state_out = pl.BlockSpec((nb, D_STATE), lambda i: (0, 0))
    return pl.pallas_call(
        functools.partial(_s5_kernel, nb=nb, steps=steps, lane_chunk=lane_chunk),
        grid=(x.shape[0] // rows,),
        in_specs=[row_spec, vec, _const_spec((D_MODEL, D_MODEL)), bspec, bspec, svec, svec, cspec, cspec, vec,
                  _const_spec((D_MODEL, 2 * D_MODEL)), state_spec, state_spec],
        out_specs=[row_spec, state_out, state_out],
        out_shape=[jax.ShapeDtypeStruct(x.shape, F32), jax.ShapeDtypeStruct((nb, D_STATE), F32),
                   jax.ShapeDtypeStruct((nb, D_STATE), F32)],
        scratch_shapes=[pltpu.VMEM((rows, D_STATE), F32), pltpu.VMEM((rows, D_STATE), F32)],
        compiler_params=_params(),
        name="s5",
    )(x, p["g"], p["win"], p["bre"], p["bim"], p["are"], p["aim"], p["cre"], p["cim"], p["dskip"], p["wglu"],
      sre0, sim0)


def _block_diag(w, n_per_blk):
    n, k_in, k_out = w.shape
    wb = w.reshape(n // n_per_blk, n_per_blk, k_in, k_out)
    eye = jnp.eye(n_per_blk, dtype=w.dtype)
    out = jnp.einsum("jgio,gk->jgiko", wb, eye)
    return out.reshape(n // n_per_blk, n_per_blk * k_in, n_per_blk * k_out)


def _to_rows(a):
    return jnp.swapaxes(a, 0, 1).reshape(a.shape[0] * a.shape[1], a.shape[2])


def _from_rows(a, nb):
    return jnp.swapaxes(a.reshape(a.shape[0] // nb, nb, a.shape[1]), 0, 1)


def kernel(x_prompt, x_sample, state_rglru_conv, state_rglru_h, state_s5_re, state_s5_im, norm_mix, norm_ffn, norm_f, w_ff1, w_ff2, w_in_a, sgu_g, w_s, b_s, w_out_a, w_in_b, conv_w, conv_b, w_a, b_a, w_x, b_x, lam, w_out_b, w_in_c, lam_re, lam_im, log_dt, b_re, b_im, c_re, c_im, d_skip, w_glu):
    bp, tp, _ = x_prompt.shape
    bs, ts, _ = x_sample.shape
    rows_p, rows_s = bp * tp, bs * ts
    assert DEPTH % N_MIXERS == 1 and DEPTH > 1, "first and last layers must be SGU layers"
    assert tp % CHUNK == 0 and ts < CHUNK
    assert rows_s % FFN_SAMPLE_ROWS == 0 and rows_p % FFN_ROWS == 0

    row = lambda v: v.reshape(1, -1)
    norm_ffn3 = norm_ffn.reshape(DEPTH, 1, D_MODEL)
    w1, w2, gf = w_ff1, w_ff2, row(norm_f)

    x_p, x_s = x_prompt, _to_rows(x_sample)
    outs_v, conv_p, h_p, conv_s, h_s, sre_p, sim_p, sre_s, sim_s = [], [], [], [], [], [], [], [], []
    for layer in range(DEPTH):
        j, kind = layer // N_MIXERS, layer % N_MIXERS
        first, last = layer == 0, layer == DEPTH - 1
        g = row(norm_mix[layer])
        if kind == 0:
            p = dict(g=g, win=w_in_a[j].astype(BF16), wout=w_out_a[j].astype(BF16), sg=row(sgu_g[j]),
                     w_tril=jnp.where(jnp.tril(jnp.ones((CHUNK, CHUNK), dtype=bool)), w_s[j], 0.0).astype(BF16),
                     bias=jnp.repeat(b_s[j].T, HD_A, axis=1),
                     wl=jnp.repeat(w_s[j][:, :ts, :ts].reshape(HEADS_A, ts * ts).T, HD_A, axis=1),
                     bl=jnp.repeat(b_s[j][:, :ts].T, HD_A, axis=1))
            if first:
                x_p = _sgu_prompt(x_p, p, bp, tp, False, True)
            elif last:
                x_p = _sgu_prompt(x_p, p, bp, tp, True, False).reshape(rows_p, D_MODEL)
            else:
                raise NotImplementedError("interior SGU layers")
            x_s, v = _sgu_sample(x_s, p, bs, ts)
            outs_v.append(_from_rows(v, bs))
        elif kind == 1:
            p = dict(g=g, win=w_in_b[j].astype(BF16), cw=conv_w[j], cb=row(conv_b[j]),
                     wa=_block_diag(w_a[j], HEADS_PER_BLK).astype(BF16), ba=row(b_a[j]),
                     wx=_block_diag(w_x[j], HEADS_PER_BLK).astype(BF16), bx=row(b_x[j]),
                     lam=row(lam[j]), wout=w_out_b[j].astype(BF16))
            dt_s = state_rglru_h.dtype
            x_p, cp, hp = _rglru(x_p, p, jnp.zeros(((CONV_W - 1) * bp, D_MODEL), dt_s),
                                 jnp.zeros((bp, D_MODEL), dt_s), bp, RGLRU_PROMPT_STEPS)
            x_s, cs, hs = _rglru(x_s, p, _to_rows(state_rglru_conv[j]), state_rglru_h[j], bs, ts)
            conv_p.append(_from_rows(cp, bp)); h_p.append(hp)
            conv_s.append(_from_rows(cs, bs)); h_s.append(hs)
        else:
            are, aim, bbre, bbim, pw = _s5_disc(lam_re[j], lam_im[j], log_dt[j],
                                                jnp.swapaxes(b_re[j], 1, 2), jnp.swapaxes(b_im[j], 1, 2))
            p = dict(g=g, win=w_in_c[j].astype(BF16),
                     bre=_block_diag(bbre, GROUPS_PER_BLK).astype(BF16),
                     bim=_block_diag(bbim, GROUPS_PER_BLK).astype(BF16),
                     are=are.reshape(1, D_STATE), aim=aim.reshape(1, D_STATE),
                     cre=_block_diag(jnp.swapaxes(c_re[j], 1, 2), GROUPS_PER_BLK).astype(BF16),
                     cim=_block_diag(jnp.swapaxes(c_im[j], 1, 2), GROUPS_PER_BLK).astype(BF16),
                     dskip=row(d_skip[j]), wglu=w_glu[j].astype(BF16))
            pwg = _pair_lanes(jnp.transpose(pw, (2, 0, 1, 3)))
            kconv, wst, kin = _s5_taps(pwg[:, :, :S5_L], pwg[:, :, 1:], pwg[:, :, S5_L - 1::-1],
                                       _pair_lanes(jnp.stack([c_re[j], c_im[j]], axis=1)),
                                       _pair_lanes(jnp.stack([bbre, bbim], axis=1)))
            zs = jnp.zeros((bp, G_C, P_C), state_s5_re.dtype)
            u, u2 = _s5_in(x_p, g, p["win"], bp, 2 * S5_TILE_BLOCKS)
            y2, sf = _s5_core(u2, wst, kconv, kin, _pair_state(pw[0, S5_L], pw[1, S5_L]).reshape(1, -1),
                              _pair_state(zs, zs), bp)
            x_p = _s5_out(x_p, u, y2, p["dskip"], p["wglu"], bp, S5_TILE_BLOCKS)
            rp, ip = _unpair_state(sf, bp)
            x_s, rs, is_ = _s5(x_s, p, state_s5_re[j].reshape(bs, D_STATE), state_s5_im[j].reshape(bs, D_STATE),
                               bs, ts, V7X_LANES)
            sre_p.append(rp); sim_p.append(ip)
            sre_s.append(rs.reshape(bs, G_C, P_C)); sim_s.append(is_.reshape(bs, G_C, P_C))
        x_p, x_s = _ffn(x_p, x_s, norm_ffn3, w1, w2, gf, layer, last)

    y_prompt = x_p.reshape(bp, tp, D_MODEL)
    y_sample = _from_rows(x_s, bs)
    return (y_prompt, y_sample, jnp.stack(outs_v), jnp.stack(conv_p), jnp.stack(h_p), jnp.stack(conv_s),
            jnp.stack(h_s), jnp.stack(sre_p), jnp.stack(sim_p), jnp.stack(sre_s), jnp.stack(sim_s))
```

```python
import functools

import jax
import jax.numpy as jnp
from jax import lax
from jax.experimental import pallas as pl
from jax.experimental.pallas import tpu as pltpu

F32 = jnp.float32
BF16 = jnp.bfloat16

D_MODEL = 1024
DEPTH = 4
N_MIXERS = 3
EPS = 1e-6
CHUNK = 128
HEADS_A = 8
HD_A = D_MODEL // HEADS_A
HEADS_B = 16
HD_B = D_MODEL // HEADS_B
CONV_W = 4
LRU_C = 8.0
GROUP_C = 16
G_C = D_MODEL // GROUP_C
P_C = 64
D_STATE = G_C * P_C
D_FF = 4 * D_MODEL

V7X_LANES = 128
V7X_MXU_DIM = 256
V7X_VMEM_BYTES = 64 * 1024 * 1024
VMEM_LIMIT = V7X_VMEM_BYTES - 8 * 1024 * 1024

N_BLK = D_MODEL // V7X_MXU_DIM
HEADS_PER_BLK = V7X_MXU_DIM // HD_B
GROUPS_PER_BLK = V7X_MXU_DIM // GROUP_C
STATE_PER_BLK = GROUPS_PER_BLK * P_C

FFN_ROWS = 1024
FFN_SAMPLE_ROWS = 512
FFN_COL_CHUNK = 1024
FFN_STAGE_BYTES = 1024 * 1024
S5_L = 16
S5_TILE_BLOCKS = 8
S5_PAIRS_PER_STEP = 2
GRANULES_PER_VREG = V7X_LANES // GROUP_C
RGLRU_PROMPT_STEPS = 128


def _rms(x, g):
    return (x * lax.rsqrt(jnp.mean(x * x, axis=-1, keepdims=True) + EPS)) * g


GELU_C0 = 0.7978845608028654
GELU_C1 = GELU_C0 * 0.044715


def _gelu(x):
    return x * (0.5 + 0.5 * jnp.tanh(x * (GELU_C0 + GELU_C1 * (x * x))))


def _dot(a, b):
    return jnp.dot(a, b, preferred_element_type=F32)


def _const_spec(shape):
    zeros = (0,) * len(shape)
    return pl.BlockSpec(shape, lambda i: zeros, pipeline_mode=pl.Buffered(1))


def _layer_spec(shape, layer):
    idx = (layer,) + (0,) * len(shape)
    return pl.BlockSpec((None,) + tuple(shape), lambda i: idx, pipeline_mode=pl.Buffered(1))


def _params():
    return pltpu.CompilerParams(dimension_semantics=("arbitrary",), vmem_limit_bytes=VMEM_LIMIT)


def _ffn_tile(x_ref, g_ref, w1_ref, w2_ref, gf_ref, o_ref, final_norm):
    h = _rms(x_ref[...], g_ref[...]).astype(BF16)
    for j in range(D_FF // FFN_COL_CHUNK):
        cols = slice(j * FFN_COL_CHUNK, (j + 1) * FFN_COL_CHUNK)
        a = jnp.square(jnp.maximum(_dot(h, w1_ref[:, cols]), 0.0)).astype(BF16)
        part = _dot(a, w2_ref[cols, :])
        if j == 0:
            o_ref[...] = part
        else:
            o_ref[...] += part
    y = x_ref[...] + o_ref[...]
    if final_norm:
        y = _rms(y, gf_ref[...])
    o_ref[...] = y


def _load_as_bf16(jobs):
    def copies_of(src, stage, sem):
        rows = stage.shape[1]
        return [pltpu.make_async_copy(src.at[pl.ds(k * rows, rows)], stage.at[k % 2], sem.at[k % 2])
                for k in range(src.shape[0] // rows)]

    plans = [copies_of(src, stage, sem) for src, _, stage, sem in jobs]
    n = len(plans[0])
    assert all(len(p) == n for p in plans)
    for p in plans:
        p[0].start()
    for k in range(n):
        if k + 1 < n:
            for p in plans:
                p[k + 1].start()
        for p, (_, dst, stage, _) in zip(plans, jobs):
            p[k].wait()
            rows = stage.shape[1]
            dst[k * rows:(k + 1) * rows, :] = stage[k % 2].astype(BF16)


def _ffn_kernel(xp_ref, xs_ref, g_ref, w1_hbm, w2_hbm, gf_ref, op_ref, os_ref, w1_s, w2_s, stage1, stage2,
                sem1, sem2, *, final_norm, n_p, layer):
    i = pl.program_id(0)

    @pl.when(i == 0)
    def _():
        _load_as_bf16([(w1_hbm.at[layer], w1_s, stage1, sem1), (w2_hbm.at[layer], w2_s, stage2, sem2)])

    @pl.when(i < n_p)
    def _():
        _ffn_tile(xp_ref, g_ref, w1_s, w2_s, gf_ref, op_ref, final_norm)

    @pl.when(i >= n_p)
    def _():
        _ffn_tile(xs_ref, g_ref, w1_s, w2_s, gf_ref, os_ref, final_norm)


def _ffn(x_p, x_s, g, w1, w2, gf, layer, final_norm):
    n_p, n_s = x_p.shape[0] // FFN_ROWS, x_s.shape[0] // FFN_SAMPLE_ROWS
    p_spec = pl.BlockSpec((FFN_ROWS, D_MODEL), lambda i: (jnp.minimum(i, n_p - 1), 0))
    s_spec = pl.BlockSpec((FFN_SAMPLE_ROWS, D_MODEL), lambda i: (jnp.maximum(i - n_p, 0), 0))
    hbm = pl.BlockSpec(memory_space=pl.ANY)
    return pl.pallas_call(
        functools.partial(_ffn_kernel, final_norm=final_norm, n_p=n_p, layer=layer),
        grid=(n_p + n_s,),
        in_specs=[p_spec, s_spec, _layer_spec((1, D_MODEL), layer), hbm, hbm, _const_spec((1, D_MODEL))],
        out_specs=[p_spec, s_spec],
        out_shape=[jax.ShapeDtypeStruct(x_p.shape, F32), jax.ShapeDtypeStruct(x_s.shape, F32)],
        scratch_shapes=[pltpu.VMEM((D_MODEL, D_FF), BF16), pltpu.VMEM((D_FF, D_MODEL), BF16),
                        pltpu.VMEM((2, FFN_STAGE_BYTES // (4 * D_FF), D_FF), F32),
                        pltpu.VMEM((2, FFN_STAGE_BYTES // (4 * D_MODEL), D_MODEL), F32),
                        pltpu.SemaphoreType.DMA((2,)), pltpu.SemaphoreType.DMA((2,))],
        compiler_params=_params(),
        name="ffn",
    )(x_p, x_s, g, w1, w2, gf)


def _sgu_front(x, g_ref, win_ref, sg_ref):
    h = _rms(x, g_ref[...]).astype(BF16)
    uv = _gelu(_dot(h, win_ref[...]))
    return uv[:, :D_MODEL], _rms(uv[:, D_MODEL:], sg_ref[...])


def _rows_to_batch_major(x, nb):
    steps = x.shape[0] // nb
    return jnp.swapaxes(x.reshape(steps, nb, x.shape[1]), 0, 1).reshape(x.shape)


def _rows_to_time_major(x, nb):
    steps = x.shape[0] // nb
    return jnp.swapaxes(x.reshape(nb, steps, x.shape[1]), 0, 1).reshape(x.shape)


def _sgu_prompt_kernel(x_ref, g_ref, win_ref, sg_ref, w_ref, bias_ref, wout_ref, o_ref, y_s, *, nb, in_tm, out_tm):
    rows = nb * CHUNK
    if in_tm:
        x = _rows_to_batch_major(x_ref[...], nb)
    else:
        x = x_ref[...].reshape(rows, D_MODEL)
    u, v = _sgu_front(x, g_ref, win_ref, sg_ref)
    vb = v.astype(BF16)
    for b in range(nb):
        rs = slice(b * CHUNK, (b + 1) * CHUNK)
        for g in range(HEADS_A):
            cs = slice(g * HD_A, (g + 1) * HD_A)
            mixed = _dot(w_ref[g], vb[rs, cs]) + bias_ref[:, cs]
            y_s[rs, cs] = (u[rs, cs] * mixed).astype(BF16)
    o = x + _dot(y_s[...], wout_ref[...])
    if out_tm:
        o_ref[...] = _rows_to_time_major(o, nb)
    else:
        o_ref[...] = o.reshape(nb, CHUNK, D_MODEL)


def _sgu_sample_kernel(x_ref, g_ref, win_ref, sg_ref, wl_ref, bl_ref, wout_ref, o_ref, v_ref, *, nb, steps):
    x = x_ref[...]
    u, v = _sgu_front(x, g_ref, win_ref, sg_ref)
    v_ref[...] = v
    mixed = []
    for t in range(steps):
        m = None
        for s in range(t + 1):
            term = wl_ref[t * steps + s:t * steps + s + 1, :] * v[s * nb:(s + 1) * nb, :]
            m = term if m is None else m + term
        mixed.append(m + bl_ref[t:t + 1, :])
    y = (u * jnp.concatenate(mixed, axis=0)).astype(BF16)
    o_ref[...] = x + _dot(y, wout_ref[...])


def _sgu_prompt(x, p, nb, n_steps, in_tm, out_tm):
    rows = nb * CHUNK
    tm_spec = pl.BlockSpec((rows, D_MODEL), lambda i: (i, 0))
    bm_spec = pl.BlockSpec((nb, CHUNK, D_MODEL), lambda i: (0, i, 0))
    out_shape = (jax.ShapeDtypeStruct((nb * n_steps, D_MODEL), F32) if out_tm
                 else jax.ShapeDtypeStruct((nb, n_steps, D_MODEL), F32))
    return pl.pallas_call(
        functools.partial(_sgu_prompt_kernel, nb=nb, in_tm=in_tm, out_tm=out_tm),
        grid=(n_steps // CHUNK,),
        in_specs=[tm_spec if in_tm else bm_spec, _const_spec((1, D_MODEL)), _const_spec((D_MODEL, 2 * D_MODEL)),
                  _const_spec((1, D_MODEL)), _const_spec((HEADS_A, CHUNK, CHUNK)),
                  _const_spec((CHUNK, D_MODEL)), _const_spec((D_MODEL, D_MODEL))],
        out_specs=tm_spec if out_tm else bm_spec,
        out_shape=out_shape,
        scratch_shapes=[pltpu.VMEM((rows, D_MODEL), BF16)],
        compiler_params=_params(),
        name="sgu_prompt",
    )(x, p["g"], p["win"], p["sg"], p["w_tril"], p["bias"], p["wout"])


def _sgu_sample(x, p, nb, steps):
    tile = steps * nb
    tile_spec = pl.BlockSpec((tile, D_MODEL), lambda i: (0, 0))
    return pl.pallas_call(
        functools.partial(_sgu_sample_kernel, nb=nb, steps=steps),
        grid=(1,),
        in_specs=[tile_spec, _const_spec((1, D_MODEL)), _const_spec((D_MODEL, 2 * D_MODEL)),
                  _const_spec((1, D_MODEL)), _const_spec((steps * steps, D_MODEL)),
                  _const_spec((steps, D_MODEL)), _const_spec((D_MODEL, D_MODEL))],
        out_specs=[tile_spec, tile_spec],
        out_shape=[jax.ShapeDtypeStruct((tile, D_MODEL), F32), jax.ShapeDtypeStruct((tile, D_MODEL), F32)],
        compiler_params=_params(),
        name="sgu_sample",
    )(x, p["g"], p["win"], p["sg"], p["wl"], p["bl"], p["wout"])


def _rglru_kernel(x_ref, g_ref, win_ref, cw_ref, cb_ref, wa_ref, ba_ref, wx_ref, bx_ref, lam_ref, wout_ref,
                  conv0_ref, h0_ref, o_ref, conv_ref, h_ref, xext_s, a_s, b_s, y_s, *, nb, steps):
    rows = nb * steps
    halo = (CONV_W - 1) * nb

    @pl.when(pl.program_id(0) == 0)
    def _():
        conv_ref[...] = conv0_ref[...]
        h_ref[...] = h0_ref[...]

    def blk_cols(j):
        return slice(j * V7X_MXU_DIM, (j + 1) * V7X_MXU_DIM)

    x = x_ref[...]
    h = _rms(x, g_ref[...]).astype(BF16)
    for j in range(N_BLK):
        cols = blk_cols(j)
        xext, a_j, b_j = xext_s.at[j], a_s.at[j], b_s.at[j]
        xext[0:halo, :] = conv_ref[:, cols]
        xext[halo:halo + rows, :] = _dot(h, win_ref[:, blk_cols(N_BLK + j)])
        gate = _gelu(_dot(h, win_ref[:, cols]))
        conv = None
        for w in range(CONV_W):
            term = xext[w * nb:w * nb + rows, :] * cw_ref[w:w + 1, cols]
            conv = term if conv is None else conv + term
        conv_ref[:, cols] = xext[rows:rows + halo, :]
        xc = cb_ref[:, cols] + conv
        xcb = xc.astype(BF16)
        r = jax.nn.sigmoid(_dot(xcb, wa_ref[j]) + ba_ref[:, cols])
        ig = jax.nn.sigmoid(_dot(xcb, wx_ref[j]) + bx_ref[:, cols])
        neg_log_a = (LRU_C * r) * jax.nn.softplus(-lam_ref[:, cols])
        a = jnp.exp(-neg_log_a)
        a_j[...] = a
        w = jnp.tanh(neg_log_a) * (a * a + 1.0)
        mult = jnp.where(w > 0.0, w * lax.rsqrt(w), 0.0)
        b_j[...] = mult * (ig * xc)
        hcur = h_ref[:, cols]
        for t in range(steps):
            sl = slice(t * nb, (t + 1) * nb)
            hcur = a_j[sl, :] * hcur + b_j[sl, :]
            b_j[sl, :] = hcur
        h_ref[:, cols] = hcur
        y_s[:, cols] = (b_j[...] * gate).astype(BF16)
    o_ref[...] = x + _dot(y_s[...], wout_ref[...])


def _rglru(x, p, conv0, h0, nb, steps):
    rows = nb * steps
    halo = (CONV_W - 1) * nb
    row_spec = pl.BlockSpec((rows, D_MODEL), lambda i: (i, 0))
    vec = _const_spec((1, D_MODEL))
    blk = _const_spec((N_BLK, V7X_MXU_DIM, V7X_MXU_DIM))
    return pl.pallas_call(
        functools.partial(_rglru_kernel, nb=nb, steps=steps),
        grid=(x.shape[0] // rows,),
        in_specs=[row_spec, vec, _const_spec((D_MODEL, 2 * D_MODEL)), _const_spec((CONV_W, D_MODEL)), vec,
                  blk, vec, blk, vec, vec, _const_spec((D_MODEL, D_MODEL)),
                  _const_spec((halo, D_MODEL)), _const_spec((nb, D_MODEL))],
        out_specs=[row_spec, pl.BlockSpec((halo, D_MODEL), lambda i: (0, 0)),
                   pl.BlockSpec((nb, D_MODEL), lambda i: (0, 0))],
        out_shape=[jax.ShapeDtypeStruct(x.shape, F32), jax.ShapeDtypeStruct((halo, D_MODEL), F32),
                   jax.ShapeDtypeStruct((nb, D_MODEL), F32)],
        scratch_shapes=[pltpu.VMEM((N_BLK, rows + halo, V7X_MXU_DIM), F32), pltpu.VMEM((N_BLK, rows, V7X_MXU_DIM), F32),
                        pltpu.VMEM((N_BLK, rows, V7X_MXU_DIM), F32), pltpu.VMEM((rows, D_MODEL), BF16)],
        compiler_params=_params(),
        name="rglru",
    )(x, p["g"], p["win"], p["cw"], p["cb"], p["wa"], p["ba"], p["wx"], p["bx"], p["lam"], p["wout"], conv0, h0)


def _cmul(a_re, a_im, b_re, b_im):
    return a_re * b_re - a_im * b_im, a_re * b_im + a_im * b_re


def _s5_disc_kernel(lre_ref, lim_ref, ldt_ref, bre_ref, bim_ref, are_ref, aim_ref, bbre_ref, bbim_ref, pw_ref):
    lr, li = lre_ref[...], lim_ref[...]
    dt = jnp.exp(ldt_ref[...])
    mag = jnp.exp(lr * dt)
    ab_re, ab_im = mag * jnp.cos(li * dt), mag * jnp.sin(li * dt)
    zr, zi = ab_re - 1.0, ab_im
    den = lr * lr + li * li
    q_re = (zr * lr + zi * li) / den
    q_im = (zi * lr - zr * li) / den
    are_ref[...] = ab_re
    aim_ref[...] = ab_im
    br, bi = bre_ref[...], bim_ref[...]
    bbre_ref[...] = q_re[:, None, :] * br - q_im[:, None, :] * bi
    bbim_ref[...] = q_re[:, None, :] * bi + q_im[:, None, :] * br
    p_re, p_im = jnp.ones_like(ab_re), jnp.zeros_like(ab_re)
    for k in range(S5_L + 1):
        pw_ref[0, k] = p_re
        pw_ref[1, k] = p_im
        p_re, p_im = _cmul(p_re, p_im, ab_re, ab_im)


def _s5_disc(lam_re, lam_im, log_dt, b_re_t, b_im_t):
    gp = jax.ShapeDtypeStruct((G_C, P_C), F32)
    ghp = jax.ShapeDtypeStruct((G_C, GROUP_C, P_C), F32)
    pw = jax.ShapeDtypeStruct((2, S5_L + 1, G_C, P_C), F32)
    return pl.pallas_call(_s5_disc_kernel, out_shape=[gp, gp, ghp, ghp, pw], name="s5_disc")(
        lam_re, lam_im, log_dt.reshape(G_C, 1), b_re_t, b_im_t)


def _s5_taps_kernel(pw0_ref, pw1_ref, pwf_ref, c_ref, b_ref, kconv_ref, wst_ref, kin_ref, *, pairs):
    n = S5_L * GROUP_C
    pair_lanes = 2 * P_C

    def rep(t):
        return jnp.broadcast_to(t[:, None, :], (S5_L, GROUP_C, pair_lanes)).reshape(n, pair_lanes)

    def til(c):
        return jnp.broadcast_to(c[None, :, :], (S5_L, GROUP_C, pair_lanes)).reshape(n, pair_lanes)

    def dot_t(a, b):
        return lax.dot_general(a, b, (((1,), (1,)), ((), ())), precision=lax.Precision.HIGHEST,
                               preferred_element_type=F32)

    def first(shape, axis):
        return lax.broadcasted_iota(jnp.int32, shape, axis) < P_C

    lane = lax.broadcasted_iota(jnp.int32, (GROUP_C, n), 1)
    for q in range(pairs):
        cr, ci = til(c_ref[q, 0]), til(c_ref[q, 1])
        b_re, b_im = b_ref[q, 0], b_ref[q, 1]
        x_re, x_im = _cmul(rep(pw0_ref[q, 0]), rep(pw0_ref[q, 1]), cr, ci)
        for gi in range(2):
            mine = first(b_re.shape, 1) == (gi == 0)
            mrow = (dot_t(jnp.where(mine, b_re, 0.0), x_re)
                    - dot_t(jnp.where(mine, b_im, 0.0), x_im))
            for s in range(S5_L):
                blk = mrow if s == 0 else jnp.where(lane >= s * GROUP_C, pltpu.roll(mrow, s * GROUP_C, 1), 0.0)
                kconv_ref[2 * q + gi, s * GROUP_C:(s + 1) * GROUP_C, :] = blk.astype(BF16)
        e_re, e_im = _cmul(rep(pw1_ref[q, 0]), rep(pw1_ref[q, 1]), cr, ci)
        et_re, et_im = e_re.T, (-e_im).T
        top = first(et_re.shape, 0)
        kin_ref[q] = jnp.concatenate(
            [jnp.concatenate([jnp.where(top, et_re, 0.0), jnp.where(top, 0.0, et_re)], axis=1),
             jnp.concatenate([jnp.where(top, et_im, 0.0), jnp.where(top, 0.0, et_im)], axis=1)], axis=0).astype(BF16)
        f_re, f_im = _cmul(rep(pwf_ref[q, 0]), rep(pwf_ref[q, 1]), til(b_re), til(b_im))
        left = first(f_re.shape, 1)
        wst_ref[q] = jnp.concatenate(
            [jnp.concatenate([jnp.where(left, f_re, 0.0), jnp.where(left, f_im, 0.0)], axis=1),
             jnp.concatenate([jnp.where(left, 0.0, f_re), jnp.where(left, 0.0, f_im)], axis=1)], axis=0).astype(BF16)


def _s5_taps(pw0, pw1, pwf, c, bb):
    pairs = 4
    n = S5_L * GROUP_C
    pspec = pl.BlockSpec((pairs, 2, S5_L, 2 * P_C), lambda i: (i, 0, 0, 0))
    hspec = pl.BlockSpec((pairs, 2, GROUP_C, 2 * P_C), lambda i: (i, 0, 0, 0))
    return pl.pallas_call(
        functools.partial(_s5_taps_kernel, pairs=pairs),
        grid=(G_C // (2 * pairs),),
        in_specs=[pspec, pspec, pspec, hspec, hspec],
        out_specs=[pl.BlockSpec((2 * pairs, n, n), lambda i: (i, 0, 0)),
                   pl.BlockSpec((pairs, 2 * n, 4 * P_C), lambda i: (i, 0, 0)),
                   pl.BlockSpec((pairs, 4 * P_C, 2 * n), lambda i: (i, 0, 0))],
        out_shape=[jax.ShapeDtypeStruct((G_C, n, n), BF16), jax.ShapeDtypeStruct((G_C // 2, 2 * n, 4 * P_C), BF16),
                   jax.ShapeDtypeStruct((G_C // 2, 4 * P_C, 2 * n), BF16)],
        compiler_params=_params(),
        name="s5_taps",
    )(pw0, pw1, pwf, c, bb)


def _pair_lanes(a):
    a2 = a.reshape((G_C // 2, 2) + a.shape[1:])
    return jnp.concatenate([a2[:, 0], a2[:, 1]], axis=-1)


def _granule_transpose(vs):
    vs = list(vs)
    granule = lax.broadcasted_iota(jnp.int32, vs[0].shape, 1) // GROUP_C
    d = 1
    while d < len(vs):
        keep = (granule & d) == 0
        for i in range(len(vs)):
            if i & d == 0:
                a, b = vs[i], vs[i + d]
                vs[i] = jnp.where(keep, a, pltpu.roll(b, d * GROUP_C, 1))
                vs[i + d] = jnp.where(keep, pltpu.roll(a, V7X_LANES - d * GROUP_C, 1), b)
        d *= 2
    return vs


def _s5_in_kernel(x_ref, g_ref, win_ref, u_ref, u2_ref, *, nb, blocks):
    h = _rms(x_ref[...], g_ref[...]).astype(BF16)
    rows2 = blocks * nb
    for cc in range(N_BLK):
        cols = slice(cc * V7X_MXU_DIM, (cc + 1) * V7X_MXU_DIM)
        u = _dot(h, win_ref[:, cols])
        u_ref[:, cols] = u
        u4 = u.reshape(blocks, S5_L, nb, V7X_MXU_DIM)
        for sub in range(V7X_MXU_DIM // V7X_LANES):
            c = cc * (V7X_MXU_DIM // V7X_LANES) + sub
            for half in range(S5_L // GRANULES_PER_VREG):
                vs = []
                for kk in range(GRANULES_PER_VREG):
                    slab = u4[:, half * GRANULES_PER_VREG + kk, :, sub * V7X_LANES:(sub + 1) * V7X_LANES]
                    vs.append(slab.reshape(rows2, V7X_LANES))
                ws = _granule_transpose(vs)
                for gi in range(GRANULES_PER_VREG):
                    lane0 = (c * GRANULES_PER_VREG + gi) * S5_L * GROUP_C + half * V7X_LANES
                    u2_ref[:, lane0:lane0 + V7X_LANES] = ws[gi].astype(BF16)


def _s5_in(x, g, win, nb, blocks):
    rows = blocks * S5_L * nb
    n_blocks = x.shape[0] // (S5_L * nb)
    row_spec = pl.BlockSpec((rows, D_MODEL), lambda i: (i, 0))
    return pl.pallas_call(
        functools.partial(_s5_in_kernel, nb=nb, blocks=blocks),
        grid=(x.shape[0] // rows,),
        in_specs=[row_spec, _const_spec((1, D_MODEL)), _const_spec((D_MODEL, D_MODEL))],
        out_specs=[row_spec, pl.BlockSpec((blocks * nb, S5_L * D_MODEL), lambda i: (i, 0))],
        out_shape=[jax.ShapeDtypeStruct(x.shape, F32),
                   jax.ShapeDtypeStruct((n_blocks * nb, S5_L * D_MODEL), BF16)],
        compiler_params=_params(),
        name="s5_in",
    )(x, g, win)


def _s5_core_kernel(u_ref, wst_ref, kconv_ref, kin_ref, a_ref, s0_ref, y_ref, sf_ref, sc_s, sin_s, *, nb, n_blocks):
    pw = 2 * S5_L * GROUP_C
    sw = 2 * P_C
    ub = [u_ref[:, q * pw:(q + 1) * pw] for q in range(S5_PAIRS_PER_STEP)]
    for q in range(S5_PAIRS_PER_STEP):
        sc_s[q] = _dot(ub[q], wst_ref[q])
    a_re = [jnp.broadcast_to(a_ref[:, 2 * sw * q:2 * sw * q + sw], (nb, sw)) for q in range(S5_PAIRS_PER_STEP)]
    a_im = [jnp.broadcast_to(a_ref[:, 2 * sw * q + sw:2 * sw * (q + 1)], (nb, sw)) for q in range(S5_PAIRS_PER_STEP)]
    s_re = [s0_ref[:, 2 * sw * q:2 * sw * q + sw] for q in range(S5_PAIRS_PER_STEP)]
    s_im = [s0_ref[:, 2 * sw * q + sw:2 * sw * (q + 1)] for q in range(S5_PAIRS_PER_STEP)]
    for blk in range(n_blocks):
        rs = slice(blk * nb, (blk + 1) * nb)
        for q in range(S5_PAIRS_PER_STEP):
            sin_s[q, rs, 0:sw] = s_re[q]
            sin_s[q, rs, sw:2 * sw] = s_im[q]
            n_re, n_im = _cmul(a_re[q], a_im[q], s_re[q], s_im[q])
            s_re[q], s_im[q] = n_re + sc_s[q, rs, 0:sw], n_im + sc_s[q, rs, sw:2 * sw]
    for q in range(S5_PAIRS_PER_STEP):
        sf_ref[:, 2 * sw * q:2 * sw * q + sw] = s_re[q]
        sf_ref[:, 2 * sw * q + sw:2 * sw * (q + 1)] = s_im[q]
        half = S5_L * GROUP_C
        conv = jnp.concatenate([_dot(ub[q][:, :half], kconv_ref[2 * q]),
                                _dot(ub[q][:, half:], kconv_ref[2 * q + 1])], axis=1)
        y_ref[:, q * pw:(q + 1) * pw] = conv + _dot(sin_s[q].astype(BF16), kin_ref[q])


def _s5_core(u2, wst, kconv, kin, a_pair, s0, nb):
    rows = u2.shape[0]
    pw = 2 * S5_L * GROUP_C
    sw = 2 * P_C
    cols = S5_PAIRS_PER_STEP * pw
    scols = S5_PAIRS_PER_STEP * 2 * sw
    return pl.pallas_call(
        functools.partial(_s5_core_kernel, nb=nb, n_blocks=rows // nb),
        grid=(u2.shape[1] // cols,),
        in_specs=[pl.BlockSpec((rows, cols), lambda i: (0, i)),
                  pl.BlockSpec((S5_PAIRS_PER_STEP, pw, 2 * sw), lambda i: (i, 0, 0)),
                  pl.BlockSpec((2 * S5_PAIRS_PER_STEP, pw // 2, pw // 2), lambda i: (i, 0, 0)),
                  pl.BlockSpec((S5_PAIRS_PER_STEP, 2 * sw, pw), lambda i: (i, 0, 0)),
                  pl.BlockSpec((1, scols), lambda i: (0, i)), pl.BlockSpec((nb, scols), lambda i: (0, i))],
        out_specs=[pl.BlockSpec((rows, cols), lambda i: (0, i)), pl.BlockSpec((nb, scols), lambda i: (0, i))],
        out_shape=[jax.ShapeDtypeStruct(u2.shape, F32), jax.ShapeDtypeStruct((nb, 2 * D_STATE), F32)],
        scratch_shapes=[pltpu.VMEM((S5_PAIRS_PER_STEP, rows, 2 * sw), F32),
                        pltpu.VMEM((S5_PAIRS_PER_STEP, rows, 2 * sw), F32)],
        compiler_params=_params(),
        name="s5_core",
    )(u2, wst, kconv, kin, a_pair, s0)


def _s5_out_kernel(x_ref, u_ref, y2_ref, dskip_ref, wglu_ref, o_ref, y_s, gate_s, *, nb, blocks):
    rows = blocks * S5_L * nb
    for cc in range(N_BLK):
        cols = slice(cc * V7X_MXU_DIM, (cc + 1) * V7X_MXU_DIM)
        for sub in range(V7X_MXU_DIM // V7X_LANES):
            c = cc * (V7X_MXU_DIM // V7X_LANES) + sub
            for half in range(S5_L // GRANULES_PER_VREG):
                lanes = [(c * GRANULES_PER_VREG + gi) * S5_L * GROUP_C + half * V7X_LANES
                         for gi in range(GRANULES_PER_VREG)]
                ws = _granule_transpose([y2_ref[:, l0:l0 + V7X_LANES] for l0 in lanes])
                for kk in range(GRANULES_PER_VREG):
                    y_s[cc, :, half * GRANULES_PER_VREG + kk, :, sub * V7X_LANES:(sub + 1) * V7X_LANES] = (
                        ws[kk].reshape(blocks, nb, V7X_LANES))
        y = y_s[cc].reshape(rows, V7X_MXU_DIM) + dskip_ref[:, cols] * u_ref[:, cols]
        gy = _gelu(y).astype(BF16)
        for acc, wcols in ((o_ref, slice(0, D_MODEL)), (gate_s, slice(D_MODEL, 2 * D_MODEL))):
            part = _dot(gy, wglu_ref[cols, wcols])
            if cc == 0:
                acc[...] = part
            else:
                acc[...] += part
    o_ref[...] = x_ref[...] + o_ref[...] * jax.nn.sigmoid(gate_s[...])


def _s5_out(x, u, y2, dskip, wglu, nb, blocks):
    rows = blocks * S5_L * nb
    row_spec = pl.BlockSpec((rows, D_MODEL), lambda i: (i, 0))
    return pl.pallas_call(
        functools.partial(_s5_out_kernel, nb=nb, blocks=blocks),
        grid=(x.shape[0] // rows,),
        in_specs=[row_spec, row_spec, pl.BlockSpec((blocks * nb, S5_L * D_MODEL), lambda i: (i, 0)),
                  _const_spec((1, D_MODEL)), _const_spec((D_MODEL, 2 * D_MODEL))],
        out_specs=row_spec,
        out_shape=jax.ShapeDtypeStruct(x.shape, F32),
        scratch_shapes=[pltpu.VMEM((N_BLK, blocks, S5_L, nb, V7X_MXU_DIM), F32), pltpu.VMEM((rows, D_MODEL), F32)],
        compiler_params=_params(),
        name="s5_out",
    )(x, u, y2, dskip, wglu)


def _pair_state(re, im):
    lead = re.shape[:-2]
    r = re.reshape(lead + (G_C // 2, 2 * P_C))
    i = im.reshape(lead + (G_C // 2, 2 * P_C))
    return jnp.concatenate([r, i], axis=-1).reshape(lead + (2 * D_STATE,))


def _unpair_state(s, nb):
    s4 = s.reshape(nb, G_C // 2, 2, 2 * P_C)
    return s4[:, :, 0].reshape(nb, G_C, P_C), s4[:, :, 1].reshape(nb, G_C, P_C)


def _s5_kernel(x_ref, g_ref, win_ref, bre_ref, bim_ref, are_ref, aim_ref, cre_ref, cim_ref, dskip_ref, wglu_ref,
               sre0_ref, sim0_ref, o_ref, sre_ref, sim_ref, xre_s, xim_s, *, nb, steps, lane_chunk):
    @pl.when(pl.program_id(0) == 0)
    def _():
        sre_ref[...] = sre0_ref[...]
        sim_ref[...] = sim0_ref[...]

    x = x_ref[...]
    h = _rms(x, g_ref[...]).astype(BF16)
    u = _dot(h, win_ref[...])
    ub = u.astype(BF16)

    def expand(j):
        ucols = slice(j * V7X_MXU_DIM, (j + 1) * V7X_MXU_DIM)
        scols = slice(j * STATE_PER_BLK, (j + 1) * STATE_PER_BLK)
        xre_s[:, scols] = _dot(ub[:, ucols], bre_ref[j])
        xim_s[:, scols] = _dot(ub[:, ucols], bim_ref[j])

    def scan(j):
        for c in range(STATE_PER_BLK // lane_chunk):
            cols = slice(j * STATE_PER_BLK + c * lane_chunk, j * STATE_PER_BLK + (c + 1) * lane_chunk)
            a_re = jnp.broadcast_to(are_ref[:, cols], (nb, lane_chunk))
            a_im = jnp.broadcast_to(aim_ref[:, cols], (nb, lane_chunk))
            s_re, s_im = sre_ref[:, cols], sim_ref[:, cols]
            for t in range(steps):
                sl = slice(t * nb, (t + 1) * nb)
                s_re, s_im = ((a_re * s_re - a_im * s_im) + xre_s[sl, cols],
                              (a_re * s_im + a_im * s_re) + xim_s[sl, cols])
                xre_s[sl, cols] = s_re
                xim_s[sl, cols] = s_im
            sre_ref[:, cols] = s_re
            sim_ref[:, cols] = s_im

    ys = []
    expand(0)
    for j in range(N_BLK):
        if j + 1 < N_BLK:
            expand(j + 1)
        scan(j)
        scols = slice(j * STATE_PER_BLK, (j + 1) * STATE_PER_BLK)
        ys.append(_dot(xre_s[:, scols].astype(BF16), cre_ref[j]) - _dot(xim_s[:, scols].astype(BF16), cim_ref[j]))
    y = jnp.concatenate(ys, axis=1) + dskip_ref[...] * u
    o = _dot(_gelu(y).astype(BF16), wglu_ref[...])
    o_ref[...] = x + o[:, :D_MODEL] * jax.nn.sigmoid(o[:, D_MODEL:])


def _s5(x, p, sre0, sim0, nb, steps, lane_chunk):
    rows = nb * steps
    row_spec = pl.BlockSpec((rows, D_MODEL), lambda i: (i, 0))
    vec = _const_spec((1, D_MODEL))
    svec = _const_spec((1, D_STATE))
    state_spec = _const_spec((nb, D_STATE))
    bspec = _const_spec((N_BLK, V7X_MXU_DIM, STATE_PER_BLK))
    cspec = _const_spec((N_BLK, STATE_PER_BLK, V7X_MXU_DIM))
    state_out = pl.BlockSpec((nb, D_STATE), lambda i: (0, 0))
    return pl.pallas_call(
        functools.partial(_s5_kernel, nb=nb, steps=steps, lane_chunk=lane_chunk),
        grid=(x.shape[0] // rows,),
        in_specs=[row_spec, vec, _const_spec((D_MODEL, D_MODEL)), bspec, bspec, svec, svec, cspec, cspec, vec,
                  _const_spec((D_MODEL, 2 * D_MODEL)), state_spec, state_spec],
        out_specs=[row_spec, state_out, state_out],
        out_shape=[jax.ShapeDtypeStruct(x.shape, F32), jax.ShapeDtypeStruct((nb, D_STATE), F32),
                   jax.ShapeDtypeStruct((nb, D_STATE), F32)],
        scratch_shapes=[pltpu.VMEM((rows, D_STATE), F32), pltpu.VMEM((rows, D_STATE), F32)],
        compiler_params=_params(),
        name="s5",
    )(x, p["g"], p["win"], p["bre"], p["bim"], p["are"], p["aim"], p["cre"], p["cim"], p["dskip"], p["wglu"],
      sre0, sim0)


def _block_diag(w, n_per_blk):
    n, k_in, k_out = w.shape
    wb = w.reshape(n // n_per_blk, n_per_blk, k_in, k_out)
    eye = jnp.eye(n_per_blk, dtype=w.dtype)
    out = jnp.einsum("jgio,gk->jgiko", wb, eye)
    return out.reshape(n // n_per_blk, n_per_blk * k_in, n_per_blk * k_out)


def _to_rows(a):
    return jnp.swapaxes(a, 0, 1).reshape(a.shape[0] * a.shape[1], a.shape[2])


def _from_rows(a, nb):
    return jnp.swapaxes(a.reshape(a.shape[0] // nb, nb, a.shape[1]), 0, 1)


def kernel(x_prompt, x_sample, state_rglru_conv, state_rglru_h, state_s5_re, state_s5_im, norm_mix, norm_ffn, norm_f, w_ff1, w_ff2, w_in_a, sgu_g, w_s, b_s, w_out_a, w_in_b, conv_w, conv_b, w_a, b_a, w_x, b_x, lam, w_out_b, w_in_c, lam_re, lam_im, log_dt, b_re, b_im, c_re, c_im, d_skip, w_glu):
    bp, tp, _ = x_prompt.shape
    bs, ts, _ = x_sample.shape
    rows_p, rows_s = bp * tp, bs * ts
    assert DEPTH % N_MIXERS == 1 and DEPTH > 1, "first and last layers must be SGU layers"
    assert tp % CHUNK == 0 and ts < CHUNK
    assert rows_s % FFN_SAMPLE_ROWS == 0 and rows_p % FFN_ROWS == 0

    row = lambda v: v.reshape(1, -1)
    norm_ffn3 = norm_ffn.reshape(DEPTH, 1, D_MODEL)
    w1, w2, gf = w_ff1, w_ff2, row(norm_f)

    x_p, x_s = x_prompt, _to_rows(x_sample)
    outs_v, conv_p, h_p, conv_s, h_s, sre_p, sim_p, sre_s, sim_s = [], [], [], [], [], [], [], [], []
    for layer in range(DEPTH):
        j, kind = layer // N_MIXERS, layer % N_MIXERS
        first, last = layer == 0, layer == DEPTH - 1
        g = row(norm_mix[layer])
        if kind == 0:
            p = dict(g=g, win=w_in_a[j].astype(BF16), wout=w_out_a[j].astype(BF16), sg=row(sgu_g[j]),
                     w_tril=jnp.where(jnp.tril(jnp.ones((CHUNK, CHUNK), dtype=bool)), w_s[j], 0.0).astype(BF16),
                     bias=jnp.repeat(b_s[j].T, HD_A, axis=1),
                     wl=jnp.repeat(w_s[j][:, :ts, :ts].reshape(HEADS_A, ts * ts).T, HD_A, axis=1),
                     bl=jnp.repeat(b_s[j][:, :ts].T, HD_A, axis=1))
            if first:
                x_p = _sgu_prompt(x_p, p, bp, tp, False, True)
            elif last:
                x_p = _sgu_prompt(x_p, p, bp, tp, True, False).reshape(rows_p, D_MODEL)
            else:
                raise NotImplementedError("interior SGU layers")
            x_s, v = _sgu_sample(x_s, p, bs, ts)
            outs_v.append(_from_rows(v, bs))
        elif kind == 1:
            p = dict(g=g, win=w_in_b[j].astype(BF16), cw=conv_w[j], cb=row(conv_b[j]),
                     wa=_block_diag(w_a[j], HEADS_PER_BLK).astype(BF16), ba=row(b_a[j]),
                     wx=_block_diag(w_x[j], HEADS_PER_BLK).astype(BF16), bx=row(b_x[j]),
                     lam=row(lam[j]), wout=w_out_b[j].astype(BF16))
            dt_s = state_rglru_h.dtype
            x_p, cp, hp = _rglru(x_p, p, jnp.zeros(((CONV_W - 1) * bp, D_MODEL), dt_s),
                                 jnp.zeros((bp, D_MODEL), dt_s), bp, RGLRU_PROMPT_STEPS)
            x_s, cs, hs = _rglru(x_s, p, _to_rows(state_rglru_conv[j]), state_rglru_h[j], bs, ts)
            conv_p.append(_from_rows(cp, bp)); h_p.append(hp)
            conv_s.append(_from_rows(cs, bs)); h_s.append(hs)
        else:
            are, aim, bbre, bbim, pw = _s5_disc(lam_re[j], lam_im[j], log_dt[j],
                                                jnp.swapaxes(b_re[j], 1, 2), jnp.swapaxes(b_im[j], 1, 2))
            p = dict(g=g, win=w_in_c[j].astype(BF16),
                     bre=_block_diag(bbre, GROUPS_PER_BLK).astype(BF16),
                     bim=_block_diag(bbim, GROUPS_PER_BLK).astype(BF16),
                     are=are.reshape(1, D_STATE), aim=aim.reshape(1, D_STATE),
                     cre=_block_diag(jnp.swapaxes(c_re[j], 1, 2), GROUPS_PER_BLK).astype(BF16),
                     cim=_block_diag(jnp.swapaxes(c_im[j], 1, 2), GROUPS_PER_BLK).astype(BF16),
                     dskip=row(d_skip[j]), wglu=w_glu[j].astype(BF16))
            pwg = _pair_lanes(jnp.transpose(pw, (2, 0, 1, 3)))
            kconv, wst, kin = _s5_taps(pwg[:, :, :S5_L], pwg[:, :, 1:], pwg[:, :, S5_L - 1::-1],
                                       _pair_lanes(jnp.stack([c_re[j], c_im[j]], axis=1)),
                                       _pair_lanes(jnp.stack([bbre, bbim], axis=1)))
            zs = jnp.zeros((bp, G_C, P_C), state_s5_re.dtype)
            u, u2 = _s5_in(x_p, g, p["win"], bp, S5_TILE_BLOCKS)
            y2, sf = _s5_core(u2, wst, kconv, kin, _pair_state(pw[0, S5_L], pw[1, S5_L]).reshape(1, -1),
                              _pair_state(zs, zs), bp)
            x_p = _s5_out(x_p, u, y2, p["dskip"], p["wglu"], bp, S5_TILE_BLOCKS)
            rp, ip = _unpair_state(sf, bp)
            x_s, rs, is_ = _s5(x_s, p, state_s5_re[j].reshape(bs, D_STATE), state_s5_im[j].reshape(bs, D_STATE),
                               bs, ts, V7X_LANES)
            sre_p.append(rp); sim_p.append(ip)
            sre_s.append(rs.reshape(bs, G_C, P_C)); sim_s.append(is_.reshape(bs, G_C, P_C))
        x_p, x_s = _ffn(x_p, x_s, norm_ffn3, w1, w2, gf, layer, last)

    y_prompt = x_p.reshape(bp, tp, D_MODEL)
    y_sample = _from_rows(x_s, bs)
    return (y_prompt, y_sample, jnp.stack(outs_v), jnp.stack(conv_p), jnp.stack(h_p), jnp.stack(conv_s),
            jnp.stack(h_s), jnp.stack(sre_p), jnp.stack(sim_p), jnp.stack(sre_s), jnp.stack(sim_s))
```

```python
import functools

import jax
import jax.numpy as jnp
from jax import lax
from jax.experimental import pallas as pl
from jax.experimental.pallas import tpu as pltpu

F32 = jnp.float32
BF16 = jnp.bfloat16

D_MODEL = 1024
DEPTH = 4
N_MIXERS = 3
EPS = 1e-6
CHUNK = 128
HEADS_A = 8
HD_A = D_MODEL // HEADS_A
HEADS_B = 16
HD_B = D_MODEL // HEADS_B
CONV_W = 4
LRU_C = 8.0
GROUP_C = 16
G_C = D_MODEL // GROUP_C
P_C = 64
D_STATE = G_C * P_C
D_FF = 4 * D_MODEL

V7X_LANES = 128
V7X_MXU_DIM = 256
V7X_VMEM_BYTES = 64 * 1024 * 1024
VMEM_LIMIT = V7X_VMEM_BYTES - 8 * 1024 * 1024

N_BLK = D_MODEL // V7X_MXU_DIM
HEADS_PER_BLK = V7X_MXU_DIM // HD_B
GROUPS_PER_BLK = V7X_MXU_DIM // GROUP_C
STATE_PER_BLK = GROUPS_PER_BLK * P_C

FFN_ROWS = 1024
FFN_SAMPLE_ROWS = 512
FFN_COL_CHUNK = 1024
FFN_STAGE_BYTES = 1024 * 1024
S5_L = 16
S5_TILE_BLOCKS = 8
S5_PAIRS_PER_STEP = 2
GRANULES_PER_VREG = V7X_LANES // GROUP_C
RGLRU_PROMPT_STEPS = 128


def _rms(x, g):
    return (x * lax.rsqrt(jnp.mean(x * x, axis=-1, keepdims=True) + EPS)) * g


GELU_C0 = 0.7978845608028654
GELU_C1 = GELU_C0 * 0.044715


def _gelu(x):
    return x * (0.5 + 0.5 * jnp.tanh(x * (GELU_C0 + GELU_C1 * (x * x))))


def _dot(a, b):
    return jnp.dot(a, b, preferred_element_type=F32)


def _const_spec(shape):
    zeros = (0,) * len(shape)
    return pl.BlockSpec(shape, lambda i: zeros, pipeline_mode=pl.Buffered(1))


def _layer_spec(shape, layer):
    idx = (layer,) + (0,) * len(shape)
    return pl.BlockSpec((None,) + tuple(shape), lambda i: idx, pipeline_mode=pl.Buffered(1))


def _params():
    return pltpu.CompilerParams(dimension_semantics=("arbitrary",), vmem_limit_bytes=VMEM_LIMIT)


def _ffn_tile(x_ref, g_ref, w1_ref, w2_ref, gf_ref, o_ref, final_norm):
    h = _rms(x_ref[...], g_ref[...]).astype(BF16)
    for j in range(D_FF // FFN_COL_CHUNK):
        cols = slice(j * FFN_COL_CHUNK, (j + 1) * FFN_COL_CHUNK)
        a = jnp.square(jnp.maximum(_dot(h, w1_ref[:, cols]), 0.0)).astype(BF16)
        part = _dot(a, w2_ref[cols, :])
        if j == 0:
            o_ref[...] = part
        else:
            o_ref[...] += part
    y = x_ref[...] + o_ref[...]
    if final_norm:
        y = _rms(y, gf_ref[...])
    o_ref[...] = y


def _load_as_bf16(jobs):
    def copies_of(src, stage, sem):
        rows = stage.shape[1]
        return [pltpu.make_async_copy(src.at[pl.ds(k * rows, rows)], stage.at[k % 2], sem.at[k % 2])
                for k in range(src.shape[0] // rows)]

    plans = [copies_of(src, stage, sem) for src, _, stage, sem in jobs]
    n = len(plans[0])
    assert all(len(p) == n for p in plans)
    for p in plans:
        p[0].start()
    for k in range(n):
        if k + 1 < n:
            for p in plans:
                p[k + 1].start()
        for p, (_, dst, stage, _) in zip(plans, jobs):
            p[k].wait()
            rows = stage.shape[1]
            dst[k * rows:(k + 1) * rows, :] = stage[k % 2].astype(BF16)


def _ffn_kernel(xp_ref, xs_ref, g_ref, w1_hbm, w2_hbm, gf_ref, op_ref, os_ref, w1_s, w2_s, stage1, stage2,
                sem1, sem2, *, final_norm, n_p, layer):
    i = pl.program_id(0)

    @pl.when(i == 0)
    def _():
        _load_as_bf16([(w1_hbm.at[layer], w1_s, stage1, sem1), (w2_hbm.at[layer], w2_s, stage2, sem2)])

    @pl.when(i < n_p)
    def _():
        _ffn_tile(xp_ref, g_ref, w1_s, w2_s, gf_ref, op_ref, final_norm)

    @pl.when(i >= n_p)
    def _():
        _ffn_tile(xs_ref, g_ref, w1_s, w2_s, gf_ref, os_ref, final_norm)


def _ffn(x_p, x_s, g, w1, w2, gf, layer, final_norm):
    n_p, n_s = x_p.shape[0] // FFN_ROWS, x_s.shape[0] // FFN_SAMPLE_ROWS
    p_spec = pl.BlockSpec((FFN_ROWS, D_MODEL), lambda i: (jnp.minimum(i, n_p - 1), 0))
    s_spec = pl.BlockSpec((FFN_SAMPLE_ROWS, D_MODEL), lambda i: (jnp.maximum(i - n_p, 0), 0))
    hbm = pl.BlockSpec(memory_space=pl.ANY)
    return pl.pallas_call(
        functools.partial(_ffn_kernel, final_norm=final_norm, n_p=n_p, layer=layer),
        grid=(n_p + n_s,),
        in_specs=[p_spec, s_spec, _layer_spec((1, D_MODEL), layer), hbm, hbm, _const_spec((1, D_MODEL))],
        out_specs=[p_spec, s_spec],
        out_shape=[jax.ShapeDtypeStruct(x_p.shape, F32), jax.ShapeDtypeStruct(x_s.shape, F32)],
        scratch_shapes=[pltpu.VMEM((D_MODEL, D_FF), BF16), pltpu.VMEM((D_FF, D_MODEL), BF16),
                        pltpu.VMEM((2, FFN_STAGE_BYTES // (4 * D_FF), D_FF), F32),
                        pltpu.VMEM((2, FFN_STAGE_BYTES // (4 * D_MODEL), D_MODEL), F32),
                        pltpu.SemaphoreType.DMA((2,)), pltpu.SemaphoreType.DMA((2,))],
        compiler_params=_params(),
        name="ffn",
    )(x_p, x_s, g, w1, w2, gf)


def _sgu_front(x, g_ref, win_ref, sg_ref):
    h = _rms(x, g_ref[...]).astype(BF16)
    v = _rms(_gelu(_dot(h, win_ref[:, D_MODEL:])), sg_ref[...])
    return _gelu(_dot(h, win_ref[:, :D_MODEL])), v


def _rows_to_batch_major(x, nb):
    steps = x.shape[0] // nb
    return jnp.swapaxes(x.reshape(steps, nb, x.shape[1]), 0, 1).reshape(x.shape)


def _rows_to_time_major(x, nb):
    steps = x.shape[0] // nb
    return jnp.swapaxes(x.reshape(nb, steps, x.shape[1]), 0, 1).reshape(x.shape)


def _sgu_prompt_kernel(x_ref, g_ref, win_ref, sg_ref, w_ref, bias_ref, wout_ref, o_ref, y_s, *, nb, in_tm, out_tm):
    rows = nb * CHUNK
    if in_tm:
        x = _rows_to_batch_major(x_ref[...], nb)
    else:
        x = x_ref[...].reshape(rows, D_MODEL)
    u, v = _sgu_front(x, g_ref, win_ref, sg_ref)
    vb = v.astype(BF16)
    for b in range(nb):
        rs = slice(b * CHUNK, (b + 1) * CHUNK)
        for g in range(HEADS_A):
            cs = slice(g * HD_A, (g + 1) * HD_A)
            mixed = _dot(w_ref[g], vb[rs, cs]) + bias_ref[:, cs]
            y_s[rs, cs] = (u[rs, cs] * mixed).astype(BF16)
    o = x + _dot(y_s[...], wout_ref[...])
    if out_tm:
        o_ref[...] = _rows_to_time_major(o, nb)
    else:
        o_ref[...] = o.reshape(nb, CHUNK, D_MODEL)


def _sgu_sample_kernel(x_ref, g_ref, win_ref, sg_ref, wl_ref, bl_ref, wout_ref, o_ref, v_ref, *, nb, steps):
    x = x_ref[...]
    u, v = _sgu_front(x, g_ref, win_ref, sg_ref)
    v_ref[...] = v
    mixed = []
    for t in range(steps):
        m = None
        for s in range(t + 1):
            term = wl_ref[t * steps + s:t * steps + s + 1, :] * v[s * nb:(s + 1) * nb, :]
            m = term if m is None else m + term
        mixed.append(m + bl_ref[t:t + 1, :])
    y = (u * jnp.concatenate(mixed, axis=0)).astype(BF16)
    o_ref[...] = x + _dot(y, wout_ref[...])


def _sgu_prompt(x, p, nb, n_steps, in_tm, out_tm):
    rows = nb * CHUNK
    tm_spec = pl.BlockSpec((rows, D_MODEL), lambda i: (i, 0))
    bm_spec = pl.BlockSpec((nb, CHUNK, D_MODEL), lambda i: (0, i, 0))
    out_shape = (jax.ShapeDtypeStruct((nb * n_steps, D_MODEL), F32) if out_tm
                 else jax.ShapeDtypeStruct((nb, n_steps, D_MODEL), F32))
    return pl.pallas_call(
        functools.partial(_sgu_prompt_kernel, nb=nb, in_tm=in_tm, out_tm=out_tm),
        grid=(n_steps // CHUNK,),
        in_specs=[tm_spec if in_tm else bm_spec, _const_spec((1, D_MODEL)), _const_spec((D_MODEL, 2 * D_MODEL)),
                  _const_spec((1, D_MODEL)), _const_spec((HEADS_A, CHUNK, CHUNK)),
                  _const_spec((CHUNK, D_MODEL)), _const_spec((D_MODEL, D_MODEL))],
        out_specs=tm_spec if out_tm else bm_spec,
        out_shape=out_shape,
        scratch_shapes=[pltpu.VMEM((rows, D_MODEL), BF16)],
        compiler_params=_params(),
        name="sgu_prompt",
    )(x, p["g"], p["win"], p["sg"], p["w_tril"], p["bias"], p["wout"])


def _sgu_sample(x, p, nb, steps):
    tile = steps * nb
    tile_spec = pl.BlockSpec((tile, D_MODEL), lambda i: (0, 0))
    return pl.pallas_call(
        functools.partial(_sgu_sample_kernel, nb=nb, steps=steps),
        grid=(1,),
        in_specs=[tile_spec, _const_spec((1, D_MODEL)), _const_spec((D_MODEL, 2 * D_MODEL)),
                  _const_spec((1, D_MODEL)), _const_spec((steps * steps, D_MODEL)),
                  _const_spec((steps, D_MODEL)), _const_spec((D_MODEL, D_MODEL))],
        out_specs=[tile_spec, tile_spec],
        out_shape=[jax.ShapeDtypeStruct((tile, D_MODEL), F32), jax.ShapeDtypeStruct((tile, D_MODEL), F32)],
        compiler_params=_params(),
        name="sgu_sample",
    )(x, p["g"], p["win"], p["sg"], p["wl"], p["bl"], p["wout"])


def _rglru_kernel(x_ref, g_ref, win_ref, cw_ref, cb_ref, wa_ref, ba_ref, wx_ref, bx_ref, lam_ref, wout_ref,
                  conv0_ref, h0_ref, o_ref, conv_ref, h_ref, xext_s, a_s, b_s, y_s, *, nb, steps):
    rows = nb * steps
    halo = (CONV_W - 1) * nb

    @pl.when(pl.program_id(0) == 0)
    def _():
        conv_ref[...] = conv0_ref[...]
        h_ref[...] = h0_ref[...]

    def blk_cols(j):
        return slice(j * V7X_MXU_DIM, (j + 1) * V7X_MXU_DIM)

    x = x_ref[...]
    h = _rms(x, g_ref[...]).astype(BF16)
    for j in range(N_BLK):
        cols = blk_cols(j)
        xext, a_j, b_j = xext_s.at[j], a_s.at[j], b_s.at[j]
        xext[0:halo, :] = conv_ref[:, cols]
        xext[halo:halo + rows, :] = _dot(h, win_ref[:, blk_cols(N_BLK + j)])
        gate = _gelu(_dot(h, win_ref[:, cols]))
        conv = None
        for w in range(CONV_W):
            term = xext[w * nb:w * nb + rows, :] * cw_ref[w:w + 1, cols]
            conv = term if conv is None else conv + term
        conv_ref[:, cols] = xext[rows:rows + halo, :]
        xc = cb_ref[:, cols] + conv
        xcb = xc.astype(BF16)
        r = jax.nn.sigmoid(_dot(xcb, wa_ref[j]) + ba_ref[:, cols])
        ig = jax.nn.sigmoid(_dot(xcb, wx_ref[j]) + bx_ref[:, cols])
        neg_log_a = (LRU_C * r) * jax.nn.softplus(-lam_ref[:, cols])
        a = jnp.exp(-neg_log_a)
        a_j[...] = a
        w = jnp.tanh(neg_log_a) * (a * a + 1.0)
        mult = jnp.where(w > 0.0, w * lax.rsqrt(w), 0.0)
        b_j[...] = mult * (ig * xc)
        hcur = h_ref[:, cols]
        for t in range(steps):
            sl = slice(t * nb, (t + 1) * nb)
            hcur = a_j[sl, :] * hcur + b_j[sl, :]
            b_j[sl, :] = hcur
        h_ref[:, cols] = hcur
        y_s[:, cols] = (b_j[...] * gate).astype(BF16)
    o_ref[...] = x + _dot(y_s[...], wout_ref[...])


def _rglru(x, p, conv0, h0, nb, steps):
    rows = nb * steps
    halo = (CONV_W - 1) * nb
    row_spec = pl.BlockSpec((rows, D_MODEL), lambda i: (i, 0))
    vec = _const_spec((1, D_MODEL))
    blk = _const_spec((N_BLK, V7X_MXU_DIM, V7X_MXU_DIM))
    return pl.pallas_call(
        functools.partial(_rglru_kernel, nb=nb, steps=steps),
        grid=(x.shape[0] // rows,),
        in_specs=[row_spec, vec, _const_spec((D_MODEL, 2 * D_MODEL)), _const_spec((CONV_W, D_MODEL)), vec,
                  blk, vec, blk, vec, vec, _const_spec((D_MODEL, D_MODEL)),
                  _const_spec((halo, D_MODEL)), _const_spec((nb, D_MODEL))],
        out_specs=[row_spec, pl.BlockSpec((halo, D_MODEL), lambda i: (0, 0)),
                   pl.BlockSpec((nb, D_MODEL), lambda i: (0, 0))],
        out_shape=[jax.ShapeDtypeStruct(x.shape, F32), jax.ShapeDtypeStruct((halo, D_MODEL), F32),
                   jax.ShapeDtypeStruct((nb, D_MODEL), F32)],
        scratch_shapes=[pltpu.VMEM((N_BLK, rows + halo, V7X_MXU_DIM), F32), pltpu.VMEM((N_BLK, rows, V7X_MXU_DIM), F32),
                        pltpu.VMEM((N_BLK, rows, V7X_MXU_DIM), F32), pltpu.VMEM((rows, D_MODEL), BF16)],
        compiler_params=_params(),
        name="rglru",
    )(x, p["g"], p["win"], p["cw"], p["cb"], p["wa"], p["ba"], p["wx"], p["bx"], p["lam"], p["wout"], conv0, h0)


def _cmul(a_re, a_im, b_re, b_im):
    return a_re * b_re - a_im * b_im, a_re * b_im + a_im * b_re


def _s5_disc_kernel(lre_ref, lim_ref, ldt_ref, bre_ref, bim_ref, are_ref, aim_ref, bbre_ref, bbim_ref, pw_ref):
    lr, li = lre_ref[...], lim_ref[...]
    dt = jnp.exp(ldt_ref[...])
    mag = jnp.exp(lr * dt)
    ab_re, ab_im = mag * jnp.cos(li * dt), mag * jnp.sin(li * dt)
    zr, zi = ab_re - 1.0, ab_im
    den = lr * lr + li * li
    q_re = (zr * lr + zi * li) / den
    q_im = (zi * lr - zr * li) / den
    are_ref[...] = ab_re
    aim_ref[...] = ab_im
    br, bi = bre_ref[...], bim_ref[...]
    bbre_ref[...] = q_re[:, None, :] * br - q_im[:, None, :] * bi
    bbim_ref[...] = q_re[:, None, :] * bi + q_im[:, None, :] * br
    p_re, p_im = jnp.ones_like(ab_re), jnp.zeros_like(ab_re)
    for k in range(S5_L + 1):
        pw_ref[0, k] = p_re
        pw_ref[1, k] = p_im
        p_re, p_im = _cmul(p_re, p_im, ab_re, ab_im)


def _s5_disc(lam_re, lam_im, log_dt, b_re_t, b_im_t):
    gp = jax.ShapeDtypeStruct((G_C, P_C), F32)
    ghp = jax.ShapeDtypeStruct((G_C, GROUP_C, P_C), F32)
    pw = jax.ShapeDtypeStruct((2, S5_L + 1, G_C, P_C), F32)
    return pl.pallas_call(_s5_disc_kernel, out_shape=[gp, gp, ghp, ghp, pw], name="s5_disc")(
        lam_re, lam_im, log_dt.reshape(G_C, 1), b_re_t, b_im_t)


def _s5_taps_kernel(pw0_ref, pw1_ref, pwf_ref, c_ref, b_ref, kconv_ref, wst_ref, kin_ref, *, pairs):
    n = S5_L * GROUP_C
    pair_lanes = 2 * P_C

    def rep(t):
        return jnp.broadcast_to(t[:, None, :], (S5_L, GROUP_C, pair_lanes)).reshape(n, pair_lanes)

    def til(c):
        return jnp.broadcast_to(c[None, :, :], (S5_L, GROUP_C, pair_lanes)).reshape(n, pair_lanes)

    def dot_t(a, b):
        return lax.dot_general(a, b, (((1,), (1,)), ((), ())), precision=lax.Precision.HIGHEST,
                               preferred_element_type=F32)

    def first(shape, axis):
        return lax.broadcasted_iota(jnp.int32, shape, axis) < P_C

    lane = lax.broadcasted_iota(jnp.int32, (GROUP_C, n), 1)
    for q in range(pairs):
        cr, ci = til(c_ref[q, 0]), til(c_ref[q, 1])
        b_re, b_im = b_ref[q, 0], b_ref[q, 1]
        x_re, x_im = _cmul(rep(pw0_ref[q, 0]), rep(pw0_ref[q, 1]), cr, ci)
        g0 = first(b_re.shape, 1)
        b2 = jnp.concatenate(
            [jnp.concatenate([jnp.where(g0, b_re, 0.0), jnp.where(g0, -b_im, 0.0)], axis=1),
             jnp.concatenate([jnp.where(g0, 0.0, b_re), jnp.where(g0, 0.0, -b_im)], axis=1)], axis=0)
        mrows = dot_t(b2, jnp.concatenate([x_re, x_im], axis=1))
        for gi in range(2):
            mrow = mrows[gi * GROUP_C:(gi + 1) * GROUP_C]
            for s in range(S5_L):
                blk = mrow if s == 0 else jnp.where(lane >= s * GROUP_C, pltpu.roll(mrow, s * GROUP_C, 1), 0.0)
                kconv_ref[2 * q + gi, s * GROUP_C:(s + 1) * GROUP_C, :] = blk.astype(BF16)
        e_re, e_im = _cmul(rep(pw1_ref[q, 0]), rep(pw1_ref[q, 1]), cr, ci)
        et_re, et_im = e_re.T, (-e_im).T
        top = first(et_re.shape, 0)
        kin_ref[q] = jnp.concatenate(
            [jnp.concatenate([jnp.where(top, et_re, 0.0), jnp.where(top, 0.0, et_re)], axis=1),
             jnp.concatenate([jnp.where(top, et_im, 0.0), jnp.where(top, 0.0, et_im)], axis=1)], axis=0).astype(BF16)
        f_re, f_im = _cmul(rep(pwf_ref[q, 0]), rep(pwf_ref[q, 1]), til(b_re), til(b_im))
        left = first(f_re.shape, 1)
        wst_ref[q] = jnp.concatenate(
            [jnp.concatenate([jnp.where(left, f_re, 0.0), jnp.where(left, f_im, 0.0)], axis=1),
             jnp.concatenate([jnp.where(left, 0.0, f_re), jnp.where(left, 0.0, f_im)], axis=1)], axis=0).astype(BF16)


def _s5_taps(pw0, pw1, pwf, c, bb):
    pairs = 4
    n = S5_L * GROUP_C
    pspec = pl.BlockSpec((pairs, 2, S5_L, 2 * P_C), lambda i: (i, 0, 0, 0))
    hspec = pl.BlockSpec((pairs, 2, GROUP_C, 2 * P_C), lambda i: (i, 0, 0, 0))
    return pl.pallas_call(
        functools.partial(_s5_taps_kernel, pairs=pairs),
        grid=(G_C // (2 * pairs),),
        in_specs=[pspec, pspec, pspec, hspec, hspec],
        out_specs=[pl.BlockSpec((2 * pairs, n, n), lambda i: (i, 0, 0)),
                   pl.BlockSpec((pairs, 2 * n, 4 * P_C), lambda i: (i, 0, 0)),
                   pl.BlockSpec((pairs, 4 * P_C, 2 * n), lambda i: (i, 0, 0))],
        out_shape=[jax.ShapeDtypeStruct((G_C, n, n), BF16), jax.ShapeDtypeStruct((G_C // 2, 2 * n, 4 * P_C), BF16),
                   jax.ShapeDtypeStruct((G_C // 2, 4 * P_C, 2 * n), BF16)],
        compiler_params=_params(),
        name="s5_taps",
    )(pw0, pw1, pwf, c, bb)


def _pair_lanes(a):
    a2 = a.reshape((G_C // 2, 2) + a.shape[1:])
    return jnp.concatenate([a2[:, 0], a2[:, 1]], axis=-1)


def _granule_transpose(vs):
    vs = list(vs)
    granule = lax.broadcasted_iota(jnp.int32, vs[0].shape, 1) // GROUP_C
    d = 1
    while d < len(vs):
        keep = (granule & d) == 0
        for i in range(len(vs)):
            if i & d == 0:
                a, b = vs[i], vs[i + d]
                vs[i] = jnp.where(keep, a, pltpu.roll(b, d * GROUP_C, 1))
                vs[i + d] = jnp.where(keep, pltpu.roll(a, V7X_LANES - d * GROUP_C, 1), b)
        d *= 2
    return vs


def _s5_in_kernel(x_ref, g_ref, win_ref, u_ref, u2_ref, *, nb, blocks):
    h = _rms(x_ref[...], g_ref[...]).astype(BF16)
    rows2 = blocks * nb
    for cc in range(N_BLK):
        cols = slice(cc * V7X_MXU_DIM, (cc + 1) * V7X_MXU_DIM)
        u = _dot(h, win_ref[:, cols])
        u_ref[:, cols] = u
        u4 = u.reshape(blocks, S5_L, nb, V7X_MXU_DIM)
        for sub in range(V7X_MXU_DIM // V7X_LANES):
            c = cc * (V7X_MXU_DIM // V7X_LANES) + sub
            for half in range(S5_L // GRANULES_PER_VREG):
                vs = []
                for kk in range(GRANULES_PER_VREG):
                    slab = u4[:, half * GRANULES_PER_VREG + kk, :, sub * V7X_LANES:(sub + 1) * V7X_LANES]
                    vs.append(slab.reshape(rows2, V7X_LANES))
                ws = _granule_transpose(vs)
                for gi in range(GRANULES_PER_VREG):
                    lane0 = (c * GRANULES_PER_VREG + gi) * S5_L * GROUP_C + half * V7X_LANES
                    u2_ref[:, lane0:lane0 + V7X_LANES] = ws[gi].astype(BF16)


def _s5_in(x, g, win, nb, blocks):
    rows = blocks * S5_L * nb
    n_blocks = x.shape[0] // (S5_L * nb)
    row_spec = pl.BlockSpec((rows, D_MODEL), lambda i: (i, 0))
    return pl.pallas_call(
        functools.partial(_s5_in_kernel, nb=nb, blocks=blocks),
        grid=(x.shape[0] // rows,),
        in_specs=[row_spec, _const_spec((1, D_MODEL)), _const_spec((D_MODEL, D_MODEL))],
        out_specs=[row_spec, pl.BlockSpec((blocks * nb, S5_L * D_MODEL), lambda i: (i, 0))],
        out_shape=[jax.ShapeDtypeStruct(x.shape, F32),
                   jax.ShapeDtypeStruct((n_blocks * nb, S5_L * D_MODEL), BF16)],
        compiler_params=_params(),
        name="s5_in",
    )(x, g, win)


def _s5_core_kernel(u_ref, wst_ref, kconv_ref, kin_ref, a_ref, s0_ref, y_ref, sf_ref, sc_s, sin_s, *, nb, n_blocks):
    pw = 2 * S5_L * GROUP_C
    sw = 2 * P_C
    ub = [u_ref[:, q * pw:(q + 1) * pw] for q in range(S5_PAIRS_PER_STEP)]
    for q in range(S5_PAIRS_PER_STEP):
        sc_s[q] = _dot(ub[q], wst_ref[q])
    a_re = [jnp.broadcast_to(a_ref[:, 2 * sw * q:2 * sw * q + sw], (nb, sw)) for q in range(S5_PAIRS_PER_STEP)]
    a_im = [jnp.broadcast_to(a_ref[:, 2 * sw * q + sw:2 * sw * (q + 1)], (nb, sw)) for q in range(S5_PAIRS_PER_STEP)]
    s_re = [s0_ref[:, 2 * sw * q:2 * sw * q + sw] for q in range(S5_PAIRS_PER_STEP)]
    s_im = [s0_ref[:, 2 * sw * q + sw:2 * sw * (q + 1)] for q in range(S5_PAIRS_PER_STEP)]
    for blk in range(n_blocks):
        rs = slice(blk * nb, (blk + 1) * nb)
        for q in range(S5_PAIRS_PER_STEP):
            sin_s[q, rs, 0:sw] = s_re[q]
            sin_s[q, rs, sw:2 * sw] = s_im[q]
            n_re, n_im = _cmul(a_re[q], a_im[q], s_re[q], s_im[q])
            s_re[q], s_im[q] = n_re + sc_s[q, rs, 0:sw], n_im + sc_s[q, rs, sw:2 * sw]
    for q in range(S5_PAIRS_PER_STEP):
        sf_ref[:, 2 * sw * q:2 * sw * q + sw] = s_re[q]
        sf_ref[:, 2 * sw * q + sw:2 * sw * (q + 1)] = s_im[q]
        half = S5_L * GROUP_C
        conv = jnp.concatenate([_dot(ub[q][:, :half], kconv_ref[2 * q]),
                                _dot(ub[q][:, half:], kconv_ref[2 * q + 1])], axis=1)
        y_ref[:, q * pw:(q + 1) * pw] = conv + _dot(sin_s[q].astype(BF16), kin_ref[q])


def _s5_core(u2, wst, kconv, kin, a_pair, s0, nb):
    rows = u2.shape[0]
    pw = 2 * S5_L * GROUP_C
    sw = 2 * P_C
    cols = S5_PAIRS_PER_STEP * pw
    scols = S5_PAIRS_PER_STEP * 2 * sw
    return pl.pallas_call(
        functools.partial(_s5_core_kernel, nb=nb, n_blocks=rows // nb),
        grid=(u2.shape[1] // cols,),
        in_specs=[pl.BlockSpec((rows, cols), lambda i: (0, i)),
                  pl.BlockSpec((S5_PAIRS_PER_STEP, pw, 2 * sw), lambda i: (i, 0, 0)),
                  pl.BlockSpec((2 * S5_PAIRS_PER_STEP, pw // 2, pw // 2), lambda i: (i, 0, 0)),
                  pl.BlockSpec((S5_PAIRS_PER_STEP, 2 * sw, pw), lambda i: (i, 0, 0)),
                  pl.BlockSpec((1, scols), lambda i: (0, i)), pl.BlockSpec((nb, scols), lambda i: (0, i))],
        out_specs=[pl.BlockSpec((rows, cols), lambda i: (0, i)), pl.BlockSpec((nb, scols), lambda i: (0, i))],
        out_shape=[jax.ShapeDtypeStruct(u2.shape, F32), jax.ShapeDtypeStruct((nb, 2 * D_STATE), F32)],
        scratch_shapes=[pltpu.VMEM((S5_PAIRS_PER_STEP, rows, 2 * sw), F32),
                        pltpu.VMEM((S5_PAIRS_PER_STEP, rows, 2 * sw), F32)],
        compiler_params=_params(),
        name="s5_core",
    )(u2, wst, kconv, kin, a_pair, s0)


def _s5_out_kernel(x_ref, u_ref, y2_ref, dskip_ref, wglu_ref, o_ref, y_s, gate_s, *, nb, blocks):
    rows = blocks * S5_L * nb
    for cc in range(N_BLK):
        cols = slice(cc * V7X_MXU_DIM, (cc + 1) * V7X_MXU_DIM)
        for sub in range(V7X_MXU_DIM // V7X_LANES):
            c = cc * (V7X_MXU_DIM // V7X_LANES) + sub
            for half in range(S5_L // GRANULES_PER_VREG):
                lanes = [(c * GRANULES_PER_VREG + gi) * S5_L * GROUP_C + half * V7X_LANES
                         for gi in range(GRANULES_PER_VREG)]
                ws = _granule_transpose([y2_ref[:, l0:l0 + V7X_LANES] for l0 in lanes])
                for kk in range(GRANULES_PER_VREG):
                    y_s[cc, :, half * GRANULES_PER_VREG + kk, :, sub * V7X_LANES:(sub + 1) * V7X_LANES] = (
                        ws[kk].reshape(blocks, nb, V7X_LANES))
        y = y_s[cc].reshape(rows, V7X_MXU_DIM) + dskip_ref[:, cols] * u_ref[:, cols]
        gy = _gelu(y).astype(BF16)
        for acc, wcols in ((o_ref, slice(0, D_MODEL)), (gate_s, slice(D_MODEL, 2 * D_MODEL))):
            part = _dot(gy, wglu_ref[cols, wcols])
            if cc == 0:
                acc[...] = part
            else:
                acc[...] += part
    o_ref[...] = x_ref[...] + o_ref[...] * jax.nn.sigmoid(gate_s[...])


def _s5_out(x, u, y2, dskip, wglu, nb, blocks):
    rows = blocks * S5_L * nb
    row_spec = pl.BlockSpec((rows, D_MODEL), lambda i: (i, 0))
    return pl.pallas_call(
        functools.partial(_s5_out_kernel, nb=nb, blocks=blocks),
        grid=(x.shape[0] // rows,),
        in_specs=[row_spec, row_spec, pl.BlockSpec((blocks * nb, S5_L * D_MODEL), lambda i: (i, 0)),
                  _const_spec((1, D_MODEL)), _const_spec((D_MODEL, 2 * D_MODEL))],
        out_specs=row_spec,
        out_shape=jax.ShapeDtypeStruct(x.shape, F32),
        scratch_shapes=[pltpu.VMEM((N_BLK, blocks, S5_L, nb, V7X_MXU_DIM), F32), pltpu.VMEM((rows, D_MODEL), F32)],
        compiler_params=_params(),
        name="s5_out",
    )(x, u, y2, dskip, wglu)


def _pair_state(re, im):
    lead = re.shape[:-2]
    r = re.reshape(lead + (G_C // 2, 2 * P_C))
    i = im.reshape(lead + (G_C // 2, 2 * P_C))
    return jnp.concatenate([r, i], axis=-1).reshape(lead + (2 * D_STATE,))


def _unpair_state(s, nb):
    s4 = s.reshape(nb, G_C // 2, 2, 2 * P_C)
    return s4[:, :, 0].reshape(nb, G_C, P_C), s4[:, :, 1].reshape(nb, G_C, P_C)


def _s5_kernel(x_ref, g_ref, win_ref, bre_ref, bim_ref, are_ref, aim_ref, cre_ref, cim_ref, dskip_ref, wglu_ref,
               sre0_ref, sim0_ref, o_ref, sre_ref, sim_ref, xre_s, xim_s, *, nb, steps, lane_chunk):
    @pl.when(pl.program_id(0) == 0)
    def _():
        sre_ref[...] = sre0_ref[...]
        sim_ref[...] = sim0_ref[...]

    x = x_ref[...]
    h = _rms(x, g_ref[...]).astype(BF16)
    u = _dot(h, win_ref[...])
    ub = u.astype(BF16)

    def expand(j):
        ucols = slice(j * V7X_MXU_DIM, (j + 1) * V7X_MXU_DIM)
        scols = slice(j * STATE_PER_BLK, (j + 1) * STATE_PER_BLK)
        xre_s[:, scols] = _dot(ub[:, ucols], bre_ref[j])
        xim_s[:, scols] = _dot(ub[:, ucols], bim_ref[j])

    def scan(j):
        for c in range(STATE_PER_BLK // lane_chunk):
            cols = slice(j * STATE_PER_BLK + c * lane_chunk, j * STATE_PER_BLK + (c + 1) * lane_chunk)
            a_re = jnp.broadcast_to(are_ref[:, cols], (nb, lane_chunk))
            a_im = jnp.broadcast_to(aim_ref[:, cols], (nb, lane_chunk))
            s_re, s_im = sre_ref[:, cols], sim_ref[:, cols]
            for t in range(steps):
                sl = slice(t * nb, (t + 1) * nb)
                s_re, s_im = ((a_re * s_re - a_im * s_im) + xre_s[sl, cols],
                              (a_re * s_im + a_im * s_re) + xim_s[sl, cols])
                xre_s[sl, cols] = s_re
                xim_s[sl, cols] = s_im
            sre_ref[:, cols] = s_re
            sim_ref[:, cols] = s_im

    ys = []
    expand(0)
    for j in range(N_BLK):
        if j + 1 < N_BLK:
            expand(j + 1)
        scan(j)
        scols = slice(j * STATE_PER_BLK, (j + 1) * STATE_PER_BLK)
        ys.append(_dot(xre_s[:, scols].astype(BF16), cre_ref[j]) - _dot(xim_s[:, scols].astype(BF16), cim_ref[j]))
    y = jnp.concatenate(ys, axis=1) + dskip_ref[...] * u
    o = _dot(_gelu(y).astype(BF16), wglu_ref[...])
    o_ref[...] = x + o[:, :D_MODEL] * jax.nn.sigmoid(o[:, D_MODEL:])


def _s5(x, p, sre0, sim0, nb, steps, lane_chunk):
    rows = nb * steps
    row_spec = pl.BlockSpec((rows, D_MODEL), lambda i: (i, 0))
    vec = _const_spec((1, D_MODEL))
    svec = _const_spec((1, D_STATE))
    state_spec = _const_spec((nb, D_STATE))
    bspec = _const_spec((N_BLK, V7X_MXU_DIM, STATE_PER_BLK))
    cspec = _const_spec((N_BLK, STATE_PER_BLK, V7X_MXU_DIM))
    state_out = pl.BlockSpec((nb, D_STATE), lambda i: (0, 0))
    return pl.pallas_call(
        functools.partial(_s5_kernel, nb=nb, steps=steps, lane_chunk=lane_chunk),
        grid=(x.shape[0] // rows,),
        in_specs=[row_spec, vec, _const_spec((D_MODEL, D_MODEL)), bspec, bspec, svec, svec, cspec, cspec, vec,
                  _const_spec((D_MODEL, 2 * D_MODEL)), state_spec, state_spec],
        out_specs=[row_spec, state_out, state_out],
        out_shape=[jax.ShapeDtypeStruct(x.shape, F32), jax.ShapeDtypeStruct((nb, D_STATE), F32),
                   jax.ShapeDtypeStruct((nb, D_STATE), F32)],
        scratch_shapes=[pltpu.VMEM((rows, D_STATE), F32), pltpu.VMEM((rows, D_STATE), F32)],
        compiler_params=_params(),
        name="s5",
    )(x, p["g"], p["win"], p["bre"], p["bim"], p["are"], p["aim"], p["cre"], p["cim"], p["dskip"], p["wglu"],
      sre0, sim0)


def _block_diag(w, n_per_blk):
    n, k_in, k_out = w.shape
    wb = w.reshape(n // n_per_blk, n_per_blk, k_in, k_out)
    eye = jnp.eye(n_per_blk, dtype=w.dtype)
    out = jnp.einsum("jgio,gk->jgiko", wb, eye)
    return out.reshape(n // n_per_blk, n_per_blk * k_in, n_per_blk * k_out)


def _to_rows(a):
    return jnp.swapaxes(a, 0, 1).reshape(a.shape[0] * a.shape[1], a.shape[2])


def _from_rows(a, nb):
    return jnp.swapaxes(a.reshape(a.shape[0] // nb, nb, a.shape[1]), 0, 1)


def kernel(x_prompt, x_sample, state_rglru_conv, state_rglru_h, state_s5_re, state_s5_im, norm_mix, norm_ffn, norm_f, w_ff1, w_ff2, w_in_a, sgu_g, w_s, b_s, w_out_a, w_in_b, conv_w, conv_b, w_a, b_a, w_x, b_x, lam, w_out_b, w_in_c, lam_re, lam_im, log_dt, b_re, b_im, c_re, c_im, d_skip, w_glu):
    bp, tp, _ = x_prompt.shape
    bs, ts, _ = x_sample.shape
    rows_p, rows_s = bp * tp, bs * ts
    assert DEPTH % N_MIXERS == 1 and DEPTH > 1, "first and last layers must be SGU layers"
    assert tp % CHUNK == 0 and ts < CHUNK
    assert rows_s % FFN_SAMPLE_ROWS == 0 and rows_p % FFN_ROWS == 0

    row = lambda v: v.reshape(1, -1)
    norm_ffn3 = norm_ffn.reshape(DEPTH, 1, D_MODEL)
    w1, w2, gf = w_ff1, w_ff2, row(norm_f)

    x_p, x_s = x_prompt, _to_rows(x_sample)
    outs_v, conv_p, h_p, conv_s, h_s, sre_p, sim_p, sre_s, sim_s = [], [], [], [], [], [], [], [], []
    for layer in range(DEPTH):
        j, kind = layer // N_MIXERS, layer % N_MIXERS
        first, last = layer == 0, layer == DEPTH - 1
        g = row(norm_mix[layer])
        if kind == 0:
            p = dict(g=g, win=w_in_a[j].astype(BF16), wout=w_out_a[j].astype(BF16), sg=row(sgu_g[j]),
                     w_tril=jnp.where(jnp.tril(jnp.ones((CHUNK, CHUNK), dtype=bool)), w_s[j], 0.0).astype(BF16),
                     bias=jnp.repeat(b_s[j].T, HD_A, axis=1),
                     wl=jnp.repeat(w_s[j][:, :ts, :ts].reshape(HEADS_A, ts * ts).T, HD_A, axis=1),
                     bl=jnp.repeat(b_s[j][:, :ts].T, HD_A, axis=1))
            if first:
                x_p = _sgu_prompt(x_p, p, bp, tp, False, True)
            elif last:
                x_p = _sgu_prompt(x_p, p, bp, tp, True, False).reshape(rows_p, D_MODEL)
            else:
                raise NotImplementedError("interior SGU layers")
            x_s, v = _sgu_sample(x_s, p, bs, ts)
            outs_v.append(_from_rows(v, bs))
        elif kind == 1:
            p = dict(g=g, win=w_in_b[j].astype(BF16), cw=conv_w[j], cb=row(conv_b[j]),
                     wa=_block_diag(w_a[j], HEADS_PER_BLK).astype(BF16), ba=row(b_a[j]),
                     wx=_block_diag(w_x[j], HEADS_PER_BLK).astype(BF16), bx=row(b_x[j]),
                     lam=row(lam[j]), wout=w_out_b[j].astype(BF16))
            dt_s = state_rglru_h.dtype
            x_p, cp, hp = _rglru(x_p, p, jnp.zeros(((CONV_W - 1) * bp, D_MODEL), dt_s),
                                 jnp.zeros((bp, D_MODEL), dt_s), bp, RGLRU_PROMPT_STEPS)
            x_s, cs, hs = _rglru(x_s, p, _to_rows(state_rglru_conv[j]), state_rglru_h[j], bs, ts)
            conv_p.append(_from_rows(cp, bp)); h_p.append(hp)
            conv_s.append(_from_rows(cs, bs)); h_s.append(hs)
        else:
            are, aim, bbre, bbim, pw = _s5_disc(lam_re[j], lam_im[j], log_dt[j],
                                                jnp.swapaxes(b_re[j], 1, 2), jnp.swapaxes(b_im[j], 1, 2))
            p = dict(g=g, win=w_in_c[j].astype(BF16),
                     bre=_block_diag(bbre, GROUPS_PER_BLK).astype(BF16),
                     bim=_block_diag(bbim, GROUPS_PER_BLK).astype(BF16),
                     are=are.reshape(1, D_STATE), aim=aim.reshape(1, D_STATE),
                     cre=_block_diag(jnp.swapaxes(c_re[j], 1, 2), GROUPS_PER_BLK).astype(BF16),
                     cim=_block_diag(jnp.swapaxes(c_im[j], 1, 2), GROUPS_PER_BLK).astype(BF16),
                     dskip=row(d_skip[j]), wglu=w_glu[j].astype(BF16))
            pwg = _pair_lanes(jnp.transpose(pw, (2, 0, 1, 3)))
            kconv, wst, kin = _s5_taps(pwg[:, :, :S5_L], pwg[:, :, 1:], pwg[:, :, S5_L - 1::-1],
                                       _pair_lanes(jnp.stack([c_re[j], c_im[j]], axis=1)),
                                       _pair_lanes(jnp.stack([bbre, bbim], axis=1)))
            zs = jnp.zeros((bp, G_C, P_C), state_s5_re.dtype)
            u, u2 = _s5_in(x_p, g, p["win"], bp, S5_TILE_BLOCKS)
            y2, sf = _s5_core(u2, wst, kconv, kin, _pair_state(pw[0, S5_L], pw[1, S5_L]).reshape(1, -1),
                              _pair_state(zs, zs), bp)
            x_p = _s5_out(x_p, u, y2, p["dskip"], p["wglu"], bp, S5_TILE_BLOCKS)
            rp, ip = _unpair_state(sf, bp)
            x_s, rs, is_ = _s5(x_s, p, state_s5_re[j].reshape(bs, D_STATE), state_s5_im[j].reshape(bs, D_STATE),
                               bs, ts, V7X_LANES)
            sre_p.append(rp); sim_p.append(ip)
            sre_s.append(rs.reshape(bs, G_C, P_C)); sim_s.append(is_.reshape(bs, G_C, P_C))
        x_p, x_s = _ffn(x_p, x_s, norm_ffn3, w1, w2, gf, layer, last)

    y_prompt = x_p.reshape(bp, tp, D_MODEL)
    y_sample = _from_rows(x_s, bs)
    return (y_prompt, y_sample, jnp.stack(outs_v), jnp.stack(conv_p), jnp.stack(h_p), jnp.stack(conv_s),
            jnp.stack(h_s), jnp.stack(sre_p), jnp.stack(sim_p), jnp.stack(sre_s), jnp.stack(sim_s))
```

```python
import functools

import jax
import jax.numpy as jnp
from jax import lax
from jax.experimental import pallas as pl
from jax.experimental.pallas import tpu as pltpu

F32 = jnp.float32
BF16 = jnp.bfloat16

D_MODEL = 1024
DEPTH = 4
N_MIXERS = 3
EPS = 1e-6
CHUNK = 128
HEADS_A = 8
HD_A = D_MODEL // HEADS_A
HEADS_B = 16
HD_B = D_MODEL // HEADS_B
CONV_W = 4
LRU_C = 8.0
GROUP_C = 16
G_C = D_MODEL // GROUP_C
P_C = 64
D_STATE = G_C * P_C
D_FF = 4 * D_MODEL

V7X_LANES = 128
V7X_MXU_DIM = 256
V7X_VMEM_BYTES = 64 * 1024 * 1024
VMEM_LIMIT = V7X_VMEM_BYTES - 8 * 1024 * 1024

N_BLK = D_MODEL // V7X_MXU_DIM
HEADS_PER_BLK = V7X_MXU_DIM // HD_B
GROUPS_PER_BLK = V7X_MXU_DIM // GROUP_C
STATE_PER_BLK = GROUPS_PER_BLK * P_C

FFN_ROWS = 1024
FFN_SAMPLE_ROWS = 512
FFN_COL_CHUNK = 1024
FFN_STAGE_BYTES = 1024 * 1024
S5_L = 16
S5_TILE_BLOCKS = 8
S5_PAIRS_PER_STEP = 2
GRANULES_PER_VREG = V7X_LANES // GROUP_C
RGLRU_PROMPT_STEPS = 128


def _rms(x, g):
    return (x * lax.rsqrt(jnp.mean(x * x, axis=-1, keepdims=True) + EPS)) * g


GELU_C0 = 0.7978845608028654
GELU_C1 = GELU_C0 * 0.044715


def _gelu(x):
    return x * (0.5 + 0.5 * jnp.tanh(x * (GELU_C0 + GELU_C1 * (x * x))))


def _dot(a, b):
    return jnp.dot(a, b, preferred_element_type=F32)


def _const_spec(shape):
    zeros = (0,) * len(shape)
    return pl.BlockSpec(shape, lambda i: zeros, pipeline_mode=pl.Buffered(1))


def _layer_spec(shape, layer):
    idx = (layer,) + (0,) * len(shape)
    return pl.BlockSpec((None,) + tuple(shape), lambda i: idx, pipeline_mode=pl.Buffered(1))


def _params():
    return pltpu.CompilerParams(dimension_semantics=("arbitrary",), vmem_limit_bytes=VMEM_LIMIT)


def _ffn_tile(x_ref, g_ref, w1_ref, w2_ref, gf_ref, o_ref, final_norm):
    h = _rms(x_ref[...], g_ref[...]).astype(BF16)
    for j in range(D_FF // FFN_COL_CHUNK):
        cols = slice(j * FFN_COL_CHUNK, (j + 1) * FFN_COL_CHUNK)
        a = jnp.square(jnp.maximum(_dot(h, w1_ref[:, cols]), 0.0)).astype(BF16)
        part = _dot(a, w2_ref[cols, :])
        if j == 0:
            o_ref[...] = part
        else:
            o_ref[...] += part
    y = x_ref[...] + o_ref[...]
    if final_norm:
        y = _rms(y, gf_ref[...])
    o_ref[...] = y


def _load_as_bf16(jobs):
    def copies_of(src, stage, sem):
        rows = stage.shape[1]
        return [pltpu.make_async_copy(src.at[pl.ds(k * rows, rows)], stage.at[k % 2], sem.at[k % 2])
                for k in range(src.shape[0] // rows)]

    plans = [copies_of(src, stage, sem) for src, _, stage, sem in jobs]
    n = len(plans[0])
    assert all(len(p) == n for p in plans)
    for p in plans:
        p[0].start()
    for k in range(n):
        if k + 1 < n:
            for p in plans:
                p[k + 1].start()
        for p, (_, dst, stage, _) in zip(plans, jobs):
            p[k].wait()
            rows = stage.shape[1]
            dst[k * rows:(k + 1) * rows, :] = stage[k % 2].astype(BF16)


def _ffn_kernel(xp_ref, xs_ref, g_ref, w1_hbm, w2_hbm, gf_ref, op_ref, os_ref, w1_s, w2_s, stage1, stage2,
                sem1, sem2, *, final_norm, n_p, layer):
    i = pl.program_id(0)

    @pl.when(i == 0)
    def _():
        _load_as_bf16([(w1_hbm.at[layer], w1_s, stage1, sem1), (w2_hbm.at[layer], w2_s, stage2, sem2)])

    @pl.when(i < n_p)
    def _():
        _ffn_tile(xp_ref, g_ref, w1_s, w2_s, gf_ref, op_ref, final_norm)

    @pl.when(i >= n_p)
    def _():
        _ffn_tile(xs_ref, g_ref, w1_s, w2_s, gf_ref, os_ref, final_norm)


def _ffn(x_p, x_s, g, w1, w2, gf, layer, final_norm):
    n_p, n_s = x_p.shape[0] // FFN_ROWS, x_s.shape[0] // FFN_SAMPLE_ROWS
    p_spec = pl.BlockSpec((FFN_ROWS, D_MODEL), lambda i: (jnp.minimum(i, n_p - 1), 0))
    s_spec = pl.BlockSpec((FFN_SAMPLE_ROWS, D_MODEL), lambda i: (jnp.maximum(i - n_p, 0), 0))
    hbm = pl.BlockSpec(memory_space=pl.ANY)
    return pl.pallas_call(
        functools.partial(_ffn_kernel, final_norm=final_norm, n_p=n_p, layer=layer),
        grid=(n_p + n_s,),
        in_specs=[p_spec, s_spec, _layer_spec((1, D_MODEL), layer), hbm, hbm, _const_spec((1, D_MODEL))],
        out_specs=[p_spec, s_spec],
        out_shape=[jax.ShapeDtypeStruct(x_p.shape, F32), jax.ShapeDtypeStruct(x_s.shape, F32)],
        scratch_shapes=[pltpu.VMEM((D_MODEL, D_FF), BF16), pltpu.VMEM((D_FF, D_MODEL), BF16),
                        pltpu.VMEM((2, FFN_STAGE_BYTES // (4 * D_FF), D_FF), F32),
                        pltpu.VMEM((2, FFN_STAGE_BYTES // (4 * D_MODEL), D_MODEL), F32),
                        pltpu.SemaphoreType.DMA((2,)), pltpu.SemaphoreType.DMA((2,))],
        compiler_params=_params(),
        name="ffn",
    )(x_p, x_s, g, w1, w2, gf)


def _sgu_front(x, g_ref, win_ref, sg_ref):
    h = _rms(x, g_ref[...]).astype(BF16)
    v = _rms(_gelu(_dot(h, win_ref[:, D_MODEL:])), sg_ref[...])
    return _gelu(_dot(h, win_ref[:, :D_MODEL])), v


def _rows_to_batch_major(x, nb):
    steps = x.shape[0] // nb
    return jnp.swapaxes(x.reshape(steps, nb, x.shape[1]), 0, 1).reshape(x.shape)


def _rows_to_time_major(x, nb):
    steps = x.shape[0] // nb
    return jnp.swapaxes(x.reshape(nb, steps, x.shape[1]), 0, 1).reshape(x.shape)


def _sgu_prompt_kernel(x_ref, g_ref, win_ref, sg_ref, w_ref, bias_ref, wout_ref, o_ref, y_s, *, nb, in_tm, out_tm):
    rows = nb * CHUNK
    if in_tm:
        x = _rows_to_batch_major(x_ref[...], nb)
    else:
        x = x_ref[...].reshape(rows, D_MODEL)
    u, v = _sgu_front(x, g_ref, win_ref, sg_ref)
    vb = v.astype(BF16)
    for b in range(nb):
        rs = slice(b * CHUNK, (b + 1) * CHUNK)
        for g in range(HEADS_A):
            cs = slice(g * HD_A, (g + 1) * HD_A)
            mixed = _dot(w_ref[g], vb[rs, cs]) + bias_ref[:, cs]
            y_s[rs, cs] = (u[rs, cs] * mixed).astype(BF16)
    o = x + _dot(y_s[...], wout_ref[...])
    if out_tm:
        o_ref[...] = _rows_to_time_major(o, nb)
    else:
        o_ref[...] = o.reshape(nb, CHUNK, D_MODEL)


def _sgu_sample_kernel(x_ref, g_ref, win_ref, sg_ref, wl_ref, bl_ref, wout_ref, o_ref, v_ref, *, nb, steps):
    x = x_ref[...]
    u, v = _sgu_front(x, g_ref, win_ref, sg_ref)
    v_ref[...] = v
    mixed = []
    for t in range(steps):
        m = None
        for s in range(t + 1):
            term = wl_ref[t * steps + s:t * steps + s + 1, :] * v[s * nb:(s + 1) * nb, :]
            m = term if m is None else m + term
        mixed.append(m + bl_ref[t:t + 1, :])
    y = (u * jnp.concatenate(mixed, axis=0)).astype(BF16)
    o_ref[...] = x + _dot(y, wout_ref[...])


def _sgu_prompt(x, p, nb, n_steps, in_tm, out_tm):
    rows = nb * CHUNK
    tm_spec = pl.BlockSpec((rows, D_MODEL), lambda i: (i, 0))
    bm_spec = pl.BlockSpec((nb, CHUNK, D_MODEL), lambda i: (0, i, 0))
    out_shape = (jax.ShapeDtypeStruct((nb * n_steps, D_MODEL), F32) if out_tm
                 else jax.ShapeDtypeStruct((nb, n_steps, D_MODEL), F32))
    return pl.pallas_call(
        functools.partial(_sgu_prompt_kernel, nb=nb, in_tm=in_tm, out_tm=out_tm),
        grid=(n_steps // CHUNK,),
        in_specs=[tm_spec if in_tm else bm_spec, _const_spec((1, D_MODEL)), _const_spec((D_MODEL, 2 * D_MODEL)),
                  _const_spec((1, D_MODEL)), _const_spec((HEADS_A, CHUNK, CHUNK)),
                  _const_spec((CHUNK, D_MODEL)), _const_spec((D_MODEL, D_MODEL))],
        out_specs=tm_spec if out_tm else bm_spec,
        out_shape=out_shape,
        scratch_shapes=[pltpu.VMEM((rows, D_MODEL), BF16)],
        compiler_params=_params(),
        name="sgu_prompt",
    )(x, p["g"], p["win"], p["sg"], p["w_tril"], p["bias"], p["wout"])


def _sgu_sample(x, p, nb, steps):
    tile = steps * nb
    tile_spec = pl.BlockSpec((tile, D_MODEL), lambda i: (0, 0))
    return pl.pallas_call(
        functools.partial(_sgu_sample_kernel, nb=nb, steps=steps),
        grid=(1,),
        in_specs=[tile_spec, _const_spec((1, D_MODEL)), _const_spec((D_MODEL, 2 * D_MODEL)),
                  _const_spec((1, D_MODEL)), _const_spec((steps * steps, D_MODEL)),
                  _const_spec((steps, D_MODEL)), _const_spec((D_MODEL, D_MODEL))],
        out_specs=[tile_spec, tile_spec],
        out_shape=[jax.ShapeDtypeStruct((tile, D_MODEL), F32), jax.ShapeDtypeStruct((tile, D_MODEL), F32)],
        compiler_params=_params(),
        name="sgu_sample",
    )(x, p["g"], p["win"], p["sg"], p["wl"], p["bl"], p["wout"])


def _rglru_kernel(x_ref, g_ref, win_ref, cw_ref, cb_ref, wa_ref, ba_ref, wx_ref, bx_ref, lam_ref, wout_ref,
                  conv0_ref, h0_ref, o_ref, conv_ref, h_ref, xext_s, a_s, b_s, y_s, *, nb, steps):
    rows = nb * steps
    halo = (CONV_W - 1) * nb

    @pl.when(pl.program_id(0) == 0)
    def _():
        conv_ref[...] = conv0_ref[...]
        h_ref[...] = h0_ref[...]

    def blk_cols(j):
        return slice(j * V7X_MXU_DIM, (j + 1) * V7X_MXU_DIM)

    x = x_ref[...]
    h = _rms(x, g_ref[...]).astype(BF16)
    for j in range(N_BLK):
        cols = blk_cols(j)
        xext, a_j, b_j = xext_s.at[j], a_s.at[j], b_s.at[j]
        xext[0:halo, :] = conv_ref[:, cols]
        xext[halo:halo + rows, :] = _dot(h, win_ref[:, blk_cols(N_BLK + j)])
        gate = _gelu(_dot(h, win_ref[:, cols]))
        conv = None
        for w in range(CONV_W):
            term = xext[w * nb:w * nb + rows, :] * cw_ref[w:w + 1, cols]
            conv = term if conv is None else conv + term
        conv_ref[:, cols] = xext[rows:rows + halo, :]
        xc = cb_ref[:, cols] + conv
        xcb = xc.astype(BF16)
        r = jax.nn.sigmoid(_dot(xcb, wa_ref[j]) + ba_ref[:, cols])
        ig = jax.nn.sigmoid(_dot(xcb, wx_ref[j]) + bx_ref[:, cols])
        neg_log_a = (LRU_C * r) * jax.nn.softplus(-lam_ref[:, cols])
        a = jnp.exp(-neg_log_a)
        a_j[...] = a
        w = jnp.tanh(neg_log_a) * (a * a + 1.0)
        mult = jnp.where(w > 0.0, w * lax.rsqrt(w), 0.0)
        b_j[...] = mult * (ig * xc)
        hcur = h_ref[:, cols]
        for t in range(steps):
            sl = slice(t * nb, (t + 1) * nb)
            hcur = a_j[sl, :] * hcur + b_j[sl, :]
            b_j[sl, :] = hcur
        h_ref[:, cols] = hcur
        y_s[:, cols] = (b_j[...] * gate).astype(BF16)
    o_ref[...] = x + _dot(y_s[...], wout_ref[...])


def _rglru(x, p, conv0, h0, nb, steps):
    rows = nb * steps
    halo = (CONV_W - 1) * nb
    row_spec = pl.BlockSpec((rows, D_MODEL), lambda i: (i, 0))
    vec = _const_spec((1, D_MODEL))
    blk = _const_spec((N_BLK, V7X_MXU_DIM, V7X_MXU_DIM))
    return pl.pallas_call(
        functools.partial(_rglru_kernel, nb=nb, steps=steps),
        grid=(x.shape[0] // rows,),
        in_specs=[row_spec, vec, _const_spec((D_MODEL, 2 * D_MODEL)), _const_spec((CONV_W, D_MODEL)), vec,
                  blk, vec, blk, vec, vec, _const_spec((D_MODEL, D_MODEL)),
                  _const_spec((halo, D_MODEL)), _const_spec((nb, D_MODEL))],
        out_specs=[row_spec, pl.BlockSpec((halo, D_MODEL), lambda i: (0, 0)),
                   pl.BlockSpec((nb, D_MODEL), lambda i: (0, 0))],
        out_shape=[jax.ShapeDtypeStruct(x.shape, F32), jax.ShapeDtypeStruct((halo, D_MODEL), F32),
                   jax.ShapeDtypeStruct((nb, D_MODEL), F32)],
        scratch_shapes=[pltpu.VMEM((N_BLK, rows + halo, V7X_MXU_DIM), F32), pltpu.VMEM((N_BLK, rows, V7X_MXU_DIM), F32),
                        pltpu.VMEM((N_BLK, rows, V7X_MXU_DIM), F32), pltpu.VMEM((rows, D_MODEL), BF16)],
        compiler_params=_params(),
        name="rglru",
    )(x, p["g"], p["win"], p["cw"], p["cb"], p["wa"], p["ba"], p["wx"], p["bx"], p["lam"], p["wout"], conv0, h0)


def _cmul(a_re, a_im, b_re, b_im):
    return a_re * b_re - a_im * b_im, a_re * b_im + a_im * b_re


def _s5_disc_kernel(lre_ref, lim_ref, ldt_ref, bre_ref, bim_ref, are_ref, aim_ref, bbre_ref, bbim_ref, pw_ref):
    lr, li = lre_ref[...], lim_ref[...]
    dt = jnp.exp(ldt_ref[...])
    mag = jnp.exp(lr * dt)
    ab_re, ab_im = mag * jnp.cos(li * dt), mag * jnp.sin(li * dt)
    zr, zi = ab_re - 1.0, ab_im
    den = lr * lr + li * li
    q_re = (zr * lr + zi * li) / den
    q_im = (zi * lr - zr * li) / den
    are_ref[...] = ab_re
    aim_ref[...] = ab_im
    br, bi = bre_ref[...], bim_ref[...]
    bbre_ref[...] = q_re[:, None, :] * br - q_im[:, None, :] * bi
    bbim_ref[...] = q_re[:, None, :] * bi + q_im[:, None, :] * br
    p_re, p_im = jnp.ones_like(ab_re), jnp.zeros_like(ab_re)
    for k in range(S5_L + 1):
        pw_ref[0, k] = p_re
        pw_ref[1, k] = p_im
        p_re, p_im = _cmul(p_re, p_im, ab_re, ab_im)


def _s5_disc(lam_re, lam_im, log_dt, b_re_t, b_im_t):
    gp = jax.ShapeDtypeStruct((G_C, P_C), F32)
    ghp = jax.ShapeDtypeStruct((G_C, GROUP_C, P_C), F32)
    pw = jax.ShapeDtypeStruct((2, S5_L + 1, G_C, P_C), F32)
    return pl.pallas_call(_s5_disc_kernel, out_shape=[gp, gp, ghp, ghp, pw], name="s5_disc")(
        lam_re, lam_im, log_dt.reshape(G_C, 1), b_re_t, b_im_t)


def _s5_taps_kernel(pw0_ref, pw1_ref, pwf_ref, c_ref, b_ref, kconv_ref, wst_ref, kin_ref, *, pairs):
    n = S5_L * GROUP_C
    pair_lanes = 2 * P_C

    def rep(t):
        return jnp.broadcast_to(t[:, None, :], (S5_L, GROUP_C, pair_lanes)).reshape(n, pair_lanes)

    def til(c):
        return jnp.broadcast_to(c[None, :, :], (S5_L, GROUP_C, pair_lanes)).reshape(n, pair_lanes)

    def dot_t(a, b):
        return lax.dot_general(a, b, (((1,), (1,)), ((), ())), precision=lax.Precision.HIGHEST,
                               preferred_element_type=F32)

    def first(shape, axis):
        return lax.broadcasted_iota(jnp.int32, shape, axis) < P_C

    lane = lax.broadcasted_iota(jnp.int32, (GROUP_C, n), 1)
    for q in range(pairs):
        cr, ci = til(c_ref[q, 0]), til(c_ref[q, 1])
        b_re, b_im = b_ref[q, 0], b_ref[q, 1]
        x_re, x_im = _cmul(rep(pw0_ref[q, 0]), rep(pw0_ref[q, 1]), cr, ci)
        g0 = first(b_re.shape, 1)
        b2 = jnp.concatenate(
            [jnp.concatenate([jnp.where(g0, b_re, 0.0), jnp.where(g0, -b_im, 0.0)], axis=1),
             jnp.concatenate([jnp.where(g0, 0.0, b_re), jnp.where(g0, 0.0, -b_im)], axis=1)], axis=0)
        mrows = dot_t(b2, jnp.concatenate([x_re, x_im], axis=1))
        for gi in range(2):
            mrow = mrows[gi * GROUP_C:(gi + 1) * GROUP_C]
            for s in range(S5_L):
                blk = mrow if s == 0 else jnp.where(lane >= s * GROUP_C, pltpu.roll(mrow, s * GROUP_C, 1), 0.0)
                kconv_ref[2 * q + gi, s * GROUP_C:(s + 1) * GROUP_C, :] = blk.astype(BF16)
        e_re, e_im = _cmul(rep(pw1_ref[q, 0]), rep(pw1_ref[q, 1]), cr, ci)
        et_re, et_im = e_re.T, (-e_im).T
        top = first(et_re.shape, 0)
        kin_ref[q] = jnp.concatenate(
            [jnp.concatenate([jnp.where(top, et_re, 0.0), jnp.where(top, 0.0, et_re)], axis=1),
             jnp.concatenate([jnp.where(top, et_im, 0.0), jnp.where(top, 0.0, et_im)], axis=1)], axis=0).astype(BF16)
        f_re, f_im = _cmul(rep(pwf_ref[q, 0]), rep(pwf_ref[q, 1]), til(b_re), til(b_im))
        left = first(f_re.shape, 1)
        wst_ref[q] = jnp.concatenate(
            [jnp.concatenate([jnp.where(left, f_re, 0.0), jnp.where(left, f_im, 0.0)], axis=1),
             jnp.concatenate([jnp.where(left, 0.0, f_re), jnp.where(left, 0.0, f_im)], axis=1)], axis=0).astype(BF16)


def _s5_taps(pw0, pw1, pwf, c, bb):
    pairs = 4
    n = S5_L * GROUP_C
    pspec = pl.BlockSpec((pairs, 2, S5_L, 2 * P_C), lambda i: (i, 0, 0, 0))
    hspec = pl.BlockSpec((pairs, 2, GROUP_C, 2 * P_C), lambda i: (i, 0, 0, 0))
    return pl.pallas_call(
        functools.partial(_s5_taps_kernel, pairs=pairs),
        grid=(G_C // (2 * pairs),),
        in_specs=[pspec, pspec, pspec, hspec, hspec],
        out_specs=[pl.BlockSpec((2 * pairs, n, n), lambda i: (i, 0, 0)),
                   pl.BlockSpec((pairs, 2 * n, 4 * P_C), lambda i: (i, 0, 0)),
                   pl.BlockSpec((pairs, 4 * P_C, 2 * n), lambda i: (i, 0, 0))],
        out_shape=[jax.ShapeDtypeStruct((G_C, n, n), BF16), jax.ShapeDtypeStruct((G_C // 2, 2 * n, 4 * P_C), BF16),
                   jax.ShapeDtypeStruct((G_C // 2, 4 * P_C, 2 * n), BF16)],
        compiler_params=_params(),
        name="s5_taps",
    )(pw0, pw1, pwf, c, bb)


def _pair_lanes(a):
    a2 = a.reshape((G_C // 2, 2) + a.shape[1:])
    return jnp.concatenate([a2[:, 0], a2[:, 1]], axis=-1)


def _granule_transpose(vs):
    vs = list(vs)
    granule = lax.broadcasted_iota(jnp.int32, vs[0].shape, 1) // GROUP_C
    d = 1
    while d < len(vs):
        keep = (granule & d) == 0
        for i in range(len(vs)):
            if i & d == 0:
                a, b = vs[i], vs[i + d]
                vs[i] = jnp.where(keep, a, pltpu.roll(b, d * GROUP_C, 1))
                vs[i + d] = jnp.where(keep, pltpu.roll(a, V7X_LANES - d * GROUP_C, 1), b)
        d *= 2
    return vs


def _s5_in_kernel(x_ref, g_ref, win_ref, u_ref, u2_ref, *, nb, blocks):
    h = _rms(x_ref[...], g_ref[...]).astype(BF16)
    rows2 = blocks * nb
    for cc in range(N_BLK):
        cols = slice(cc * V7X_MXU_DIM, (cc + 1) * V7X_MXU_DIM)
        u = _dot(h, win_ref[:, cols])
        u_ref[:, cols] = u
        u4 = u.reshape(blocks, S5_L, nb, V7X_MXU_DIM)
        for sub in range(V7X_MXU_DIM // V7X_LANES):
            c = cc * (V7X_MXU_DIM // V7X_LANES) + sub
            for half in range(S5_L // GRANULES_PER_VREG):
                vs = []
                for kk in range(GRANULES_PER_VREG):
                    slab = u4[:, half * GRANULES_PER_VREG + kk, :, sub * V7X_LANES:(sub + 1) * V7X_LANES]
                    vs.append(slab.reshape(rows2, V7X_LANES))
                ws = _granule_transpose(vs)
                for gi in range(GRANULES_PER_VREG):
                    lane0 = (c * GRANULES_PER_VREG + gi) * S5_L * GROUP_C + half * V7X_LANES
                    u2_ref[:, lane0:lane0 + V7X_LANES] = ws[gi].astype(BF16)


def _s5_in(x, g, win, nb, blocks):
    rows = blocks * S5_L * nb
    n_blocks = x.shape[0] // (S5_L * nb)
    row_spec = pl.BlockSpec((rows, D_MODEL), lambda i: (i, 0))
    return pl.pallas_call(
        functools.partial(_s5_in_kernel, nb=nb, blocks=blocks),
        grid=(x.shape[0] // rows,),
        in_specs=[row_spec, _const_spec((1, D_MODEL)), _const_spec((D_MODEL, D_MODEL))],
        out_specs=[row_spec, pl.BlockSpec((blocks * nb, S5_L * D_MODEL), lambda i: (i, 0))],
        out_shape=[jax.ShapeDtypeStruct(x.shape, F32),
                   jax.ShapeDtypeStruct((n_blocks * nb, S5_L * D_MODEL), BF16)],
        compiler_params=_params(),
        name="s5_in",
    )(x, g, win)


def _s5_core_kernel(u_ref, wst_ref, kconv_ref, kin_ref, a_ref, s0_ref, y_ref, sf_ref, sc_s, sin_s, *, nb, n_blocks):
    pw = 2 * S5_L * GROUP_C
    sw = 2 * P_C
    ub = [u_ref[:, q * pw:(q + 1) * pw] for q in range(S5_PAIRS_PER_STEP)]
    for q in range(S5_PAIRS_PER_STEP):
        sc_s[q] = _dot(ub[q], wst_ref[q])
    a_re = [jnp.broadcast_to(a_ref[:, 2 * sw * q:2 * sw * q + sw], (nb, sw)) for q in range(S5_PAIRS_PER_STEP)]
    a_im = [jnp.broadcast_to(a_ref[:, 2 * sw * q + sw:2 * sw * (q + 1)], (nb, sw)) for q in range(S5_PAIRS_PER_STEP)]
    s_re = [s0_ref[:, 2 * sw * q:2 * sw * q + sw] for q in range(S5_PAIRS_PER_STEP)]
    s_im = [s0_ref[:, 2 * sw * q + sw:2 * sw * (q + 1)] for q in range(S5_PAIRS_PER_STEP)]
    for blk in range(n_blocks):
        rs = slice(blk * nb, (blk + 1) * nb)
        for q in range(S5_PAIRS_PER_STEP):
            sin_s[q, rs, 0:sw] = s_re[q]
            sin_s[q, rs, sw:2 * sw] = s_im[q]
            n_re, n_im = _cmul(a_re[q], a_im[q], s_re[q], s_im[q])
            s_re[q], s_im[q] = n_re + sc_s[q, rs, 0:sw], n_im + sc_s[q, rs, sw:2 * sw]
    for q in range(S5_PAIRS_PER_STEP):
        sf_ref[:, 2 * sw * q:2 * sw * q + sw] = s_re[q]
        sf_ref[:, 2 * sw * q + sw:2 * sw * (q + 1)] = s_im[q]
        half = S5_L * GROUP_C
        conv = jnp.concatenate([_dot(ub[q][:, :half], kconv_ref[2 * q]),
                                _dot(ub[q][:, half:], kconv_ref[2 * q + 1])], axis=1)
        y_ref[:, q * pw:(q + 1) * pw] = conv + _dot(sin_s[q].astype(BF16), kin_ref[q])


def _s5_core(u2, wst, kconv, kin, a_pair, s0, nb):
    rows = u2.shape[0]
    pw = 2 * S5_L * GROUP_C
    sw = 2 * P_C
    cols = S5_PAIRS_PER_STEP * pw
    scols = S5_PAIRS_PER_STEP * 2 * sw
    return pl.pallas_call(
        functools.partial(_s5_core_kernel, nb=nb, n_blocks=rows // nb),
        grid=(u2.shape[1] // cols,),
        in_specs=[pl.BlockSpec((rows, cols), lambda i: (0, i)),
                  pl.BlockSpec((S5_PAIRS_PER_STEP, pw, 2 * sw), lambda i: (i, 0, 0)),
                  pl.BlockSpec((2 * S5_PAIRS_PER_STEP, pw // 2, pw // 2), lambda i: (i, 0, 0)),
                  pl.BlockSpec((S5_PAIRS_PER_STEP, 2 * sw, pw), lambda i: (i, 0, 0)),
                  pl.BlockSpec((1, scols), lambda i: (0, i)), pl.BlockSpec((nb, scols), lambda i: (0, i))],
        out_specs=[pl.BlockSpec((rows, cols), lambda i: (0, i)), pl.BlockSpec((nb, scols), lambda i: (0, i))],
        out_shape=[jax.ShapeDtypeStruct(u2.shape, F32), jax.ShapeDtypeStruct((nb, 2 * D_STATE), F32)],
        scratch_shapes=[pltpu.VMEM((S5_PAIRS_PER_STEP, rows, 2 * sw), F32),
                        pltpu.VMEM((S5_PAIRS_PER_STEP, rows, 2 * sw), F32)],
        compiler_params=_params(),
        name="s5_core",
    )(u2, wst, kconv, kin, a_pair, s0)


def _s5_out_kernel(x_ref, u_ref, y2_ref, dskip_ref, wglu_ref, o_ref, y_s, gate_s, *, nb, blocks):
    rows = blocks * S5_L * nb
    for cc in range(N_BLK):
        cols = slice(cc * V7X_MXU_DIM, (cc + 1) * V7X_MXU_DIM)
        for sub in range(V7X_MXU_DIM // V7X_LANES):
            c = cc * (V7X_MXU_DIM // V7X_LANES) + sub
            for half in range(S5_L // GRANULES_PER_VREG):
                lanes = [(c * GRANULES_PER_VREG + gi) * S5_L * GROUP_C + half * V7X_LANES
                         for gi in range(GRANULES_PER_VREG)]
                ws = _granule_transpose([y2_ref[:, l0:l0 + V7X_LANES] for l0 in lanes])
                for kk in range(GRANULES_PER_VREG):
                    y_s[cc, :, half * GRANULES_PER_VREG + kk, :, sub * V7X_LANES:(sub + 1) * V7X_LANES] = (
                        ws[kk].reshape(blocks, nb, V7X_LANES))
        y = y_s[cc].reshape(rows, V7X_MXU_DIM) + dskip_ref[:, cols] * u_ref[:, cols]
        gy = _gelu(y).astype(BF16)
        for acc, wcols in ((o_ref, slice(0, D_MODEL)), (gate_s, slice(D_MODEL, 2 * D_MODEL))):
            part = _dot(gy, wglu_ref[cols, wcols])
            if cc == 0:
                acc[...] = part
            else:
                acc[...] += part
    o_ref[...] = x_ref[...] + o_ref[...] * jax.nn.sigmoid(gate_s[...])


def _s5_out(x, u, y2, dskip, wglu, nb, blocks):
    rows = blocks * S5_L * nb
    row_spec = pl.BlockSpec((rows, D_MODEL), lambda i: (i, 0))
    return pl.pallas_call(
        functools.partial(_s5_out_kernel, nb=nb, blocks=blocks),
        grid=(x.shape[0] // rows,),
        in_specs=[row_spec, row_spec, pl.BlockSpec((blocks * nb, S5_L * D_MODEL), lambda i: (i, 0)),
                  _const_spec((1, D_MODEL)), _const_spec((D_MODEL, 2 * D_MODEL))],
        out_specs=row_spec,
        out_shape=jax.ShapeDtypeStruct(x.shape, F32),
        scratch_shapes=[pltpu.VMEM((N_BLK, blocks, S5_L, nb, V7X_MXU_DIM), F32), pltpu.VMEM((rows, D_MODEL), F32)],
        compiler_params=_params(),
        name="s5_out",
    )(x, u, y2, dskip, wglu)


def _pair_state(re, im):
    lead = re.shape[:-2]
    r = re.reshape(lead + (G_C // 2, 2 * P_C))
    i = im.reshape(lead + (G_C // 2, 2 * P_C))
    return jnp.concatenate([r, i], axis=-1).reshape(lead + (2 * D_STATE,))


def _unpair_state(s, nb):
    s4 = s.reshape(nb, G_C // 2, 2, 2 * P_C)
    return s4[:, :, 0].reshape(nb, G_C, P_C), s4[:, :, 1].reshape(nb, G_C, P_C)


def _s5_short_kernel(x_ref, g_ref, win_ref, kconv_ref, wsta_ref, wstb_ref, kina_ref, kinb_ref, a_ref, dskip_ref,
                     wglu_ref, s0_ref, o_ref, sf_ref, u8_s, y8_s, y_s, *, nb, steps):
    slots = GRANULES_PER_VREG
    gl = slots * GROUP_C
    sw = 2 * P_C
    x = x_ref[...]
    u = _dot(_rms(x, g_ref[...]).astype(BF16), win_ref[...])
    zero = jnp.zeros((nb, V7X_LANES), F32)
    for c in range(D_MODEL // V7X_LANES):
        cols = slice(c * V7X_LANES, (c + 1) * V7X_LANES)
        ws = _granule_transpose([u[k * nb:(k + 1) * nb, cols] for k in range(steps)] + [zero] * (slots - steps))
        for gi in range(GRANULES_PER_VREG):
            g = c * GRANULES_PER_VREG + gi
            u8_s[:, g * gl:(g + 1) * gl] = ws[gi].astype(BF16)
    for q in range(G_C // 2):
        up = u8_s[:, 2 * q * gl:(2 * q + 2) * gl]
        reps = slots // steps
        sc = _dot(up, jnp.concatenate([wsta_ref[q]] * reps + [wstb_ref[q]] * reps, axis=0))
        s_re, s_im = s0_ref[:, 2 * sw * q:2 * sw * q + sw], s0_ref[:, 2 * sw * q + sw:2 * sw * (q + 1)]
        n_re, n_im = _cmul(a_ref[:, 2 * sw * q:2 * sw * q + sw], a_ref[:, 2 * sw * q + sw:2 * sw * (q + 1)], s_re, s_im)
        sf_ref[:, 2 * sw * q:2 * sw * q + sw] = n_re + sc[:, :sw]
        sf_ref[:, 2 * sw * q + sw:2 * sw * (q + 1)] = n_im + sc[:, sw:]
        yk = _dot(jnp.concatenate([s_re, s_im], axis=1).astype(BF16),
                  jnp.concatenate([kina_ref[q], kinb_ref[q]], axis=1))
        for gi in range(2):
            y8_s[:, (2 * q + gi) * gl:(2 * q + gi + 1) * gl] = (
                _dot(up[:, gi * gl:(gi + 1) * gl], kconv_ref[2 * q + gi]) + yk[:, gi * gl:(gi + 1) * gl])
    for c in range(D_MODEL // V7X_LANES):
        ws = _granule_transpose([y8_s[:, (c * GRANULES_PER_VREG + gi) * gl:(c * GRANULES_PER_VREG + gi + 1) * gl]
                                 for gi in range(GRANULES_PER_VREG)])
        for k in range(steps):
            y_s[k * nb:(k + 1) * nb, c * V7X_LANES:(c + 1) * V7X_LANES] = ws[k]
    y = y_s[...] + dskip_ref[...] * u
    o = _dot(_gelu(y).astype(BF16), wglu_ref[...])
    o_ref[...] = x + o[:, :D_MODEL] * jax.nn.sigmoid(o[:, D_MODEL:])


def _s5_short(x, g, win, kconv, wst, kin, a_pair, dskip, wglu, s0, nb, steps):
    assert S5_L % steps == 0 and GRANULES_PER_VREG % steps == 0
    rows = nb * steps
    slots = GRANULES_PER_VREG
    gl = slots * GROUP_C
    n = S5_L * GROUP_C
    srows = steps * GROUP_C
    row_spec = pl.BlockSpec((rows, D_MODEL), lambda i: (0, 0))
    state_spec = pl.BlockSpec((nb, 2 * D_STATE), lambda i: (0, 0))
    return pl.pallas_call(
        functools.partial(_s5_short_kernel, nb=nb, steps=steps),
        grid=(1,),
        in_specs=[row_spec, _const_spec((1, D_MODEL)), _const_spec((D_MODEL, D_MODEL)),
                  pl.BlockSpec((G_C, gl, gl), lambda i: (0, 0, 0)),
                  pl.BlockSpec((G_C // 2, srows, 4 * P_C), lambda i: (0, n // srows - 1, 0)),
                  pl.BlockSpec((G_C // 2, srows, 4 * P_C), lambda i: (0, 2 * n // srows - 1, 0)),
                  pl.BlockSpec((G_C // 2, 4 * P_C, gl), lambda i: (0, 0, 0)),
                  pl.BlockSpec((G_C // 2, 4 * P_C, gl), lambda i: (0, 0, n // gl)),
                  _const_spec((1, 2 * D_STATE)), _const_spec((1, D_MODEL)), _const_spec((D_MODEL, 2 * D_MODEL)),
                  state_spec],
        out_specs=[row_spec, state_spec],
        out_shape=[jax.ShapeDtypeStruct(x.shape, F32), jax.ShapeDtypeStruct((nb, 2 * D_STATE), F32)],
        scratch_shapes=[pltpu.VMEM((nb, G_C * gl), BF16), pltpu.VMEM((nb, G_C * gl), F32),
                        pltpu.VMEM((rows, D_MODEL), F32)],
        compiler_params=_params(),
        name="s5_short",
    )(x, g, win, kconv, wst, wst, kin, kin, a_pair, dskip, wglu, s0)


def _block_diag(w, n_per_blk):
    n, k_in, k_out = w.shape
    wb = w.reshape(n // n_per_blk, n_per_blk, k_in, k_out)
    eye = jnp.eye(n_per_blk, dtype=w.dtype)
    out = jnp.einsum("jgio,gk->jgiko", wb, eye)
    return out.reshape(n // n_per_blk, n_per_blk * k_in, n_per_blk * k_out)


def _to_rows(a):
    return jnp.swapaxes(a, 0, 1).reshape(a.shape[0] * a.shape[1], a.shape[2])


def _from_rows(a, nb):
    return jnp.swapaxes(a.reshape(a.shape[0] // nb, nb, a.shape[1]), 0, 1)


def kernel(x_prompt, x_sample, state_rglru_conv, state_rglru_h, state_s5_re, state_s5_im, norm_mix, norm_ffn, norm_f, w_ff1, w_ff2, w_in_a, sgu_g, w_s, b_s, w_out_a, w_in_b, conv_w, conv_b, w_a, b_a, w_x, b_x, lam, w_out_b, w_in_c, lam_re, lam_im, log_dt, b_re, b_im, c_re, c_im, d_skip, w_glu):
    bp, tp, _ = x_prompt.shape
    bs, ts, _ = x_sample.shape
    rows_p, rows_s = bp * tp, bs * ts
    assert DEPTH % N_MIXERS == 1 and DEPTH > 1, "first and last layers must be SGU layers"
    assert tp % CHUNK == 0 and ts < CHUNK
    assert rows_s % FFN_SAMPLE_ROWS == 0 and rows_p % FFN_ROWS == 0

    row = lambda v: v.reshape(1, -1)
    norm_ffn3 = norm_ffn.reshape(DEPTH, 1, D_MODEL)
    w1, w2, gf = w_ff1, w_ff2, row(norm_f)

    x_p, x_s = x_prompt, _to_rows(x_sample)
    outs_v, conv_p, h_p, conv_s, h_s, sre_p, sim_p, sre_s, sim_s = [], [], [], [], [], [], [], [], []
    for layer in range(DEPTH):
        j, kind = layer // N_MIXERS, layer % N_MIXERS
        first, last = layer == 0, layer == DEPTH - 1
        g = row(norm_mix[layer])
        if kind == 0:
            p = dict(g=g, win=w_in_a[j].astype(BF16), wout=w_out_a[j].astype(BF16), sg=row(sgu_g[j]),
                     w_tril=jnp.where(jnp.tril(jnp.ones((CHUNK, CHUNK), dtype=bool)), w_s[j], 0.0).astype(BF16),
                     bias=jnp.repeat(b_s[j].T, HD_A, axis=1),
                     wl=jnp.repeat(w_s[j][:, :ts, :ts].reshape(HEADS_A, ts * ts).T, HD_A, axis=1),
                     bl=jnp.repeat(b_s[j][:, :ts].T, HD_A, axis=1))
            if first:
                x_p = _sgu_prompt(x_p, p, bp, tp, False, True)
            elif last:
                x_p = _sgu_prompt(x_p, p, bp, tp, True, False).reshape(rows_p, D_MODEL)
            else:
                raise NotImplementedError("interior SGU layers")
            x_s, v = _sgu_sample(x_s, p, bs, ts)
            outs_v.append(_from_rows(v, bs))
        elif kind == 1:
            p = dict(g=g, win=w_in_b[j].astype(BF16), cw=conv_w[j], cb=row(conv_b[j]),
                     wa=_block_diag(w_a[j], HEADS_PER_BLK).astype(BF16), ba=row(b_a[j]),
                     wx=_block_diag(w_x[j], HEADS_PER_BLK).astype(BF16), bx=row(b_x[j]),
                     lam=row(lam[j]), wout=w_out_b[j].astype(BF16))
            dt_s = state_rglru_h.dtype
            x_p, cp, hp = _rglru(x_p, p, jnp.zeros(((CONV_W - 1) * bp, D_MODEL), dt_s),
                                 jnp.zeros((bp, D_MODEL), dt_s), bp, RGLRU_PROMPT_STEPS)
            x_s, cs, hs = _rglru(x_s, p, _to_rows(state_rglru_conv[j]), state_rglru_h[j], bs, ts)
            conv_p.append(_from_rows(cp, bp)); h_p.append(hp)
            conv_s.append(_from_rows(cs, bs)); h_s.append(hs)
        else:
            are, aim, bbre, bbim, pw = _s5_disc(lam_re[j], lam_im[j], log_dt[j],
                                                jnp.swapaxes(b_re[j], 1, 2), jnp.swapaxes(b_im[j], 1, 2))
            p = dict(g=g, win=w_in_c[j].astype(BF16), dskip=row(d_skip[j]), wglu=w_glu[j].astype(BF16))
            pwg = _pair_lanes(jnp.transpose(pw, (2, 0, 1, 3)))
            kconv, wst, kin = _s5_taps(pwg[:, :, :S5_L], pwg[:, :, 1:], pwg[:, :, S5_L - 1::-1],
                                       _pair_lanes(jnp.stack([c_re[j], c_im[j]], axis=1)),
                                       _pair_lanes(jnp.stack([bbre, bbim], axis=1)))
            zs = jnp.zeros((bp, G_C, P_C), state_s5_re.dtype)
            u, u2 = _s5_in(x_p, g, p["win"], bp, S5_TILE_BLOCKS)
            y2, sf = _s5_core(u2, wst, kconv, kin, _pair_state(pw[0, S5_L], pw[1, S5_L]).reshape(1, -1),
                              _pair_state(zs, zs), bp)
            x_p = _s5_out(x_p, u, y2, p["dskip"], p["wglu"], bp, S5_TILE_BLOCKS)
            rp, ip = _unpair_state(sf, bp)
            x_s, sf_s = _s5_short(x_s, g, p["win"], kconv, wst, kin,
                                  _pair_state(pw[0, ts], pw[1, ts]).reshape(1, -1), p["dskip"], p["wglu"],
                                  _pair_state(state_s5_re[j], state_s5_im[j]), bs, ts)
            rs, is_ = _unpair_state(sf_s, bs)
            sre_p.append(rp); sim_p.append(ip)
            sre_s.append(rs); sim_s.append(is_)
        x_p, x_s = _ffn(x_p, x_s, norm_ffn3, w1, w2, gf, layer, last)

    y_prompt = x_p.reshape(bp, tp, D_MODEL)
    y_sample = _from_rows(x_s, bs)
    return (y_prompt, y_sample, jnp.stack(outs_v), jnp.stack(conv_p), jnp.stack(h_p), jnp.stack(conv_s),
            jnp.stack(h_s), jnp.stack(sre_p), jnp.stack(sim_p), jnp.stack(sre_s), jnp.stack(sim_s))
```

```python
import functools

import jax
import jax.numpy as jnp
from jax import lax
from jax.experimental import pallas as pl
from jax.experimental.pallas import tpu as pltpu

F32 = jnp.float32
BF16 = jnp.bfloat16

D_MODEL = 1024
DEPTH = 4
N_MIXERS = 3
EPS = 1e-6
CHUNK = 128
HEADS_A = 8
HD_A = D_MODEL // HEADS_A
HEADS_B = 16
HD_B = D_MODEL // HEADS_B
CONV_W = 4
LRU_C = 8.0
GROUP_C = 16
G_C = D_MODEL // GROUP_C
P_C = 64
D_STATE = G_C * P_C
D_FF = 4 * D_MODEL

V7X_LANES = 128
V7X_MXU_DIM = 256
V7X_VMEM_BYTES = 64 * 1024 * 1024
VMEM_LIMIT = V7X_VMEM_BYTES - 8 * 1024 * 1024

N_BLK = D_MODEL // V7X_MXU_DIM
HEADS_PER_BLK = V7X_MXU_DIM // HD_B

FFN_ROWS = 1024
FFN_SAMPLE_ROWS = 512
FFN_COL_CHUNK = 1024
FFN_STAGE_BYTES = 1024 * 1024
S5_L = 16
S5_TILE_BLOCKS = 8
S5_PAIRS_PER_STEP = 4
GRANULES_PER_VREG = V7X_LANES // GROUP_C
RGLRU_PROMPT_STEPS = 128


def _rms(x, g):
    return (x * lax.rsqrt(jnp.mean(x * x, axis=-1, keepdims=True) + EPS)) * g


GELU_C0 = 0.7978845608028654
GELU_C1 = GELU_C0 * 0.044715


def _gelu(x):
    return x * (0.5 + 0.5 * jnp.tanh(x * (GELU_C0 + GELU_C1 * (x * x))))


def _dot(a, b):
    return jnp.dot(a, b, preferred_element_type=F32)


def _const_spec(shape):
    zeros = (0,) * len(shape)
    return pl.BlockSpec(shape, lambda i: zeros, pipeline_mode=pl.Buffered(1))


def _layer_spec(shape, layer):
    idx = (layer,) + (0,) * len(shape)
    return pl.BlockSpec((None,) + tuple(shape), lambda i: idx, pipeline_mode=pl.Buffered(1))


def _params():
    return pltpu.CompilerParams(dimension_semantics=("arbitrary",), vmem_limit_bytes=VMEM_LIMIT)


def _ffn_tile(x_ref, g_ref, w1_ref, w2_ref, gf_ref, o_ref, final_norm):
    h = _rms(x_ref[...], g_ref[...]).astype(BF16)
    for j in range(D_FF // FFN_COL_CHUNK):
        cols = slice(j * FFN_COL_CHUNK, (j + 1) * FFN_COL_CHUNK)
        a = jnp.square(jnp.maximum(_dot(h, w1_ref[:, cols]), 0.0)).astype(BF16)
        part = _dot(a, w2_ref[cols, :])
        if j == 0:
            o_ref[...] = part
        else:
            o_ref[...] += part
    y = x_ref[...] + o_ref[...]
    if final_norm:
        y = _rms(y, gf_ref[...])
    o_ref[...] = y


def _load_as_bf16(jobs):
    def copies_of(src, stage, sem):
        rows = stage.shape[1]
        return [pltpu.make_async_copy(src.at[pl.ds(k * rows, rows)], stage.at[k % 2], sem.at[k % 2])
                for k in range(src.shape[0] // rows)]

    plans = [copies_of(src, stage, sem) for src, _, stage, sem in jobs]
    n = len(plans[0])
    assert all(len(p) == n for p in plans)
    for p in plans:
        p[0].start()
    for k in range(n):
        if k + 1 < n:
            for p in plans:
                p[k + 1].start()
        for p, (_, dst, stage, _) in zip(plans, jobs):
            p[k].wait()
            rows = stage.shape[1]
            dst[k * rows:(k + 1) * rows, :] = stage[k % 2].astype(BF16)


def _ffn_kernel(xp_ref, xs_ref, g_ref, w1_hbm, w2_hbm, gf_ref, op_ref, os_ref, w1_s, w2_s, stage1, stage2,
                sem1, sem2, *, final_norm, n_p, layer):
    i = pl.program_id(0)

    @pl.when(i == 0)
    def _():
        _load_as_bf16([(w1_hbm.at[layer], w1_s, stage1, sem1), (w2_hbm.at[layer], w2_s, stage2, sem2)])

    @pl.when(i < n_p)
    def _():
        _ffn_tile(xp_ref, g_ref, w1_s, w2_s, gf_ref, op_ref, final_norm)

    @pl.when(i >= n_p)
    def _():
        _ffn_tile(xs_ref, g_ref, w1_s, w2_s, gf_ref, os_ref, final_norm)


def _ffn(x_p, x_s, g, w1, w2, gf, layer, final_norm):
    n_p, n_s = x_p.shape[0] // FFN_ROWS, x_s.shape[0] // FFN_SAMPLE_ROWS
    p_spec = pl.BlockSpec((FFN_ROWS, D_MODEL), lambda i: (jnp.minimum(i, n_p - 1), 0))
    s_spec = pl.BlockSpec((FFN_SAMPLE_ROWS, D_MODEL), lambda i: (jnp.maximum(i - n_p, 0), 0))
    hbm = pl.BlockSpec(memory_space=pl.ANY)
    return pl.pallas_call(
        functools.partial(_ffn_kernel, final_norm=final_norm, n_p=n_p, layer=layer),
        grid=(n_p + n_s,),
        in_specs=[p_spec, s_spec, _layer_spec((1, D_MODEL), layer), hbm, hbm, _const_spec((1, D_MODEL))],
        out_specs=[p_spec, s_spec],
        out_shape=[jax.ShapeDtypeStruct(x_p.shape, F32), jax.ShapeDtypeStruct(x_s.shape, F32)],
        scratch_shapes=[pltpu.VMEM((D_MODEL, D_FF), BF16), pltpu.VMEM((D_FF, D_MODEL), BF16),
                        pltpu.VMEM((2, FFN_STAGE_BYTES // (4 * D_FF), D_FF), F32),
                        pltpu.VMEM((2, FFN_STAGE_BYTES // (4 * D_MODEL), D_MODEL), F32),
                        pltpu.SemaphoreType.DMA((2,)), pltpu.SemaphoreType.DMA((2,))],
        compiler_params=_params(),
        name="ffn",
    )(x_p, x_s, g, w1, w2, gf)


def _sgu_front(x, g_ref, win_ref, sg_ref):
    h = _rms(x, g_ref[...]).astype(BF16)
    v = _rms(_gelu(_dot(h, win_ref[:, D_MODEL:])), sg_ref[...])
    return _gelu(_dot(h, win_ref[:, :D_MODEL])), v


def _rows_to_batch_major(x, nb):
    steps = x.shape[0] // nb
    return jnp.swapaxes(x.reshape(steps, nb, x.shape[1]), 0, 1).reshape(x.shape)


def _rows_to_time_major(x, nb):
    steps = x.shape[0] // nb
    return jnp.swapaxes(x.reshape(nb, steps, x.shape[1]), 0, 1).reshape(x.shape)


def _sgu_prompt_kernel(x_ref, g_ref, win_ref, sg_ref, w_ref, bias_ref, wout_ref, o_ref, y_s, *, nb, in_tm, out_tm):
    rows = nb * CHUNK
    if in_tm:
        x = _rows_to_batch_major(x_ref[...], nb)
    else:
        x = x_ref[...].reshape(rows, D_MODEL)
    u, v = _sgu_front(x, g_ref, win_ref, sg_ref)
    vb = v.astype(BF16)
    for b in range(nb):
        rs = slice(b * CHUNK, (b + 1) * CHUNK)
        for g in range(HEADS_A):
            cs = slice(g * HD_A, (g + 1) * HD_A)
            mixed = _dot(w_ref[g], vb[rs, cs]) + bias_ref[:, cs]
            y_s[rs, cs] = (u[rs, cs] * mixed).astype(BF16)
    o = x + _dot(y_s[...], wout_ref[...])
    if out_tm:
        o_ref[...] = _rows_to_time_major(o, nb)
    else:
        o_ref[...] = o.reshape(nb, CHUNK, D_MODEL)


def _sgu_sample_kernel(x_ref, g_ref, win_ref, sg_ref, wl_ref, bl_ref, wout_ref, o_ref, v_ref, *, nb, steps):
    x = x_ref[...]
    u, v = _sgu_front(x, g_ref, win_ref, sg_ref)
    v_ref[...] = v
    mixed = []
    for t in range(steps):
        m = None
        for s in range(t + 1):
            term = wl_ref[t * steps + s:t * steps + s + 1, :] * v[s * nb:(s + 1) * nb, :]
            m = term if m is None else m + term
        mixed.append(m + bl_ref[t:t + 1, :])
    y = (u * jnp.concatenate(mixed, axis=0)).astype(BF16)
    o_ref[...] = x + _dot(y, wout_ref[...])


def _sgu_prompt(x, p, nb, n_steps, in_tm, out_tm):
    rows = nb * CHUNK
    tm_spec = pl.BlockSpec((rows, D_MODEL), lambda i: (i, 0))
    bm_spec = pl.BlockSpec((nb, CHUNK, D_MODEL), lambda i: (0, i, 0))
    out_shape = (jax.ShapeDtypeStruct((nb * n_steps, D_MODEL), F32) if out_tm
                 else jax.ShapeDtypeStruct((nb, n_steps, D_MODEL), F32))
    return pl.pallas_call(
        functools.partial(_sgu_prompt_kernel, nb=nb, in_tm=in_tm, out_tm=out_tm),
        grid=(n_steps // CHUNK,),
        in_specs=[tm_spec if in_tm else bm_spec, _const_spec((1, D_MODEL)), _const_spec((D_MODEL, 2 * D_MODEL)),
                  _const_spec((1, D_MODEL)), _const_spec((HEADS_A, CHUNK, CHUNK)),
                  _const_spec((CHUNK, D_MODEL)), _const_spec((D_MODEL, D_MODEL))],
        out_specs=tm_spec if out_tm else bm_spec,
        out_shape=out_shape,
        scratch_shapes=[pltpu.VMEM((rows, D_MODEL), BF16)],
        compiler_params=_params(),
        name="sgu_prompt",
    )(x, p["g"], p["win"], p["sg"], p["w_tril"], p["bias"], p["wout"])


def _sgu_sample(x, p, nb, steps):
    tile = steps * nb
    tile_spec = pl.BlockSpec((tile, D_MODEL), lambda i: (0, 0))
    return pl.pallas_call(
        functools.partial(_sgu_sample_kernel, nb=nb, steps=steps),
        grid=(1,),
        in_specs=[tile_spec, _const_spec((1, D_MODEL)), _const_spec((D_MODEL, 2 * D_MODEL)),
                  _const_spec((1, D_MODEL)), _const_spec((steps * steps, D_MODEL)),
                  _const_spec((steps, D_MODEL)), _const_spec((D_MODEL, D_MODEL))],
        out_specs=[tile_spec, tile_spec],
        out_shape=[jax.ShapeDtypeStruct((tile, D_MODEL), F32), jax.ShapeDtypeStruct((tile, D_MODEL), F32)],
        compiler_params=_params(),
        name="sgu_sample",
    )(x, p["g"], p["win"], p["sg"], p["wl"], p["bl"], p["wout"])


def _rglru_kernel(x_ref, g_ref, win_ref, cw_ref, cb_ref, wa_ref, ba_ref, wx_ref, bx_ref, lam_ref, wout_ref,
                  conv0_ref, h0_ref, o_ref, conv_ref, h_ref, xext_s, a_s, b_s, y_s, *, nb, steps):
    rows = nb * steps
    halo = (CONV_W - 1) * nb

    @pl.when(pl.program_id(0) == 0)
    def _():
        conv_ref[...] = conv0_ref[...]
        h_ref[...] = h0_ref[...]

    def blk_cols(j):
        return slice(j * V7X_MXU_DIM, (j + 1) * V7X_MXU_DIM)

    x = x_ref[...]
    h = _rms(x, g_ref[...]).astype(BF16)
    for j in range(N_BLK):
        cols = blk_cols(j)
        xext, a_j, b_j = xext_s.at[j], a_s.at[j], b_s.at[j]
        xext[0:halo, :] = conv_ref[:, cols]
        xext[halo:halo + rows, :] = _dot(h, win_ref[:, blk_cols(N_BLK + j)])
        gate = _gelu(_dot(h, win_ref[:, cols]))
        conv = None
        for w in range(CONV_W):
            term = xext[w * nb:w * nb + rows, :] * cw_ref[w:w + 1, cols]
            conv = term if conv is None else conv + term
        conv_ref[:, cols] = xext[rows:rows + halo, :]
        xc = cb_ref[:, cols] + conv
        xcb = xc.astype(BF16)
        r = jax.nn.sigmoid(_dot(xcb, wa_ref[j]) + ba_ref[:, cols])
        ig = jax.nn.sigmoid(_dot(xcb, wx_ref[j]) + bx_ref[:, cols])
        neg_log_a = (LRU_C * r) * jax.nn.softplus(-lam_ref[:, cols])
        a = jnp.exp(-neg_log_a)
        a_j[...] = a
        w = jnp.tanh(neg_log_a) * (a * a + 1.0)
        mult = jnp.where(w > 0.0, w * lax.rsqrt(w), 0.0)
        b_j[...] = mult * (ig * xc)
        hcur = h_ref[:, cols]
        for t in range(steps):
            sl = slice(t * nb, (t + 1) * nb)
            hcur = a_j[sl, :] * hcur + b_j[sl, :]
            b_j[sl, :] = hcur
        h_ref[:, cols] = hcur
        y_s[:, cols] = (b_j[...] * gate).astype(BF16)
    o_ref[...] = x + _dot(y_s[...], wout_ref[...])


def _rglru(x, p, conv0, h0, nb, steps):
    rows = nb * steps
    halo = (CONV_W - 1) * nb
    row_spec = pl.BlockSpec((rows, D_MODEL), lambda i: (i, 0))
    vec = _const_spec((1, D_MODEL))
    blk = _const_spec((N_BLK, V7X_MXU_DIM, V7X_MXU_DIM))
    return pl.pallas_call(
        functools.partial(_rglru_kernel, nb=nb, steps=steps),
        grid=(x.shape[0] // rows,),
        in_specs=[row_spec, vec, _const_spec((D_MODEL, 2 * D_MODEL)), _const_spec((CONV_W, D_MODEL)), vec,
                  blk, vec, blk, vec, vec, _const_spec((D_MODEL, D_MODEL)),
                  _const_spec((halo, D_MODEL)), _const_spec((nb, D_MODEL))],
        out_specs=[row_spec, pl.BlockSpec((halo, D_MODEL), lambda i: (0, 0)),
                   pl.BlockSpec((nb, D_MODEL), lambda i: (0, 0))],
        out_shape=[jax.ShapeDtypeStruct(x.shape, F32), jax.ShapeDtypeStruct((halo, D_MODEL), F32),
                   jax.ShapeDtypeStruct((nb, D_MODEL), F32)],
        scratch_shapes=[pltpu.VMEM((N_BLK, rows + halo, V7X_MXU_DIM), F32), pltpu.VMEM((N_BLK, rows, V7X_MXU_DIM), F32),
                        pltpu.VMEM((N_BLK, rows, V7X_MXU_DIM), F32), pltpu.VMEM((rows, D_MODEL), BF16)],
        compiler_params=_params(),
        name="rglru",
    )(x, p["g"], p["win"], p["cw"], p["cb"], p["wa"], p["ba"], p["wx"], p["bx"], p["lam"], p["wout"], conv0, h0)


def _cmul(a_re, a_im, b_re, b_im):
    return a_re * b_re - a_im * b_im, a_re * b_im + a_im * b_re


def _s5_disc_kernel(lre_ref, lim_ref, ldt_ref, bre_ref, bim_ref, bbre_ref, bbim_ref, pw_ref):
    lr, li = lre_ref[...], lim_ref[...]
    dt = jnp.exp(ldt_ref[...])
    mag = jnp.exp(lr * dt)
    ab_re, ab_im = mag * jnp.cos(li * dt), mag * jnp.sin(li * dt)
    zr, zi = ab_re - 1.0, ab_im
    den = lr * lr + li * li
    q_re = (zr * lr + zi * li) / den
    q_im = (zi * lr - zr * li) / den
    br, bi = bre_ref[...], bim_ref[...]
    bbre_ref[...] = q_re[:, None, :] * br - q_im[:, None, :] * bi
    bbim_ref[...] = q_re[:, None, :] * bi + q_im[:, None, :] * br
    p_re, p_im = jnp.ones_like(ab_re), jnp.zeros_like(ab_re)
    for k in range(S5_L + 1):
        pw_ref[0, k] = p_re
        pw_ref[1, k] = p_im
        p_re, p_im = _cmul(p_re, p_im, ab_re, ab_im)


def _s5_disc(lam_re, lam_im, log_dt, b_re_t, b_im_t):
    ghp = jax.ShapeDtypeStruct((G_C, GROUP_C, P_C), F32)
    pw = jax.ShapeDtypeStruct((2, S5_L + 1, G_C, P_C), F32)
    return pl.pallas_call(_s5_disc_kernel, out_shape=[ghp, ghp, pw], name="s5_disc")(
        lam_re, lam_im, log_dt.reshape(G_C, 1), b_re_t, b_im_t)


def _s5_taps_kernel(pw0_ref, pw1_ref, pwf_ref, c_ref, b_ref, kconv_ref, wst_ref, kin_ref, *, pairs):
    n = S5_L * GROUP_C
    pair_lanes = 2 * P_C

    def rep(t):
        return jnp.broadcast_to(t[:, None, :], (S5_L, GROUP_C, pair_lanes)).reshape(n, pair_lanes)

    def til(c):
        return jnp.broadcast_to(c[None, :, :], (S5_L, GROUP_C, pair_lanes)).reshape(n, pair_lanes)

    def dot_t(a, b):
        return lax.dot_general(a, b, (((1,), (1,)), ((), ())), precision=lax.Precision.HIGHEST,
                               preferred_element_type=F32)

    def first(shape, axis):
        return lax.broadcasted_iota(jnp.int32, shape, axis) < P_C

    lane = lax.broadcasted_iota(jnp.int32, (GROUP_C, n), 1)
    for q in range(pairs):
        cr, ci = til(c_ref[q, 0]), til(c_ref[q, 1])
        b_re, b_im = b_ref[q, 0], b_ref[q, 1]
        x_re, x_im = _cmul(rep(pw0_ref[q, 0]), rep(pw0_ref[q, 1]), cr, ci)
        g0 = first(b_re.shape, 1)
        b2 = jnp.concatenate(
            [jnp.concatenate([jnp.where(g0, b_re, 0.0), jnp.where(g0, -b_im, 0.0)], axis=1),
             jnp.concatenate([jnp.where(g0, 0.0, b_re), jnp.where(g0, 0.0, -b_im)], axis=1)], axis=0)
        mrows = dot_t(b2, jnp.concatenate([x_re, x_im], axis=1))
        for gi in range(2):
            mrow = mrows[gi * GROUP_C:(gi + 1) * GROUP_C]
            for s in range(S5_L):
                blk = mrow if s == 0 else jnp.where(lane >= s * GROUP_C, pltpu.roll(mrow, s * GROUP_C, 1), 0.0)
                kconv_ref[2 * q + gi, s * GROUP_C:(s + 1) * GROUP_C, :] = blk.astype(BF16)
        e_re, e_im = _cmul(rep(pw1_ref[q, 0]), rep(pw1_ref[q, 1]), cr, ci)
        et_re, et_im = e_re.T, (-e_im).T
        top = first(et_re.shape, 0)
        kin_ref[q] = jnp.concatenate(
            [jnp.concatenate([jnp.where(top, et_re, 0.0), jnp.where(top, 0.0, et_re)], axis=1),
             jnp.concatenate([jnp.where(top, et_im, 0.0), jnp.where(top, 0.0, et_im)], axis=1)], axis=0).astype(BF16)
        f_re, f_im = _cmul(rep(pwf_ref[q, 0]), rep(pwf_ref[q, 1]), til(b_re), til(b_im))
        left = first(f_re.shape, 1)
        wst_ref[q] = jnp.concatenate(
            [jnp.concatenate([jnp.where(left, f_re, 0.0), jnp.where(left, f_im, 0.0)], axis=1),
             jnp.concatenate([jnp.where(left, 0.0, f_re), jnp.where(left, 0.0, f_im)], axis=1)], axis=0).astype(BF16)


def _s5_taps(pw0, pw1, pwf, c, bb):
    pairs = 4
    n = S5_L * GROUP_C
    pspec = pl.BlockSpec((pairs, 2, S5_L, 2 * P_C), lambda i: (i, 0, 0, 0))
    hspec = pl.BlockSpec((pairs, 2, GROUP_C, 2 * P_C), lambda i: (i, 0, 0, 0))
    return pl.pallas_call(
        functools.partial(_s5_taps_kernel, pairs=pairs),
        grid=(G_C // (2 * pairs),),
        in_specs=[pspec, pspec, pspec, hspec, hspec],
        out_specs=[pl.BlockSpec((2 * pairs, n, n), lambda i: (i, 0, 0)),
                   pl.BlockSpec((pairs, 2 * n, 4 * P_C), lambda i: (i, 0, 0)),
                   pl.BlockSpec((pairs, 4 * P_C, 2 * n), lambda i: (i, 0, 0))],
        out_shape=[jax.ShapeDtypeStruct((G_C, n, n), BF16), jax.ShapeDtypeStruct((G_C // 2, 2 * n, 4 * P_C), BF16),
                   jax.ShapeDtypeStruct((G_C // 2, 4 * P_C, 2 * n), BF16)],
        compiler_params=_params(),
        name="s5_taps",
    )(pw0, pw1, pwf, c, bb)


def _pair_lanes(a):
    a2 = a.reshape((G_C // 2, 2) + a.shape[1:])
    return jnp.concatenate([a2[:, 0], a2[:, 1]], axis=-1)


def _granule_transpose(vs):
    vs = list(vs)
    granule = lax.broadcasted_iota(jnp.int32, vs[0].shape, 1) // GROUP_C
    d = 1
    while d < len(vs):
        keep = (granule & d) == 0
        for i in range(len(vs)):
            if i & d == 0:
                a, b = vs[i], vs[i + d]
                vs[i] = jnp.where(keep, a, pltpu.roll(b, d * GROUP_C, 1))
                vs[i + d] = jnp.where(keep, pltpu.roll(a, V7X_LANES - d * GROUP_C, 1), b)
        d *= 2
    return vs


def _s5_in_kernel(x_ref, g_ref, win_ref, u_ref, u2_ref, *, nb, blocks):
    h = _rms(x_ref[...], g_ref[...]).astype(BF16)
    rows2 = blocks * nb
    for cc in range(N_BLK):
        cols = slice(cc * V7X_MXU_DIM, (cc + 1) * V7X_MXU_DIM)
        u = _dot(h, win_ref[:, cols])
        u_ref[:, cols] = u
        u4 = u.reshape(blocks, S5_L, nb, V7X_MXU_DIM)
        for sub in range(V7X_MXU_DIM // V7X_LANES):
            c = cc * (V7X_MXU_DIM // V7X_LANES) + sub
            for half in range(S5_L // GRANULES_PER_VREG):
                vs = []
                for kk in range(GRANULES_PER_VREG):
                    slab = u4[:, half * GRANULES_PER_VREG + kk, :, sub * V7X_LANES:(sub + 1) * V7X_LANES]
                    vs.append(slab.reshape(rows2, V7X_LANES))
                ws = _granule_transpose(vs)
                for gi in range(GRANULES_PER_VREG):
                    lane0 = (c * GRANULES_PER_VREG + gi) * S5_L * GROUP_C + half * V7X_LANES
                    u2_ref[:, lane0:lane0 + V7X_LANES] = ws[gi].astype(BF16)


def _s5_in(x, g, win, nb, blocks):
    rows = blocks * S5_L * nb
    n_blocks = x.shape[0] // (S5_L * nb)
    row_spec = pl.BlockSpec((rows, D_MODEL), lambda i: (i, 0))
    return pl.pallas_call(
        functools.partial(_s5_in_kernel, nb=nb, blocks=blocks),
        grid=(x.shape[0] // rows,),
        in_specs=[row_spec, _const_spec((1, D_MODEL)), _const_spec((D_MODEL, D_MODEL))],
        out_specs=[row_spec, pl.BlockSpec((blocks * nb, S5_L * D_MODEL), lambda i: (i, 0))],
        out_shape=[jax.ShapeDtypeStruct(x.shape, F32),
                   jax.ShapeDtypeStruct((n_blocks * nb, S5_L * D_MODEL), BF16)],
        compiler_params=_params(),
        name="s5_in",
    )(x, g, win)


def _s5_core_kernel(u_ref, wst_ref, kconv_ref, kin_ref, a_ref, s0_ref, y_ref, sf_ref, sc_s, sin_s, *, nb, n_blocks):
    pw = 2 * S5_L * GROUP_C
    sw = 2 * P_C
    ub = [u_ref[:, q * pw:(q + 1) * pw] for q in range(S5_PAIRS_PER_STEP)]
    for q in range(S5_PAIRS_PER_STEP):
        sc_s[q] = _dot(ub[q], wst_ref[q])
    a_re = [jnp.broadcast_to(a_ref[:, 2 * sw * q:2 * sw * q + sw], (nb, sw)) for q in range(S5_PAIRS_PER_STEP)]
    a_im = [jnp.broadcast_to(a_ref[:, 2 * sw * q + sw:2 * sw * (q + 1)], (nb, sw)) for q in range(S5_PAIRS_PER_STEP)]
    s_re = [s0_ref[:, 2 * sw * q:2 * sw * q + sw] for q in range(S5_PAIRS_PER_STEP)]
    s_im = [s0_ref[:, 2 * sw * q + sw:2 * sw * (q + 1)] for q in range(S5_PAIRS_PER_STEP)]
    for blk in range(n_blocks):
        rs = slice(blk * nb, (blk + 1) * nb)
        for q in range(S5_PAIRS_PER_STEP):
            sin_s[q, rs, 0:sw] = s_re[q]
            sin_s[q, rs, sw:2 * sw] = s_im[q]
            n_re, n_im = _cmul(a_re[q], a_im[q], s_re[q], s_im[q])
            s_re[q], s_im[q] = n_re + sc_s[q, rs, 0:sw], n_im + sc_s[q, rs, sw:2 * sw]
    for q in range(S5_PAIRS_PER_STEP):
        sf_ref[:, 2 * sw * q:2 * sw * q + sw] = s_re[q]
        sf_ref[:, 2 * sw * q + sw:2 * sw * (q + 1)] = s_im[q]
        half = S5_L * GROUP_C
        conv = jnp.concatenate([_dot(ub[q][:, :half], kconv_ref[2 * q]),
                                _dot(ub[q][:, half:], kconv_ref[2 * q + 1])], axis=1)
        y_ref[:, q * pw:(q + 1) * pw] = conv + _dot(sin_s[q].astype(BF16), kin_ref[q])


def _s5_core(u2, wst, kconv, kin, a_pair, s0, nb):
    rows = u2.shape[0]
    pw = 2 * S5_L * GROUP_C
    sw = 2 * P_C
    cols = S5_PAIRS_PER_STEP * pw
    scols = S5_PAIRS_PER_STEP * 2 * sw
    return pl.pallas_call(
        functools.partial(_s5_core_kernel, nb=nb, n_blocks=rows // nb),
        grid=(u2.shape[1] // cols,),
        in_specs=[pl.BlockSpec((rows, cols), lambda i: (0, i)),
                  pl.BlockSpec((S5_PAIRS_PER_STEP, pw, 2 * sw), lambda i: (i, 0, 0)),
                  pl.BlockSpec((2 * S5_PAIRS_PER_STEP, pw // 2, pw // 2), lambda i: (i, 0, 0)),
                  pl.BlockSpec((S5_PAIRS_PER_STEP, 2 * sw, pw), lambda i: (i, 0, 0)),
                  pl.BlockSpec((1, scols), lambda i: (0, i)), pl.BlockSpec((nb, scols), lambda i: (0, i))],
        out_specs=[pl.BlockSpec((rows, cols), lambda i: (0, i)), pl.BlockSpec((nb, scols), lambda i: (0, i))],
        out_shape=[jax.ShapeDtypeStruct(u2.shape, F32), jax.ShapeDtypeStruct((nb, 2 * D_STATE), F32)],
        scratch_shapes=[pltpu.VMEM((S5_PAIRS_PER_STEP, rows, 2 * sw), F32),
                        pltpu.VMEM((S5_PAIRS_PER_STEP, rows, 2 * sw), F32)],
        compiler_params=_params(),
        name="s5_core",
    )(u2, wst, kconv, kin, a_pair, s0)


def _s5_out_kernel(x_ref, u_ref, y2_ref, dskip_ref, wglu_ref, o_ref, y_s, gate_s, *, nb, blocks):
    rows = blocks * S5_L * nb
    for cc in range(N_BLK):
        cols = slice(cc * V7X_MXU_DIM, (cc + 1) * V7X_MXU_DIM)
        for sub in range(V7X_MXU_DIM // V7X_LANES):
            c = cc * (V7X_MXU_DIM // V7X_LANES) + sub
            for half in range(S5_L // GRANULES_PER_VREG):
                lanes = [(c * GRANULES_PER_VREG + gi) * S5_L * GROUP_C + half * V7X_LANES
                         for gi in range(GRANULES_PER_VREG)]
                ws = _granule_transpose([y2_ref[:, l0:l0 + V7X_LANES] for l0 in lanes])
                for kk in range(GRANULES_PER_VREG):
                    y_s[cc, :, half * GRANULES_PER_VREG + kk, :, sub * V7X_LANES:(sub + 1) * V7X_LANES] = (
                        ws[kk].reshape(blocks, nb, V7X_LANES))
        y = y_s[cc].reshape(rows, V7X_MXU_DIM) + dskip_ref[:, cols] * u_ref[:, cols]
        gy = _gelu(y).astype(BF16)
        for acc, wcols in ((o_ref, slice(0, D_MODEL)), (gate_s, slice(D_MODEL, 2 * D_MODEL))):
            part = _dot(gy, wglu_ref[cols, wcols])
            if cc == 0:
                acc[...] = part
            else:
                acc[...] += part
    o_ref[...] = x_ref[...] + o_ref[...] * jax.nn.sigmoid(gate_s[...])


def _s5_out(x, u, y2, dskip, wglu, nb, blocks):
    rows = blocks * S5_L * nb
    row_spec = pl.BlockSpec((rows, D_MODEL), lambda i: (i, 0))
    return pl.pallas_call(
        functools.partial(_s5_out_kernel, nb=nb, blocks=blocks),
        grid=(x.shape[0] // rows,),
        in_specs=[row_spec, row_spec, pl.BlockSpec((blocks * nb, S5_L * D_MODEL), lambda i: (i, 0)),
                  _const_spec((1, D_MODEL)), _const_spec((D_MODEL, 2 * D_MODEL))],
        out_specs=row_spec,
        out_shape=jax.ShapeDtypeStruct(x.shape, F32),
        scratch_shapes=[pltpu.VMEM((N_BLK, blocks, S5_L, nb, V7X_MXU_DIM), F32), pltpu.VMEM((rows, D_MODEL), F32)],
        compiler_params=_params(),
        name="s5_out",
    )(x, u, y2, dskip, wglu)


def _pair_state(re, im):
    lead = re.shape[:-2]
    r = re.reshape(lead + (G_C // 2, 2 * P_C))
    i = im.reshape(lead + (G_C // 2, 2 * P_C))
    return jnp.concatenate([r, i], axis=-1).reshape(lead + (2 * D_STATE,))


def _unpair_state(s, nb):
    s4 = s.reshape(nb, G_C // 2, 2, 2 * P_C)
    return s4[:, :, 0].reshape(nb, G_C, P_C), s4[:, :, 1].reshape(nb, G_C, P_C)


def _s5_short_kernel(x_ref, g_ref, win_ref, kconv_ref, wsta_ref, wstb_ref, kina_ref, kinb_ref, a_ref, dskip_ref,
                     wglu_ref, s0_ref, o_ref, sf_ref, u8_s, y8_s, y_s, *, nb, steps):
    slots = GRANULES_PER_VREG
    gl = slots * GROUP_C
    sw = 2 * P_C
    x = x_ref[...]
    u = _dot(_rms(x, g_ref[...]).astype(BF16), win_ref[...])
    zero = jnp.zeros((nb, V7X_LANES), F32)
    for c in range(D_MODEL // V7X_LANES):
        cols = slice(c * V7X_LANES, (c + 1) * V7X_LANES)
        ws = _granule_transpose([u[k * nb:(k + 1) * nb, cols] for k in range(steps)] + [zero] * (slots - steps))
        for gi in range(GRANULES_PER_VREG):
            g = c * GRANULES_PER_VREG + gi
            u8_s[:, g * gl:(g + 1) * gl] = ws[gi].astype(BF16)
    for q in range(G_C // 2):
        up = u8_s[:, 2 * q * gl:(2 * q + 2) * gl]
        reps = slots // steps
        sc = _dot(up, jnp.concatenate([wsta_ref[q]] * reps + [wstb_ref[q]] * reps, axis=0))
        s_re, s_im = s0_ref[:, 2 * sw * q:2 * sw * q + sw], s0_ref[:, 2 * sw * q + sw:2 * sw * (q + 1)]
        n_re, n_im = _cmul(a_ref[:, 2 * sw * q:2 * sw * q + sw], a_ref[:, 2 * sw * q + sw:2 * sw * (q + 1)], s_re, s_im)
        sf_ref[:, 2 * sw * q:2 * sw * q + sw] = n_re + sc[:, :sw]
        sf_ref[:, 2 * sw * q + sw:2 * sw * (q + 1)] = n_im + sc[:, sw:]
        yk = _dot(jnp.concatenate([s_re, s_im], axis=1).astype(BF16),
                  jnp.concatenate([kina_ref[q], kinb_ref[q]], axis=1))
        for gi in range(2):
            y8_s[:, (2 * q + gi) * gl:(2 * q + gi + 1) * gl] = (
                _dot(up[:, gi * gl:(gi + 1) * gl], kconv_ref[2 * q + gi]) + yk[:, gi * gl:(gi + 1) * gl])
    for c in range(D_MODEL // V7X_LANES):
        ws = _granule_transpose([y8_s[:, (c * GRANULES_PER_VREG + gi) * gl:(c * GRANULES_PER_VREG + gi + 1) * gl]
                                 for gi in range(GRANULES_PER_VREG)])
        for k in range(steps):
            y_s[k * nb:(k + 1) * nb, c * V7X_LANES:(c + 1) * V7X_LANES] = ws[k]
    y = y_s[...] + dskip_ref[...] * u
    o = _dot(_gelu(y).astype(BF16), wglu_ref[...])
    o_ref[...] = x + o[:, :D_MODEL] * jax.nn.sigmoid(o[:, D_MODEL:])


def _s5_short(x, g, win, kconv, wst, kin, a_pair, dskip, wglu, s0, nb, steps):
    assert S5_L % steps == 0 and GRANULES_PER_VREG % steps == 0
    rows = nb * steps
    slots = GRANULES_PER_VREG
    gl = slots * GROUP_C
    n = S5_L * GROUP_C
    srows = steps * GROUP_C
    row_spec = pl.BlockSpec((rows, D_MODEL), lambda i: (0, 0))
    state_spec = pl.BlockSpec((nb, 2 * D_STATE), lambda i: (0, 0))
    return pl.pallas_call(
        functools.partial(_s5_short_kernel, nb=nb, steps=steps),
        grid=(1,),
        in_specs=[row_spec, _const_spec((1, D_MODEL)), _const_spec((D_MODEL, D_MODEL)),
                  pl.BlockSpec((G_C, gl, gl), lambda i: (0, 0, 0)),
                  pl.BlockSpec((G_C // 2, srows, 4 * P_C), lambda i: (0, n // srows - 1, 0)),
                  pl.BlockSpec((G_C // 2, srows, 4 * P_C), lambda i: (0, 2 * n // srows - 1, 0)),
                  pl.BlockSpec((G_C // 2, 4 * P_C, gl), lambda i: (0, 0, 0)),
                  pl.BlockSpec((G_C // 2, 4 * P_C, gl), lambda i: (0, 0, n // gl)),
                  _const_spec((1, 2 * D_STATE)), _const_spec((1, D_MODEL)), _const_spec((D_MODEL, 2 * D_MODEL)),
                  state_spec],
        out_specs=[row_spec, state_spec],
        out_shape=[jax.ShapeDtypeStruct(x.shape, F32), jax.ShapeDtypeStruct((nb, 2 * D_STATE), F32)],
        scratch_shapes=[pltpu.VMEM((nb, G_C * gl), BF16), pltpu.VMEM((nb, G_C * gl), F32),
                        pltpu.VMEM((rows, D_MODEL), F32)],
        compiler_params=_params(),
        name="s5_short",
    )(x, g, win, kconv, wst, wst, kin, kin, a_pair, dskip, wglu, s0)


def _block_diag(w, n_per_blk):
    n, k_in, k_out = w.shape
    wb = w.reshape(n // n_per_blk, n_per_blk, k_in, k_out)
    eye = jnp.eye(n_per_blk, dtype=w.dtype)
    out = jnp.einsum("jgio,gk->jgiko", wb, eye)
    return out.reshape(n // n_per_blk, n_per_blk * k_in, n_per_blk * k_out)


def _to_rows(a):
    return jnp.swapaxes(a, 0, 1).reshape(a.shape[0] * a.shape[1], a.shape[2])


def _from_rows(a, nb):
    return jnp.swapaxes(a.reshape(a.shape[0] // nb, nb, a.shape[1]), 0, 1)


def kernel(x_prompt, x_sample, state_rglru_conv, state_rglru_h, state_s5_re, state_s5_im, norm_mix, norm_ffn, norm_f, w_ff1, w_ff2, w_in_a, sgu_g, w_s, b_s, w_out_a, w_in_b, conv_w, conv_b, w_a, b_a, w_x, b_x, lam, w_out_b, w_in_c, lam_re, lam_im, log_dt, b_re, b_im, c_re, c_im, d_skip, w_glu):
    bp, tp, _ = x_prompt.shape
    bs, ts, _ = x_sample.shape
    rows_p, rows_s = bp * tp, bs * ts
    assert DEPTH % N_MIXERS == 1 and DEPTH > 1, "first and last layers must be SGU layers"
    assert tp % CHUNK == 0 and ts < CHUNK
    assert rows_s % FFN_SAMPLE_ROWS == 0 and rows_p % FFN_ROWS == 0

    row = lambda v: v.reshape(1, -1)
    norm_ffn3 = norm_ffn.reshape(DEPTH, 1, D_MODEL)
    w1, w2, gf = w_ff1, w_ff2, row(norm_f)

    x_p, x_s = x_prompt, _to_rows(x_sample)
    outs_v, conv_p, h_p, conv_s, h_s, sre_p, sim_p, sre_s, sim_s = [], [], [], [], [], [], [], [], []
    for layer in range(DEPTH):
        j, kind = layer // N_MIXERS, layer % N_MIXERS
        first, last = layer == 0, layer == DEPTH - 1
        g = row(norm_mix[layer])
        if kind == 0:
            p = dict(g=g, win=w_in_a[j].astype(BF16), wout=w_out_a[j].astype(BF16), sg=row(sgu_g[j]),
                     w_tril=jnp.where(jnp.tril(jnp.ones((CHUNK, CHUNK), dtype=bool)), w_s[j], 0.0).astype(BF16),
                     bias=jnp.repeat(b_s[j].T, HD_A, axis=1),
                     wl=jnp.repeat(w_s[j][:, :ts, :ts].reshape(HEADS_A, ts * ts).T, HD_A, axis=1),
                     bl=jnp.repeat(b_s[j][:, :ts].T, HD_A, axis=1))
            if first:
                x_p = _sgu_prompt(x_p, p, bp, tp, False, True)
            elif last:
                x_p = _sgu_prompt(x_p, p, bp, tp, True, False).reshape(rows_p, D_MODEL)
            else:
                raise NotImplementedError("interior SGU layers")
            x_s, v = _sgu_sample(x_s, p, bs, ts)
            outs_v.append(_from_rows(v, bs))
        elif kind == 1:
            p = dict(g=g, win=w_in_b[j].astype(BF16), cw=conv_w[j], cb=row(conv_b[j]),
                     wa=_block_diag(w_a[j], HEADS_PER_BLK).astype(BF16), ba=row(b_a[j]),
                     wx=_block_diag(w_x[j], HEADS_PER_BLK).astype(BF16), bx=row(b_x[j]),
                     lam=row(lam[j]), wout=w_out_b[j].astype(BF16))
            dt_s = state_rglru_h.dtype
            x_p, cp, hp = _rglru(x_p, p, jnp.zeros(((CONV_W - 1) * bp, D_MODEL), dt_s),
                                 jnp.zeros((bp, D_MODEL), dt_s), bp, RGLRU_PROMPT_STEPS)
            x_s, cs, hs = _rglru(x_s, p, _to_rows(state_rglru_conv[j]), state_rglru_h[j], bs, ts)
            conv_p.append(_from_rows(cp, bp)); h_p.append(hp)
            conv_s.append(_from_rows(cs, bs)); h_s.append(hs)
        else:
            bbre, bbim, pw = _s5_disc(lam_re[j], lam_im[j], log_dt[j],
                                      jnp.swapaxes(b_re[j], 1, 2), jnp.swapaxes(b_im[j], 1, 2))
            p = dict(g=g, win=w_in_c[j].astype(BF16), dskip=row(d_skip[j]), wglu=w_glu[j].astype(BF16))
            pwg = _pair_lanes(jnp.transpose(pw, (2, 0, 1, 3)))
            kconv, wst, kin = _s5_taps(pwg[:, :, :S5_L], pwg[:, :, 1:], pwg[:, :, S5_L - 1::-1],
                                       _pair_lanes(jnp.stack([c_re[j], c_im[j]], axis=1)),
                                       _pair_lanes(jnp.stack([bbre, bbim], axis=1)))
            zs = jnp.zeros((bp, G_C, P_C), state_s5_re.dtype)
            u, u2 = _s5_in(x_p, g, p["win"], bp, S5_TILE_BLOCKS)
            y2, sf = _s5_core(u2, wst, kconv, kin, _pair_state(pw[0, S5_L], pw[1, S5_L]).reshape(1, -1),
                              _pair_state(zs, zs), bp)
            x_p = _s5_out(x_p, u, y2, p["dskip"], p["wglu"], bp, S5_TILE_BLOCKS)
            rp, ip = _unpair_state(sf, bp)
            x_s, sf_s = _s5_short(x_s, g, p["win"], kconv, wst, kin,
                                  _pair_state(pw[0, ts], pw[1, ts]).reshape(1, -1), p["dskip"], p["wglu"],
                                  _pair_state(state_s5_re[j], state_s5_im[j]), bs, ts)
            rs, is_ = _unpair_state(sf_s, bs)
            sre_p.append(rp); sim_p.append(ip)
            sre_s.append(rs); sim_s.append(is_)
        x_p, x_s = _ffn(x_p, x_s, norm_ffn3, w1, w2, gf, layer, last)

    y_prompt = x_p.reshape(bp, tp, D_MODEL)
    y_sample = _from_rows(x_s, bs)
    return (y_prompt, y_sample, jnp.stack(outs_v), jnp.stack(conv_p), jnp.stack(h_p), jnp.stack(conv_s),
            jnp.stack(h_s), jnp.stack(sre_p), jnp.stack(sim_p), jnp.stack(sre_s), jnp.stack(sim_s))
```

```python
import functools

import jax
import jax.numpy as jnp
from jax import lax
from jax.experimental import pallas as pl
from jax.experimental.pallas import tpu as pltpu

F32 = jnp.float32
BF16 = jnp.bfloat16

D_MODEL = 1024
DEPTH = 4
N_MIXERS = 3
EPS = 1e-6
CHUNK = 128
HEADS_A = 8
HD_A = D_MODEL // HEADS_A
HEADS_B = 16
HD_B = D_MODEL // HEADS_B
CONV_W = 4
LRU_C = 8.0
GROUP_C = 16
G_C = D_MODEL // GROUP_C
P_C = 64
D_STATE = G_C * P_C
D_FF = 4 * D_MODEL

V7X_LANES = 128
V7X_MXU_DIM = 256
V7X_VMEM_BYTES = 64 * 1024 * 1024
VMEM_LIMIT = V7X_VMEM_BYTES - 8 * 1024 * 1024

N_BLK = D_MODEL // V7X_MXU_DIM
HEADS_PER_BLK = V7X_MXU_DIM // HD_B

FFN_ROWS = 1024
FFN_SAMPLE_ROWS = 512
FFN_COL_CHUNK = 1024
FFN_STAGE_BYTES = 1024 * 1024
S5_L = 16
S5_TILE_BLOCKS = 8
S5_PAIRS_PER_STEP = 4
GRANULES_PER_VREG = V7X_LANES // GROUP_C
RGLRU_PROMPT_STEPS = 128


def _rms(x, g):
    return (x * lax.rsqrt(jnp.mean(x * x, axis=-1, keepdims=True) + EPS)) * g


GELU_C0 = 0.7978845608028654
GELU_C1 = GELU_C0 * 0.044715


def _gelu(x):
    return x * (0.5 + 0.5 * jnp.tanh(x * (GELU_C0 + GELU_C1 * (x * x))))


def _dot(a, b):
    return jnp.dot(a, b, preferred_element_type=F32)


def _const_spec(shape):
    zeros = (0,) * len(shape)
    return pl.BlockSpec(shape, lambda i: zeros, pipeline_mode=pl.Buffered(1))


def _layer_spec(shape, layer):
    idx = (layer,) + (0,) * len(shape)
    return pl.BlockSpec((None,) + tuple(shape), lambda i: idx, pipeline_mode=pl.Buffered(1))


def _params():
    return pltpu.CompilerParams(dimension_semantics=("arbitrary",), vmem_limit_bytes=VMEM_LIMIT)


def _ffn_tile(x_ref, g_ref, w1_ref, w2_ref, gf_ref, o_ref, final_norm):
    h = _rms(x_ref[...], g_ref[...]).astype(BF16)
    for j in range(D_FF // FFN_COL_CHUNK):
        cols = slice(j * FFN_COL_CHUNK, (j + 1) * FFN_COL_CHUNK)
        a = jnp.square(jnp.maximum(_dot(h, w1_ref[:, cols]), 0.0)).astype(BF16)
        part = _dot(a, w2_ref[cols, :])
        if j == 0:
            o_ref[...] = part
        else:
            o_ref[...] += part
    y = x_ref[...] + o_ref[...]
    if final_norm:
        y = _rms(y, gf_ref[...])
    o_ref[...] = y


def _load_as_bf16(jobs):
    def copies_of(src, stage, sem):
        rows = stage.shape[1]
        return [pltpu.make_async_copy(src.at[pl.ds(k * rows, rows)], stage.at[k % 2], sem.at[k % 2])
                for k in range(src.shape[0] // rows)]

    plans = [copies_of(src, stage, sem) for src, _, stage, sem in jobs]
    n = len(plans[0])
    assert all(len(p) == n for p in plans)
    for p in plans:
        p[0].start()
    for k in range(n):
        if k + 1 < n:
            for p in plans:
                p[k + 1].start()
        for p, (_, dst, stage, _) in zip(plans, jobs):
            p[k].wait()
            rows = stage.shape[1]
            dst[k * rows:(k + 1) * rows, :] = stage[k % 2].astype(BF16)


def _ffn_kernel(xp_ref, xs_ref, g_ref, w1_hbm, w2_hbm, gf_ref, op_ref, os_ref, w1_s, w2_s, stage1, stage2,
                sem1, sem2, *, final_norm, n_p, layer):
    i = pl.program_id(0)

    @pl.when(i == 0)
    def _():
        _load_as_bf16([(w1_hbm.at[layer], w1_s, stage1, sem1), (w2_hbm.at[layer], w2_s, stage2, sem2)])

    @pl.when(i < n_p)
    def _():
        _ffn_tile(xp_ref, g_ref, w1_s, w2_s, gf_ref, op_ref, final_norm)

    @pl.when(i >= n_p)
    def _():
        _ffn_tile(xs_ref, g_ref, w1_s, w2_s, gf_ref, os_ref, final_norm)


def _ffn(x_p, x_s, g, w1, w2, gf, layer, final_norm):
    n_p, n_s = x_p.shape[0] // FFN_ROWS, x_s.shape[0] // FFN_SAMPLE_ROWS
    p_spec = pl.BlockSpec((FFN_ROWS, D_MODEL), lambda i: (jnp.minimum(i, n_p - 1), 0))
    s_spec = pl.BlockSpec((FFN_SAMPLE_ROWS, D_MODEL), lambda i: (jnp.maximum(i - n_p, 0), 0))
    hbm = pl.BlockSpec(memory_space=pl.ANY)
    return pl.pallas_call(
        functools.partial(_ffn_kernel, final_norm=final_norm, n_p=n_p, layer=layer),
        grid=(n_p + n_s,),
        in_specs=[p_spec, s_spec, _layer_spec((1, D_MODEL), layer), hbm, hbm, _const_spec((1, D_MODEL))],
        out_specs=[p_spec, s_spec],
        out_shape=[jax.ShapeDtypeStruct(x_p.shape, F32), jax.ShapeDtypeStruct(x_s.shape, F32)],
        scratch_shapes=[pltpu.VMEM((D_MODEL, D_FF), BF16), pltpu.VMEM((D_FF, D_MODEL), BF16),
                        pltpu.VMEM((2, FFN_STAGE_BYTES // (4 * D_FF), D_FF), F32),
                        pltpu.VMEM((2, FFN_STAGE_BYTES // (4 * D_MODEL), D_MODEL), F32),
                        pltpu.SemaphoreType.DMA((2,)), pltpu.SemaphoreType.DMA((2,))],
        compiler_params=_params(),
        name="ffn",
    )(x_p, x_s, g, w1, w2, gf)


def _sgu_front(x, g_ref, win_ref, sg_ref):
    h = _rms(x, g_ref[...]).astype(BF16)
    v = _rms(_gelu(_dot(h, win_ref[:, D_MODEL:])), sg_ref[...])
    return _gelu(_dot(h, win_ref[:, :D_MODEL])), v


def _rows_to_batch_major(x, nb):
    steps = x.shape[0] // nb
    return jnp.swapaxes(x.reshape(steps, nb, x.shape[1]), 0, 1).reshape(x.shape)


def _rows_to_time_major(x, nb):
    steps = x.shape[0] // nb
    return jnp.swapaxes(x.reshape(nb, steps, x.shape[1]), 0, 1).reshape(x.shape)


def _sgu_prompt_kernel(x_ref, g_ref, win_ref, sg_ref, w_ref, bias_ref, wout_ref, o_ref, y_s, *, nb, in_tm, out_tm):
    rows = nb * CHUNK
    if in_tm:
        x = _rows_to_batch_major(x_ref[...], nb)
    else:
        x = x_ref[...].reshape(rows, D_MODEL)
    u, v = _sgu_front(x, g_ref, win_ref, sg_ref)
    vb = v.astype(BF16)
    for b in range(nb):
        rs = slice(b * CHUNK, (b + 1) * CHUNK)
        for g in range(HEADS_A):
            cs = slice(g * HD_A, (g + 1) * HD_A)
            mixed = _dot(w_ref[g], vb[rs, cs]) + bias_ref[:, cs]
            y_s[rs, cs] = (u[rs, cs] * mixed).astype(BF16)
    o = x + _dot(y_s[...], wout_ref[...])
    if out_tm:
        o_ref[...] = _rows_to_time_major(o, nb)
    else:
        o_ref[...] = o.reshape(nb, CHUNK, D_MODEL)


def _sgu_sample_kernel(x_ref, g_ref, win_ref, sg_ref, wl_ref, bl_ref, wout_ref, o_ref, v_ref, *, nb, steps):
    x = x_ref[...]
    u, v = _sgu_front(x, g_ref, win_ref, sg_ref)
    v_ref[...] = v
    mixed = []
    for t in range(steps):
        m = None
        for s in range(t + 1):
            term = wl_ref[t * steps + s:t * steps + s + 1, :] * v[s * nb:(s + 1) * nb, :]
            m = term if m is None else m + term
        mixed.append(m + bl_ref[t:t + 1, :])
    y = (u * jnp.concatenate(mixed, axis=0)).astype(BF16)
    o_ref[...] = x + _dot(y, wout_ref[...])


def _sgu_prompt(x, p, nb, n_steps, in_tm, out_tm):
    rows = nb * CHUNK
    tm_spec = pl.BlockSpec((rows, D_MODEL), lambda i: (i, 0))
    bm_spec = pl.BlockSpec((nb, CHUNK, D_MODEL), lambda i: (0, i, 0))
    out_shape = (jax.ShapeDtypeStruct((nb * n_steps, D_MODEL), F32) if out_tm
                 else jax.ShapeDtypeStruct((nb, n_steps, D_MODEL), F32))
    return pl.pallas_call(
        functools.partial(_sgu_prompt_kernel, nb=nb, in_tm=in_tm, out_tm=out_tm),
        grid=(n_steps // CHUNK,),
        in_specs=[tm_spec if in_tm else bm_spec, _const_spec((1, D_MODEL)),
                  _layer_spec((D_MODEL, 2 * D_MODEL), p["j"]),
                  _const_spec((1, D_MODEL)), _const_spec((HEADS_A, CHUNK, CHUNK)),
                  _const_spec((CHUNK, D_MODEL)), _layer_spec((D_MODEL, D_MODEL), p["j"])],
        out_specs=tm_spec if out_tm else bm_spec,
        out_shape=out_shape,
        scratch_shapes=[pltpu.VMEM((rows, D_MODEL), BF16)],
        compiler_params=_params(),
        name="sgu_prompt",
    )(x, p["g"], p["win"], p["sg"], p["w_tril"], p["bias"], p["wout"])


def _sgu_sample(x, p, nb, steps):
    tile = steps * nb
    tile_spec = pl.BlockSpec((tile, D_MODEL), lambda i: (0, 0))
    return pl.pallas_call(
        functools.partial(_sgu_sample_kernel, nb=nb, steps=steps),
        grid=(1,),
        in_specs=[tile_spec, _const_spec((1, D_MODEL)), _layer_spec((D_MODEL, 2 * D_MODEL), p["j"]),
                  _const_spec((1, D_MODEL)), _const_spec((steps * steps, D_MODEL)),
                  _const_spec((steps, D_MODEL)), _layer_spec((D_MODEL, D_MODEL), p["j"])],
        out_specs=[tile_spec, tile_spec],
        out_shape=[jax.ShapeDtypeStruct((tile, D_MODEL), F32), jax.ShapeDtypeStruct((tile, D_MODEL), F32)],
        compiler_params=_params(),
        name="sgu_sample",
    )(x, p["g"], p["win"], p["sg"], p["wl"], p["bl"], p["wout"])


def _rglru_kernel(x_ref, g_ref, win_ref, cw_ref, cb_ref, wa_ref, ba_ref, wx_ref, bx_ref, lam_ref, wout_ref,
                  conv0_ref, h0_ref, o_ref, conv_ref, h_ref, xext_s, a_s, b_s, y_s, *, nb, steps):
    rows = nb * steps
    halo = (CONV_W - 1) * nb

    @pl.when(pl.program_id(0) == 0)
    def _():
        conv_ref[...] = conv0_ref[...]
        h_ref[...] = h0_ref[...]

    def blk_cols(j):
        return slice(j * V7X_MXU_DIM, (j + 1) * V7X_MXU_DIM)

    x = x_ref[...]
    h = _rms(x, g_ref[...]).astype(BF16)
    for j in range(N_BLK):
        cols = blk_cols(j)
        xext, a_j, b_j = xext_s.at[j], a_s.at[j], b_s.at[j]
        xext[0:halo, :] = conv_ref[:, cols]
        xext[halo:halo + rows, :] = _dot(h, win_ref[:, blk_cols(N_BLK + j)])
        gate = _gelu(_dot(h, win_ref[:, cols]))
        conv = None
        for w in range(CONV_W):
            term = xext[w * nb:w * nb + rows, :] * cw_ref[w:w + 1, cols]
            conv = term if conv is None else conv + term
        conv_ref[:, cols] = xext[rows:rows + halo, :]
        xc = cb_ref[:, cols] + conv
        xcb = xc.astype(BF16)
        r = jax.nn.sigmoid(_dot(xcb, wa_ref[j]) + ba_ref[:, cols])
        ig = jax.nn.sigmoid(_dot(xcb, wx_ref[j]) + bx_ref[:, cols])
        neg_log_a = (LRU_C * r) * jax.nn.softplus(-lam_ref[:, cols])
        a = jnp.exp(-neg_log_a)
        a_j[...] = a
        w = jnp.tanh(neg_log_a) * (a * a + 1.0)
        mult = jnp.where(w > 0.0, w * lax.rsqrt(w), 0.0)
        b_j[...] = mult * (ig * xc)
        hcur = h_ref[:, cols]
        for t in range(steps):
            sl = slice(t * nb, (t + 1) * nb)
            hcur = a_j[sl, :] * hcur + b_j[sl, :]
            b_j[sl, :] = hcur
        h_ref[:, cols] = hcur
        y_s[:, cols] = (b_j[...] * gate).astype(BF16)
    o_ref[...] = x + _dot(y_s[...], wout_ref[...])


def _rglru(x, p, conv0, h0, nb, steps):
    rows = nb * steps
    halo = (CONV_W - 1) * nb
    row_spec = pl.BlockSpec((rows, D_MODEL), lambda i: (i, 0))
    vec = _const_spec((1, D_MODEL))
    blk = _const_spec((N_BLK, V7X_MXU_DIM, V7X_MXU_DIM))
    return pl.pallas_call(
        functools.partial(_rglru_kernel, nb=nb, steps=steps),
        grid=(x.shape[0] // rows,),
        in_specs=[row_spec, vec, _const_spec((D_MODEL, 2 * D_MODEL)), _const_spec((CONV_W, D_MODEL)), vec,
                  blk, vec, blk, vec, vec, _const_spec((D_MODEL, D_MODEL)),
                  _const_spec((halo, D_MODEL)), _const_spec((nb, D_MODEL))],
        out_specs=[row_spec, pl.BlockSpec((halo, D_MODEL), lambda i: (0, 0)),
                   pl.BlockSpec((nb, D_MODEL), lambda i: (0, 0))],
        out_shape=[jax.ShapeDtypeStruct(x.shape, F32), jax.ShapeDtypeStruct((halo, D_MODEL), F32),
                   jax.ShapeDtypeStruct((nb, D_MODEL), F32)],
        scratch_shapes=[pltpu.VMEM((N_BLK, rows + halo, V7X_MXU_DIM), F32), pltpu.VMEM((N_BLK, rows, V7X_MXU_DIM), F32),
                        pltpu.VMEM((N_BLK, rows, V7X_MXU_DIM), F32), pltpu.VMEM((rows, D_MODEL), BF16)],
        compiler_params=_params(),
        name="rglru",
    )(x, p["g"], p["win"], p["cw"], p["cb"], p["wa"], p["ba"], p["wx"], p["bx"], p["lam"], p["wout"], conv0, h0)


def _cmul(a_re, a_im, b_re, b_im):
    return a_re * b_re - a_im * b_im, a_re * b_im + a_im * b_re


def _s5_disc_kernel(lre_ref, lim_ref, ldt_ref, bre_ref, bim_ref, bbre_ref, bbim_ref, pw_ref):
    lr, li = lre_ref[...], lim_ref[...]
    dt = jnp.exp(ldt_ref[...])
    mag = jnp.exp(lr * dt)
    ab_re, ab_im = mag * jnp.cos(li * dt), mag * jnp.sin(li * dt)
    zr, zi = ab_re - 1.0, ab_im
    den = lr * lr + li * li
    q_re = (zr * lr + zi * li) / den
    q_im = (zi * lr - zr * li) / den
    br, bi = bre_ref[...], bim_ref[...]
    bbre_ref[...] = q_re[:, None, :] * br - q_im[:, None, :] * bi
    bbim_ref[...] = q_re[:, None, :] * bi + q_im[:, None, :] * br
    p_re, p_im = jnp.ones_like(ab_re), jnp.zeros_like(ab_re)
    for k in range(S5_L + 1):
        pw_ref[0, k] = p_re
        pw_ref[1, k] = p_im
        p_re, p_im = _cmul(p_re, p_im, ab_re, ab_im)


def _s5_disc(lam_re, lam_im, log_dt, b_re_t, b_im_t):
    ghp = jax.ShapeDtypeStruct((G_C, GROUP_C, P_C), F32)
    pw = jax.ShapeDtypeStruct((2, S5_L + 1, G_C, P_C), F32)
    return pl.pallas_call(_s5_disc_kernel, out_shape=[ghp, ghp, pw], name="s5_disc")(
        lam_re, lam_im, log_dt.reshape(G_C, 1), b_re_t, b_im_t)


def _s5_taps_kernel(pw0_ref, pw1_ref, pwf_ref, c_ref, b_ref, kconv_ref, wst_ref, kin_ref, *, pairs):
    n = S5_L * GROUP_C
    pair_lanes = 2 * P_C

    def rep(t):
        return jnp.broadcast_to(t[:, None, :], (S5_L, GROUP_C, pair_lanes)).reshape(n, pair_lanes)

    def til(c):
        return jnp.broadcast_to(c[None, :, :], (S5_L, GROUP_C, pair_lanes)).reshape(n, pair_lanes)

    def dot_t(a, b):
        return lax.dot_general(a, b, (((1,), (1,)), ((), ())), precision=lax.Precision.HIGHEST,
                               preferred_element_type=F32)

    def first(shape, axis):
        return lax.broadcasted_iota(jnp.int32, shape, axis) < P_C

    lane = lax.broadcasted_iota(jnp.int32, (GROUP_C, n), 1)
    for q in range(pairs):
        cr, ci = til(c_ref[q, 0]), til(c_ref[q, 1])
        b_re, b_im = b_ref[q, 0], b_ref[q, 1]
        x_re, x_im = _cmul(rep(pw0_ref[q, 0]), rep(pw0_ref[q, 1]), cr, ci)
        g0 = first(b_re.shape, 1)
        b2 = jnp.concatenate(
            [jnp.concatenate([jnp.where(g0, b_re, 0.0), jnp.where(g0, -b_im, 0.0)], axis=1),
             jnp.concatenate([jnp.where(g0, 0.0, b_re), jnp.where(g0, 0.0, -b_im)], axis=1)], axis=0)
        mrows = dot_t(b2, jnp.concatenate([x_re, x_im], axis=1))
        for gi in range(2):
            mrow = mrows[gi * GROUP_C:(gi + 1) * GROUP_C]
            for s in range(S5_L):
                blk = mrow if s == 0 else jnp.where(lane >= s * GROUP_C, pltpu.roll(mrow, s * GROUP_C, 1), 0.0)
                kconv_ref[2 * q + gi, s * GROUP_C:(s + 1) * GROUP_C, :] = blk.astype(BF16)
        e_re, e_im = _cmul(rep(pw1_ref[q, 0]), rep(pw1_ref[q, 1]), cr, ci)
        et_re, et_im = e_re.T, (-e_im).T
        top = first(et_re.shape, 0)
        kin_ref[q] = jnp.concatenate(
            [jnp.concatenate([jnp.where(top, et_re, 0.0), jnp.where(top, 0.0, et_re)], axis=1),
             jnp.concatenate([jnp.where(top, et_im, 0.0), jnp.where(top, 0.0, et_im)], axis=1)], axis=0).astype(BF16)
        f_re, f_im = _cmul(rep(pwf_ref[q, 0]), rep(pwf_ref[q, 1]), til(b_re), til(b_im))
        left = first(f_re.shape, 1)
        wst_ref[q] = jnp.concatenate(
            [jnp.concatenate([jnp.where(left, f_re, 0.0), jnp.where(left, f_im, 0.0)], axis=1),
             jnp.concatenate([jnp.where(left, 0.0, f_re), jnp.where(left, 0.0, f_im)], axis=1)], axis=0).astype(BF16)


def _s5_taps(pw0, pw1, pwf, c, bb):
    pairs = 4
    n = S5_L * GROUP_C
    pspec = pl.BlockSpec((pairs, 2, S5_L, 2 * P_C), lambda i: (i, 0, 0, 0))
    hspec = pl.BlockSpec((pairs, 2, GROUP_C, 2 * P_C), lambda i: (i, 0, 0, 0))
    return pl.pallas_call(
        functools.partial(_s5_taps_kernel, pairs=pairs),
        grid=(G_C // (2 * pairs),),
        in_specs=[pspec, pspec, pspec, hspec, hspec],
        out_specs=[pl.BlockSpec((2 * pairs, n, n), lambda i: (i, 0, 0)),
                   pl.BlockSpec((pairs, 2 * n, 4 * P_C), lambda i: (i, 0, 0)),
                   pl.BlockSpec((pairs, 4 * P_C, 2 * n), lambda i: (i, 0, 0))],
        out_shape=[jax.ShapeDtypeStruct((G_C, n, n), BF16), jax.ShapeDtypeStruct((G_C // 2, 2 * n, 4 * P_C), BF16),
                   jax.ShapeDtypeStruct((G_C // 2, 4 * P_C, 2 * n), BF16)],
        compiler_params=_params(),
        name="s5_taps",
    )(pw0, pw1, pwf, c, bb)


def _pair_lanes(a):
    a2 = a.reshape((G_C // 2, 2) + a.shape[1:])
    return jnp.concatenate([a2[:, 0], a2[:, 1]], axis=-1)


def _granule_transpose(vs):
    vs = list(vs)
    granule = lax.broadcasted_iota(jnp.int32, vs[0].shape, 1) // GROUP_C
    d = 1
    while d < len(vs):
        keep = (granule & d) == 0
        for i in range(len(vs)):
            if i & d == 0:
                a, b = vs[i], vs[i + d]
                vs[i] = jnp.where(keep, a, pltpu.roll(b, d * GROUP_C, 1))
                vs[i + d] = jnp.where(keep, pltpu.roll(a, V7X_LANES - d * GROUP_C, 1), b)
        d *= 2
    return vs


def _s5_in_kernel(x_ref, g_ref, win_ref, u_ref, u2_ref, *, nb, blocks):
    h = _rms(x_ref[...], g_ref[...]).astype(BF16)
    rows2 = blocks * nb
    for cc in range(N_BLK):
        cols = slice(cc * V7X_MXU_DIM, (cc + 1) * V7X_MXU_DIM)
        u = _dot(h, win_ref[:, cols])
        u_ref[:, cols] = u
        u4 = u.reshape(blocks, S5_L, nb, V7X_MXU_DIM)
        for sub in range(V7X_MXU_DIM // V7X_LANES):
            c = cc * (V7X_MXU_DIM // V7X_LANES) + sub
            for half in range(S5_L // GRANULES_PER_VREG):
                vs = []
                for kk in range(GRANULES_PER_VREG):
                    slab = u4[:, half * GRANULES_PER_VREG + kk, :, sub * V7X_LANES:(sub + 1) * V7X_LANES]
                    vs.append(slab.reshape(rows2, V7X_LANES))
                ws = _granule_transpose(vs)
                for gi in range(GRANULES_PER_VREG):
                    lane0 = (c * GRANULES_PER_VREG + gi) * S5_L * GROUP_C + half * V7X_LANES
                    u2_ref[:, lane0:lane0 + V7X_LANES] = ws[gi].astype(BF16)


def _s5_in(x, g, win, nb, blocks):
    rows = blocks * S5_L * nb
    n_blocks = x.shape[0] // (S5_L * nb)
    row_spec = pl.BlockSpec((rows, D_MODEL), lambda i: (i, 0))
    return pl.pallas_call(
        functools.partial(_s5_in_kernel, nb=nb, blocks=blocks),
        grid=(x.shape[0] // rows,),
        in_specs=[row_spec, _const_spec((1, D_MODEL)), _const_spec((D_MODEL, D_MODEL))],
        out_specs=[row_spec, pl.BlockSpec((blocks * nb, S5_L * D_MODEL), lambda i: (i, 0))],
        out_shape=[jax.ShapeDtypeStruct(x.shape, F32),
                   jax.ShapeDtypeStruct((n_blocks * nb, S5_L * D_MODEL), BF16)],
        compiler_params=_params(),
        name="s5_in",
    )(x, g, win)


def _s5_core_kernel(u_ref, wst_ref, kconv_ref, kin_ref, a_ref, s0re_ref, s0im_ref, y_ref, sfre_ref, sfim_ref,
                    sc_s, sin_s, *, nb, n_blocks):
    pw = 2 * S5_L * GROUP_C
    sw = 2 * P_C
    ub = [u_ref[:, q * pw:(q + 1) * pw] for q in range(S5_PAIRS_PER_STEP)]
    for q in range(S5_PAIRS_PER_STEP):
        sc_s[q] = _dot(ub[q], wst_ref[q])
    a_re = [jnp.broadcast_to(a_ref[:, 2 * sw * q:2 * sw * q + sw], (nb, sw)) for q in range(S5_PAIRS_PER_STEP)]
    a_im = [jnp.broadcast_to(a_ref[:, 2 * sw * q + sw:2 * sw * (q + 1)], (nb, sw)) for q in range(S5_PAIRS_PER_STEP)]
    s_re = [s0re_ref[:, sw * q:sw * (q + 1)] for q in range(S5_PAIRS_PER_STEP)]
    s_im = [s0im_ref[:, sw * q:sw * (q + 1)] for q in range(S5_PAIRS_PER_STEP)]
    for blk in range(n_blocks):
        rs = slice(blk * nb, (blk + 1) * nb)
        for q in range(S5_PAIRS_PER_STEP):
            sin_s[q, rs, 0:sw] = s_re[q]
            sin_s[q, rs, sw:2 * sw] = s_im[q]
            n_re, n_im = _cmul(a_re[q], a_im[q], s_re[q], s_im[q])
            s_re[q], s_im[q] = n_re + sc_s[q, rs, 0:sw], n_im + sc_s[q, rs, sw:2 * sw]
    for q in range(S5_PAIRS_PER_STEP):
        sfre_ref[:, sw * q:sw * (q + 1)] = s_re[q]
        sfim_ref[:, sw * q:sw * (q + 1)] = s_im[q]
        half = S5_L * GROUP_C
        conv = jnp.concatenate([_dot(ub[q][:, :half], kconv_ref[2 * q]),
                                _dot(ub[q][:, half:], kconv_ref[2 * q + 1])], axis=1)
        y_ref[:, q * pw:(q + 1) * pw] = conv + _dot(sin_s[q].astype(BF16), kin_ref[q])


def _s5_core(u2, wst, kconv, kin, a_pair, s0_re, s0_im, nb):
    rows = u2.shape[0]
    pw = 2 * S5_L * GROUP_C
    sw = 2 * P_C
    cols = S5_PAIRS_PER_STEP * pw
    scols = S5_PAIRS_PER_STEP * 2 * sw
    state_spec = pl.BlockSpec((nb, S5_PAIRS_PER_STEP * sw), lambda i: (0, i))
    return pl.pallas_call(
        functools.partial(_s5_core_kernel, nb=nb, n_blocks=rows // nb),
        grid=(u2.shape[1] // cols,),
        in_specs=[pl.BlockSpec((rows, cols), lambda i: (0, i)),
                  pl.BlockSpec((S5_PAIRS_PER_STEP, pw, 2 * sw), lambda i: (i, 0, 0)),
                  pl.BlockSpec((2 * S5_PAIRS_PER_STEP, pw // 2, pw // 2), lambda i: (i, 0, 0)),
                  pl.BlockSpec((S5_PAIRS_PER_STEP, 2 * sw, pw), lambda i: (i, 0, 0)),
                  pl.BlockSpec((1, scols), lambda i: (0, i)), state_spec, state_spec],
        out_specs=[pl.BlockSpec((rows, cols), lambda i: (0, i)), state_spec, state_spec],
        out_shape=[jax.ShapeDtypeStruct(u2.shape, F32), jax.ShapeDtypeStruct((nb, D_STATE), F32),
                   jax.ShapeDtypeStruct((nb, D_STATE), F32)],
        scratch_shapes=[pltpu.VMEM((S5_PAIRS_PER_STEP, rows, 2 * sw), F32),
                        pltpu.VMEM((S5_PAIRS_PER_STEP, rows, 2 * sw), F32)],
        compiler_params=_params(),
        name="s5_core",
    )(u2, wst, kconv, kin, a_pair, s0_re, s0_im)


def _s5_out_kernel(x_ref, u_ref, y2_ref, dskip_ref, wglu_ref, o_ref, y_s, gate_s, *, nb, blocks):
    rows = blocks * S5_L * nb
    for cc in range(N_BLK):
        cols = slice(cc * V7X_MXU_DIM, (cc + 1) * V7X_MXU_DIM)
        for sub in range(V7X_MXU_DIM // V7X_LANES):
            c = cc * (V7X_MXU_DIM // V7X_LANES) + sub
            for half in range(S5_L // GRANULES_PER_VREG):
                lanes = [(c * GRANULES_PER_VREG + gi) * S5_L * GROUP_C + half * V7X_LANES
                         for gi in range(GRANULES_PER_VREG)]
                ws = _granule_transpose([y2_ref[:, l0:l0 + V7X_LANES] for l0 in lanes])
                for kk in range(GRANULES_PER_VREG):
                    y_s[cc, :, half * GRANULES_PER_VREG + kk, :, sub * V7X_LANES:(sub + 1) * V7X_LANES] = (
                        ws[kk].reshape(blocks, nb, V7X_LANES))
        y = y_s[cc].reshape(rows, V7X_MXU_DIM) + dskip_ref[:, cols] * u_ref[:, cols]
        gy = _gelu(y).astype(BF16)
        for acc, wcols in ((o_ref, slice(0, D_MODEL)), (gate_s, slice(D_MODEL, 2 * D_MODEL))):
            part = _dot(gy, wglu_ref[cols, wcols])
            if cc == 0:
                acc[...] = part
            else:
                acc[...] += part
    o_ref[...] = x_ref[...] + o_ref[...] * jax.nn.sigmoid(gate_s[...])


def _s5_out(x, u, y2, dskip, wglu, nb, blocks):
    rows = blocks * S5_L * nb
    row_spec = pl.BlockSpec((rows, D_MODEL), lambda i: (i, 0))
    return pl.pallas_call(
        functools.partial(_s5_out_kernel, nb=nb, blocks=blocks),
        grid=(x.shape[0] // rows,),
        in_specs=[row_spec, row_spec, pl.BlockSpec((blocks * nb, S5_L * D_MODEL), lambda i: (i, 0)),
                  _const_spec((1, D_MODEL)), _const_spec((D_MODEL, 2 * D_MODEL))],
        out_specs=row_spec,
        out_shape=jax.ShapeDtypeStruct(x.shape, F32),
        scratch_shapes=[pltpu.VMEM((N_BLK, blocks, S5_L, nb, V7X_MXU_DIM), F32), pltpu.VMEM((rows, D_MODEL), F32)],
        compiler_params=_params(),
        name="s5_out",
    )(x, u, y2, dskip, wglu)


def _pair_state(re, im):
    r = re.reshape(G_C // 2, 2 * P_C)
    i = im.reshape(G_C // 2, 2 * P_C)
    return jnp.concatenate([r, i], axis=-1).reshape(1, 2 * D_STATE)


def _s5_short_kernel(x_ref, g_ref, win_ref, kconv_ref, wsta_ref, wstb_ref, kina_ref, kinb_ref, a_ref, dskip_ref,
                     wglu_ref, s0re_ref, s0im_ref, o_ref, sfre_ref, sfim_ref, u8_s, y8_s, y_s, *, nb, steps):
    slots = GRANULES_PER_VREG
    gl = slots * GROUP_C
    sw = 2 * P_C
    x = x_ref[...]
    u = _dot(_rms(x, g_ref[...]).astype(BF16), win_ref[...])
    zero = jnp.zeros((nb, V7X_LANES), F32)
    for c in range(D_MODEL // V7X_LANES):
        cols = slice(c * V7X_LANES, (c + 1) * V7X_LANES)
        ws = _granule_transpose([u[k * nb:(k + 1) * nb, cols] for k in range(steps)] + [zero] * (slots - steps))
        for gi in range(GRANULES_PER_VREG):
            g = c * GRANULES_PER_VREG + gi
            u8_s[:, g * gl:(g + 1) * gl] = ws[gi].astype(BF16)
    for q in range(G_C // 2):
        up = u8_s[:, 2 * q * gl:(2 * q + 2) * gl]
        reps = slots // steps
        sc = _dot(up, jnp.concatenate([wsta_ref[q]] * reps + [wstb_ref[q]] * reps, axis=0))
        s_re, s_im = s0re_ref[:, sw * q:sw * (q + 1)], s0im_ref[:, sw * q:sw * (q + 1)]
        n_re, n_im = _cmul(a_ref[:, 2 * sw * q:2 * sw * q + sw], a_ref[:, 2 * sw * q + sw:2 * sw * (q + 1)], s_re, s_im)
        sfre_ref[:, sw * q:sw * (q + 1)] = n_re + sc[:, :sw]
        sfim_ref[:, sw * q:sw * (q + 1)] = n_im + sc[:, sw:]
        yk = _dot(jnp.concatenate([s_re, s_im], axis=1).astype(BF16),
                  jnp.concatenate([kina_ref[q], kinb_ref[q]], axis=1))
        for gi in range(2):
            y8_s[:, (2 * q + gi) * gl:(2 * q + gi + 1) * gl] = (
                _dot(up[:, gi * gl:(gi + 1) * gl], kconv_ref[2 * q + gi]) + yk[:, gi * gl:(gi + 1) * gl])
    for c in range(D_MODEL // V7X_LANES):
        ws = _granule_transpose([y8_s[:, (c * GRANULES_PER_VREG + gi) * gl:(c * GRANULES_PER_VREG + gi + 1) * gl]
                                 for gi in range(GRANULES_PER_VREG)])
        for k in range(steps):
            y_s[k * nb:(k + 1) * nb, c * V7X_LANES:(c + 1) * V7X_LANES] = ws[k]
    y = y_s[...] + dskip_ref[...] * u
    o = _dot(_gelu(y).astype(BF16), wglu_ref[...])
    o_ref[...] = x + o[:, :D_MODEL] * jax.nn.sigmoid(o[:, D_MODEL:])


def _s5_short(x, g, win, kconv, wst, kin, a_pair, dskip, wglu, s0_re, s0_im, nb, steps):
    assert S5_L % steps == 0 and GRANULES_PER_VREG % steps == 0
    rows = nb * steps
    slots = GRANULES_PER_VREG
    gl = slots * GROUP_C
    n = S5_L * GROUP_C
    srows = steps * GROUP_C
    row_spec = pl.BlockSpec((rows, D_MODEL), lambda i: (0, 0))
    state_spec = pl.BlockSpec((nb, D_STATE), lambda i: (0, 0))
    return pl.pallas_call(
        functools.partial(_s5_short_kernel, nb=nb, steps=steps),
        grid=(1,),
        in_specs=[row_spec, _const_spec((1, D_MODEL)), _const_spec((D_MODEL, D_MODEL)),
                  pl.BlockSpec((G_C, gl, gl), lambda i: (0, 0, 0)),
                  pl.BlockSpec((G_C // 2, srows, 4 * P_C), lambda i: (0, n // srows - 1, 0)),
                  pl.BlockSpec((G_C // 2, srows, 4 * P_C), lambda i: (0, 2 * n // srows - 1, 0)),
                  pl.BlockSpec((G_C // 2, 4 * P_C, gl), lambda i: (0, 0, 0)),
                  pl.BlockSpec((G_C // 2, 4 * P_C, gl), lambda i: (0, 0, n // gl)),
                  _const_spec((1, 2 * D_STATE)), _const_spec((1, D_MODEL)), _const_spec((D_MODEL, 2 * D_MODEL)),
                  state_spec, state_spec],
        out_specs=[row_spec, state_spec, state_spec],
        out_shape=[jax.ShapeDtypeStruct(x.shape, F32), jax.ShapeDtypeStruct((nb, D_STATE), F32),
                   jax.ShapeDtypeStruct((nb, D_STATE), F32)],
        scratch_shapes=[pltpu.VMEM((nb, G_C * gl), BF16), pltpu.VMEM((nb, G_C * gl), F32),
                        pltpu.VMEM((rows, D_MODEL), F32)],
        compiler_params=_params(),
        name="s5_short",
    )(x, g, win, kconv, wst, wst, kin, kin, a_pair, dskip, wglu, s0_re, s0_im)


def _block_diag(w, n_per_blk):
    n, k_in, k_out = w.shape
    wb = w.reshape(n // n_per_blk, n_per_blk, k_in, k_out)
    eye = jnp.eye(n_per_blk, dtype=w.dtype)
    out = jnp.einsum("jgio,gk->jgiko", wb, eye)
    return out.reshape(n // n_per_blk, n_per_blk * k_in, n_per_blk * k_out)


def _to_rows(a):
    return jnp.swapaxes(a, 0, 1).reshape(a.shape[0] * a.shape[1], a.shape[2])


def _from_rows(a, nb):
    return jnp.swapaxes(a.reshape(a.shape[0] // nb, nb, a.shape[1]), 0, 1)


def kernel(x_prompt, x_sample, state_rglru_conv, state_rglru_h, state_s5_re, state_s5_im, norm_mix, norm_ffn, norm_f, w_ff1, w_ff2, w_in_a, sgu_g, w_s, b_s, w_out_a, w_in_b, conv_w, conv_b, w_a, b_a, w_x, b_x, lam, w_out_b, w_in_c, lam_re, lam_im, log_dt, b_re, b_im, c_re, c_im, d_skip, w_glu):
    bp, tp, _ = x_prompt.shape
    bs, ts, _ = x_sample.shape
    rows_p, rows_s = bp * tp, bs * ts
    assert DEPTH % N_MIXERS == 1 and DEPTH > 1, "first and last layers must be SGU layers"
    assert tp % CHUNK == 0 and ts < CHUNK
    assert rows_s % FFN_SAMPLE_ROWS == 0 and rows_p % FFN_ROWS == 0

    row = lambda v: v.reshape(1, -1)
    norm_ffn3 = norm_ffn.reshape(DEPTH, 1, D_MODEL)
    w1, w2, gf = w_ff1, w_ff2, row(norm_f)
    w_in_a_bf, w_out_a_bf = w_in_a.astype(BF16), w_out_a.astype(BF16)

    x_p, x_s = x_prompt, _to_rows(x_sample)
    outs_v, conv_p, h_p, conv_s, h_s, sre_p, sim_p, sre_s, sim_s = [], [], [], [], [], [], [], [], []
    for layer in range(DEPTH):
        j, kind = layer // N_MIXERS, layer % N_MIXERS
        first, last = layer == 0, layer == DEPTH - 1
        g = row(norm_mix[layer])
        if kind == 0:
            p = dict(g=g, j=j, win=w_in_a_bf, wout=w_out_a_bf, sg=row(sgu_g[j]),
                     w_tril=jnp.where(jnp.tril(jnp.ones((CHUNK, CHUNK), dtype=bool)), w_s[j], 0.0).astype(BF16),
                     bias=jnp.repeat(b_s[j].T, HD_A, axis=1),
                     wl=jnp.repeat(w_s[j][:, :ts, :ts].reshape(HEADS_A, ts * ts).T, HD_A, axis=1),
                     bl=jnp.repeat(b_s[j][:, :ts].T, HD_A, axis=1))
            if first:
                x_p = _sgu_prompt(x_p, p, bp, tp, False, True)
            elif last:
                x_p = _sgu_prompt(x_p, p, bp, tp, True, False).reshape(rows_p, D_MODEL)
            else:
                raise NotImplementedError("interior SGU layers")
            x_s, v = _sgu_sample(x_s, p, bs, ts)
            outs_v.append(_from_rows(v, bs))
        elif kind == 1:
            p = dict(g=g, win=w_in_b[j].astype(BF16), cw=conv_w[j], cb=row(conv_b[j]),
                     wa=_block_diag(w_a[j], HEADS_PER_BLK).astype(BF16), ba=row(b_a[j]),
                     wx=_block_diag(w_x[j], HEADS_PER_BLK).astype(BF16), bx=row(b_x[j]),
                     lam=row(lam[j]), wout=w_out_b[j].astype(BF16))
            dt_s = state_rglru_h.dtype
            x_p, cp, hp = _rglru(x_p, p, jnp.zeros(((CONV_W - 1) * bp, D_MODEL), dt_s),
                                 jnp.zeros((bp, D_MODEL), dt_s), bp, RGLRU_PROMPT_STEPS)
            x_s, cs, hs = _rglru(x_s, p, _to_rows(state_rglru_conv[j]), state_rglru_h[j], bs, ts)
            conv_p.append(_from_rows(cp, bp)); h_p.append(hp)
            conv_s.append(_from_rows(cs, bs)); h_s.append(hs)
        else:
            bbre, bbim, pw = _s5_disc(lam_re[j], lam_im[j], log_dt[j],
                                      jnp.swapaxes(b_re[j], 1, 2), jnp.swapaxes(b_im[j], 1, 2))
            p = dict(g=g, win=w_in_c[j].astype(BF16), dskip=row(d_skip[j]), wglu=w_glu[j].astype(BF16))
            pwg = _pair_lanes(jnp.transpose(pw, (2, 0, 1, 3)))
            kconv, wst, kin = _s5_taps(pwg[:, :, :S5_L], pwg[:, :, 1:], pwg[:, :, S5_L - 1::-1],
                                       _pair_lanes(jnp.stack([c_re[j], c_im[j]], axis=1)),
                                       _pair_lanes(jnp.stack([bbre, bbim], axis=1)))
            zs = jnp.zeros((bp, D_STATE), state_s5_re.dtype)
            u, u2 = _s5_in(x_p, g, p["win"], bp, S5_TILE_BLOCKS)
            y2, rp, ip = _s5_core(u2, wst, kconv, kin, _pair_state(pw[0, S5_L], pw[1, S5_L]), zs, zs, bp)
            x_p = _s5_out(x_p, u, y2, p["dskip"], p["wglu"], bp, S5_TILE_BLOCKS)
            x_s, rs, is_ = _s5_short(x_s, g, p["win"], kconv, wst, kin, _pair_state(pw[0, ts], pw[1, ts]),
                                     p["dskip"], p["wglu"], state_s5_re[j].reshape(bs, D_STATE),
                                     state_s5_im[j].reshape(bs, D_STATE), bs, ts)
            sre_p.append(rp.reshape(bp, G_C, P_C)); sim_p.append(ip.reshape(bp, G_C, P_C))
            sre_s.append(rs.reshape(bs, G_C, P_C)); sim_s.append(is_.reshape(bs, G_C, P_C))
        x_p, x_s = _ffn(x_p, x_s, norm_ffn3, w1, w2, gf, layer, last)

    y_prompt = x_p.reshape(bp, tp, D_MODEL)
    y_sample = _from_rows(x_s, bs)
    return (y_prompt, y_sample, jnp.stack(outs_v), jnp.stack(conv_p), jnp.stack(h_p), jnp.stack(conv_s),
            jnp.stack(h_s), jnp.stack(sre_p), jnp.stack(sim_p), jnp.stack(sre_s), jnp.stack(sim_s))
```

```python
import functools

import jax
import jax.numpy as jnp
from jax import lax
from jax.experimental import pallas as pl
from jax.experimental.pallas import tpu as pltpu

F32 = jnp.float32
BF16 = jnp.bfloat16

D_MODEL = 1024
DEPTH = 4
N_MIXERS = 3
EPS = 1e-6
CHUNK = 128
HEADS_A = 8
HD_A = D_MODEL // HEADS_A
HEADS_B = 16
HD_B = D_MODEL // HEADS_B
CONV_W = 4
LRU_C = 8.0
GROUP_C = 16
G_C = D_MODEL // GROUP_C
P_C = 64
D_STATE = G_C * P_C
D_FF = 4 * D_MODEL

V7X_LANES = 128
V7X_MXU_DIM = 256
V7X_VMEM_BYTES = 64 * 1024 * 1024
VMEM_LIMIT = V7X_VMEM_BYTES - 8 * 1024 * 1024

N_BLK = D_MODEL // V7X_MXU_DIM
HEADS_PER_BLK = V7X_MXU_DIM // HD_B

FFN_ROWS = 1024
FFN_SAMPLE_ROWS = 512
FFN_COL_CHUNK = 1024
FFN_STAGE_BYTES = 1024 * 1024
S5_L = 16
S5_TILE_BLOCKS = 8
S5_PAIRS_PER_STEP = 4
GRANULES_PER_VREG = V7X_LANES // GROUP_C
RGLRU_PROMPT_STEPS = 128
SGU_CHUNKS_PER_STEP = 2


def _rms(x, g):
    return (x * lax.rsqrt(jnp.mean(x * x, axis=-1, keepdims=True) + EPS)) * g


GELU_C0 = 0.7978845608028654
GELU_C1 = GELU_C0 * 0.044715


def _gelu(x):
    return x * (0.5 + 0.5 * jnp.tanh(x * (GELU_C0 + GELU_C1 * (x * x))))


def _dot(a, b):
    return jnp.dot(a, b, preferred_element_type=F32)


def _const_spec(shape):
    zeros = (0,) * len(shape)
    return pl.BlockSpec(shape, lambda i: zeros, pipeline_mode=pl.Buffered(1))


def _layer_spec(shape, layer):
    idx = (layer,) + (0,) * len(shape)
    return pl.BlockSpec((None,) + tuple(shape), lambda i: idx, pipeline_mode=pl.Buffered(1))


def _params():
    return pltpu.CompilerParams(dimension_semantics=("arbitrary",), vmem_limit_bytes=VMEM_LIMIT)


def _ffn_tile(x_ref, g_ref, w1_ref, w2_ref, gf_ref, o_ref, final_norm):
    h = _rms(x_ref[...], g_ref[...]).astype(BF16)
    for j in range(D_FF // FFN_COL_CHUNK):
        cols = slice(j * FFN_COL_CHUNK, (j + 1) * FFN_COL_CHUNK)
        a = jnp.square(jnp.maximum(_dot(h, w1_ref[:, cols]), 0.0)).astype(BF16)
        part = _dot(a, w2_ref[cols, :])
        if j == 0:
            o_ref[...] = part
        else:
            o_ref[...] += part
    y = x_ref[...] + o_ref[...]
    if final_norm:
        y = _rms(y, gf_ref[...])
    o_ref[...] = y


def _load_as_bf16(jobs):
    def copies_of(src, stage, sem):
        rows = stage.shape[1]
        return [pltpu.make_async_copy(src.at[pl.ds(k * rows, rows)], stage.at[k % 2], sem.at[k % 2])
                for k in range(src.shape[0] // rows)]

    plans = [copies_of(src, stage, sem) for src, _, stage, sem in jobs]
    n = len(plans[0])
    assert all(len(p) == n for p in plans)
    for p in plans:
        p[0].start()
    for k in range(n):
        if k + 1 < n:
            for p in plans:
                p[k + 1].start()
        for p, (_, dst, stage, _) in zip(plans, jobs):
            p[k].wait()
            rows = stage.shape[1]
            dst[k * rows:(k + 1) * rows, :] = stage[k % 2].astype(BF16)


def _ffn_kernel(xp_ref, xs_ref, g_ref, w1_hbm, w2_hbm, gf_ref, op_ref, os_ref, w1_s, w2_s, stage1, stage2,
                sem1, sem2, *, final_norm, n_p, layer):
    i = pl.program_id(0)

    @pl.when(i == 0)
    def _():
        _load_as_bf16([(w1_hbm.at[layer], w1_s, stage1, sem1), (w2_hbm.at[layer], w2_s, stage2, sem2)])

    @pl.when(i < n_p)
    def _():
        _ffn_tile(xp_ref, g_ref, w1_s, w2_s, gf_ref, op_ref, final_norm)

    @pl.when(i >= n_p)
    def _():
        _ffn_tile(xs_ref, g_ref, w1_s, w2_s, gf_ref, os_ref, final_norm)


def _ffn(x_p, x_s, g, w1, w2, gf, layer, final_norm):
    n_p, n_s = x_p.shape[0] // FFN_ROWS, x_s.shape[0] // FFN_SAMPLE_ROWS
    p_spec = pl.BlockSpec((FFN_ROWS, D_MODEL), lambda i: (jnp.minimum(i, n_p - 1), 0))
    s_spec = pl.BlockSpec((FFN_SAMPLE_ROWS, D_MODEL), lambda i: (jnp.maximum(i - n_p, 0), 0))
    hbm = pl.BlockSpec(memory_space=pl.ANY)
    return pl.pallas_call(
        functools.partial(_ffn_kernel, final_norm=final_norm, n_p=n_p, layer=layer),
        grid=(n_p + n_s,),
        in_specs=[p_spec, s_spec, _layer_spec((1, D_MODEL), layer), hbm, hbm, _const_spec((1, D_MODEL))],
        out_specs=[p_spec, s_spec],
        out_shape=[jax.ShapeDtypeStruct(x_p.shape, F32), jax.ShapeDtypeStruct(x_s.shape, F32)],
        scratch_shapes=[pltpu.VMEM((D_MODEL, D_FF), BF16), pltpu.VMEM((D_FF, D_MODEL), BF16),
                        pltpu.VMEM((2, FFN_STAGE_BYTES // (4 * D_FF), D_FF), F32),
                        pltpu.VMEM((2, FFN_STAGE_BYTES // (4 * D_MODEL), D_MODEL), F32),
                        pltpu.SemaphoreType.DMA((2,)), pltpu.SemaphoreType.DMA((2,))],
        compiler_params=_params(),
        name="ffn",
    )(x_p, x_s, g, w1, w2, gf)


def _sgu_front(x, g_ref, win_ref, sg_ref):
    h = _rms(x, g_ref[...]).astype(BF16)
    v = _rms(_gelu(_dot(h, win_ref[:, D_MODEL:])), sg_ref[...])
    return _gelu(_dot(h, win_ref[:, :D_MODEL])), v


def _rows_to_batch_major(x, nb):
    steps = x.shape[0] // nb
    return jnp.swapaxes(x.reshape(steps, nb, x.shape[1]), 0, 1).reshape(x.shape)


def _rows_to_time_major(x, nb):
    steps = x.shape[0] // nb
    return jnp.swapaxes(x.reshape(nb, steps, x.shape[1]), 0, 1).reshape(x.shape)


def _sgu_prompt_kernel(x_ref, g_ref, win_ref, sg_ref, w_ref, bias_ref, wout_ref, o_ref, y_s, *, nb, in_tm, out_tm):
    rows = nb * CHUNK
    for c in range(SGU_CHUNKS_PER_STEP):
        if in_tm:
            x = _rows_to_batch_major(x_ref[c * rows:(c + 1) * rows, :], nb)
        else:
            x = x_ref[:, c * CHUNK:(c + 1) * CHUNK, :].reshape(rows, D_MODEL)
        u, v = _sgu_front(x, g_ref, win_ref, sg_ref)
        vb = v.astype(BF16)
        y = y_s.at[c]
        for b in range(nb):
            rs = slice(b * CHUNK, (b + 1) * CHUNK)
            for g in range(HEADS_A):
                cs = slice(g * HD_A, (g + 1) * HD_A)
                mixed = _dot(w_ref[g], vb[rs, cs]) + bias_ref[:, cs]
                y[rs, cs] = (u[rs, cs] * mixed).astype(BF16)
        o = x + _dot(y[...], wout_ref[...])
        if out_tm:
            o_ref[c * rows:(c + 1) * rows, :] = _rows_to_time_major(o, nb)
        else:
            o_ref[:, c * CHUNK:(c + 1) * CHUNK, :] = o.reshape(nb, CHUNK, D_MODEL)


def _sgu_sample_kernel(x_ref, g_ref, win_ref, sg_ref, wl_ref, bl_ref, wout_ref, o_ref, v_ref, *, nb, steps):
    x = x_ref[...]
    u, v = _sgu_front(x, g_ref, win_ref, sg_ref)
    v_ref[...] = v
    mixed = []
    for t in range(steps):
        m = None
        for s in range(t + 1):
            term = wl_ref[t * steps + s:t * steps + s + 1, :] * v[s * nb:(s + 1) * nb, :]
            m = term if m is None else m + term
        mixed.append(m + bl_ref[t:t + 1, :])
    y = (u * jnp.concatenate(mixed, axis=0)).astype(BF16)
    o_ref[...] = x + _dot(y, wout_ref[...])


def _sgu_prompt(x, p, nb, n_steps, in_tm, out_tm):
    rows = nb * CHUNK
    span = SGU_CHUNKS_PER_STEP * CHUNK
    tm_spec = pl.BlockSpec((SGU_CHUNKS_PER_STEP * rows, D_MODEL), lambda i: (i, 0))
    bm_spec = pl.BlockSpec((nb, span, D_MODEL), lambda i: (0, i, 0))
    out_shape = (jax.ShapeDtypeStruct((nb * n_steps, D_MODEL), F32) if out_tm
                 else jax.ShapeDtypeStruct((nb, n_steps, D_MODEL), F32))
    return pl.pallas_call(
        functools.partial(_sgu_prompt_kernel, nb=nb, in_tm=in_tm, out_tm=out_tm),
        grid=(n_steps // span,),
        in_specs=[tm_spec if in_tm else bm_spec, _const_spec((1, D_MODEL)),
                  _layer_spec((D_MODEL, 2 * D_MODEL), p["j"]),
                  _const_spec((1, D_MODEL)), _const_spec((HEADS_A, CHUNK, CHUNK)),
                  _const_spec((CHUNK, D_MODEL)), _layer_spec((D_MODEL, D_MODEL), p["j"])],
        out_specs=tm_spec if out_tm else bm_spec,
        out_shape=out_shape,
        scratch_shapes=[pltpu.VMEM((SGU_CHUNKS_PER_STEP, rows, D_MODEL), BF16)],
        compiler_params=_params(),
        name="sgu_prompt",
    )(x, p["g"], p["win"], p["sg"], p["w_tril"], p["bias"], p["wout"])


def _sgu_sample(x, p, nb, steps):
    tile = steps * nb
    tile_spec = pl.BlockSpec((tile, D_MODEL), lambda i: (0, 0))
    return pl.pallas_call(
        functools.partial(_sgu_sample_kernel, nb=nb, steps=steps),
        grid=(1,),
        in_specs=[tile_spec, _const_spec((1, D_MODEL)), _layer_spec((D_MODEL, 2 * D_MODEL), p["j"]),
                  _const_spec((1, D_MODEL)), _const_spec((steps * steps, D_MODEL)),
                  _const_spec((steps, D_MODEL)), _layer_spec((D_MODEL, D_MODEL), p["j"])],
        out_specs=[tile_spec, tile_spec],
        out_shape=[jax.ShapeDtypeStruct((tile, D_MODEL), F32), jax.ShapeDtypeStruct((tile, D_MODEL), F32)],
        compiler_params=_params(),
        name="sgu_sample",
    )(x, p["g"], p["win"], p["sg"], p["wl"], p["bl"], p["wout"])


def _rglru_kernel(x_ref, g_ref, win_ref, cw_ref, cb_ref, wa_ref, ba_ref, wx_ref, bx_ref, lam_ref, wout_ref,
                  conv0_ref, h0_ref, o_ref, conv_ref, h_ref, xext_s, a_s, b_s, y_s, *, nb, steps):
    rows = nb * steps
    halo = (CONV_W - 1) * nb

    @pl.when(pl.program_id(0) == 0)
    def _():
        conv_ref[...] = conv0_ref[...]
        h_ref[...] = h0_ref[...]

    def blk_cols(j):
        return slice(j * V7X_MXU_DIM, (j + 1) * V7X_MXU_DIM)

    x = x_ref[...]
    h = _rms(x, g_ref[...]).astype(BF16)
    for j in range(N_BLK):
        cols = blk_cols(j)
        xext, a_j, b_j = xext_s.at[j], a_s.at[j], b_s.at[j]
        xext[0:halo, :] = conv_ref[:, cols]
        xext[halo:halo + rows, :] = _dot(h, win_ref[:, blk_cols(N_BLK + j)])
        gate = _gelu(_dot(h, win_ref[:, cols]))
        conv = None
        for w in range(CONV_W):
            term = xext[w * nb:w * nb + rows, :] * cw_ref[w:w + 1, cols]
            conv = term if conv is None else conv + term
        conv_ref[:, cols] = xext[rows:rows + halo, :]
        xc = cb_ref[:, cols] + conv
        xcb = xc.astype(BF16)
        r = jax.nn.sigmoid(_dot(xcb, wa_ref[j]) + ba_ref[:, cols])
        ig = jax.nn.sigmoid(_dot(xcb, wx_ref[j]) + bx_ref[:, cols])
        neg_log_a = (LRU_C * r) * jax.nn.softplus(-lam_ref[:, cols])
        a = jnp.exp(-neg_log_a)
        a_j[...] = a
        w = jnp.tanh(neg_log_a) * (a * a + 1.0)
        mult = jnp.where(w > 0.0, w * lax.rsqrt(w), 0.0)
        b_j[...] = mult * (ig * xc)
        hcur = h_ref[:, cols]
        for t in range(steps):
            sl = slice(t * nb, (t + 1) * nb)
            hcur = a_j[sl, :] * hcur + b_j[sl, :]
            b_j[sl, :] = hcur
        h_ref[:, cols] = hcur
        y_s[:, cols] = (b_j[...] * gate).astype(BF16)
    o_ref[...] = x + _dot(y_s[...], wout_ref[...])


def _rglru(x, p, conv0, h0, nb, steps):
    rows = nb * steps
    halo = (CONV_W - 1) * nb
    row_spec = pl.BlockSpec((rows, D_MODEL), lambda i: (i, 0))
    vec = _const_spec((1, D_MODEL))
    blk = _const_spec((N_BLK, V7X_MXU_DIM, V7X_MXU_DIM))
    return pl.pallas_call(
        functools.partial(_rglru_kernel, nb=nb, steps=steps),
        grid=(x.shape[0] // rows,),
        in_specs=[row_spec, vec, _const_spec((D_MODEL, 2 * D_MODEL)), _const_spec((CONV_W, D_MODEL)), vec,
                  blk, vec, blk, vec, vec, _const_spec((D_MODEL, D_MODEL)),
                  _const_spec((halo, D_MODEL)), _const_spec((nb, D_MODEL))],
        out_specs=[row_spec, pl.BlockSpec((halo, D_MODEL), lambda i: (0, 0)),
                   pl.BlockSpec((nb, D_MODEL), lambda i: (0, 0))],
        out_shape=[jax.ShapeDtypeStruct(x.shape, F32), jax.ShapeDtypeStruct((halo, D_MODEL), F32),
                   jax.ShapeDtypeStruct((nb, D_MODEL), F32)],
        scratch_shapes=[pltpu.VMEM((N_BLK, rows + halo, V7X_MXU_DIM), F32), pltpu.VMEM((N_BLK, rows, V7X_MXU_DIM), F32),
                        pltpu.VMEM((N_BLK, rows, V7X_MXU_DIM), F32), pltpu.VMEM((rows, D_MODEL), BF16)],
        compiler_params=_params(),
        name="rglru",
    )(x, p["g"], p["win"], p["cw"], p["cb"], p["wa"], p["ba"], p["wx"], p["bx"], p["lam"], p["wout"], conv0, h0)


def _cmul(a_re, a_im, b_re, b_im):
    return a_re * b_re - a_im * b_im, a_re * b_im + a_im * b_re


def _s5_disc_kernel(lre_ref, lim_ref, ldt_ref, bre_ref, bim_ref, bbre_ref, bbim_ref, pw_ref):
    lr, li = lre_ref[...], lim_ref[...]
    dt = jnp.exp(ldt_ref[...])
    mag = jnp.exp(lr * dt)
    ab_re, ab_im = mag * jnp.cos(li * dt), mag * jnp.sin(li * dt)
    zr, zi = ab_re - 1.0, ab_im
    den = lr * lr + li * li
    q_re = (zr * lr + zi * li) / den
    q_im = (zi * lr - zr * li) / den
    br, bi = bre_ref[...], bim_ref[...]
    bbre_ref[...] = q_re[:, None, :] * br - q_im[:, None, :] * bi
    bbim_ref[...] = q_re[:, None, :] * bi + q_im[:, None, :] * br
    p_re, p_im = jnp.ones_like(ab_re), jnp.zeros_like(ab_re)
    for k in range(S5_L + 1):
        pw_ref[0, k] = p_re
        pw_ref[1, k] = p_im
        p_re, p_im = _cmul(p_re, p_im, ab_re, ab_im)


def _s5_disc(lam_re, lam_im, log_dt, b_re_t, b_im_t):
    ghp = jax.ShapeDtypeStruct((G_C, GROUP_C, P_C), F32)
    pw = jax.ShapeDtypeStruct((2, S5_L + 1, G_C, P_C), F32)
    return pl.pallas_call(_s5_disc_kernel, out_shape=[ghp, ghp, pw], name="s5_disc")(
        lam_re, lam_im, log_dt.reshape(G_C, 1), b_re_t, b_im_t)


def _s5_taps_kernel(pw0_ref, pw1_ref, pwf_ref, c_ref, b_ref, kconv_ref, wst_ref, kin_ref, *, pairs):
    n = S5_L * GROUP_C
    pair_lanes = 2 * P_C

    def rep(t):
        return jnp.broadcast_to(t[:, None, :], (S5_L, GROUP_C, pair_lanes)).reshape(n, pair_lanes)

    def til(c):
        return jnp.broadcast_to(c[None, :, :], (S5_L, GROUP_C, pair_lanes)).reshape(n, pair_lanes)

    def dot_t(a, b):
        return lax.dot_general(a, b, (((1,), (1,)), ((), ())), precision=lax.Precision.HIGHEST,
                               preferred_element_type=F32)

    def first(shape, axis):
        return lax.broadcasted_iota(jnp.int32, shape, axis) < P_C

    lane = lax.broadcasted_iota(jnp.int32, (GROUP_C, n), 1)
    for q in range(pairs):
        cr, ci = til(c_ref[q, 0]), til(c_ref[q, 1])
        b_re, b_im = b_ref[q, 0], b_ref[q, 1]
        x_re, x_im = _cmul(rep(pw0_ref[q, 0]), rep(pw0_ref[q, 1]), cr, ci)
        g0 = first(b_re.shape, 1)
        b2 = jnp.concatenate(
            [jnp.concatenate([jnp.where(g0, b_re, 0.0), jnp.where(g0, -b_im, 0.0)], axis=1),
             jnp.concatenate([jnp.where(g0, 0.0, b_re), jnp.where(g0, 0.0, -b_im)], axis=1)], axis=0)
        mrows = dot_t(b2, jnp.concatenate([x_re, x_im], axis=1))
        for gi in range(2):
            mrow = mrows[gi * GROUP_C:(gi + 1) * GROUP_C]
            for s in range(S5_L):
                blk = mrow if s == 0 else jnp.where(lane >= s * GROUP_C, pltpu.roll(mrow, s * GROUP_C, 1), 0.0)
                kconv_ref[2 * q + gi, s * GROUP_C:(s + 1) * GROUP_C, :] = blk.astype(BF16)
        e_re, e_im = _cmul(rep(pw1_ref[q, 0]), rep(pw1_ref[q, 1]), cr, ci)
        et_re, et_im = e_re.T, (-e_im).T
        top = first(et_re.shape, 0)
        kin_ref[q] = jnp.concatenate(
            [jnp.concatenate([jnp.where(top, et_re, 0.0), jnp.where(top, 0.0, et_re)], axis=1),
             jnp.concatenate([jnp.where(top, et_im, 0.0), jnp.where(top, 0.0, et_im)], axis=1)], axis=0).astype(BF16)
        f_re, f_im = _cmul(rep(pwf_ref[q, 0]), rep(pwf_ref[q, 1]), til(b_re), til(b_im))
        left = first(f_re.shape, 1)
        wst_ref[q] = jnp.concatenate(
            [jnp.concatenate([jnp.where(left, f_re, 0.0), jnp.where(left, f_im, 0.0)], axis=1),
             jnp.concatenate([jnp.where(left, 0.0, f_re), jnp.where(left, 0.0, f_im)], axis=1)], axis=0).astype(BF16)


def _s5_taps(pw0, pw1, pwf, c, bb):
    pairs = 4
    n = S5_L * GROUP_C
    pspec = pl.BlockSpec((pairs, 2, S5_L, 2 * P_C), lambda i: (i, 0, 0, 0))
    hspec = pl.BlockSpec((pairs, 2, GROUP_C, 2 * P_C), lambda i: (i, 0, 0, 0))
    return pl.pallas_call(
        functools.partial(_s5_taps_kernel, pairs=pairs),
        grid=(G_C // (2 * pairs),),
        in_specs=[pspec, pspec, pspec, hspec, hspec],
        out_specs=[pl.BlockSpec((2 * pairs, n, n), lambda i: (i, 0, 0)),
                   pl.BlockSpec((pairs, 2 * n, 4 * P_C), lambda i: (i, 0, 0)),
                   pl.BlockSpec((pairs, 4 * P_C, 2 * n), lambda i: (i, 0, 0))],
        out_shape=[jax.ShapeDtypeStruct((G_C, n, n), BF16), jax.ShapeDtypeStruct((G_C // 2, 2 * n, 4 * P_C), BF16),
                   jax.ShapeDtypeStruct((G_C // 2, 4 * P_C, 2 * n), BF16)],
        compiler_params=_params(),
        name="s5_taps",
    )(pw0, pw1, pwf, c, bb)


def _pair_lanes(a):
    a2 = a.reshape((G_C // 2, 2) + a.shape[1:])
    return jnp.concatenate([a2[:, 0], a2[:, 1]], axis=-1)


def _granule_transpose(vs):
    vs = list(vs)
    granule = lax.broadcasted_iota(jnp.int32, vs[0].shape, 1) // GROUP_C
    d = 1
    while d < len(vs):
        keep = (granule & d) == 0
        for i in range(len(vs)):
            if i & d == 0:
                a, b = vs[i], vs[i + d]
                vs[i] = jnp.where(keep, a, pltpu.roll(b, d * GROUP_C, 1))
                vs[i + d] = jnp.where(keep, pltpu.roll(a, V7X_LANES - d * GROUP_C, 1), b)
        d *= 2
    return vs


def _s5_in_kernel(x_ref, g_ref, win_ref, u_ref, u2_ref, *, nb, blocks):
    h = _rms(x_ref[...], g_ref[...]).astype(BF16)
    rows2 = blocks * nb
    for cc in range(N_BLK):
        cols = slice(cc * V7X_MXU_DIM, (cc + 1) * V7X_MXU_DIM)
        u = _dot(h, win_ref[:, cols])
        u_ref[:, cols] = u
        u4 = u.reshape(blocks, S5_L, nb, V7X_MXU_DIM)
        for sub in range(V7X_MXU_DIM // V7X_LANES):
            c = cc * (V7X_MXU_DIM // V7X_LANES) + sub
            for half in range(S5_L // GRANULES_PER_VREG):
                vs = []
                for kk in range(GRANULES_PER_VREG):
                    slab = u4[:, half * GRANULES_PER_VREG + kk, :, sub * V7X_LANES:(sub + 1) * V7X_LANES]
                    vs.append(slab.reshape(rows2, V7X_LANES))
                ws = _granule_transpose(vs)
                for gi in range(GRANULES_PER_VREG):
                    lane0 = (c * GRANULES_PER_VREG + gi) * S5_L * GROUP_C + half * V7X_LANES
                    u2_ref[:, lane0:lane0 + V7X_LANES] = ws[gi].astype(BF16)


def _s5_in(x, g, win, nb, blocks):
    rows = blocks * S5_L * nb
    n_blocks = x.shape[0] // (S5_L * nb)
    row_spec = pl.BlockSpec((rows, D_MODEL), lambda i: (i, 0))
    return pl.pallas_call(
        functools.partial(_s5_in_kernel, nb=nb, blocks=blocks),
        grid=(x.shape[0] // rows,),
        in_specs=[row_spec, _const_spec((1, D_MODEL)), _const_spec((D_MODEL, D_MODEL))],
        out_specs=[row_spec, pl.BlockSpec((blocks * nb, S5_L * D_MODEL), lambda i: (i, 0))],
        out_shape=[jax.ShapeDtypeStruct(x.shape, F32),
                   jax.ShapeDtypeStruct((n_blocks * nb, S5_L * D_MODEL), BF16)],
        compiler_params=_params(),
        name="s5_in",
    )(x, g, win)


def _s5_core_kernel(u_ref, wst_ref, kconv_ref, kin_ref, a_ref, s0re_ref, s0im_ref, y_ref, sfre_ref, sfim_ref,
                    sc_s, sin_s, *, nb, n_blocks):
    pw = 2 * S5_L * GROUP_C
    sw = 2 * P_C
    ub = [u_ref[:, q * pw:(q + 1) * pw] for q in range(S5_PAIRS_PER_STEP)]
    for q in range(S5_PAIRS_PER_STEP):
        sc_s[q] = _dot(ub[q], wst_ref[q])
    a_re = [jnp.broadcast_to(a_ref[:, 2 * sw * q:2 * sw * q + sw], (nb, sw)) for q in range(S5_PAIRS_PER_STEP)]
    a_im = [jnp.broadcast_to(a_ref[:, 2 * sw * q + sw:2 * sw * (q + 1)], (nb, sw)) for q in range(S5_PAIRS_PER_STEP)]
    s_re = [s0re_ref[:, sw * q:sw * (q + 1)] for q in range(S5_PAIRS_PER_STEP)]
    s_im = [s0im_ref[:, sw * q:sw * (q + 1)] for q in range(S5_PAIRS_PER_STEP)]
    for blk in range(n_blocks):
        rs = slice(blk * nb, (blk + 1) * nb)
        for q in range(S5_PAIRS_PER_STEP):
            sin_s[q, rs, 0:sw] = s_re[q]
            sin_s[q, rs, sw:2 * sw] = s_im[q]
            n_re, n_im = _cmul(a_re[q], a_im[q], s_re[q], s_im[q])
            s_re[q], s_im[q] = n_re + sc_s[q, rs, 0:sw], n_im + sc_s[q, rs, sw:2 * sw]
    for q in range(S5_PAIRS_PER_STEP):
        sfre_ref[:, sw * q:sw * (q + 1)] = s_re[q]
        sfim_ref[:, sw * q:sw * (q + 1)] = s_im[q]
        half = S5_L * GROUP_C
        conv = jnp.concatenate([_dot(ub[q][:, :half], kconv_ref[2 * q]),
                                _dot(ub[q][:, half:], kconv_ref[2 * q + 1])], axis=1)
        y_ref[:, q * pw:(q + 1) * pw] = conv + _dot(sin_s[q].astype(BF16), kin_ref[q])


def _s5_core(u2, wst, kconv, kin, a_pair, s0_re, s0_im, nb):
    rows = u2.shape[0]
    pw = 2 * S5_L * GROUP_C
    sw = 2 * P_C
    cols = S5_PAIRS_PER_STEP * pw
    scols = S5_PAIRS_PER_STEP * 2 * sw
    state_spec = pl.BlockSpec((nb, S5_PAIRS_PER_STEP * sw), lambda i: (0, i))
    return pl.pallas_call(
        functools.partial(_s5_core_kernel, nb=nb, n_blocks=rows // nb),
        grid=(u2.shape[1] // cols,),
        in_specs=[pl.BlockSpec((rows, cols), lambda i: (0, i)),
                  pl.BlockSpec((S5_PAIRS_PER_STEP, pw, 2 * sw), lambda i: (i, 0, 0)),
                  pl.BlockSpec((2 * S5_PAIRS_PER_STEP, pw // 2, pw // 2), lambda i: (i, 0, 0)),
                  pl.BlockSpec((S5_PAIRS_PER_STEP, 2 * sw, pw), lambda i: (i, 0, 0)),
                  pl.BlockSpec((1, scols), lambda i: (0, i)), state_spec, state_spec],
        out_specs=[pl.BlockSpec((rows, cols), lambda i: (0, i)), state_spec, state_spec],
        out_shape=[jax.ShapeDtypeStruct(u2.shape, F32), jax.ShapeDtypeStruct((nb, D_STATE), F32),
                   jax.ShapeDtypeStruct((nb, D_STATE), F32)],
        scratch_shapes=[pltpu.VMEM((S5_PAIRS_PER_STEP, rows, 2 * sw), F32),
                        pltpu.VMEM((S5_PAIRS_PER_STEP, rows, 2 * sw), F32)],
        compiler_params=_params(),
        name="s5_core",
    )(u2, wst, kconv, kin, a_pair, s0_re, s0_im)


def _s5_out_kernel(x_ref, u_ref, y2_ref, dskip_ref, wglu_ref, o_ref, y_s, gate_s, *, nb, blocks):
    rows = blocks * S5_L * nb
    for cc in range(N_BLK):
        cols = slice(cc * V7X_MXU_DIM, (cc + 1) * V7X_MXU_DIM)
        for sub in range(V7X_MXU_DIM // V7X_LANES):
            c = cc * (V7X_MXU_DIM // V7X_LANES) + sub
            for half in range(S5_L // GRANULES_PER_VREG):
                lanes = [(c * GRANULES_PER_VREG + gi) * S5_L * GROUP_C + half * V7X_LANES
                         for gi in range(GRANULES_PER_VREG)]
                ws = _granule_transpose([y2_ref[:, l0:l0 + V7X_LANES] for l0 in lanes])
                for kk in range(GRANULES_PER_VREG):
                    y_s[cc, :, half * GRANULES_PER_VREG + kk, :, sub * V7X_LANES:(sub + 1) * V7X_LANES] = (
                        ws[kk].reshape(blocks, nb, V7X_LANES))
        y = y_s[cc].reshape(rows, V7X_MXU_DIM) + dskip_ref[:, cols] * u_ref[:, cols]
        gy = _gelu(y).astype(BF16)
        for acc, wcols in ((o_ref, slice(0, D_MODEL)), (gate_s, slice(D_MODEL, 2 * D_MODEL))):
            part = _dot(gy, wglu_ref[cols, wcols])
            if cc == 0:
                acc[...] = part
            else:
                acc[...] += part
    o_ref[...] = x_ref[...] + o_ref[...] * jax.nn.sigmoid(gate_s[...])


def _s5_out(x, u, y2, dskip, wglu, nb, blocks):
    rows = blocks * S5_L * nb
    row_spec = pl.BlockSpec((rows, D_MODEL), lambda i: (i, 0))
    return pl.pallas_call(
        functools.partial(_s5_out_kernel, nb=nb, blocks=blocks),
        grid=(x.shape[0] // rows,),
        in_specs=[row_spec, row_spec, pl.BlockSpec((blocks * nb, S5_L * D_MODEL), lambda i: (i, 0)),
                  _const_spec((1, D_MODEL)), _const_spec((D_MODEL, 2 * D_MODEL))],
        out_specs=row_spec,
        out_shape=jax.ShapeDtypeStruct(x.shape, F32),
        scratch_shapes=[pltpu.VMEM((N_BLK, blocks, S5_L, nb, V7X_MXU_DIM), F32), pltpu.VMEM((rows, D_MODEL), F32)],
        compiler_params=_params(),
        name="s5_out",
    )(x, u, y2, dskip, wglu)


def _pair_state(re, im):
    r = re.reshape(G_C // 2, 2 * P_C)
    i = im.reshape(G_C // 2, 2 * P_C)
    return jnp.concatenate([r, i], axis=-1).reshape(1, 2 * D_STATE)


def _s5_short_kernel(x_ref, g_ref, win_ref, kconv_ref, wsta_ref, wstb_ref, kina_ref, kinb_ref, a_ref, dskip_ref,
                     wglu_ref, s0re_ref, s0im_ref, o_ref, sfre_ref, sfim_ref, u8_s, y8_s, y_s, *, nb, steps):
    slots = GRANULES_PER_VREG
    gl = slots * GROUP_C
    sw = 2 * P_C
    x = x_ref[...]
    u = _dot(_rms(x, g_ref[...]).astype(BF16), win_ref[...])
    zero = jnp.zeros((nb, V7X_LANES), F32)
    for c in range(D_MODEL // V7X_LANES):
        cols = slice(c * V7X_LANES, (c + 1) * V7X_LANES)
        ws = _granule_transpose([u[k * nb:(k + 1) * nb, cols] for k in range(steps)] + [zero] * (slots - steps))
        for gi in range(GRANULES_PER_VREG):
            g = c * GRANULES_PER_VREG + gi
            u8_s[:, g * gl:(g + 1) * gl] = ws[gi].astype(BF16)
    for q in range(G_C // 2):
        up = u8_s[:, 2 * q * gl:(2 * q + 2) * gl]
        reps = slots // steps
        sc = _dot(up, jnp.concatenate([wsta_ref[q]] * reps + [wstb_ref[q]] * reps, axis=0))
        s_re, s_im = s0re_ref[:, sw * q:sw * (q + 1)], s0im_ref[:, sw * q:sw * (q + 1)]
        n_re, n_im = _cmul(a_ref[:, 2 * sw * q:2 * sw * q + sw], a_ref[:, 2 * sw * q + sw:2 * sw * (q + 1)], s_re, s_im)
        sfre_ref[:, sw * q:sw * (q + 1)] = n_re + sc[:, :sw]
        sfim_ref[:, sw * q:sw * (q + 1)] = n_im + sc[:, sw:]
        yk = _dot(jnp.concatenate([s_re, s_im], axis=1).astype(BF16),
                  jnp.concatenate([kina_ref[q], kinb_ref[q]], axis=1))
        for gi in range(2):
            y8_s[:, (2 * q + gi) * gl:(2 * q + gi + 1) * gl] = (
                _dot(up[:, gi * gl:(gi + 1) * gl], kconv_ref[2 * q + gi]) + yk[:, gi * gl:(gi + 1) * gl])
    for c in range(D_MODEL // V7X_LANES):
        ws = _granule_transpose([y8_s[:, (c * GRANULES_PER_VREG + gi) * gl:(c * GRANULES_PER_VREG + gi + 1) * gl]
                                 for gi in range(GRANULES_PER_VREG)])
        for k in range(steps):
            y_s[k * nb:(k + 1) * nb, c * V7X_LANES:(c + 1) * V7X_LANES] = ws[k]
    y = y_s[...] + dskip_ref[...] * u
    o = _dot(_gelu(y).astype(BF16), wglu_ref[...])
    o_ref[...] = x + o[:, :D_MODEL] * jax.nn.sigmoid(o[:, D_MODEL:])


def _s5_short(x, g, win, kconv, wst, kin, a_pair, dskip, wglu, s0_re, s0_im, nb, steps):
    assert S5_L % steps == 0 and GRANULES_PER_VREG % steps == 0
    rows = nb * steps
    slots = GRANULES_PER_VREG
    gl = slots * GROUP_C
    n = S5_L * GROUP_C
    srows = steps * GROUP_C
    row_spec = pl.BlockSpec((rows, D_MODEL), lambda i: (0, 0))
    state_spec = pl.BlockSpec((nb, D_STATE), lambda i: (0, 0))
    return pl.pallas_call(
        functools.partial(_s5_short_kernel, nb=nb, steps=steps),
        grid=(1,),
        in_specs=[row_spec, _const_spec((1, D_MODEL)), _const_spec((D_MODEL, D_MODEL)),
                  pl.BlockSpec((G_C, gl, gl), lambda i: (0, 0, 0)),
                  pl.BlockSpec((G_C // 2, srows, 4 * P_C), lambda i: (0, n // srows - 1, 0)),
                  pl.BlockSpec((G_C // 2, srows, 4 * P_C), lambda i: (0, 2 * n // srows - 1, 0)),
                  pl.BlockSpec((G_C // 2, 4 * P_C, gl), lambda i: (0, 0, 0)),
                  pl.BlockSpec((G_C // 2, 4 * P_C, gl), lambda i: (0, 0, n // gl)),
                  _const_spec((1, 2 * D_STATE)), _const_spec((1, D_MODEL)), _const_spec((D_MODEL, 2 * D_MODEL)),
                  state_spec, state_spec],
        out_specs=[row_spec, state_spec, state_spec],
        out_shape=[jax.ShapeDtypeStruct(x.shape, F32), jax.ShapeDtypeStruct((nb, D_STATE), F32),
                   jax.ShapeDtypeStruct((nb, D_STATE), F32)],
        scratch_shapes=[pltpu.VMEM((nb, G_C * gl), BF16), pltpu.VMEM((nb, G_C * gl), F32),
                        pltpu.VMEM((rows, D_MODEL), F32)],
        compiler_params=_params(),
        name="s5_short",
    )(x, g, win, kconv, wst, wst, kin, kin, a_pair, dskip, wglu, s0_re, s0_im)


def _block_diag(w, n_per_blk):
    n, k_in, k_out = w.shape
    wb = w.reshape(n // n_per_blk, n_per_blk, k_in, k_out)
    eye = jnp.eye(n_per_blk, dtype=w.dtype)
    out = jnp.einsum("jgio,gk->jgiko", wb, eye)
    return out.reshape(n // n_per_blk, n_per_blk * k_in, n_per_blk * k_out)


def _to_rows(a):
    return jnp.swapaxes(a, 0, 1).reshape(a.shape[0] * a.shape[1], a.shape[2])


def _from_rows(a, nb):
    return jnp.swapaxes(a.reshape(a.shape[0] // nb, nb, a.shape[1]), 0, 1)


def kernel(x_prompt, x_sample, state_rglru_conv, state_rglru_h, state_s5_re, state_s5_im, norm_mix, norm_ffn, norm_f, w_ff1, w_ff2, w_in_a, sgu_g, w_s, b_s, w_out_a, w_in_b, conv_w, conv_b, w_a, b_a, w_x, b_x, lam, w_out_b, w_in_c, lam_re, lam_im, log_dt, b_re, b_im, c_re, c_im, d_skip, w_glu):
    bp, tp, _ = x_prompt.shape
    bs, ts, _ = x_sample.shape
    rows_p, rows_s = bp * tp, bs * ts
    assert DEPTH % N_MIXERS == 1 and DEPTH > 1, "first and last layers must be SGU layers"
    assert tp % CHUNK == 0 and ts < CHUNK
    assert rows_s % FFN_SAMPLE_ROWS == 0 and rows_p % FFN_ROWS == 0

    row = lambda v: v.reshape(1, -1)
    norm_ffn3 = norm_ffn.reshape(DEPTH, 1, D_MODEL)
    w1, w2, gf = w_ff1, w_ff2, row(norm_f)
    w_in_a_bf, w_out_a_bf = w_in_a.astype(BF16), w_out_a.astype(BF16)

    x_p, x_s = x_prompt, _to_rows(x_sample)
    outs_v, conv_p, h_p, conv_s, h_s, sre_p, sim_p, sre_s, sim_s = [], [], [], [], [], [], [], [], []
    for layer in range(DEPTH):
        j, kind = layer // N_MIXERS, layer % N_MIXERS
        first, last = layer == 0, layer == DEPTH - 1
        g = row(norm_mix[layer])
        if kind == 0:
            p = dict(g=g, j=j, win=w_in_a_bf, wout=w_out_a_bf, sg=row(sgu_g[j]),
                     w_tril=jnp.where(jnp.tril(jnp.ones((CHUNK, CHUNK), dtype=bool)), w_s[j], 0.0).astype(BF16),
                     bias=jnp.repeat(b_s[j].T, HD_A, axis=1),
                     wl=jnp.repeat(w_s[j][:, :ts, :ts].reshape(HEADS_A, ts * ts).T, HD_A, axis=1),
                     bl=jnp.repeat(b_s[j][:, :ts].T, HD_A, axis=1))
            if first:
                x_p = _sgu_prompt(x_p, p, bp, tp, False, True)
            elif last:
                x_p = _sgu_prompt(x_p, p, bp, tp, True, False).reshape(rows_p, D_MODEL)
            else:
                raise NotImplementedError("interior SGU layers")
            x_s, v = _sgu_sample(x_s, p, bs, ts)
            outs_v.append(_from_rows(v, bs))
        elif kind == 1:
            p = dict(g=g, win=w_in_b[j].astype(BF16), cw=conv_w[j], cb=row(conv_b[j]),
                     wa=_block_diag(w_a[j], HEADS_PER_BLK).astype(BF16), ba=row(b_a[j]),
                     wx=_block_diag(w_x[j], HEADS_PER_BLK).astype(BF16), bx=row(b_x[j]),
                     lam=row(lam[j]), wout=w_out_b[j].astype(BF16))
            dt_s = state_rglru_h.dtype
            x_p, cp, hp = _rglru(x_p, p, jnp.zeros(((CONV_W - 1) * bp, D_MODEL), dt_s),
                                 jnp.zeros((bp, D_MODEL), dt_s), bp, RGLRU_PROMPT_STEPS)
            x_s, cs, hs = _rglru(x_s, p, _to_rows(state_rglru_conv[j]), state_rglru_h[j], bs, ts)
            conv_p.append(_from_rows(cp, bp)); h_p.append(hp)
            conv_s.append(_from_rows(cs, bs)); h_s.append(hs)
        else:
            bbre, bbim, pw = _s5_disc(lam_re[j], lam_im[j], log_dt[j],
                                      jnp.swapaxes(b_re[j], 1, 2), jnp.swapaxes(b_im[j], 1, 2))
            p = dict(g=g, win=w_in_c[j].astype(BF16), dskip=row(d_skip[j]), wglu=w_glu[j].astype(BF16))
            pwg = _pair_lanes(jnp.transpose(pw, (2, 0, 1, 3)))
            kconv, wst, kin = _s5_taps(pwg[:, :, :S5_L], pwg[:, :, 1:], pwg[:, :, S5_L - 1::-1],
                                       _pair_lanes(jnp.stack([c_re[j], c_im[j]], axis=1)),
                                       _pair_lanes(jnp.stack([bbre, bbim], axis=1)))
            zs = jnp.zeros((bp, D_STATE), state_s5_re.dtype)
            u, u2 = _s5_in(x_p, g, p["win"], bp, S5_TILE_BLOCKS)
            y2, rp, ip = _s5_core(u2, wst, kconv, kin, _pair_state(pw[0, S5_L], pw[1, S5_L]), zs, zs, bp)
            x_p = _s5_out(x_p, u, y2, p["dskip"], p["wglu"], bp, S5_TILE_BLOCKS)
            x_s, rs, is_ = _s5_short(x_s, g, p["win"], kconv, wst, kin, _pair_state(pw[0, ts], pw[1, ts]),
                                     p["dskip"], p["wglu"], state_s5_re[j].reshape(bs, D_STATE),
                                     state_s5_im[j].reshape(bs, D_STATE), bs, ts)
            sre_p.append(rp.reshape(bp, G_C, P_C)); sim_p.append(ip.reshape(bp, G_C, P_C))
            sre_s.append(rs.reshape(bs, G_C, P_C)); sim_s.append(is_.reshape(bs, G_C, P_C))
        x_p, x_s = _ffn(x_p, x_s, norm_ffn3, w1, w2, gf, layer, last)

    y_prompt = x_p.reshape(bp, tp, D_MODEL)
    y_sample = _from_rows(x_s, bs)
    return (y_prompt, y_sample, jnp.stack(outs_v), jnp.stack(conv_p), jnp.stack(h_p), jnp.stack(conv_s),
            jnp.stack(h_s), jnp.stack(sre_p), jnp.stack(sim_p), jnp.stack(sre_s), jnp.stack(sim_s))
```

```python
import functools

import jax
import jax.numpy as jnp
from jax import lax
from jax.experimental import pallas as pl
from jax.experimental.pallas import tpu as pltpu

F32 = jnp.float32
BF16 = jnp.bfloat16

D_MODEL = 1024
DEPTH = 4
N_MIXERS = 3
EPS = 1e-6
CHUNK = 128
HEADS_A = 8
HD_A = D_MODEL // HEADS_A
HEADS_B = 16
HD_B = D_MODEL // HEADS_B
CONV_W = 4
LRU_C = 8.0
GROUP_C = 16
G_C = D_MODEL // GROUP_C
P_C = 64
D_STATE = G_C * P_C
D_FF = 4 * D_MODEL

V7X_LANES = 128
V7X_MXU_DIM = 256
V7X_VMEM_BYTES = 64 * 1024 * 1024
VMEM_LIMIT = V7X_VMEM_BYTES - 8 * 1024 * 1024

N_BLK = D_MODEL // V7X_MXU_DIM
HEADS_PER_BLK = V7X_MXU_DIM // HD_B

FFN_ROWS = 1024
FFN_SAMPLE_ROWS = 512
FFN_COL_CHUNK = 1024
FFN_STAGE_BYTES = 1024 * 1024
S5_L = 16
S5_TILE_BLOCKS = 8
S5_PAIRS_PER_STEP = 4
GRANULES_PER_VREG = V7X_LANES // GROUP_C
RGLRU_PROMPT_STEPS = 128


def _rms(x, g):
    return (x * lax.rsqrt(jnp.mean(x * x, axis=-1, keepdims=True) + EPS)) * g


GELU_C0 = 0.7978845608028654
GELU_C1 = GELU_C0 * 0.044715


def _gelu(x):
    return x * (0.5 + 0.5 * jnp.tanh(x * (GELU_C0 + GELU_C1 * (x * x))))


def _dot(a, b):
    return jnp.dot(a, b, preferred_element_type=F32)


def _const_spec(shape):
    zeros = (0,) * len(shape)
    return pl.BlockSpec(shape, lambda i: zeros, pipeline_mode=pl.Buffered(1))


def _layer_spec(shape, layer):
    idx = (layer,) + (0,) * len(shape)
    return pl.BlockSpec((None,) + tuple(shape), lambda i: idx, pipeline_mode=pl.Buffered(1))


def _params():
    return pltpu.CompilerParams(dimension_semantics=("arbitrary",), vmem_limit_bytes=VMEM_LIMIT)


def _ffn_tile(x_ref, g_ref, w1_ref, w2_ref, gf_ref, o_ref, final_norm):
    h = _rms(x_ref[...], g_ref[...]).astype(BF16)
    for j in range(D_FF // FFN_COL_CHUNK):
        cols = slice(j * FFN_COL_CHUNK, (j + 1) * FFN_COL_CHUNK)
        a = jnp.square(jnp.maximum(_dot(h, w1_ref[:, cols]), 0.0)).astype(BF16)
        part = _dot(a, w2_ref[cols, :])
        if j == 0:
            o_ref[...] = part
        else:
            o_ref[...] += part
    y = x_ref[...] + o_ref[...]
    if final_norm:
        y = _rms(y, gf_ref[...])
    o_ref[...] = y


def _load_as_bf16(jobs):
    def copies_of(src, stage, sem):
        rows = stage.shape[1]
        return [pltpu.make_async_copy(src.at[pl.ds(k * rows, rows)], stage.at[k % 2], sem.at[k % 2])
                for k in range(src.shape[0] // rows)]

    plans = [copies_of(src, stage, sem) for src, _, stage, sem in jobs]
    n = len(plans[0])
    assert all(len(p) == n for p in plans)
    for p in plans:
        p[0].start()
    for k in range(n):
        if k + 1 < n:
            for p in plans:
                p[k + 1].start()
        for p, (_, dst, stage, _) in zip(plans, jobs):
            p[k].wait()
            rows = stage.shape[1]
            dst[k * rows:(k + 1) * rows, :] = stage[k % 2].astype(BF16)


def _ffn_kernel(xp_ref, xs_ref, g_ref, w1_hbm, w2_hbm, gf_ref, op_ref, os_ref, w1_s, w2_s, stage1, stage2,
                sem1, sem2, *, final_norm, n_p, layer):
    i = pl.program_id(0)

    @pl.when(i == 0)
    def _():
        _load_as_bf16([(w1_hbm.at[layer], w1_s, stage1, sem1), (w2_hbm.at[layer], w2_s, stage2, sem2)])

    @pl.when(i < n_p)
    def _():
        _ffn_tile(xp_ref, g_ref, w1_s, w2_s, gf_ref, op_ref, final_norm)

    @pl.when(i >= n_p)
    def _():
        _ffn_tile(xs_ref, g_ref, w1_s, w2_s, gf_ref, os_ref, final_norm)


def _ffn(x_p, x_s, g, w1, w2, gf, layer, final_norm):
    n_p, n_s = x_p.shape[0] // FFN_ROWS, x_s.shape[0] // FFN_SAMPLE_ROWS
    p_spec = pl.BlockSpec((FFN_ROWS, D_MODEL), lambda i: (jnp.minimum(i, n_p - 1), 0))
    s_spec = pl.BlockSpec((FFN_SAMPLE_ROWS, D_MODEL), lambda i: (jnp.maximum(i - n_p, 0), 0))
    hbm = pl.BlockSpec(memory_space=pl.ANY)
    return pl.pallas_call(
        functools.partial(_ffn_kernel, final_norm=final_norm, n_p=n_p, layer=layer),
        grid=(n_p + n_s,),
        in_specs=[p_spec, s_spec, _layer_spec((1, D_MODEL), layer), hbm, hbm, _const_spec((1, D_MODEL))],
        out_specs=[p_spec, s_spec],
        out_shape=[jax.ShapeDtypeStruct(x_p.shape, F32), jax.ShapeDtypeStruct(x_s.shape, F32)],
        scratch_shapes=[pltpu.VMEM((D_MODEL, D_FF), BF16), pltpu.VMEM((D_FF, D_MODEL), BF16),
                        pltpu.VMEM((2, FFN_STAGE_BYTES // (4 * D_FF), D_FF), F32),
                        pltpu.VMEM((2, FFN_STAGE_BYTES // (4 * D_MODEL), D_MODEL), F32),
                        pltpu.SemaphoreType.DMA((2,)), pltpu.SemaphoreType.DMA((2,))],
        compiler_params=_params(),
        name="ffn",
    )(x_p, x_s, g, w1, w2, gf)


def _sgu_front(x, g_ref, win_ref, sg_ref):
    h = _rms(x, g_ref[...]).astype(BF16)
    v = _rms(_gelu(_dot(h, win_ref[:, D_MODEL:])), sg_ref[...])
    return _gelu(_dot(h, win_ref[:, :D_MODEL])), v


def _rows_to_batch_major(x, nb):
    steps = x.shape[0] // nb
    return jnp.swapaxes(x.reshape(steps, nb, x.shape[1]), 0, 1).reshape(x.shape)


def _rows_to_time_major(x, nb):
    steps = x.shape[0] // nb
    return jnp.swapaxes(x.reshape(nb, steps, x.shape[1]), 0, 1).reshape(x.shape)


def _sgu_prompt_kernel(x_ref, g_ref, win_ref, sg_ref, w_ref, bias_ref, wout_ref, o_ref, y_s, *, nb, in_tm, out_tm):
    rows = nb * CHUNK
    if in_tm:
        x = _rows_to_batch_major(x_ref[...], nb)
    else:
        x = x_ref[...].reshape(rows, D_MODEL)
    u, v = _sgu_front(x, g_ref, win_ref, sg_ref)
    vb = v.astype(BF16)
    for b in range(nb):
        rs = slice(b * CHUNK, (b + 1) * CHUNK)
        for g in range(HEADS_A):
            cs = slice(g * HD_A, (g + 1) * HD_A)
            mixed = _dot(w_ref[g], vb[rs, cs]) + bias_ref[:, cs]
            y_s[rs, cs] = (u[rs, cs] * mixed).astype(BF16)
    o = x + _dot(y_s[...], wout_ref[...])
    if out_tm:
        o_ref[...] = _rows_to_time_major(o, nb)
    else:
        o_ref[...] = o.reshape(nb, CHUNK, D_MODEL)


def _sgu_sample_kernel(x_ref, g_ref, win_ref, sg_ref, wl_ref, bl_ref, wout_ref, o_ref, v_ref, *, nb, steps):
    x = x_ref[...]
    u, v = _sgu_front(x, g_ref, win_ref, sg_ref)
    v_ref[...] = v
    mixed = []
    for t in range(steps):
        m = None
        for s in range(t + 1):
            term = wl_ref[t * steps + s:t * steps + s + 1, :] * v[s * nb:(s + 1) * nb, :]
            m = term if m is None else m + term
        mixed.append(m + bl_ref[t:t + 1, :])
    y = (u * jnp.concatenate(mixed, axis=0)).astype(BF16)
    o_ref[...] = x + _dot(y, wout_ref[...])


def _sgu_prompt(x, p, nb, n_steps, in_tm, out_tm):
    rows = nb * CHUNK
    tm_spec = pl.BlockSpec((rows, D_MODEL), lambda i: (i, 0))
    bm_spec = pl.BlockSpec((nb, CHUNK, D_MODEL), lambda i: (0, i, 0))
    out_shape = (jax.ShapeDtypeStruct((nb * n_steps, D_MODEL), F32) if out_tm
                 else jax.ShapeDtypeStruct((nb, n_steps, D_MODEL), F32))
    return pl.pallas_call(
        functools.partial(_sgu_prompt_kernel, nb=nb, in_tm=in_tm, out_tm=out_tm),
        grid=(n_steps // CHUNK,),
        in_specs=[tm_spec if in_tm else bm_spec, _const_spec((1, D_MODEL)),
                  _layer_spec((D_MODEL, 2 * D_MODEL), p["j"]),
                  _const_spec((1, D_MODEL)), _const_spec((HEADS_A, CHUNK, CHUNK)),
                  _const_spec((CHUNK, D_MODEL)), _layer_spec((D_MODEL, D_MODEL), p["j"])],
        out_specs=tm_spec if out_tm else bm_spec,
        out_shape=out_shape,
        scratch_shapes=[pltpu.VMEM((rows, D_MODEL), BF16)],
        compiler_params=_params(),
        name="sgu_prompt",
    )(x, p["g"], p["win"], p["sg"], p["w_tril"], p["bias"], p["wout"])


def _sgu_sample(x, p, nb, steps):
    tile = steps * nb
    tile_spec = pl.BlockSpec((tile, D_MODEL), lambda i: (0, 0))
    return pl.pallas_call(
        functools.partial(_sgu_sample_kernel, nb=nb, steps=steps),
        grid=(1,),
        in_specs=[tile_spec, _const_spec((1, D_MODEL)), _layer_spec((D_MODEL, 2 * D_MODEL), p["j"]),
                  _const_spec((1, D_MODEL)), _const_spec((steps * steps, D_MODEL)),
                  _const_spec((steps, D_MODEL)), _layer_spec((D_MODEL, D_MODEL), p["j"])],
        out_specs=[tile_spec, tile_spec],
        out_shape=[jax.ShapeDtypeStruct((tile, D_MODEL), F32), jax.ShapeDtypeStruct((tile, D_MODEL), F32)],
        compiler_params=_params(),
        name="sgu_sample",
    )(x, p["g"], p["win"], p["sg"], p["wl"], p["bl"], p["wout"])


def _rglru_kernel(x_ref, g_ref, win_ref, cw_ref, cb_ref, wa_ref, ba_ref, wx_ref, bx_ref, lam_ref, wout_ref,
                  conv0_ref, h0_ref, o_ref, conv_ref, h_ref, xext_s, a_s, b_s, y_s, *, nb, steps):
    rows = nb * steps
    halo = (CONV_W - 1) * nb

    @pl.when(pl.program_id(0) == 0)
    def _():
        conv_ref[...] = conv0_ref[...]
        h_ref[...] = h0_ref[...]

    def blk_cols(j):
        return slice(j * V7X_MXU_DIM, (j + 1) * V7X_MXU_DIM)

    x = x_ref[...]
    h = _rms(x, g_ref[...]).astype(BF16)
    for j in range(N_BLK):
        cols = blk_cols(j)
        xext, a_j, b_j = xext_s.at[j], a_s.at[j], b_s.at[j]
        xext[0:halo, :] = conv_ref[:, cols]
        xext[halo:halo + rows, :] = _dot(h, win_ref[:, blk_cols(N_BLK + j)])
        gate = _gelu(_dot(h, win_ref[:, cols]))
        conv = None
        for w in range(CONV_W):
            term = xext[w * nb:w * nb + rows, :] * cw_ref[w:w + 1, cols]
            conv = term if conv is None else conv + term
        conv_ref[:, cols] = xext[rows:rows + halo, :]
        xc = cb_ref[:, cols] + conv
        xcb = xc.astype(BF16)
        tr = jnp.tanh(_dot(xcb, wa_ref[j]) + ba_ref[:, cols])
        ig = jax.nn.sigmoid(_dot(xcb, wx_ref[j]) + bx_ref[:, cols])
        neg_log_a = (tr + 1.0) * ((0.5 * LRU_C) * jax.nn.softplus(-lam_ref[:, cols]))
        a = jnp.exp(-neg_log_a)
        a_j[...] = a
        w = jnp.tanh(neg_log_a) * (a * a + 1.0)
        mult = jnp.where(w > 0.0, w * lax.rsqrt(w), 0.0)
        b_j[...] = mult * (ig * xc)
        hcur = h_ref[:, cols]
        for t in range(steps):
            sl = slice(t * nb, (t + 1) * nb)
            hcur = a_j[sl, :] * hcur + b_j[sl, :]
            b_j[sl, :] = hcur
        h_ref[:, cols] = hcur
        y_s[:, cols] = (b_j[...] * gate).astype(BF16)
    o_ref[...] = x + _dot(y_s[...], wout_ref[...])


def _rglru(x, p, conv0, h0, nb, steps):
    rows = nb * steps
    halo = (CONV_W - 1) * nb
    row_spec = pl.BlockSpec((rows, D_MODEL), lambda i: (i, 0))
    vec = _const_spec((1, D_MODEL))
    blk = _const_spec((N_BLK, V7X_MXU_DIM, V7X_MXU_DIM))
    return pl.pallas_call(
        functools.partial(_rglru_kernel, nb=nb, steps=steps),
        grid=(x.shape[0] // rows,),
        in_specs=[row_spec, vec, _const_spec((D_MODEL, 2 * D_MODEL)), _const_spec((CONV_W, D_MODEL)), vec,
                  blk, vec, blk, vec, vec, _const_spec((D_MODEL, D_MODEL)),
                  _const_spec((halo, D_MODEL)), _const_spec((nb, D_MODEL))],
        out_specs=[row_spec, pl.BlockSpec((halo, D_MODEL), lambda i: (0, 0)),
                   pl.BlockSpec((nb, D_MODEL), lambda i: (0, 0))],
        out_shape=[jax.ShapeDtypeStruct(x.shape, F32), jax.ShapeDtypeStruct((halo, D_MODEL), F32),
                   jax.ShapeDtypeStruct((nb, D_MODEL), F32)],
        scratch_shapes=[pltpu.VMEM((N_BLK, rows + halo, V7X_MXU_DIM), F32), pltpu.VMEM((N_BLK, rows, V7X_MXU_DIM), F32),
                        pltpu.VMEM((N_BLK, rows, V7X_MXU_DIM), F32), pltpu.VMEM((rows, D_MODEL), BF16)],
        compiler_params=_params(),
        name="rglru",
    )(x, p["g"], p["win"], p["cw"], p["cb"], p["wa"], p["ba"], p["wx"], p["bx"], p["lam"], p["wout"], conv0, h0)


def _cmul(a_re, a_im, b_re, b_im):
    return a_re * b_re - a_im * b_im, a_re * b_im + a_im * b_re


def _s5_disc_kernel(lre_ref, lim_ref, ldt_ref, bre_ref, bim_ref, bbre_ref, bbim_ref, pw_ref):
    lr, li = lre_ref[...], lim_ref[...]
    dt = jnp.exp(ldt_ref[...])
    mag = jnp.exp(lr * dt)
    ab_re, ab_im = mag * jnp.cos(li * dt), mag * jnp.sin(li * dt)
    zr, zi = ab_re - 1.0, ab_im
    den = lr * lr + li * li
    q_re = (zr * lr + zi * li) / den
    q_im = (zi * lr - zr * li) / den
    br, bi = bre_ref[...], bim_ref[...]
    bbre_ref[...] = q_re[:, None, :] * br - q_im[:, None, :] * bi
    bbim_ref[...] = q_re[:, None, :] * bi + q_im[:, None, :] * br
    p_re, p_im = jnp.ones_like(ab_re), jnp.zeros_like(ab_re)
    for k in range(S5_L + 1):
        pw_ref[0, k] = p_re
        pw_ref[1, k] = p_im
        p_re, p_im = _cmul(p_re, p_im, ab_re, ab_im)


def _s5_disc(lam_re, lam_im, log_dt, b_re_t, b_im_t):
    ghp = jax.ShapeDtypeStruct((G_C, GROUP_C, P_C), F32)
    pw = jax.ShapeDtypeStruct((2, S5_L + 1, G_C, P_C), F32)
    return pl.pallas_call(_s5_disc_kernel, out_shape=[ghp, ghp, pw], name="s5_disc")(
        lam_re, lam_im, log_dt.reshape(G_C, 1), b_re_t, b_im_t)


def _s5_taps_kernel(pw0_ref, pw1_ref, pwf_ref, c_ref, b_ref, kconv_ref, wst_ref, kin_ref, *, pairs):
    n = S5_L * GROUP_C
    pair_lanes = 2 * P_C

    def rep(t):
        return jnp.broadcast_to(t[:, None, :], (S5_L, GROUP_C, pair_lanes)).reshape(n, pair_lanes)

    def til(c):
        return jnp.broadcast_to(c[None, :, :], (S5_L, GROUP_C, pair_lanes)).reshape(n, pair_lanes)

    def dot_t(a, b):
        return lax.dot_general(a, b, (((1,), (1,)), ((), ())), precision=lax.Precision.HIGHEST,
                               preferred_element_type=F32)

    def first(shape, axis):
        return lax.broadcasted_iota(jnp.int32, shape, axis) < P_C

    lane = lax.broadcasted_iota(jnp.int32, (GROUP_C, n), 1)
    for q in range(pairs):
        cr, ci = til(c_ref[q, 0]), til(c_ref[q, 1])
        b_re, b_im = b_ref[q, 0], b_ref[q, 1]
        x_re, x_im = _cmul(rep(pw0_ref[q, 0]), rep(pw0_ref[q, 1]), cr, ci)
        g0 = first(b_re.shape, 1)
        b2 = jnp.concatenate(
            [jnp.concatenate([jnp.where(g0, b_re, 0.0), jnp.where(g0, -b_im, 0.0)], axis=1),
             jnp.concatenate([jnp.where(g0, 0.0, b_re), jnp.where(g0, 0.0, -b_im)], axis=1)], axis=0)
        mrows = dot_t(b2, jnp.concatenate([x_re, x_im], axis=1))
        for gi in range(2):
            mrow = mrows[gi * GROUP_C:(gi + 1) * GROUP_C]
            for s in range(S5_L):
                blk = mrow if s == 0 else jnp.where(lane >= s * GROUP_C, pltpu.roll(mrow, s * GROUP_C, 1), 0.0)
                kconv_ref[2 * q + gi, s * GROUP_C:(s + 1) * GROUP_C, :] = blk.astype(BF16)
        e_re, e_im = _cmul(rep(pw1_ref[q, 0]), rep(pw1_ref[q, 1]), cr, ci)
        et_re, et_im = e_re.T, (-e_im).T
        top = first(et_re.shape, 0)
        kin_ref[q] = jnp.concatenate(
            [jnp.concatenate([jnp.where(top, et_re, 0.0), jnp.where(top, 0.0, et_re)], axis=1),
             jnp.concatenate([jnp.where(top, et_im, 0.0), jnp.where(top, 0.0, et_im)], axis=1)], axis=0).astype(BF16)
        f_re, f_im = _cmul(rep(pwf_ref[q, 0]), rep(pwf_ref[q, 1]), til(b_re), til(b_im))
        left = first(f_re.shape, 1)
        wst_ref[q] = jnp.concatenate(
            [jnp.concatenate([jnp.where(left, f_re, 0.0), jnp.where(left, f_im, 0.0)], axis=1),
             jnp.concatenate([jnp.where(left, 0.0, f_re), jnp.where(left, 0.0, f_im)], axis=1)], axis=0).astype(BF16)


def _s5_taps(pw0, pw1, pwf, c, bb):
    pairs = 4
    n = S5_L * GROUP_C
    pspec = pl.BlockSpec((pairs, 2, S5_L, 2 * P_C), lambda i: (i, 0, 0, 0))
    hspec = pl.BlockSpec((pairs, 2, GROUP_C, 2 * P_C), lambda i: (i, 0, 0, 0))
    return pl.pallas_call(
        functools.partial(_s5_taps_kernel, pairs=pairs),
        grid=(G_C // (2 * pairs),),
        in_specs=[pspec, pspec, pspec, hspec, hspec],
        out_specs=[pl.BlockSpec((2 * pairs, n, n), lambda i: (i, 0, 0)),
                   pl.BlockSpec((pairs, 2 * n, 4 * P_C), lambda i: (i, 0, 0)),
                   pl.BlockSpec((pairs, 4 * P_C, 2 * n), lambda i: (i, 0, 0))],
        out_shape=[jax.ShapeDtypeStruct((G_C, n, n), BF16), jax.ShapeDtypeStruct((G_C // 2, 2 * n, 4 * P_C), BF16),
                   jax.ShapeDtypeStruct((G_C // 2, 4 * P_C, 2 * n), BF16)],
        compiler_params=_params(),
        name="s5_taps",
    )(pw0, pw1, pwf, c, bb)


def _pair_lanes(a):
    a2 = a.reshape((G_C // 2, 2) + a.shape[1:])
    return jnp.concatenate([a2[:, 0], a2[:, 1]], axis=-1)


def _granule_transpose(vs):
    vs = list(vs)
    granule = lax.broadcasted_iota(jnp.int32, vs[0].shape, 1) // GROUP_C
    d = 1
    while d < len(vs):
        keep = (granule & d) == 0
        for i in range(len(vs)):
            if i & d == 0:
                a, b = vs[i], vs[i + d]
                vs[i] = jnp.where(keep, a, pltpu.roll(b, d * GROUP_C, 1))
                vs[i + d] = jnp.where(keep, pltpu.roll(a, V7X_LANES - d * GROUP_C, 1), b)
        d *= 2
    return vs


def _s5_in_kernel(x_ref, g_ref, win_ref, u_ref, u2_ref, *, nb, blocks):
    h = _rms(x_ref[...], g_ref[...]).astype(BF16)
    rows2 = blocks * nb
    for cc in range(N_BLK):
        cols = slice(cc * V7X_MXU_DIM, (cc + 1) * V7X_MXU_DIM)
        u = _dot(h, win_ref[:, cols])
        u_ref[:, cols] = u
        u4 = u.reshape(blocks, S5_L, nb, V7X_MXU_DIM)
        for sub in range(V7X_MXU_DIM // V7X_LANES):
            c = cc * (V7X_MXU_DIM // V7X_LANES) + sub
            for half in range(S5_L // GRANULES_PER_VREG):
                vs = []
                for kk in range(GRANULES_PER_VREG):
                    slab = u4[:, half * GRANULES_PER_VREG + kk, :, sub * V7X_LANES:(sub + 1) * V7X_LANES]
                    vs.append(slab.reshape(rows2, V7X_LANES))
                ws = _granule_transpose(vs)
                for gi in range(GRANULES_PER_VREG):
                    lane0 = (c * GRANULES_PER_VREG + gi) * S5_L * GROUP_C + half * V7X_LANES
                    u2_ref[:, lane0:lane0 + V7X_LANES] = ws[gi].astype(BF16)


def _s5_in(x, g, win, nb, blocks):
    rows = blocks * S5_L * nb
    n_blocks = x.shape[0] // (S5_L * nb)
    row_spec = pl.BlockSpec((rows, D_MODEL), lambda i: (i, 0))
    return pl.pallas_call(
        functools.partial(_s5_in_kernel, nb=nb, blocks=blocks),
        grid=(x.shape[0] // rows,),
        in_specs=[row_spec, _const_spec((1, D_MODEL)), _const_spec((D_MODEL, D_MODEL))],
        out_specs=[row_spec, pl.BlockSpec((blocks * nb, S5_L * D_MODEL), lambda i: (i, 0))],
        out_shape=[jax.ShapeDtypeStruct(x.shape, F32),
                   jax.ShapeDtypeStruct((n_blocks * nb, S5_L * D_MODEL), BF16)],
        compiler_params=_params(),
        name="s5_in",
    )(x, g, win)


def _s5_core_kernel(u_ref, wst_ref, kconv_ref, kin_ref, a_ref, s0re_ref, s0im_ref, y_ref, sfre_ref, sfim_ref,
                    sc_s, sin_s, *, nb, n_blocks):
    pw = 2 * S5_L * GROUP_C
    sw = 2 * P_C
    ub = [u_ref[:, q * pw:(q + 1) * pw] for q in range(S5_PAIRS_PER_STEP)]
    for q in range(S5_PAIRS_PER_STEP):
        sc_s[q] = _dot(ub[q], wst_ref[q])
    a_re = [jnp.broadcast_to(a_ref[:, 2 * sw * q:2 * sw * q + sw], (nb, sw)) for q in range(S5_PAIRS_PER_STEP)]
    a_im = [jnp.broadcast_to(a_ref[:, 2 * sw * q + sw:2 * sw * (q + 1)], (nb, sw)) for q in range(S5_PAIRS_PER_STEP)]
    s_re = [s0re_ref[:, sw * q:sw * (q + 1)] for q in range(S5_PAIRS_PER_STEP)]
    s_im = [s0im_ref[:, sw * q:sw * (q + 1)] for q in range(S5_PAIRS_PER_STEP)]
    for blk in range(n_blocks):
        rs = slice(blk * nb, (blk + 1) * nb)
        for q in range(S5_PAIRS_PER_STEP):
            sin_s[q, rs, 0:sw] = s_re[q]
            sin_s[q, rs, sw:2 * sw] = s_im[q]
            n_re, n_im = _cmul(a_re[q], a_im[q], s_re[q], s_im[q])
            s_re[q], s_im[q] = n_re + sc_s[q, rs, 0:sw], n_im + sc_s[q, rs, sw:2 * sw]
    for q in range(S5_PAIRS_PER_STEP):
        sfre_ref[:, sw * q:sw * (q + 1)] = s_re[q]
        sfim_ref[:, sw * q:sw * (q + 1)] = s_im[q]
        half = S5_L * GROUP_C
        conv = jnp.concatenate([_dot(ub[q][:, :half], kconv_ref[2 * q]),
                                _dot(ub[q][:, half:], kconv_ref[2 * q + 1])], axis=1)
        y_ref[:, q * pw:(q + 1) * pw] = conv + _dot(sin_s[q].astype(BF16), kin_ref[q])


def _s5_core(u2, wst, kconv, kin, a_pair, s0_re, s0_im, nb):
    rows = u2.shape[0]
    pw = 2 * S5_L * GROUP_C
    sw = 2 * P_C
    cols = S5_PAIRS_PER_STEP * pw
    scols = S5_PAIRS_PER_STEP * 2 * sw
    state_spec = pl.BlockSpec((nb, S5_PAIRS_PER_STEP * sw), lambda i: (0, i))
    return pl.pallas_call(
        functools.partial(_s5_core_kernel, nb=nb, n_blocks=rows // nb),
        grid=(u2.shape[1] // cols,),
        in_specs=[pl.BlockSpec((rows, cols), lambda i: (0, i)),
                  pl.BlockSpec((S5_PAIRS_PER_STEP, pw, 2 * sw), lambda i: (i, 0, 0)),
                  pl.BlockSpec((2 * S5_PAIRS_PER_STEP, pw // 2, pw // 2), lambda i: (i, 0, 0)),
                  pl.BlockSpec((S5_PAIRS_PER_STEP, 2 * sw, pw), lambda i: (i, 0, 0)),
                  pl.BlockSpec((1, scols), lambda i: (0, i)), state_spec, state_spec],
        out_specs=[pl.BlockSpec((rows, cols), lambda i: (0, i)), state_spec, state_spec],
        out_shape=[jax.ShapeDtypeStruct(u2.shape, F32), jax.ShapeDtypeStruct((nb, D_STATE), F32),
                   jax.ShapeDtypeStruct((nb, D_STATE), F32)],
        scratch_shapes=[pltpu.VMEM((S5_PAIRS_PER_STEP, rows, 2 * sw), F32),
                        pltpu.VMEM((S5_PAIRS_PER_STEP, rows, 2 * sw), F32)],
        compiler_params=_params(),
        name="s5_core",
    )(u2, wst, kconv, kin, a_pair, s0_re, s0_im)


def _s5_out_kernel(x_ref, u_ref, y2_ref, dskip_ref, wglu_ref, o_ref, y_s, gate_s, *, nb, blocks):
    rows = blocks * S5_L * nb
    for cc in range(N_BLK):
        cols = slice(cc * V7X_MXU_DIM, (cc + 1) * V7X_MXU_DIM)
        for sub in range(V7X_MXU_DIM // V7X_LANES):
            c = cc * (V7X_MXU_DIM // V7X_LANES) + sub
            for half in range(S5_L // GRANULES_PER_VREG):
                lanes = [(c * GRANULES_PER_VREG + gi) * S5_L * GROUP_C + half * V7X_LANES
                         for gi in range(GRANULES_PER_VREG)]
                ws = _granule_transpose([y2_ref[:, l0:l0 + V7X_LANES] for l0 in lanes])
                for kk in range(GRANULES_PER_VREG):
                    y_s[cc, :, half * GRANULES_PER_VREG + kk, :, sub * V7X_LANES:(sub + 1) * V7X_LANES] = (
                        ws[kk].reshape(blocks, nb, V7X_LANES))
        y = y_s[cc].reshape(rows, V7X_MXU_DIM) + dskip_ref[:, cols] * u_ref[:, cols]
        gy = _gelu(y).astype(BF16)
        for acc, wcols in ((o_ref, slice(0, D_MODEL)), (gate_s, slice(D_MODEL, 2 * D_MODEL))):
            part = _dot(gy, wglu_ref[cols, wcols])
            if cc == 0:
                acc[...] = part
            else:
                acc[...] += part
    o_ref[...] = x_ref[...] + o_ref[...] * jax.nn.sigmoid(gate_s[...])


def _s5_out(x, u, y2, dskip, wglu, nb, blocks):
    rows = blocks * S5_L * nb
    row_spec = pl.BlockSpec((rows, D_MODEL), lambda i: (i, 0))
    return pl.pallas_call(
        functools.partial(_s5_out_kernel, nb=nb, blocks=blocks),
        grid=(x.shape[0] // rows,),
        in_specs=[row_spec, row_spec, pl.BlockSpec((blocks * nb, S5_L * D_MODEL), lambda i: (i, 0)),
                  _const_spec((1, D_MODEL)), _const_spec((D_MODEL, 2 * D_MODEL))],
        out_specs=row_spec,
        out_shape=jax.ShapeDtypeStruct(x.shape, F32),
        scratch_shapes=[pltpu.VMEM((N_BLK, blocks, S5_L, nb, V7X_MXU_DIM), F32), pltpu.VMEM((rows, D_MODEL), F32)],
        compiler_params=_params(),
        name="s5_out",
    )(x, u, y2, dskip, wglu)


def _pair_state(re, im):
    r = re.reshape(G_C // 2, 2 * P_C)
    i = im.reshape(G_C // 2, 2 * P_C)
    return jnp.concatenate([r, i], axis=-1).reshape(1, 2 * D_STATE)


def _s5_short_kernel(x_ref, g_ref, win_ref, kconv_ref, wsta_ref, wstb_ref, kina_ref, kinb_ref, a_ref, dskip_ref,
                     wglu_ref, s0re_ref, s0im_ref, o_ref, sfre_ref, sfim_ref, u8_s, y8_s, y_s, *, nb, steps):
    slots = GRANULES_PER_VREG
    gl = slots * GROUP_C
    sw = 2 * P_C
    x = x_ref[...]
    u = _dot(_rms(x, g_ref[...]).astype(BF16), win_ref[...])
    zero = jnp.zeros((nb, V7X_LANES), F32)
    for c in range(D_MODEL // V7X_LANES):
        cols = slice(c * V7X_LANES, (c + 1) * V7X_LANES)
        ws = _granule_transpose([u[k * nb:(k + 1) * nb, cols] for k in range(steps)] + [zero] * (slots - steps))
        for gi in range(GRANULES_PER_VREG):
            g = c * GRANULES_PER_VREG + gi
            u8_s[:, g * gl:(g + 1) * gl] = ws[gi].astype(BF16)
    for q in range(G_C // 2):
        up = u8_s[:, 2 * q * gl:(2 * q + 2) * gl]
        reps = slots // steps
        sc = _dot(up, jnp.concatenate([wsta_ref[q]] * reps + [wstb_ref[q]] * reps, axis=0))
        s_re, s_im = s0re_ref[:, sw * q:sw * (q + 1)], s0im_ref[:, sw * q:sw * (q + 1)]
        n_re, n_im = _cmul(a_ref[:, 2 * sw * q:2 * sw * q + sw], a_ref[:, 2 * sw * q + sw:2 * sw * (q + 1)], s_re, s_im)
        sfre_ref[:, sw * q:sw * (q + 1)] = n_re + sc[:, :sw]
        sfim_ref[:, sw * q:sw * (q + 1)] = n_im + sc[:, sw:]
        yk = _dot(jnp.concatenate([s_re, s_im], axis=1).astype(BF16),
                  jnp.concatenate([kina_ref[q], kinb_ref[q]], axis=1))
        for gi in range(2):
            y8_s[:, (2 * q + gi) * gl:(2 * q + gi + 1) * gl] = (
                _dot(up[:, gi * gl:(gi + 1) * gl], kconv_ref[2 * q + gi]) + yk[:, gi * gl:(gi + 1) * gl])
    for c in range(D_MODEL // V7X_LANES):
        ws = _granule_transpose([y8_s[:, (c * GRANULES_PER_VREG + gi) * gl:(c * GRANULES_PER_VREG + gi + 1) * gl]
                                 for gi in range(GRANULES_PER_VREG)])
        for k in range(steps):
            y_s[k * nb:(k + 1) * nb, c * V7X_LANES:(c + 1) * V7X_LANES] = ws[k]
    y = y_s[...] + dskip_ref[...] * u
    o = _dot(_gelu(y).astype(BF16), wglu_ref[...])
    o_ref[...] = x + o[:, :D_MODEL] * jax.nn.sigmoid(o[:, D_MODEL:])


def _s5_short(x, g, win, kconv, wst, kin, a_pair, dskip, wglu, s0_re, s0_im, nb, steps):
    assert S5_L % steps == 0 and GRANULES_PER_VREG % steps == 0
    rows = nb * steps
    slots = GRANULES_PER_VREG
    gl = slots * GROUP_C
    n = S5_L * GROUP_C
    srows = steps * GROUP_C
    row_spec = pl.BlockSpec((rows, D_MODEL), lambda i: (0, 0))
    state_spec = pl.BlockSpec((nb, D_STATE), lambda i: (0, 0))
    return pl.pallas_call(
        functools.partial(_s5_short_kernel, nb=nb, steps=steps),
        grid=(1,),
        in_specs=[row_spec, _const_spec((1, D_MODEL)), _const_spec((D_MODEL, D_MODEL)),
                  pl.BlockSpec((G_C, gl, gl), lambda i: (0, 0, 0)),
                  pl.BlockSpec((G_C // 2, srows, 4 * P_C), lambda i: (0, n // srows - 1, 0)),
                  pl.BlockSpec((G_C // 2, srows, 4 * P_C), lambda i: (0, 2 * n // srows - 1, 0)),
                  pl.BlockSpec((G_C // 2, 4 * P_C, gl), lambda i: (0, 0, 0)),
                  pl.BlockSpec((G_C // 2, 4 * P_C, gl), lambda i: (0, 0, n // gl)),
                  _const_spec((1, 2 * D_STATE)), _const_spec((1, D_MODEL)), _const_spec((D_MODEL, 2 * D_MODEL)),
                  state_spec, state_spec],
        out_specs=[row_spec, state_spec, state_spec],
        out_shape=[jax.ShapeDtypeStruct(x.shape, F32), jax.ShapeDtypeStruct((nb, D_STATE), F32),
                   jax.ShapeDtypeStruct((nb, D_STATE), F32)],
        scratch_shapes=[pltpu.VMEM((nb, G_C * gl), BF16), pltpu.VMEM((nb, G_C * gl), F32),
                        pltpu.VMEM((rows, D_MODEL), F32)],
        compiler_params=_params(),
        name="s5_short",
    )(x, g, win, kconv, wst, wst, kin, kin, a_pair, dskip, wglu, s0_re, s0_im)


def _block_diag(w, n_per_blk):
    n, k_in, k_out = w.shape
    wb = w.reshape(n // n_per_blk, n_per_blk, k_in, k_out)
    eye = jnp.eye(n_per_blk, dtype=w.dtype)
    out = jnp.einsum("jgio,gk->jgiko", wb, eye)
    return out.reshape(n // n_per_blk, n_per_blk * k_in, n_per_blk * k_out)


def _to_rows(a):
    return jnp.swapaxes(a, 0, 1).reshape(a.shape[0] * a.shape[1], a.shape[2])


def _from_rows(a, nb):
    return jnp.swapaxes(a.reshape(a.shape[0] // nb, nb, a.shape[1]), 0, 1)


def kernel(x_prompt, x_sample, state_rglru_conv, state_rglru_h, state_s5_re, state_s5_im, norm_mix, norm_ffn, norm_f, w_ff1, w_ff2, w_in_a, sgu_g, w_s, b_s, w_out_a, w_in_b, conv_w, conv_b, w_a, b_a, w_x, b_x, lam, w_out_b, w_in_c, lam_re, lam_im, log_dt, b_re, b_im, c_re, c_im, d_skip, w_glu):
    bp, tp, _ = x_prompt.shape
    bs, ts, _ = x_sample.shape
    rows_p, rows_s = bp * tp, bs * ts
    assert DEPTH % N_MIXERS == 1 and DEPTH > 1, "first and last layers must be SGU layers"
    assert tp % CHUNK == 0 and ts < CHUNK
    assert rows_s % FFN_SAMPLE_ROWS == 0 and rows_p % FFN_ROWS == 0

    row = lambda v: v.reshape(1, -1)
    norm_ffn3 = norm_ffn.reshape(DEPTH, 1, D_MODEL)
    w1, w2, gf = w_ff1, w_ff2, row(norm_f)
    w_in_a_bf, w_out_a_bf = w_in_a.astype(BF16), w_out_a.astype(BF16)

    x_p, x_s = x_prompt, _to_rows(x_sample)
    outs_v, conv_p, h_p, conv_s, h_s, sre_p, sim_p, sre_s, sim_s = [], [], [], [], [], [], [], [], []
    for layer in range(DEPTH):
        j, kind = layer // N_MIXERS, layer % N_MIXERS
        first, last = layer == 0, layer == DEPTH - 1
        g = row(norm_mix[layer])
        if kind == 0:
            p = dict(g=g, j=j, win=w_in_a_bf, wout=w_out_a_bf, sg=row(sgu_g[j]),
                     w_tril=jnp.where(jnp.tril(jnp.ones((CHUNK, CHUNK), dtype=bool)), w_s[j], 0.0).astype(BF16),
                     bias=jnp.repeat(b_s[j].T, HD_A, axis=1),
                     wl=jnp.repeat(w_s[j][:, :ts, :ts].reshape(HEADS_A, ts * ts).T, HD_A, axis=1),
                     bl=jnp.repeat(b_s[j][:, :ts].T, HD_A, axis=1))
            if first:
                x_p = _sgu_prompt(x_p, p, bp, tp, False, True)
            elif last:
                x_p = _sgu_prompt(x_p, p, bp, tp, True, False).reshape(rows_p, D_MODEL)
            else:
                raise NotImplementedError("interior SGU layers")
            x_s, v = _sgu_sample(x_s, p, bs, ts)
            outs_v.append(_from_rows(v, bs))
        elif kind == 1:
            p = dict(g=g, win=w_in_b[j].astype(BF16), cw=conv_w[j], cb=row(conv_b[j]),
                     wa=_block_diag(0.5 * w_a[j], HEADS_PER_BLK).astype(BF16), ba=row(0.5 * b_a[j]),
                     wx=_block_diag(w_x[j], HEADS_PER_BLK).astype(BF16), bx=row(b_x[j]),
                     lam=row(lam[j]), wout=w_out_b[j].astype(BF16))
            dt_s = state_rglru_h.dtype
            x_p, cp, hp = _rglru(x_p, p, jnp.zeros(((CONV_W - 1) * bp, D_MODEL), dt_s),
                                 jnp.zeros((bp, D_MODEL), dt_s), bp, RGLRU_PROMPT_STEPS)
            x_s, cs, hs = _rglru(x_s, p, _to_rows(state_rglru_conv[j]), state_rglru_h[j], bs, ts)
            conv_p.append(_from_rows(cp, bp)); h_p.append(hp)
            conv_s.append(_from_rows(cs, bs)); h_s.append(hs)
        else:
            bbre, bbim, pw = _s5_disc(lam_re[j], lam_im[j], log_dt[j],
                                      jnp.swapaxes(b_re[j], 1, 2), jnp.swapaxes(b_im[j], 1, 2))
            p = dict(g=g, win=w_in_c[j].astype(BF16), dskip=row(d_skip[j]), wglu=w_glu[j].astype(BF16))
            pwg = _pair_lanes(jnp.transpose(pw, (2, 0, 1, 3)))
            kconv, wst, kin = _s5_taps(pwg[:, :, :S5_L], pwg[:, :, 1:], pwg[:, :, S5_L - 1::-1],
                                       _pair_lanes(jnp.stack([c_re[j], c_im[j]], axis=1)),
                                       _pair_lanes(jnp.stack([bbre, bbim], axis=1)))
            zs = jnp.zeros((bp, D_STATE), state_s5_re.dtype)
            u, u2 = _s5_in(x_p, g, p["win"], bp, S5_TILE_BLOCKS)
            y2, rp, ip = _s5_core(u2, wst, kconv, kin, _pair_state(pw[0, S5_L], pw[1, S5_L]), zs, zs, bp)
            x_p = _s5_out(x_p, u, y2, p["dskip"], p["wglu"], bp, S5_TILE_BLOCKS)
            x_s, rs, is_ = _s5_short(x_s, g, p["win"], kconv, wst, kin, _pair_state(pw[0, ts], pw[1, ts]),
                                     p["dskip"], p["wglu"], state_s5_re[j].reshape(bs, D_STATE),
                                     state_s5_im[j].reshape(bs, D_STATE), bs, ts)
            sre_p.append(rp.reshape(bp, G_C, P_C)); sim_p.append(ip.reshape(bp, G_C, P_C))
            sre_s.append(rs.reshape(bs, G_C, P_C)); sim_s.append(is_.reshape(bs, G_C, P_C))
        x_p, x_s = _ffn(x_p, x_s, norm_ffn3, w1, w2, gf, layer, last)

    y_prompt = x_p.reshape(bp, tp, D_MODEL)
    y_sample = _from_rows(x_s, bs)
    return (y_prompt, y_sample, jnp.stack(outs_v), jnp.stack(conv_p), jnp.stack(h_p), jnp.stack(conv_s),
            jnp.stack(h_s), jnp.stack(sre_p), jnp.stack(sim_p), jnp.stack(sre_s), jnp.stack(sim_s))
```

```python
import functools

import jax
import jax.numpy as jnp
from jax import lax
from jax.experimental import pallas as pl
from jax.experimental.pallas import tpu as pltpu

F32 = jnp.float32
BF16 = jnp.bfloat16

D_MODEL = 1024
DEPTH = 4
N_MIXERS = 3
EPS = 1e-6
CHUNK = 128
HEADS_A = 8
HD_A = D_MODEL // HEADS_A
HEADS_B = 16
HD_B = D_MODEL // HEADS_B
CONV_W = 4
LRU_C = 8.0
GROUP_C = 16
G_C = D_MODEL // GROUP_C
P_C = 64
D_STATE = G_C * P_C
D_FF = 4 * D_MODEL

V7X_LANES = 128
V7X_MXU_DIM = 256
V7X_VMEM_BYTES = 64 * 1024 * 1024
VMEM_LIMIT = V7X_VMEM_BYTES - 8 * 1024 * 1024

N_BLK = D_MODEL // V7X_MXU_DIM
HEADS_PER_BLK = V7X_MXU_DIM // HD_B

FFN_ROWS = 1024
FFN_SAMPLE_ROWS = 512
FFN_COL_CHUNK = 1024
FFN_STAGE_BYTES = 1024 * 1024
S5_L = 16
S5_TILE_BLOCKS = 8
S5_PAIRS_PER_STEP = 4
S5_U_SLOTS = 3
GRANULES_PER_VREG = V7X_LANES // GROUP_C
RGLRU_PROMPT_STEPS = 128


def _rms(x, g):
    return (x * lax.rsqrt(jnp.mean(x * x, axis=-1, keepdims=True) + EPS)) * g


GELU_C0 = 0.7978845608028654
GELU_C1 = GELU_C0 * 0.044715


def _gelu(x):
    return x * (0.5 + 0.5 * jnp.tanh(x * (GELU_C0 + GELU_C1 * (x * x))))


def _dot(a, b):
    return jnp.dot(a, b, preferred_element_type=F32)


def _const_spec(shape):
    zeros = (0,) * len(shape)
    return pl.BlockSpec(shape, lambda i: zeros, pipeline_mode=pl.Buffered(1))


def _layer_spec(shape, layer):
    idx = (layer,) + (0,) * len(shape)
    return pl.BlockSpec((None,) + tuple(shape), lambda i: idx, pipeline_mode=pl.Buffered(1))


def _params():
    return pltpu.CompilerParams(dimension_semantics=("arbitrary",), vmem_limit_bytes=VMEM_LIMIT)


def _ffn_tile(x_ref, g_ref, w1_ref, w2_ref, gf_ref, o_ref, final_norm):
    h = _rms(x_ref[...], g_ref[...]).astype(BF16)
    for j in range(D_FF // FFN_COL_CHUNK):
        cols = slice(j * FFN_COL_CHUNK, (j + 1) * FFN_COL_CHUNK)
        a = jnp.square(jnp.maximum(_dot(h, w1_ref[:, cols]), 0.0)).astype(BF16)
        part = _dot(a, w2_ref[cols, :])
        if j == 0:
            o_ref[...] = part
        else:
            o_ref[...] += part
    y = x_ref[...] + o_ref[...]
    if final_norm:
        y = _rms(y, gf_ref[...])
    o_ref[...] = y


def _load_as_bf16(jobs):
    def copies_of(src, stage, sem):
        rows = stage.shape[1]
        return [pltpu.make_async_copy(src.at[pl.ds(k * rows, rows)], stage.at[k % 2], sem.at[k % 2])
                for k in range(src.shape[0] // rows)]

    plans = [copies_of(src, stage, sem) for src, _, stage, sem in jobs]
    n = len(plans[0])
    assert all(len(p) == n for p in plans)
    for p in plans:
        p[0].start()
    for k in range(n):
        if k + 1 < n:
            for p in plans:
                p[k + 1].start()
        for p, (_, dst, stage, _) in zip(plans, jobs):
            p[k].wait()
            rows = stage.shape[1]
            dst[k * rows:(k + 1) * rows, :] = stage[k % 2].astype(BF16)


def _ffn_kernel(xp_ref, xs_ref, g_ref, w1_hbm, w2_hbm, gf_ref, op_ref, os_ref, w1_s, w2_s, stage1, stage2,
                sem1, sem2, *, final_norm, n_p, layer):
    i = pl.program_id(0)

    @pl.when(i == 0)
    def _():
        _load_as_bf16([(w1_hbm.at[layer], w1_s, stage1, sem1), (w2_hbm.at[layer], w2_s, stage2, sem2)])

    @pl.when(i < n_p)
    def _():
        _ffn_tile(xp_ref, g_ref, w1_s, w2_s, gf_ref, op_ref, final_norm)

    @pl.when(i >= n_p)
    def _():
        _ffn_tile(xs_ref, g_ref, w1_s, w2_s, gf_ref, os_ref, final_norm)


def _ffn(x_p, x_s, g, w1, w2, gf, layer, final_norm):
    n_p, n_s = x_p.shape[0] // FFN_ROWS, x_s.shape[0] // FFN_SAMPLE_ROWS
    p_spec = pl.BlockSpec((FFN_ROWS, D_MODEL), lambda i: (jnp.minimum(i, n_p - 1), 0))
    s_spec = pl.BlockSpec((FFN_SAMPLE_ROWS, D_MODEL), lambda i: (jnp.maximum(i - n_p, 0), 0))
    hbm = pl.BlockSpec(memory_space=pl.ANY)
    return pl.pallas_call(
        functools.partial(_ffn_kernel, final_norm=final_norm, n_p=n_p, layer=layer),
        grid=(n_p + n_s,),
        in_specs=[p_spec, s_spec, _layer_spec((1, D_MODEL), layer), hbm, hbm, _const_spec((1, D_MODEL))],
        out_specs=[p_spec, s_spec],
        out_shape=[jax.ShapeDtypeStruct(x_p.shape, F32), jax.ShapeDtypeStruct(x_s.shape, F32)],
        scratch_shapes=[pltpu.VMEM((D_MODEL, D_FF), BF16), pltpu.VMEM((D_FF, D_MODEL), BF16),
                        pltpu.VMEM((2, FFN_STAGE_BYTES // (4 * D_FF), D_FF), F32),
                        pltpu.VMEM((2, FFN_STAGE_BYTES // (4 * D_MODEL), D_MODEL), F32),
                        pltpu.SemaphoreType.DMA((2,)), pltpu.SemaphoreType.DMA((2,))],
        compiler_params=_params(),
        name="ffn",
    )(x_p, x_s, g, w1, w2, gf)


def _sgu_front(x, g_ref, win_ref, sg_ref):
    h = _rms(x, g_ref[...]).astype(BF16)
    v = _rms(_gelu(_dot(h, win_ref[:, D_MODEL:])), sg_ref[...])
    return _gelu(_dot(h, win_ref[:, :D_MODEL])), v


def _rows_to_batch_major(x, nb):
    steps = x.shape[0] // nb
    return jnp.swapaxes(x.reshape(steps, nb, x.shape[1]), 0, 1).reshape(x.shape)


def _rows_to_time_major(x, nb):
    steps = x.shape[0] // nb
    return jnp.swapaxes(x.reshape(nb, steps, x.shape[1]), 0, 1).reshape(x.shape)


def _sgu_prompt_kernel(x_ref, g_ref, win_ref, sg_ref, w_ref, bias_ref, wout_ref, o_ref, y_s, *, nb, in_tm, out_tm):
    rows = nb * CHUNK
    if in_tm:
        x = _rows_to_batch_major(x_ref[...], nb)
    else:
        x = x_ref[...].reshape(rows, D_MODEL)
    u, v = _sgu_front(x, g_ref, win_ref, sg_ref)
    vb = v.astype(BF16)
    for b in range(nb):
        rs = slice(b * CHUNK, (b + 1) * CHUNK)
        for g in range(HEADS_A):
            cs = slice(g * HD_A, (g + 1) * HD_A)
            mixed = _dot(w_ref[g], vb[rs, cs]) + bias_ref[:, cs]
            y_s[rs, cs] = (u[rs, cs] * mixed).astype(BF16)
    o = x + _dot(y_s[...], wout_ref[...])
    if out_tm:
        o_ref[...] = _rows_to_time_major(o, nb)
    else:
        o_ref[...] = o.reshape(nb, CHUNK, D_MODEL)


def _sgu_sample_kernel(x_ref, g_ref, win_ref, sg_ref, wl_ref, bl_ref, wout_ref, o_ref, v_ref, *, nb, steps):
    x = x_ref[...]
    u, v = _sgu_front(x, g_ref, win_ref, sg_ref)
    v_ref[...] = v
    mixed = []
    for t in range(steps):
        m = None
        for s in range(t + 1):
            term = wl_ref[t * steps + s:t * steps + s + 1, :] * v[s * nb:(s + 1) * nb, :]
            m = term if m is None else m + term
        mixed.append(m + bl_ref[t:t + 1, :])
    y = (u * jnp.concatenate(mixed, axis=0)).astype(BF16)
    o_ref[...] = x + _dot(y, wout_ref[...])


def _sgu_prompt(x, p, nb, n_steps, in_tm, out_tm):
    rows = nb * CHUNK
    tm_spec = pl.BlockSpec((rows, D_MODEL), lambda i: (i, 0))
    bm_spec = pl.BlockSpec((nb, CHUNK, D_MODEL), lambda i: (0, i, 0))
    out_shape = (jax.ShapeDtypeStruct((nb * n_steps, D_MODEL), F32) if out_tm
                 else jax.ShapeDtypeStruct((nb, n_steps, D_MODEL), F32))
    return pl.pallas_call(
        functools.partial(_sgu_prompt_kernel, nb=nb, in_tm=in_tm, out_tm=out_tm),
        grid=(n_steps // CHUNK,),
        in_specs=[tm_spec if in_tm else bm_spec, _const_spec((1, D_MODEL)),
                  _layer_spec((D_MODEL, 2 * D_MODEL), p["j"]),
                  _const_spec((1, D_MODEL)), _const_spec((HEADS_A, CHUNK, CHUNK)),
                  _const_spec((CHUNK, D_MODEL)), _layer_spec((D_MODEL, D_MODEL), p["j"])],
        out_specs=tm_spec if out_tm else bm_spec,
        out_shape=out_shape,
        scratch_shapes=[pltpu.VMEM((rows, D_MODEL), BF16)],
        compiler_params=_params(),
        name="sgu_prompt",
    )(x, p["g"], p["win"], p["sg"], p["w_tril"], p["bias"], p["wout"])


def _sgu_sample(x, p, nb, steps):
    tile = steps * nb
    tile_spec = pl.BlockSpec((tile, D_MODEL), lambda i: (0, 0))
    return pl.pallas_call(
        functools.partial(_sgu_sample_kernel, nb=nb, steps=steps),
        grid=(1,),
        in_specs=[tile_spec, _const_spec((1, D_MODEL)), _layer_spec((D_MODEL, 2 * D_MODEL), p["j"]),
                  _const_spec((1, D_MODEL)), _const_spec((steps * steps, D_MODEL)),
                  _const_spec((steps, D_MODEL)), _layer_spec((D_MODEL, D_MODEL), p["j"])],
        out_specs=[tile_spec, tile_spec],
        out_shape=[jax.ShapeDtypeStruct((tile, D_MODEL), F32), jax.ShapeDtypeStruct((tile, D_MODEL), F32)],
        compiler_params=_params(),
        name="sgu_sample",
    )(x, p["g"], p["win"], p["sg"], p["wl"], p["bl"], p["wout"])


def _rglru_kernel(x_ref, g_ref, win_ref, cw_ref, cb_ref, wa_ref, ba_ref, wx_ref, bx_ref, lam_ref, wout_ref,
                  conv0_ref, h0_ref, o_ref, conv_ref, h_ref, xext_s, a_s, b_s, y_s, *, nb, steps):
    rows = nb * steps
    halo = (CONV_W - 1) * nb

    @pl.when(pl.program_id(0) == 0)
    def _():
        conv_ref[...] = conv0_ref[...]
        h_ref[...] = h0_ref[...]

    def blk_cols(j):
        return slice(j * V7X_MXU_DIM, (j + 1) * V7X_MXU_DIM)

    x = x_ref[...]
    h = _rms(x, g_ref[...]).astype(BF16)
    for j in range(N_BLK):
        cols = blk_cols(j)
        xext, a_j, b_j = xext_s.at[j], a_s.at[j], b_s.at[j]
        xext[0:halo, :] = conv_ref[:, cols]
        xext[halo:halo + rows, :] = _dot(h, win_ref[:, blk_cols(N_BLK + j)])
        gate = _gelu(_dot(h, win_ref[:, cols]))
        conv = None
        for w in range(CONV_W):
            term = xext[w * nb:w * nb + rows, :] * cw_ref[w:w + 1, cols]
            conv = term if conv is None else conv + term
        conv_ref[:, cols] = xext[rows:rows + halo, :]
        xc = cb_ref[:, cols] + conv
        xcb = xc.astype(BF16)
        tr = jnp.tanh(_dot(xcb, wa_ref[j]) + ba_ref[:, cols])
        ig = jax.nn.sigmoid(_dot(xcb, wx_ref[j]) + bx_ref[:, cols])
        neg_log_a = (tr + 1.0) * ((0.5 * LRU_C) * jax.nn.softplus(-lam_ref[:, cols]))
        a = jnp.exp(-neg_log_a)
        a_j[...] = a
        w = jnp.tanh(neg_log_a) * (a * a + 1.0)
        mult = jnp.where(w > 0.0, w * lax.rsqrt(w), 0.0)
        b_j[...] = mult * (ig * xc)
        hcur = h_ref[:, cols]
        for t in range(steps):
            sl = slice(t * nb, (t + 1) * nb)
            hcur = a_j[sl, :] * hcur + b_j[sl, :]
            b_j[sl, :] = hcur
        h_ref[:, cols] = hcur
        y_s[:, cols] = (b_j[...] * gate).astype(BF16)
    o_ref[...] = x + _dot(y_s[...], wout_ref[...])


def _rglru(x, p, conv0, h0, nb, steps):
    rows = nb * steps
    halo = (CONV_W - 1) * nb
    row_spec = pl.BlockSpec((rows, D_MODEL), lambda i: (i, 0))
    vec = _const_spec((1, D_MODEL))
    blk = _const_spec((N_BLK, V7X_MXU_DIM, V7X_MXU_DIM))
    return pl.pallas_call(
        functools.partial(_rglru_kernel, nb=nb, steps=steps),
        grid=(x.shape[0] // rows,),
        in_specs=[row_spec, vec, _const_spec((D_MODEL, 2 * D_MODEL)), _const_spec((CONV_W, D_MODEL)), vec,
                  blk, vec, blk, vec, vec, _const_spec((D_MODEL, D_MODEL)),
                  _const_spec((halo, D_MODEL)), _const_spec((nb, D_MODEL))],
        out_specs=[row_spec, pl.BlockSpec((halo, D_MODEL), lambda i: (0, 0)),
                   pl.BlockSpec((nb, D_MODEL), lambda i: (0, 0))],
        out_shape=[jax.ShapeDtypeStruct(x.shape, F32), jax.ShapeDtypeStruct((halo, D_MODEL), F32),
                   jax.ShapeDtypeStruct((nb, D_MODEL), F32)],
        scratch_shapes=[pltpu.VMEM((N_BLK, rows + halo, V7X_MXU_DIM), F32), pltpu.VMEM((N_BLK, rows, V7X_MXU_DIM), F32),
                        pltpu.VMEM((N_BLK, rows, V7X_MXU_DIM), F32), pltpu.VMEM((rows, D_MODEL), BF16)],
        compiler_params=_params(),
        name="rglru",
    )(x, p["g"], p["win"], p["cw"], p["cb"], p["wa"], p["ba"], p["wx"], p["bx"], p["lam"], p["wout"], conv0, h0)


def _cmul(a_re, a_im, b_re, b_im):
    return a_re * b_re - a_im * b_im, a_re * b_im + a_im * b_re


def _s5_disc_kernel(lre_ref, lim_ref, ldt_ref, bre_ref, bim_ref, bbre_ref, bbim_ref, pw_ref):
    lr, li = lre_ref[...], lim_ref[...]
    dt = jnp.exp(ldt_ref[...])
    mag = jnp.exp(lr * dt)
    ab_re, ab_im = mag * jnp.cos(li * dt), mag * jnp.sin(li * dt)
    zr, zi = ab_re - 1.0, ab_im
    den = lr * lr + li * li
    q_re = (zr * lr + zi * li) / den
    q_im = (zi * lr - zr * li) / den
    br, bi = bre_ref[...], bim_ref[...]
    bbre_ref[...] = q_re[:, None, :] * br - q_im[:, None, :] * bi
    bbim_ref[...] = q_re[:, None, :] * bi + q_im[:, None, :] * br
    p_re, p_im = jnp.ones_like(ab_re), jnp.zeros_like(ab_re)
    for k in range(S5_L + 1):
        pw_ref[0, k] = p_re
        pw_ref[1, k] = p_im
        p_re, p_im = _cmul(p_re, p_im, ab_re, ab_im)


def _s5_disc(lam_re, lam_im, log_dt, b_re_t, b_im_t):
    ghp = jax.ShapeDtypeStruct((G_C, GROUP_C, P_C), F32)
    pw = jax.ShapeDtypeStruct((2, S5_L + 1, G_C, P_C), F32)
    return pl.pallas_call(_s5_disc_kernel, out_shape=[ghp, ghp, pw], name="s5_disc")(
        lam_re, lam_im, log_dt.reshape(G_C, 1), b_re_t, b_im_t)


def _s5_taps_kernel(pw0_ref, pw1_ref, pwf_ref, c_ref, b_ref, kconv_ref, wst_ref, kin_ref, *, pairs):
    n = S5_L * GROUP_C
    pair_lanes = 2 * P_C

    def rep(t):
        return jnp.broadcast_to(t[:, None, :], (S5_L, GROUP_C, pair_lanes)).reshape(n, pair_lanes)

    def til(c):
        return jnp.broadcast_to(c[None, :, :], (S5_L, GROUP_C, pair_lanes)).reshape(n, pair_lanes)

    def dot_t(a, b):
        return lax.dot_general(a, b, (((1,), (1,)), ((), ())), precision=lax.Precision.HIGHEST,
                               preferred_element_type=F32)

    def first(shape, axis):
        return lax.broadcasted_iota(jnp.int32, shape, axis) < P_C

    lane = lax.broadcasted_iota(jnp.int32, (GROUP_C, n), 1)
    for q in range(pairs):
        cr, ci = til(c_ref[q, 0]), til(c_ref[q, 1])
        b_re, b_im = b_ref[q, 0], b_ref[q, 1]
        x_re, x_im = _cmul(rep(pw0_ref[q, 0]), rep(pw0_ref[q, 1]), cr, ci)
        g0 = first(b_re.shape, 1)
        b2 = jnp.concatenate(
            [jnp.concatenate([jnp.where(g0, b_re, 0.0), jnp.where(g0, -b_im, 0.0)], axis=1),
             jnp.concatenate([jnp.where(g0, 0.0, b_re), jnp.where(g0, 0.0, -b_im)], axis=1)], axis=0)
        mrows = dot_t(b2, jnp.concatenate([x_re, x_im], axis=1))
        for gi in range(2):
            mrow = mrows[gi * GROUP_C:(gi + 1) * GROUP_C]
            for s in range(S5_L):
                blk = mrow if s == 0 else jnp.where(lane >= s * GROUP_C, pltpu.roll(mrow, s * GROUP_C, 1), 0.0)
                kconv_ref[2 * q + gi, s * GROUP_C:(s + 1) * GROUP_C, :] = blk.astype(BF16)
        e_re, e_im = _cmul(rep(pw1_ref[q, 0]), rep(pw1_ref[q, 1]), cr, ci)
        et_re, et_im = e_re.T, (-e_im).T
        top = first(et_re.shape, 0)
        kin_ref[q] = jnp.concatenate(
            [jnp.concatenate([jnp.where(top, et_re, 0.0), jnp.where(top, 0.0, et_re)], axis=1),
             jnp.concatenate([jnp.where(top, et_im, 0.0), jnp.where(top, 0.0, et_im)], axis=1)], axis=0).astype(BF16)
        f_re, f_im = _cmul(rep(pwf_ref[q, 0]), rep(pwf_ref[q, 1]), til(b_re), til(b_im))
        left = first(f_re.shape, 1)
        wst_ref[q] = jnp.concatenate(
            [jnp.concatenate([jnp.where(left, f_re, 0.0), jnp.where(left, f_im, 0.0)], axis=1),
             jnp.concatenate([jnp.where(left, 0.0, f_re), jnp.where(left, 0.0, f_im)], axis=1)], axis=0).astype(BF16)


def _s5_taps(pw0, pw1, pwf, c, bb):
    pairs = 4
    n = S5_L * GROUP_C
    pspec = pl.BlockSpec((pairs, 2, S5_L, 2 * P_C), lambda i: (i, 0, 0, 0))
    hspec = pl.BlockSpec((pairs, 2, GROUP_C, 2 * P_C), lambda i: (i, 0, 0, 0))
    return pl.pallas_call(
        functools.partial(_s5_taps_kernel, pairs=pairs),
        grid=(G_C // (2 * pairs),),
        in_specs=[pspec, pspec, pspec, hspec, hspec],
        out_specs=[pl.BlockSpec((2 * pairs, n, n), lambda i: (i, 0, 0)),
                   pl.BlockSpec((pairs, 2 * n, 4 * P_C), lambda i: (i, 0, 0)),
                   pl.BlockSpec((pairs, 4 * P_C, 2 * n), lambda i: (i, 0, 0))],
        out_shape=[jax.ShapeDtypeStruct((G_C, n, n), BF16), jax.ShapeDtypeStruct((G_C // 2, 2 * n, 4 * P_C), BF16),
                   jax.ShapeDtypeStruct((G_C // 2, 4 * P_C, 2 * n), BF16)],
        compiler_params=_params(),
        name="s5_taps",
    )(pw0, pw1, pwf, c, bb)


def _pair_lanes(a):
    a2 = a.reshape((G_C // 2, 2) + a.shape[1:])
    return jnp.concatenate([a2[:, 0], a2[:, 1]], axis=-1)


def _granule_transpose(vs):
    vs = list(vs)
    granule = lax.broadcasted_iota(jnp.int32, vs[0].shape, 1) // GROUP_C
    d = 1
    while d < len(vs):
        keep = (granule & d) == 0
        for i in range(len(vs)):
            if i & d == 0:
                a, b = vs[i], vs[i + d]
                vs[i] = jnp.where(keep, a, pltpu.roll(b, d * GROUP_C, 1))
                vs[i + d] = jnp.where(keep, pltpu.roll(a, V7X_LANES - d * GROUP_C, 1), b)
        d *= 2
    return vs


def _s5_in_kernel(x_ref, g_ref, win_ref, u_ref, u2_ref, *, nb, blocks):
    h = _rms(x_ref[...], g_ref[...]).astype(BF16)
    rows2 = blocks * nb
    for cc in range(N_BLK):
        cols = slice(cc * V7X_MXU_DIM, (cc + 1) * V7X_MXU_DIM)
        u = _dot(h, win_ref[:, cols])
        u_ref[:, cols] = u
        u4 = u.reshape(blocks, S5_L, nb, V7X_MXU_DIM)
        for sub in range(V7X_MXU_DIM // V7X_LANES):
            c = cc * (V7X_MXU_DIM // V7X_LANES) + sub
            for half in range(S5_L // GRANULES_PER_VREG):
                vs = []
                for kk in range(GRANULES_PER_VREG):
                    slab = u4[:, half * GRANULES_PER_VREG + kk, :, sub * V7X_LANES:(sub + 1) * V7X_LANES]
                    vs.append(slab.reshape(rows2, V7X_LANES))
                ws = _granule_transpose(vs)
                for gi in range(GRANULES_PER_VREG):
                    lane0 = (c * GRANULES_PER_VREG + gi) * S5_L * GROUP_C + half * V7X_LANES
                    u2_ref[:, lane0:lane0 + V7X_LANES] = ws[gi].astype(BF16)


def _s5_in(x, g, win, nb, blocks):
    rows = blocks * S5_L * nb
    n_blocks = x.shape[0] // (S5_L * nb)
    row_spec = pl.BlockSpec((rows, D_MODEL), lambda i: (i, 0))
    return pl.pallas_call(
        functools.partial(_s5_in_kernel, nb=nb, blocks=blocks),
        grid=(x.shape[0] // rows,),
        in_specs=[row_spec, _const_spec((1, D_MODEL)), _const_spec((D_MODEL, D_MODEL))],
        out_specs=[row_spec, pl.BlockSpec((blocks * nb, S5_L * D_MODEL), lambda i: (i, 0))],
        out_shape=[jax.ShapeDtypeStruct(x.shape, F32),
                   jax.ShapeDtypeStruct((n_blocks * nb, S5_L * D_MODEL), BF16)],
        compiler_params=_params(),
        name="s5_in",
    )(x, g, win)


def _s5_core_kernel(u_hbm, wst_ref, kconv_ref, kin_ref, a_ref, s0re_ref, s0im_ref, y_ref, sfre_ref, sfim_ref,
                    sc_s, sin_s, ubuf, usem, *, nb, n_blocks, n_steps):
    pw = 2 * S5_L * GROUP_C
    sw = 2 * P_C
    cols = S5_PAIRS_PER_STEP * pw
    i = pl.program_id(0)
    ahead = S5_U_SLOTS - 1

    def u_copy(step, slot):
        return pltpu.make_async_copy(u_hbm.at[:, pl.ds(pl.multiple_of(step * cols, cols), cols)],
                                     ubuf.at[slot], usem.at[slot])

    @pl.when(i == 0)
    def _():
        for s in range(min(ahead, n_steps)):
            u_copy(s, s).start()

    @pl.when(i + ahead < n_steps)
    def _():
        u_copy(i + ahead, (i + ahead) % S5_U_SLOTS).start()

    u_copy(i, i % S5_U_SLOTS).wait()
    u_ref = ubuf.at[i % S5_U_SLOTS]
    ub = [u_ref[:, q * pw:(q + 1) * pw] for q in range(S5_PAIRS_PER_STEP)]
    for q in range(S5_PAIRS_PER_STEP):
        sc_s[q] = _dot(ub[q], wst_ref[q])
    a_re = [jnp.broadcast_to(a_ref[:, 2 * sw * q:2 * sw * q + sw], (nb, sw)) for q in range(S5_PAIRS_PER_STEP)]
    a_im = [jnp.broadcast_to(a_ref[:, 2 * sw * q + sw:2 * sw * (q + 1)], (nb, sw)) for q in range(S5_PAIRS_PER_STEP)]
    s_re = [s0re_ref[:, sw * q:sw * (q + 1)] for q in range(S5_PAIRS_PER_STEP)]
    s_im = [s0im_ref[:, sw * q:sw * (q + 1)] for q in range(S5_PAIRS_PER_STEP)]
    for blk in range(n_blocks):
        rs = slice(blk * nb, (blk + 1) * nb)
        for q in range(S5_PAIRS_PER_STEP):
            sin_s[q, rs, 0:sw] = s_re[q]
            sin_s[q, rs, sw:2 * sw] = s_im[q]
            n_re, n_im = _cmul(a_re[q], a_im[q], s_re[q], s_im[q])
            s_re[q], s_im[q] = n_re + sc_s[q, rs, 0:sw], n_im + sc_s[q, rs, sw:2 * sw]
    for q in range(S5_PAIRS_PER_STEP):
        sfre_ref[:, sw * q:sw * (q + 1)] = s_re[q]
        sfim_ref[:, sw * q:sw * (q + 1)] = s_im[q]
        half = S5_L * GROUP_C
        conv = jnp.concatenate([_dot(ub[q][:, :half], kconv_ref[2 * q]),
                                _dot(ub[q][:, half:], kconv_ref[2 * q + 1])], axis=1)
        y_ref[:, q * pw:(q + 1) * pw] = conv + _dot(sin_s[q].astype(BF16), kin_ref[q])


def _s5_core(u2, wst, kconv, kin, a_pair, s0_re, s0_im, nb):
    rows = u2.shape[0]
    pw = 2 * S5_L * GROUP_C
    sw = 2 * P_C
    cols = S5_PAIRS_PER_STEP * pw
    scols = S5_PAIRS_PER_STEP * 2 * sw
    state_spec = pl.BlockSpec((nb, S5_PAIRS_PER_STEP * sw), lambda i: (0, i))
    return pl.pallas_call(
        functools.partial(_s5_core_kernel, nb=nb, n_blocks=rows // nb, n_steps=u2.shape[1] // cols),
        grid=(u2.shape[1] // cols,),
        in_specs=[pl.BlockSpec(memory_space=pl.ANY),
                  pl.BlockSpec((S5_PAIRS_PER_STEP, pw, 2 * sw), lambda i: (i, 0, 0)),
                  pl.BlockSpec((2 * S5_PAIRS_PER_STEP, pw // 2, pw // 2), lambda i: (i, 0, 0)),
                  pl.BlockSpec((S5_PAIRS_PER_STEP, 2 * sw, pw), lambda i: (i, 0, 0)),
                  pl.BlockSpec((1, scols), lambda i: (0, i)), state_spec, state_spec],
        out_specs=[pl.BlockSpec((rows, cols), lambda i: (0, i)), state_spec, state_spec],
        out_shape=[jax.ShapeDtypeStruct(u2.shape, F32), jax.ShapeDtypeStruct((nb, D_STATE), F32),
                   jax.ShapeDtypeStruct((nb, D_STATE), F32)],
        scratch_shapes=[pltpu.VMEM((S5_PAIRS_PER_STEP, rows, 2 * sw), F32),
                        pltpu.VMEM((S5_PAIRS_PER_STEP, rows, 2 * sw), F32),
                        pltpu.VMEM((S5_U_SLOTS, rows, cols), BF16), pltpu.SemaphoreType.DMA((S5_U_SLOTS,))],
        compiler_params=_params(),
        name="s5_core",
    )(u2, wst, kconv, kin, a_pair, s0_re, s0_im)


def _s5_out_kernel(x_ref, u_ref, y2_ref, dskip_ref, wglu_ref, o_ref, y_s, gate_s, *, nb, blocks):
    rows = blocks * S5_L * nb
    for cc in range(N_BLK):
        cols = slice(cc * V7X_MXU_DIM, (cc + 1) * V7X_MXU_DIM)
        for sub in range(V7X_MXU_DIM // V7X_LANES):
            c = cc * (V7X_MXU_DIM // V7X_LANES) + sub
            for half in range(S5_L // GRANULES_PER_VREG):
                lanes = [(c * GRANULES_PER_VREG + gi) * S5_L * GROUP_C + half * V7X_LANES
                         for gi in range(GRANULES_PER_VREG)]
                ws = _granule_transpose([y2_ref[:, l0:l0 + V7X_LANES] for l0 in lanes])
                for kk in range(GRANULES_PER_VREG):
                    y_s[cc, :, half * GRANULES_PER_VREG + kk, :, sub * V7X_LANES:(sub + 1) * V7X_LANES] = (
                        ws[kk].reshape(blocks, nb, V7X_LANES))
        y = y_s[cc].reshape(rows, V7X_MXU_DIM) + dskip_ref[:, cols] * u_ref[:, cols]
        gy = _gelu(y).astype(BF16)
        for acc, wcols in ((o_ref, slice(0, D_MODEL)), (gate_s, slice(D_MODEL, 2 * D_MODEL))):
            part = _dot(gy, wglu_ref[cols, wcols])
            if cc == 0:
                acc[...] = part
            else:
                acc[...] += part
    o_ref[...] = x_ref[...] + o_ref[...] * jax.nn.sigmoid(gate_s[...])


def _s5_out(x, u, y2, dskip, wglu, nb, blocks):
    rows = blocks * S5_L * nb
    row_spec = pl.BlockSpec((rows, D_MODEL), lambda i: (i, 0))
    return pl.pallas_call(
        functools.partial(_s5_out_kernel, nb=nb, blocks=blocks),
        grid=(x.shape[0] // rows,),
        in_specs=[row_spec, row_spec, pl.BlockSpec((blocks * nb, S5_L * D_MODEL), lambda i: (i, 0)),
                  _const_spec((1, D_MODEL)), _const_spec((D_MODEL, 2 * D_MODEL))],
        out_specs=row_spec,
        out_shape=jax.ShapeDtypeStruct(x.shape, F32),
        scratch_shapes=[pltpu.VMEM((N_BLK, blocks, S5_L, nb, V7X_MXU_DIM), F32), pltpu.VMEM((rows, D_MODEL), F32)],
        compiler_params=_params(),
        name="s5_out",
    )(x, u, y2, dskip, wglu)


def _pair_state(re, im):
    r = re.reshape(G_C // 2, 2 * P_C)
    i = im.reshape(G_C // 2, 2 * P_C)
    return jnp.concatenate([r, i], axis=-1).reshape(1, 2 * D_STATE)


def _s5_short_kernel(x_ref, g_ref, win_ref, kconv_ref, wsta_ref, wstb_ref, kina_ref, kinb_ref, a_ref, dskip_ref,
                     wglu_ref, s0re_ref, s0im_ref, o_ref, sfre_ref, sfim_ref, u8_s, y8_s, y_s, *, nb, steps):
    slots = GRANULES_PER_VREG
    gl = slots * GROUP_C
    sw = 2 * P_C
    x = x_ref[...]
    u = _dot(_rms(x, g_ref[...]).astype(BF16), win_ref[...])
    zero = jnp.zeros((nb, V7X_LANES), F32)
    for c in range(D_MODEL // V7X_LANES):
        cols = slice(c * V7X_LANES, (c + 1) * V7X_LANES)
        ws = _granule_transpose([u[k * nb:(k + 1) * nb, cols] for k in range(steps)] + [zero] * (slots - steps))
        for gi in range(GRANULES_PER_VREG):
            g = c * GRANULES_PER_VREG + gi
            u8_s[:, g * gl:(g + 1) * gl] = ws[gi].astype(BF16)
    for q in range(G_C // 2):
        up = u8_s[:, 2 * q * gl:(2 * q + 2) * gl]
        reps = slots // steps
        sc = _dot(up, jnp.concatenate([wsta_ref[q]] * reps + [wstb_ref[q]] * reps, axis=0))
        s_re, s_im = s0re_ref[:, sw * q:sw * (q + 1)], s0im_ref[:, sw * q:sw * (q + 1)]
        n_re, n_im = _cmul(a_ref[:, 2 * sw * q:2 * sw * q + sw], a_ref[:, 2 * sw * q + sw:2 * sw * (q + 1)], s_re, s_im)
        sfre_ref[:, sw * q:sw * (q + 1)] = n_re + sc[:, :sw]
        sfim_ref[:, sw * q:sw * (q + 1)] = n_im + sc[:, sw:]
        yk = _dot(jnp.concatenate([s_re, s_im], axis=1).astype(BF16),
                  jnp.concatenate([kina_ref[q], kinb_ref[q]], axis=1))
        for gi in range(2):
            y8_s[:, (2 * q + gi) * gl:(2 * q + gi + 1) * gl] = (
                _dot(up[:, gi * gl:(gi + 1) * gl], kconv_ref[2 * q + gi]) + yk[:, gi * gl:(gi + 1) * gl])
    for c in range(D_MODEL // V7X_LANES):
        ws = _granule_transpose([y8_s[:, (c * GRANULES_PER_VREG + gi) * gl:(c * GRANULES_PER_VREG + gi + 1) * gl]
                                 for gi in range(GRANULES_PER_VREG)])
        for k in range(steps):
            y_s[k * nb:(k + 1) * nb, c * V7X_LANES:(c + 1) * V7X_LANES] = ws[k]
    y = y_s[...] + dskip_ref[...] * u
    o = _dot(_gelu(y).astype(BF16), wglu_ref[...])
    o_ref[...] = x + o[:, :D_MODEL] * jax.nn.sigmoid(o[:, D_MODEL:])


def _s5_short(x, g, win, kconv, wst, kin, a_pair, dskip, wglu, s0_re, s0_im, nb, steps):
    assert S5_L % steps == 0 and GRANULES_PER_VREG % steps == 0
    rows = nb * steps
    slots = GRANULES_PER_VREG
    gl = slots * GROUP_C
    n = S5_L * GROUP_C
    srows = steps * GROUP_C
    row_spec = pl.BlockSpec((rows, D_MODEL), lambda i: (0, 0))
    state_spec = pl.BlockSpec((nb, D_STATE), lambda i: (0, 0))
    return pl.pallas_call(
        functools.partial(_s5_short_kernel, nb=nb, steps=steps),
        grid=(1,),
        in_specs=[row_spec, _const_spec((1, D_MODEL)), _const_spec((D_MODEL, D_MODEL)),
                  pl.BlockSpec((G_C, gl, gl), lambda i: (0, 0, 0)),
                  pl.BlockSpec((G_C // 2, srows, 4 * P_C), lambda i: (0, n // srows - 1, 0)),
                  pl.BlockSpec((G_C // 2, srows, 4 * P_C), lambda i: (0, 2 * n // srows - 1, 0)),
                  pl.BlockSpec((G_C // 2, 4 * P_C, gl), lambda i: (0, 0, 0)),
                  pl.BlockSpec((G_C // 2, 4 * P_C, gl), lambda i: (0, 0, n // gl)),
                  _const_spec((1, 2 * D_STATE)), _const_spec((1, D_MODEL)), _const_spec((D_MODEL, 2 * D_MODEL)),
                  state_spec, state_spec],
        out_specs=[row_spec, state_spec, state_spec],
        out_shape=[jax.ShapeDtypeStruct(x.shape, F32), jax.ShapeDtypeStruct((nb, D_STATE), F32),
                   jax.ShapeDtypeStruct((nb, D_STATE), F32)],
        scratch_shapes=[pltpu.VMEM((nb, G_C * gl), BF16), pltpu.VMEM((nb, G_C * gl), F32),
                        pltpu.VMEM((rows, D_MODEL), F32)],
        compiler_params=_params(),
        name="s5_short",
    )(x, g, win, kconv, wst, wst, kin, kin, a_pair, dskip, wglu, s0_re, s0_im)


def _block_diag(w, n_per_blk):
    n, k_in, k_out = w.shape
    wb = w.reshape(n // n_per_blk, n_per_blk, k_in, k_out)
    eye = jnp.eye(n_per_blk, dtype=w.dtype)
    out = jnp.einsum("jgio,gk->jgiko", wb, eye)
    return out.reshape(n // n_per_blk, n_per_blk * k_in, n_per_blk * k_out)


def _to_rows(a):
    return jnp.swapaxes(a, 0, 1).reshape(a.shape[0] * a.shape[1], a.shape[2])


def _from_rows(a, nb):
    return jnp.swapaxes(a.reshape(a.shape[0] // nb, nb, a.shape[1]), 0, 1)


def kernel(x_prompt, x_sample, state_rglru_conv, state_rglru_h, state_s5_re, state_s5_im, norm_mix, norm_ffn, norm_f, w_ff1, w_ff2, w_in_a, sgu_g, w_s, b_s, w_out_a, w_in_b, conv_w, conv_b, w_a, b_a, w_x, b_x, lam, w_out_b, w_in_c, lam_re, lam_im, log_dt, b_re, b_im, c_re, c_im, d_skip, w_glu):
    bp, tp, _ = x_prompt.shape
    bs, ts, _ = x_sample.shape
    rows_p, rows_s = bp * tp, bs * ts
    assert DEPTH % N_MIXERS == 1 and DEPTH > 1, "first and last layers must be SGU layers"
    assert tp % CHUNK == 0 and ts < CHUNK
    assert rows_s % FFN_SAMPLE_ROWS == 0 and rows_p % FFN_ROWS == 0

    row = lambda v: v.reshape(1, -1)
    norm_ffn3 = norm_ffn.reshape(DEPTH, 1, D_MODEL)
    w1, w2, gf = w_ff1, w_ff2, row(norm_f)
    w_in_a_bf, w_out_a_bf = w_in_a.astype(BF16), w_out_a.astype(BF16)

    x_p, x_s = x_prompt, _to_rows(x_sample)
    outs_v, conv_p, h_p, conv_s, h_s, sre_p, sim_p, sre_s, sim_s = [], [], [], [], [], [], [], [], []
    for layer in range(DEPTH):
        j, kind = layer // N_MIXERS, layer % N_MIXERS
        first, last = layer == 0, layer == DEPTH - 1
        g = row(norm_mix[layer])
        if kind == 0:
            p = dict(g=g, j=j, win=w_in_a_bf, wout=w_out_a_bf, sg=row(sgu_g[j]),
                     w_tril=jnp.where(jnp.tril(jnp.ones((CHUNK, CHUNK), dtype=bool)), w_s[j], 0.0).astype(BF16),
                     bias=jnp.repeat(b_s[j].T, HD_A, axis=1),
                     wl=jnp.repeat(w_s[j][:, :ts, :ts].reshape(HEADS_A, ts * ts).T, HD_A, axis=1),
                     bl=jnp.repeat(b_s[j][:, :ts].T, HD_A, axis=1))
            if first:
                x_p = _sgu_prompt(x_p, p, bp, tp, False, True)
            elif last:
                x_p = _sgu_prompt(x_p, p, bp, tp, True, False).reshape(rows_p, D_MODEL)
            else:
                raise NotImplementedError("interior SGU layers")
            x_s, v = _sgu_sample(x_s, p, bs, ts)
            outs_v.append(_from_rows(v, bs))
        elif kind == 1:
            p = dict(g=g, win=w_in_b[j].astype(BF16), cw=conv_w[j], cb=row(conv_b[j]),
                     wa=_block_diag(0.5 * w_a[j], HEADS_PER_BLK).astype(BF16), ba=row(0.5 * b_a[j]),
                     wx=_block_diag(w_x[j], HEADS_PER_BLK).astype(BF16), bx=row(b_x[j]),
                     lam=row(lam[j]), wout=w_out_b[j].astype(BF16))
            dt_s = state_rglru_h.dtype
            x_p, cp, hp = _rglru(x_p, p, jnp.zeros(((CONV_W - 1) * bp, D_MODEL), dt_s),
                                 jnp.zeros((bp, D_MODEL), dt_s), bp, RGLRU_PROMPT_STEPS)
            x_s, cs, hs = _rglru(x_s, p, _to_rows(state_rglru_conv[j]), state_rglru_h[j], bs, ts)
            conv_p.append(_from_rows(cp, bp)); h_p.append(hp)
            conv_s.append(_from_rows(cs, bs)); h_s.append(hs)
        else:
            bbre, bbim, pw = _s5_disc(lam_re[j], lam_im[j], log_dt[j],
                                      jnp.swapaxes(b_re[j], 1, 2), jnp.swapaxes(b_im[j], 1, 2))
            p = dict(g=g, win=w_in_c[j].astype(BF16), dskip=row(d_skip[j]), wglu=w_glu[j].astype(BF16))
            pwg = _pair_lanes(jnp.transpose(pw, (2, 0, 1, 3)))
            kconv, wst, kin = _s5_taps(pwg[:, :, :S5_L], pwg[:, :, 1:], pwg[:, :, S5_L - 1::-1],
                                       _pair_lanes(jnp.stack([c_re[j], c_im[j]], axis=1)),
                                       _pair_lanes(jnp.stack([bbre, bbim], axis=1)))
            zs = jnp.zeros((bp, D_STATE), state_s5_re.dtype)
            u, u2 = _s5_in(x_p, g, p["win"], bp, S5_TILE_BLOCKS)
            y2, rp, ip = _s5_core(u2, wst, kconv, kin, _pair_state(pw[0, S5_L], pw[1, S5_L]), zs, zs, bp)
            x_p = _s5_out(x_p, u, y2, p["dskip"], p["wglu"], bp, S5_TILE_BLOCKS)
            x_s, rs, is_ = _s5_short(x_s, g, p["win"], kconv, wst, kin, _pair_state(pw[0, ts], pw[1, ts]),
                                     p["dskip"], p["wglu"], state_s5_re[j].reshape(bs, D_STATE),
                                     state_s5_im[j].reshape(bs, D_STATE), bs, ts)
            sre_p.append(rp.reshape(bp, G_C, P_C)); sim_p.append(ip.reshape(bp, G_C, P_C))
            sre_s.append(rs.reshape(bs, G_C, P_C)); sim_s.append(is_.reshape(bs, G_C, P_C))
        x_p, x_s = _ffn(x_p, x_s, norm_ffn3, w1, w2, gf, layer, last)

    y_prompt = x_p.reshape(bp, tp, D_MODEL)
    y_sample = _from_rows(x_s, bs)
    return (y_prompt, y_sample, jnp.stack(outs_v), jnp.stack(conv_p), jnp.stack(h_p), jnp.stack(conv_s),
            jnp.stack(h_s), jnp.stack(sre_p), jnp.stack(sim_p), jnp.stack(sre_s), jnp.stack(sim_s))
```
